```python
import math
import jax, jax.numpy as jnp
from jax import lax
import numpy as np

D_MODEL = 2048
BATCH = 4
SEQ = 2048
DEPTH = 2
DEC_BATCH = 8
DEC_SEQ = 4
PAST_LEN = 16384
PAGE_SIZE = 128

HEAD = 128
D_MIX = D_MODEL
D_A = D_MIX // 4
D_B = D_MIX // 4
D_C = D_MIX - D_A - D_B
H_A = D_A // HEAD
DQ_A = HEAD // 2
DV_A = HEAD
H_B = D_B // HEAD
DH_B = HEAD
CMP_BLOCK = 64
N_SEL = 16
WINDOW = 512
H_C = D_C // HEAD
DQK_C = HEAD // 2
DV_C = HEAD
CONV_W = 4
MLSTM_CHUNK = 64
Q_BLOCK = 128
EPS = 1e-6

IN_WIDTHS = (
    H_A * 2 * DQ_A, H_A * 2 * DQ_A, D_A, D_A,
    D_B, 6 * DH_B, 3 * H_B, D_B,
    2 * H_C * DQK_C, D_C, 2 * H_C, D_C, D_C,
)
N_IN = sum(IN_WIDTHS)

kernel_name = 'hybrid_diffattn_nsa_mlstm_step'


def rms_norm(x, g):
    xf = x.astype(jnp.float32)
    y = xf * lax.rsqrt(jnp.mean(xf * xf, axis=-1, keepdims=True) + EPS)
    return (y * g.astype(jnp.float32)).astype(x.dtype)


def masked_softmax(s, mask):
    s = jnp.where(mask, s.astype(jnp.float32), -jnp.inf)
    m = jnp.max(s, axis=-1, keepdims=True)
    m = jnp.where(jnp.isfinite(m), m, 0.0)
    p = jnp.exp(s - m)
    return p / jnp.maximum(jnp.sum(p, axis=-1, keepdims=True), 1e-30)


def _query_block(T):
    return Q_BLOCK if T % Q_BLOCK == 0 else T


def map_query_blocks(fn, q, q_pos):
    B, T = q.shape[:2]
    qb = _query_block(T)
    nb = T // qb
    qs = jnp.moveaxis(q.reshape(B, nb, qb, *q.shape[2:]), 1, 0)
    out = lax.map(lambda xs: fn(xs[0], xs[1]), (qs, q_pos.reshape(nb, qb)))
    return jnp.moveaxis(out, 0, 1).reshape(B, T, *out.shape[3:])


def diff_attention(q, k, v, q_pos, k_pos, lam):
    kf = k.astype(jnp.float32)
    vf = v.astype(jnp.float32)
    scale = DQ_A ** -0.5

    def block(qb, pos):
        s = jnp.einsum('bqhcd,bkhcd->bchqk', qb.astype(jnp.float32), kf) * scale
        p = masked_softmax(s, k_pos[None, :] <= pos[:, None])
        w = p[:, 0] - lam * p[:, 1]
        return jnp.einsum('bhqk,bkhv->bqhv', w, vf)

    return map_query_blocks(block, q, q_pos)


def nsa_compressed_selected(q, kv, q_pos, w_pool, g_kc):
    B, L = kv.shape[:2]
    nblk = -(-L // CMP_BLOCK)
    pad = nblk * CMP_BLOCK - L
    kvb = jnp.pad(kv, ((0, 0), (0, pad), (0, 0), (0, 0))).reshape(B, nblk, CMP_BLOCK, 4, DH_B)
    wp = w_pool.astype(jnp.float32)
    kcb = rms_norm(jnp.einsum('bnpd,p->bnd', kvb[:, :, :, 0].astype(jnp.float32), wp[0]), g_kc)
    vcb = jnp.einsum('bnpd,p->bnd', kvb[:, :, :, 1].astype(jnp.float32), wp[1])
    ksb = kvb[:, :, :, 2]
    vsb = kvb[:, :, :, 3]
    n_top = min(N_SEL, nblk)
    blk_ids = jnp.arange(nblk)
    scale = DH_B ** -0.5

    def block(qb, pos):
        Q = qb.shape[1]
        qf = qb.astype(jnp.float32)
        cur = pos // CMP_BLOCK
        cmask = blk_ids[None, :] < cur[:, None]
        sc = jnp.einsum('bqhd,bnd->bqhn', qf, kcb) * scale
        pc = masked_softmax(sc, cmask[None, :, None, :])
        o_cmp = jnp.einsum('bqhn,bnd->bqhd', pc, vcb)
        imp = jnp.where(cmask[None], jnp.sum(pc, axis=2), -1.0)
        top_v, top_i = lax.top_k(imp, n_top)
        idx = jnp.concatenate([top_i, jnp.broadcast_to(cur[None, :, None], (B, Q, 1)).astype(top_i.dtype)], axis=-1)
        valid = jnp.concatenate([top_v >= 0.0, jnp.ones((B, Q, 1), bool)], axis=-1)
        kg = jax.vmap(lambda a, i: a[i])(ksb, idx).astype(jnp.float32)
        vg = jax.vmap(lambda a, i: a[i])(vsb, idx).astype(jnp.float32)
        kpos = idx[..., None] * CMP_BLOCK + jnp.arange(CMP_BLOCK)
        smask = valid[..., None] & (kpos <= pos[None, :, None, None])
        ss = jnp.einsum('bqhd,bqnpd->bqhnp', qf, kg) * scale
        ps = masked_softmax(ss.reshape(B, Q, H_B, -1), smask.reshape(B, Q, 1, -1))
        o_sel = jnp.einsum('bqhk,bqkd->bqhd', ps, vg.reshape(B, Q, -1, DH_B))
        return jnp.stack([o_cmp, o_sel], axis=-2)

    return map_query_blocks(block, q, q_pos)


def window_attention_banded(q, k, v):
    B, T = q.shape[:2]
    qb = _query_block(T)
    nb = T // qb
    span = WINDOW + qb
    kpad = jnp.pad(k, ((0, 0), (WINDOW, 0), (0, 0)))
    vpad = jnp.pad(v, ((0, 0), (WINDOW, 0), (0, 0)))
    idx = jnp.arange(nb)[:, None] * qb + jnp.arange(span)[None, :]
    k_pos = idx - WINDOW
    q_pos = jnp.arange(nb)[:, None] * qb + jnp.arange(qb)[None, :]
    kg = kpad[:, idx].astype(jnp.float32)
    vg = vpad[:, idx].astype(jnp.float32)
    diff = q_pos[:, :, None] - k_pos[:, None, :]
    mask = (k_pos[:, None, :] >= 0) & (diff >= 0) & (diff < WINDOW)
    s = jnp.einsum('bnqhd,bnkd->bnqhk', q.reshape(B, nb, qb, H_B, DH_B).astype(jnp.float32), kg) * DH_B ** -0.5
    p = masked_softmax(s, mask[None, :, :, None, :])
    return jnp.einsum('bnqhk,bnkd->bnqhd', p, vg).reshape(B, T, H_B, DH_B)


def window_attention_dense(q, k, v, q_pos, k_pos):
    s = jnp.einsum('bqhd,bkd->bqhk', q.astype(jnp.float32), k.astype(jnp.float32)) * DH_B ** -0.5
    diff = q_pos[:, None] - k_pos[None, :]
    mask = (diff >= 0) & (diff < WINDOW)
    p = masked_softmax(s, mask[None, :, None, :])
    return jnp.einsum('bqhk,bkd->bqhd', p, v.astype(jnp.float32))


def mlstm_chunkwise(q, k, v, i_pre, f_pre, C0, n0, m0):
    B, T, H, dk = q.shape
    dv = v.shape[-1]
    L = math.gcd(T, MLSTM_CHUNK)
    nc = T // L

    def to_chunks(a):
        return jnp.moveaxis(a.astype(jnp.float32).reshape(B, nc, L, *a.shape[2:]), 1, 0)

    causal = jnp.tril(jnp.ones((L, L), bool))

    def step(carry, xs):
        C, n, m = carry
        qc, kc, vc, ic, fc = xs
        b = jnp.cumsum(jax.nn.log_sigmoid(fc), axis=1)
        D = b[:, :, None, :] - b[:, None, :, :] + ic[:, None, :, :]
        D = jnp.where(causal[None, :, :, None], D, -jnp.inf)
        inter = b + m[:, None, :]
        m_t = jnp.maximum(inter, jnp.max(D, axis=2))
        w_intra = jnp.exp(D - m_t[:, :, None, :])
        w_inter = jnp.exp(inter - m_t)
        qk = jnp.einsum('bthd,bshd->btsh', qc, kc) * w_intra
        num = w_inter[..., None] * jnp.einsum('bhvd,bthd->bthv', C, qc) + jnp.einsum('btsh,bshv->bthv', qk, vc)
        den = w_inter * jnp.einsum('bhd,bthd->bth', n, qc) + jnp.sum(qk, axis=2)
        h = num / jnp.maximum(jnp.abs(den), jnp.exp(-m_t))[..., None]
        bL = b[:, -1]
        dec = bL[:, None, :] - b + ic
        m_new = jnp.maximum(bL + m, jnp.max(dec, axis=1))
        wk = jnp.exp(dec - m_new[:, None, :])
        carry_scale = jnp.exp(bL + m - m_new)
        C_new = carry_scale[..., None, None] * C + jnp.einsum('bsh,bshv,bshd->bhvd', wk, vc, kc)
        n_new = carry_scale[..., None] * n + jnp.einsum('bsh,bshd->bhd', wk, kc)
        return (C_new, n_new, m_new), h

    carry, hs = lax.scan(step, (C0.astype(jnp.float32), n0.astype(jnp.float32), m0.astype(jnp.float32)),
                         (to_chunks(q), to_chunks(k), to_chunks(v), to_chunks(i_pre), to_chunks(f_pre)))
    return jnp.moveaxis(hs, 0, 1).reshape(B, T, H, dv), carry


def mixer_layer(x, l, lw, past, past_len, win_buf):
    (norm_g, w_in, w_out, a_qk_g, a_lambda, a_out_g, b_qk_g, b_cmp_w,
     c_conv_w, c_conv_b, c_gate_b, c_out_g) = lw
    B, T, _ = x.shape
    f32 = jnp.float32
    h = rms_norm(x, norm_g)
    proj = jnp.einsum('btd,dn->btn', h, w_in)
    splits = np.cumsum(IN_WIDTHS)[:-1].tolist()
    (aq, ak, av, az, bq, bkv, bg, bz, cqk, cv, cif, co, cz) = jnp.split(proj, splits, axis=-1)
    q_pos = past_len + jnp.arange(T)

    lam_init = 0.8 - 0.6 * math.exp(-0.3 * l)
    qa = rms_norm(aq.reshape(B, T, H_A, 2, DQ_A), a_qk_g[0])
    ka = rms_norm(ak.reshape(B, T, H_A, 2, DQ_A), a_qk_g[1])
    new_a = jnp.stack([ka.reshape(B, T, H_A, HEAD), av.reshape(B, T, H_A, DV_A)], axis=2)
    kv_a = new_a if past is None else jnp.concatenate([past[0].astype(new_a.dtype), new_a], axis=1)
    L = kv_a.shape[1]
    lp = a_lambda.astype(f32)
    lam = jnp.exp(jnp.sum(lp[0] * lp[1])) - jnp.exp(jnp.sum(lp[2] * lp[3])) + lam_init
    oa = diff_attention(qa, kv_a[:, :, 0].reshape(B, L, H_A, 2, DQ_A), kv_a[:, :, 1], q_pos, jnp.arange(L), lam)
    oa = rms_norm(oa, a_out_g) * (1.0 - lam_init)
    ya = oa.reshape(B, T, D_A) * jax.nn.silu(az.astype(f32))

    qb_ = rms_norm(bq.reshape(B, T, H_B, DH_B), b_qk_g[0])
    bkv = bkv.reshape(B, T, 6, DH_B)
    new_b = jnp.stack([bkv[:, :, 0], bkv[:, :, 1], rms_norm(bkv[:, :, 2], b_qk_g[2]), bkv[:, :, 3]], axis=2)
    new_w = jnp.stack([rms_norm(bkv[:, :, 4], b_qk_g[3]), bkv[:, :, 5]], axis=2)
    if past is None:
        kv_b = new_b
        o_win = window_attention_banded(qb_, new_w[:, :, 0], new_w[:, :, 1])
        win_state = jnp.concatenate([jnp.zeros((B, win_buf, 2, DH_B), new_w.dtype), new_w], axis=1)[:, -win_buf:]
    else:
        kv_b = jnp.concatenate([past[1].astype(new_b.dtype), new_b], axis=1)
        w_all = jnp.concatenate([past[2].astype(new_w.dtype), new_w], axis=1)
        k_pos_w = past_len - win_buf + jnp.arange(win_buf + T)
        o_win = window_attention_dense(qb_, w_all[:, :, 0], w_all[:, :, 1], q_pos, k_pos_w)
        win_state = w_all[:, -win_buf:]
    o_cs = nsa_compressed_selected(qb_, kv_b, q_pos, b_cmp_w, b_qk_g[1])
    g = jax.nn.sigmoid(bg.astype(f32).reshape(B, T, H_B, 3))
    ob = g[..., 0:1] * o_cs[..., 0, :] + g[..., 1:2] * o_cs[..., 1, :] + g[..., 2:3] * o_win
    yb = ob.reshape(B, T, D_B) * jax.nn.silu(bz.astype(f32))

    buf = jnp.zeros((B, CONV_W - 1, cqk.shape[-1]), cqk.dtype) if past is None else past[3].astype(cqk.dtype)
    xp = jnp.concatenate([buf, cqk], axis=1)
    conv_state = xp[:, -(CONV_W - 1):]
    acc = c_conv_b.astype(f32)
    for j in range(CONV_W):
        acc = acc + xp[:, j:j + T].astype(f32) * c_conv_w[j].astype(f32)
    qk_c = jax.nn.silu(acc)
    qc = qk_c[..., :H_C * DQK_C].reshape(B, T, H_C, DQK_C) * (DQK_C ** -0.5)
    kc = qk_c[..., H_C * DQK_C:].reshape(B, T, H_C, DQK_C)
    vc = cv.reshape(B, T, H_C, DV_C)
    gif = cif.astype(f32).reshape(B, T, 2, H_C) + c_gate_b.astype(f32)
    if past is None:
        C0 = jnp.zeros((B, H_C, DV_C, DQK_C), f32)
        n0 = jnp.zeros((B, H_C, DQK_C), f32)
        m0 = jnp.zeros((B, H_C), f32)
    else:
        C0, n0, m0 = past[4], past[5], past[6]
    hc, (C1, n1, m1) = mlstm_chunkwise(qc, kc, vc, gif[:, :, 0], gif[:, :, 1], C0, n0, m0)
    hc = rms_norm(hc, c_out_g) * jax.nn.sigmoid(co.astype(f32).reshape(B, T, H_C, DV_C))
    yc = hc.reshape(B, T, D_C) * jax.nn.silu(cz.astype(f32))

    mix = jnp.concatenate([ya, yb, yc], axis=-1)
    y = x + jnp.einsum('btm,md->btd', mix, w_out.astype(f32)).astype(x.dtype)
    return y, (new_a, new_b, win_state, conv_state, C1, n1, m1)


def setup_inputs(seed: int = 0) -> dict:
    key = jax.random.key(seed)
    ks = jax.random.split(key, 32)
    n_pages = PAST_LEN // PAGE_SIZE
    used = DEC_BATCH * n_pages
    n_pool = used + max(1, used // 4)
    win_buf = min(WINDOW, PAST_LEN)
    nrm = jax.random.normal
    inp = {}
    inp['x_prompt'] = nrm(ks[0], (BATCH, SEQ, D_MODEL), jnp.float32)
    inp['x_sample'] = nrm(ks[1], (DEC_BATCH, DEC_SEQ, D_MODEL), jnp.float32)
    inp['cache_a_kv'] = nrm(ks[2], (DEPTH, n_pool, PAGE_SIZE, 2, H_A, HEAD), jnp.float32)
    inp['cache_b_kv'] = nrm(ks[3], (DEPTH, n_pool, PAGE_SIZE, 4, DH_B), jnp.float32)
    inp['state_b_win'] = nrm(ks[4], (DEPTH, DEC_BATCH, win_buf, 2, DH_B), jnp.float32)
    inp['state_c_conv'] = nrm(ks[5], (DEPTH, DEC_BATCH, CONV_W - 1, 2 * H_C * DQK_C), jnp.float32)
    inp['state_c_C'] = 0.1 * nrm(ks[6], (DEPTH, DEC_BATCH, H_C, DV_C, DQK_C), jnp.float32)
    inp['state_c_n'] = 0.1 * nrm(ks[7], (DEPTH, DEC_BATCH, H_C, DQK_C), jnp.float32)
    inp['state_c_m'] = nrm(ks[8], (DEPTH, DEC_BATCH, H_C), jnp.float32)
    perm = jax.random.permutation(ks[9], n_pool)
    inp['page_table'] = perm[:used].reshape(DEC_BATCH, n_pages).astype(jnp.int32)
    inp['norm_g'] = 1.0 + 0.02 * nrm(ks[10], (DEPTH, D_MODEL), jnp.float32)
    inp['w_in'] = nrm(ks[11], (DEPTH, D_MODEL, N_IN), jnp.float32) * D_MODEL ** -0.5
    inp['w_out'] = nrm(ks[12], (DEPTH, D_MIX, D_MODEL), jnp.float32) * D_MIX ** -0.5
    inp['a_qk_g'] = 1.0 + 0.02 * nrm(ks[13], (DEPTH, 2, DQ_A), jnp.float32)
    inp['a_lambda'] = 0.1 * nrm(ks[14], (DEPTH, 4, DQ_A), jnp.float32)
    inp['a_out_g'] = 1.0 + 0.02 * nrm(ks[15], (DEPTH, DV_A), jnp.float32)
    inp['b_qk_g'] = 1.0 + 0.02 * nrm(ks[16], (DEPTH, 4, DH_B), jnp.float32)
    inp['b_cmp_w'] = (1.0 + 0.1 * nrm(ks[17], (DEPTH, 2, CMP_BLOCK), jnp.float32)) / CMP_BLOCK
    inp['c_conv_w'] = nrm(ks[18], (DEPTH, CONV_W, 2 * H_C * DQK_C), jnp.float32) * CONV_W ** -0.5
    inp['c_conv_b'] = 0.02 * nrm(ks[19], (DEPTH, 2 * H_C * DQK_C), jnp.float32)
    gate_noise = 0.1 * nrm(ks[20], (DEPTH, 2, H_C), jnp.float32)
    inp['c_gate_b'] = gate_noise + jnp.array([0.0, 3.0], jnp.float32)[None, :, None]
    inp['c_out_g'] = 1.0 + 0.02 * nrm(ks[21], (DEPTH, DV_C), jnp.float32)
    return inp


def reference(x_prompt, x_sample, cache_a_kv, cache_b_kv, state_b_win, state_c_conv, state_c_C,
              state_c_n, state_c_m, page_table, norm_g, w_in, w_out, a_qk_g, a_lambda, a_out_g,
              b_qk_g, b_cmp_w, c_conv_w, c_conv_b, c_gate_b, c_out_g):
    weights = (norm_g, w_in, w_out, a_qk_g, a_lambda, a_out_g, b_qk_g, b_cmp_w,
               c_conv_w, c_conv_b, c_gate_b, c_out_g)
    db, n_pages = page_table.shape
    past_len = n_pages * PAGE_SIZE
    win_buf = state_b_win.shape[2]
    hp, hs = x_prompt, x_sample
    new_p = [[] for _ in range(7)]
    new_s = [[] for _ in range(7)]
    for l in range(DEPTH):
        lw = tuple(w[l] for w in weights)
        hp, st_p = mixer_layer(hp, l, lw, None, 0, win_buf)
        past = (cache_a_kv[l][page_table].reshape(db, past_len, 2, H_A, HEAD),
                cache_b_kv[l][page_table].reshape(db, past_len, 4, DH_B),
                state_b_win[l], state_c_conv[l], state_c_C[l], state_c_n[l], state_c_m[l])
        hs, st_s = mixer_layer(hs, l, lw, past, past_len, win_buf)
        for lst, a in zip(new_p, st_p):
            lst.append(a)
        for lst, a in zip(new_s, st_s):
            lst.append(a)
    a_kv_p, b_kv_p, win_p, conv_p, C_p, n_p, m_p = [jnp.stack(lst) for lst in new_p]
    a_kv_s, b_kv_s, win_s, conv_s, C_s, n_s, m_s = [jnp.stack(lst) for lst in new_s]
    return (hp, hs, a_kv_p, a_kv_s, b_kv_p, b_kv_s, win_p, win_s, conv_p, conv_s,
            C_p, C_s, n_p, n_s, m_p, m_s)
```

```python
import functools
import math

import jax
import jax.numpy as jnp
from jax import lax
from jax.experimental import pallas as pl
from jax.experimental.pallas import tpu as pltpu

F32 = jnp.float32
BF16 = jnp.bfloat16

EPS = 1e-6
LANE = 128
HEAD = 128
H_A = 4
DQ_A = HEAD // 2
H_B = 4
H_C = 8
DQK_C = HEAD // 2
CMP_BLOCK = 64
N_SEL = 16
WINDOW = 512
CONV_W = 4
MLSTM_CHUNK = 64
PAGE = 128
Q_BLOCK = 128
NEG = -1e30
VMEM_LIMIT = 56 * 1024 * 1024

_SRC = (("aq", 512), ("ak", 512), ("av", 512), ("az", 512),
        ("bq", 512), ("bkv", 768), ("bg", 12), ("bz", 512),
        ("cqk", 1024), ("cv", 1024), ("cif", 16), ("co", 1024), ("cz", 1024))
_DST_ORDER = ("aq", "ak", "av", "az", "bq", "bz", "bkv", "bg", "cif", "cqk", "cv", "co", "cz")
_UNIT = {"aq": 0, "ak": 4, "av": 8, "az": 12, "bq": 16, "bz": 20, "bkv": 24, "bg": 30,
         "cif": 31, "cqk": 32, "cv": 40, "co": 48, "cz": 56}
N_PROJ = 64 * LANE


def _nt(a, b):
    return lax.dot_general(a, b, (((1,), (1,)), ((), ())), preferred_element_type=F32)


def _tn(a, b):
    return lax.dot_general(a, b, (((0,), (0,)), ((), ())), preferred_element_type=F32)


def _mm(a, b):
    return jnp.dot(a, b, preferred_element_type=F32)


def _split2(x):
    hi = x.astype(BF16)
    lo = (x - hi.astype(F32)).astype(BF16)
    return hi, lo


def _split3(x):
    hi = x.astype(BF16)
    r = x - hi.astype(F32)
    mid = r.astype(BF16)
    lo = (r - mid.astype(F32)).astype(BF16)
    return hi, mid, lo


def _sigmoid(z):
    return 1.0 / (1.0 + jnp.exp(-z))


def _silu(z):
    return z * _sigmoid(z)


def _rms(x, g):
    return x * lax.rsqrt(jnp.mean(x * x, axis=-1, keepdims=True) + EPS) * g


def _div(x, n):
    return lax.shift_right_logical(x, int(math.log2(n)))


def _mod(x, n):
    return x & (n - 1)


def _params(*sem):
    return pltpu.CompilerParams(dimension_semantics=sem, vmem_limit_bytes=VMEM_LIMIT)


def _norm_matmul(x, g, w, *, tm, tn):
    m, d = x.shape
    n = w.shape[1]
    rc = min(tm, 256)

    def body(x_ref, g_ref, w_ref, o_ref, h_scr):
        @pl.when(pl.program_id(1) == 0)
        def _():
            def chunk(c, carry):
                r = pl.ds(pl.multiple_of(c * rc, rc), rc)
                h_scr[r, :] = _rms(x_ref[r, :], g_ref[...]).astype(BF16)
                return carry
            lax.fori_loop(0, tm // rc, chunk, 0)
        o_ref[...] = _mm(h_scr[...], w_ref[...])

    return pl.pallas_call(
        body, grid=(m // tm, n // tn),
        in_specs=[pl.BlockSpec((tm, d), lambda i, j: (i, 0)),
                  pl.BlockSpec((1, d), lambda i, j: (0, 0)),
                  pl.BlockSpec((d, tn), lambda i, j: (0, j))],
        out_specs=pl.BlockSpec((tm, tn), lambda i, j: (i, j)),
        out_shape=jax.ShapeDtypeStruct((m, n), F32),
        scratch_shapes=[pltpu.VMEM((tm, d), BF16)],
        compiler_params=_params("parallel", "arbitrary"),
        name="norm_matmul")(x, g.reshape(1, d), w)


def _out_proj(x, ya, yb, yc, w, *, tm, tn):
    m, d = x.shape
    da, db, dc = ya.shape[1], yb.shape[1], yc.shape[1]

    def body(x_ref, a_ref, b_ref, c_ref, wa_ref, wb_ref, wc_ref, o_ref):
        acc = _mm(a_ref[...], wa_ref[...])
        acc += _mm(b_ref[...], wb_ref[...])
        acc += _mm(c_ref[...], wc_ref[...])
        o_ref[...] = x_ref[...] + acc

    return pl.pallas_call(
        body, grid=(m // tm, d // tn),
        in_specs=[pl.BlockSpec((tm, tn), lambda i, j: (i, j)),
                  pl.BlockSpec((tm, da), lambda i, j: (i, 0)),
                  pl.BlockSpec((tm, db), lambda i, j: (i, 0)),
                  pl.BlockSpec((tm, dc), lambda i, j: (i, 0)),
                  pl.BlockSpec((da, tn), lambda i, j: (0, j)),
                  pl.BlockSpec((db, tn), lambda i, j: (da // db, j)),
                  pl.BlockSpec((dc, tn), lambda i, j: ((da + db) // dc, j))],
        out_specs=pl.BlockSpec((tm, tn), lambda i, j: (i, j)),
        out_shape=jax.ShapeDtypeStruct((m, d), F32),
        compiler_params=_params("parallel", "parallel"),
        name="out_proj")(x, ya, yb, yc, w, w, w)


def _a_prep(proj, g2, *, tm):
    m = proj.shape[0]

    def body(q_ref, k_ref, v_ref, g_ref, qn_ref, kn_ref, vb_ref, akv_ref):
        r = _div(lax.broadcasted_iota(jnp.int32, (LANE, LANE), 0), DQ_A)
        c = _div(lax.broadcasted_iota(jnp.int32, (LANE, LANE), 1), DQ_A)
        seg = (r == c).astype(BF16)

        def segnorm(x, g):
            outs = []
            for u in range(4):
                xc = x[:, u * LANE:(u + 1) * LANE]
                hi, lo = _split2(xc * xc)
                s = _mm(hi, seg) + _mm(lo, seg)
                outs.append(xc * lax.rsqrt(s * (1.0 / DQ_A) + EPS) * g)
            return jnp.concatenate(outs, axis=1)

        qn = segnorm(q_ref[...], g_ref[0:1, :]) * (DQ_A ** -0.5)
        kn = segnorm(k_ref[...], g_ref[1:2, :])
        v = v_ref[...]
        qn_ref[...] = qn.astype(BF16)
        kn_ref[...] = kn.astype(BF16)
        vb_ref[...] = v.astype(BF16)
        akv_ref[:, 0:512] = kn
        akv_ref[:, 512:1024] = v

    blk = lambda u: pl.BlockSpec((tm, 512), lambda i, u=u: (i, u))
    row = lambda w: pl.BlockSpec((tm, w), lambda i: (i, 0))
    return pl.pallas_call(
        body, grid=(m // tm,),
        in_specs=[blk(_UNIT["aq"] // 4), blk(_UNIT["ak"] // 4), blk(_UNIT["av"] // 4),
                  pl.BlockSpec((2, LANE), lambda i: (0, 0))],
        out_specs=[row(512), row(512), row(512), row(1024)],
        out_shape=[jax.ShapeDtypeStruct((m, 512), BF16)] * 3 + [jax.ShapeDtypeStruct((m, 1024), F32)],
        compiler_params=_params("parallel"),
        name="a_prep")(proj, proj, proj, g2)


def _diff_lambda(lp, lam_init):
    a = jnp.sum(lp[0:1, :] * lp[1:2, :], axis=-1, keepdims=True)
    b = jnp.sum(lp[2:3, :] * lp[3:4, :], axis=-1, keepdims=True)
    return jnp.exp(a) - jnp.exp(b) + lam_init


def _a_finish(acc1, l1, acc2, l2, lam, g, z, lam_init):
    o = acc1 / l1 - lam * (acc2 / l2)
    return _rms(o, g) * (1.0 - lam_init) * _silu(z)


def _a_attn_prompt(qn, kn, vb, proj3, lam_p, out_g, lam_init, *, tq):
    b_, t, _ = qn.shape
    tk = tq

    def body(q_ref, k_ref, v_ref, z_ref, lam_ref, g_ref, o_ref, m_scr, l_scr, acc_scr):
        qi = pl.program_id(2)
        q = q_ref[0]
        lane = lax.broadcasted_iota(jnp.int32, (tq, LANE), 1)
        zero = jnp.zeros_like(q)
        qs = (jnp.where(lane < DQ_A, q, zero), jnp.where(lane >= DQ_A, q, zero))
        m_scr[...] = jnp.full(m_scr.shape, NEG, F32)
        l_scr[...] = jnp.zeros(l_scr.shape, F32)
        acc_scr[...] = jnp.zeros(acc_scr.shape, F32)
        qpos = qi * tq + lax.broadcasted_iota(jnp.int32, (tq, tk), 0)

        def chunk(kc, carry):
            ks = pl.ds(pl.multiple_of(kc * tk, tk), tk)
            k = k_ref[0, ks, :]
            v = v_ref[0, ks, :]
            kpos = kc * tk + lax.broadcasted_iota(jnp.int32, (tq, tk), 1)
            mask = kpos <= qpos
            for c in range(2):
                s = jnp.where(mask, _nt(qs[c], k), NEG)
                m_old = m_scr[c]
                m_new = jnp.maximum(m_old, jnp.max(s, axis=-1, keepdims=True))
                alpha = jnp.exp(m_old - m_new)
                p = jnp.exp(s - m_new)
                l_scr[c] = alpha * l_scr[c] + jnp.sum(p, axis=-1, keepdims=True)
                acc_scr[c] = alpha * acc_scr[c] + _mm(p.astype(BF16), v)
                m_scr[c] = m_new
            return carry

        lax.fori_loop(0, qi + 1, chunk, 0)
        lam = _diff_lambda(lam_ref[...], lam_init)
        y = _a_finish(acc_scr[0], l_scr[0], acc_scr[1], l_scr[1], lam, g_ref[...], z_ref[0], lam_init)
        o_ref[0] = y.astype(BF16)

    head = lambda rows, qdep: pl.BlockSpec(
        (1, rows, LANE), (lambda b, h, i: (b, i, h)) if qdep else (lambda b, h, i: (b, 0, h)))
    return pl.pallas_call(
        body, grid=(b_, H_A, t // tq),
        in_specs=[head(tq, True), head(t, False), head(t, False),
                  pl.BlockSpec((1, tq, LANE), lambda b, h, i: (b, i, _UNIT["az"] + h)),
                  pl.BlockSpec((4, DQ_A), lambda b, h, i: (0, 0)),
                  pl.BlockSpec((1, LANE), lambda b, h, i: (0, 0))],
        out_specs=head(tq, True),
        out_shape=jax.ShapeDtypeStruct((b_, t, 512), BF16),
        scratch_shapes=[pltpu.VMEM((2, tq, 1), F32), pltpu.VMEM((2, tq, 1), F32),
                        pltpu.VMEM((2, tq, LANE), F32)],
        compiler_params=_params("parallel", "parallel", "arbitrary"),
        name="a_attn_prompt")(qn, kn, vb, proj3, lam_p, out_g.reshape(1, LANE))


def _a_attn_sample(page_table, cache2, qn, kvnew, proj3, lam_p, out_g, lam_init, page_base,
                   *, n_valid, pg):
    db, n_pages = page_table.shape
    tp = qn.shape[1]
    ng = n_pages // pg

    def body(pt_ref, *refs):
        pages = refs[:pg]
        q_ref, new_ref, z_ref, lam_ref, g_ref, o_ref, m_scr, l_scr, acc_scr = refs[pg:]
        gi = pl.program_id(1)

        @pl.when(gi == 0)
        def _():
            m_scr[...] = jnp.full(m_scr.shape, NEG, F32)
            l_scr[...] = jnp.zeros(l_scr.shape, F32)
            acc_scr[...] = jnp.zeros(acc_scr.shape, F32)

        lane = lax.broadcasted_iota(jnp.int32, (tp, LANE), 1)

        def qstack(h):
            q = q_ref[0, :, h * LANE:(h + 1) * LANE]
            zero = jnp.zeros_like(q)
            return jnp.concatenate([jnp.where(lane < DQ_A, q, zero),
                                    jnp.where(lane >= DQ_A, q, zero)], axis=0)

        def update(h, s, vs, mask):
            m_old = m_scr[h]
            m_new = jnp.maximum(m_old, jnp.max(s, axis=-1, keepdims=True))
            alpha = jnp.exp(m_old - m_new)
            p = jnp.exp(s - m_new)
            if mask is not None:
                p = jnp.where(mask, p, 0.0)
            pv = None
            for j, v in enumerate(vs):
                d = _mm(p[:, j * PAGE:(j + 1) * PAGE].astype(BF16), v)
                pv = d if pv is None else pv + d
            l_scr[h] = alpha * l_scr[h] + jnp.sum(p, axis=-1, keepdims=True)
            acc_scr[h] = alpha * acc_scr[h] + pv
            m_scr[h] = m_new

        for h in range(H_A):
            qh = qstack(h)
            ss, vs = [], []
            for j in range(pg):
                k = pages[j][0, :, h * LANE:(h + 1) * LANE].astype(BF16)
                vs.append(pages[j][0, :, 512 + h * LANE:512 + (h + 1) * LANE].astype(BF16))
                ss.append(_nt(qh, k))
            update(h, jnp.concatenate(ss, axis=1), vs, None)

        @pl.when(gi == ng - 1)
        def _():
            lam = _diff_lambda(lam_ref[...], lam_init)
            row = _mod(lax.broadcasted_iota(jnp.int32, (2 * tp, PAGE), 0), tp)
            col = lax.broadcasted_iota(jnp.int32, (2 * tp, PAGE), 1)
            mask = (col <= row) & (col < n_valid)
            for h in range(H_A):
                qh = qstack(h)
                k = new_ref[0, :, h * LANE:(h + 1) * LANE]
                v = new_ref[0, :, 512 + h * LANE:512 + (h + 1) * LANE]
                s = jnp.where(mask, _nt(qh, k), NEG)
                update(h, s, [v], mask)
                acc = acc_scr[h]
                l = l_scr[h]
                y = _a_finish(acc[0:tp], l[0:tp], acc[tp:], l[tp:], lam, g_ref[...],
                              z_ref[0, :, h * LANE:(h + 1) * LANE], lam_init)
                o_ref[0, :, h * LANE:(h + 1) * LANE] = y.astype(BF16)

    def page_spec(j):
        return pl.BlockSpec((1, PAGE, 1024),
                            lambda b, g, pt, j=j: (page_base + pt[b, g * pg + j], 0, 0))

    grid_spec = pltpu.PrefetchScalarGridSpec(
        num_scalar_prefetch=1, grid=(db, ng),
        in_specs=[page_spec(j) for j in range(pg)] + [
            pl.BlockSpec((1, tp, 512), lambda b, g, pt: (b, 0, 0)),
            pl.BlockSpec((1, PAGE, 1024), lambda b, g, pt: (b, 0, 0)),
            pl.BlockSpec((1, tp, 512), lambda b, g, pt: (b, 0, _UNIT["az"] // 4)),
            pl.BlockSpec((4, DQ_A), lambda b, g, pt: (0, 0)),
            pl.BlockSpec((1, LANE), lambda b, g, pt: (0, 0))],
        out_specs=pl.BlockSpec((1, tp, 512), lambda b, g, pt: (b, 0, 0)),
        scratch_shapes=[pltpu.VMEM((H_A, 2 * tp, 1), F32), pltpu.VMEM((H_A, 2 * tp, 1), F32),
                        pltpu.VMEM((H_A, 2 * tp, LANE), F32)])
    return pl.pallas_call(
        body, grid_spec=grid_spec,
        out_shape=jax.ShapeDtypeStruct((db, tp, 512), BF16),
        compiler_params=_params("parallel", "arbitrary"),
        name="a_attn_sample")(page_table, *([cache2] * pg), qn, kvnew, proj3, lam_p,
                              out_g.reshape(1, LANE))


def _b_prep(proj, g4, wpb, *, tm, pool):
    m = proj.shape[0]
    nb = tm // CMP_BLOCK

    def body(kv_ref, g_ref, wp_ref, bkv_ref, nw_ref, kvb_ref, *pool_refs):
        kc = kv_ref[:, 0:128]
        vc = kv_ref[:, 128:256]
        ks = _rms(kv_ref[:, 256:384], g_ref[2:3, :])
        vs = kv_ref[:, 384:512]
        kw = _rms(kv_ref[:, 512:640], g_ref[3:4, :])
        vw = kv_ref[:, 640:768]
        bkv_ref[:, 0:128] = kc
        bkv_ref[:, 128:256] = vc
        bkv_ref[:, 256:384] = ks
        bkv_ref[:, 384:512] = vs
        nw_ref[:, 0:128] = kw
        nw_ref[:, 128:256] = vw
        kvb_ref[:, 0:128] = ks.astype(BF16)
        kvb_ref[:, 128:256] = vs.astype(BF16)
        kvb_ref[:, 256:384] = kw.astype(BF16)
        kvb_ref[:, 384:512] = vw.astype(BF16)
        if pool:
            kcb_ref, vcb_ref = pool_refs
            kp = jnp.sum(kc.reshape(nb, CMP_BLOCK, LANE) * wp_ref[0][None], axis=1)
            vp = jnp.sum(vc.reshape(nb, CMP_BLOCK, LANE) * wp_ref[1][None], axis=1)
            kcb_ref[...] = _rms(kp, g_ref[1:2, :])
            vcb_ref[...] = vp

    row = lambda w: pl.BlockSpec((tm, w), lambda i: (i, 0))
    out_specs = [row(512), row(256), row(512)]
    out_shape = [jax.ShapeDtypeStruct((m, 512), F32), jax.ShapeDtypeStruct((m, 256), F32),
                 jax.ShapeDtypeStruct((m, 512), BF16)]
    if pool:
        out_specs += [pl.BlockSpec((nb, LANE), lambda i: (i, 0))] * 2
        out_shape += [jax.ShapeDtypeStruct((m // CMP_BLOCK, LANE), F32)] * 2
    return pl.pallas_call(
        body, grid=(m // tm,),
        in_specs=[pl.BlockSpec((tm, 768), lambda i: (i, _UNIT["bkv"] // 6)),
                  pl.BlockSpec((4, LANE), lambda i: (0, 0)),
                  pl.BlockSpec((2, CMP_BLOCK, LANE), lambda i: (0, 0, 0))],
        out_specs=out_specs, out_shape=out_shape,
        compiler_params=_params("parallel"),
        name="b_prep")(proj, g4, wpb)


def _cmp_scores(qf, kcb):
    qh, ql = _split2(qf)
    kh, kl = _split2(kcb)
    return _nt(qh, kh) + _nt(qh, kl) + _nt(ql, kh)


def _b_attn_prompt(proj3, kcb, vcb, kvb, g4, *, t):
    b_ = proj3.shape[0]
    tq = Q_BLOCK
    nblk = t // CMP_BLOCK
    n_top = min(N_SEL, nblk)
    tk = min(512, t)
    span = min(WINDOW + tq, t)
    scale = HEAD ** -0.5

    def body(q_ref, kcb_ref, vcb_ref, kv_ref, bg_ref, bz_ref, g_ref, o_ref, m_scr, l_scr, acc_scr):
        qi = pl.program_id(1)
        qf = [_rms(q_ref[0, :, h * LANE:(h + 1) * LANE], g_ref[0:1, :]) * scale for h in range(H_B)]
        qb = [x.astype(BF16) for x in qf]
        pos = qi * tq + lax.broadcasted_iota(jnp.int32, (tq, 1), 0)
        cur = _div(pos, CMP_BLOCK)
        blk = lax.broadcasted_iota(jnp.int32, (tq, nblk), 1)
        cmask = blk < cur

        kc = kcb_ref[0]
        vch = vcb_ref[0].astype(BF16)
        o_cmp, imp = [], jnp.zeros((tq, nblk), F32)
        for h in range(H_B):
            s = jnp.where(cmask, _cmp_scores(qf[h], kc), NEG)
            mx = jnp.max(s, axis=-1, keepdims=True)
            p = jnp.where(cmask, jnp.exp(s - mx), 0.0)
            pc = p / jnp.maximum(jnp.sum(p, axis=-1, keepdims=True), 1e-30)
            o_cmp.append(_mm(pc.astype(BF16), vch))
            imp = imp + pc
        imp = jnp.where(cmask, imp, -1.0)

        rank = jnp.zeros((tq, nblk), F32)
        for mcol in range(nblk):
            cm = imp[:, mcol:mcol + 1]
            ahead = (cm > imp) | ((cm == imp) & (mcol < blk))
            rank = rank + ahead.astype(F32)
        selc = (((rank < n_top) & (imp >= 0.0)) | (blk == cur)).astype(BF16)

        m_scr[...] = jnp.full(m_scr.shape, NEG, F32)
        l_scr[...] = jnp.zeros(l_scr.shape, F32)
        acc_scr[...] = jnp.zeros(acc_scr.shape, F32)
        nk = _div(qi * tq + tq + tk - 1, tk)

        def chunk(c, carry):
            ks_ = pl.ds(pl.multiple_of(c * tk, tk), tk)
            k = kv_ref[0, ks_, 0:128]
            v = kv_ref[0, ks_, 128:256]
            kidx = c * tk + lax.broadcasted_iota(jnp.int32, (nblk, tk), 1)
            e = (_div(kidx, CMP_BLOCK) == lax.broadcasted_iota(jnp.int32, (nblk, tk), 0)).astype(BF16)
            kpos = c * tk + lax.broadcasted_iota(jnp.int32, (tq, tk), 1)
            mask = (_mm(selc, e) > 0.5) & (kpos <= pos)
            for h in range(H_B):
                s = jnp.where(mask, _nt(qb[h], k), NEG)
                m_old = m_scr[h]
                m_new = jnp.maximum(m_old, jnp.max(s, axis=-1, keepdims=True))
                alpha = jnp.exp(m_old - m_new)
                p = jnp.where(mask, jnp.exp(s - m_new), 0.0)
                l_scr[h] = alpha * l_scr[h] + jnp.sum(p, axis=-1, keepdims=True)
                acc_scr[h] = alpha * acc_scr[h] + _mm(p.astype(BF16), v)
                m_scr[h] = m_new
            return carry

        lax.fori_loop(0, nk, chunk, 0)

        start = jnp.clip(qi * tq + tq - span, 0, t - span)
        ws = pl.ds(pl.multiple_of(start, tq), span)
        kw = kv_ref[0, ws, 256:384]
        vw = kv_ref[0, ws, 384:512]
        diff = pos - (start + lax.broadcasted_iota(jnp.int32, (tq, span), 1))
        wmask = (diff >= 0) & (diff < WINDOW)

        gate = _sigmoid(bg_ref[0])
        for h in range(H_B):
            s = jnp.where(wmask, _nt(qb[h], kw), NEG)
            mx = jnp.max(s, axis=-1, keepdims=True)
            p = jnp.where(wmask, jnp.exp(s - mx), 0.0)
            o_win = _mm(p.astype(BF16), vw) / jnp.maximum(jnp.sum(p, axis=-1, keepdims=True), 1e-30)
            o_sel = acc_scr[h] / jnp.maximum(l_scr[h], 1e-30)
            ob = (gate[:, 3 * h:3 * h + 1] * o_cmp[h] + gate[:, 3 * h + 1:3 * h + 2] * o_sel
                  + gate[:, 3 * h + 2:3 * h + 3] * o_win)
            y = ob * _silu(bz_ref[0, :, h * LANE:(h + 1) * LANE])
            o_ref[0, :, h * LANE:(h + 1) * LANE] = y.astype(BF16)

    full = lambda rows, w: pl.BlockSpec((1, rows, w), lambda b, i: (b, 0, 0))
    return pl.pallas_call(
        body, grid=(b_, t // tq),
        in_specs=[pl.BlockSpec((1, tq, 512), lambda b, i: (b, i, _UNIT["bq"] // 4)),
                  full(nblk, LANE), full(nblk, LANE), full(t, 512),
                  pl.BlockSpec((1, tq, LANE), lambda b, i: (b, i, _UNIT["bg"])),
                  pl.BlockSpec((1, tq, 512), lambda b, i: (b, i, _UNIT["bz"] // 4)),
                  pl.BlockSpec((4, LANE), lambda b, i: (0, 0))],
        out_specs=pl.BlockSpec((1, tq, 512), lambda b, i: (b, i, 0)),
        out_shape=jax.ShapeDtypeStruct((b_, t, 512), BF16),
        scratch_shapes=[pltpu.VMEM((H_B, tq, 1), F32), pltpu.VMEM((H_B, tq, 1), F32),
                        pltpu.VMEM((H_B, tq, LANE), F32)],
        compiler_params=_params("parallel", "arbitrary"),
        name="b_attn_prompt")(proj3, kcb, vcb, kvb, proj3, proj3, g4)


def _b_pool_pages(page_table, cache2, wpb2, page_base, *, pg):
    db, n_pages = page_table.shape
    ng = n_pages // pg
    per = PAGE // CMP_BLOCK

    def body(pt_ref, *refs):
        pages = refs[:pg]
        wp_ref, o_ref = refs[pg:]
        rows = []
        for j in range(pg):
            for u in range(per):
                x = pages[j][0, u * CMP_BLOCK:(u + 1) * CMP_BLOCK, :]
                rows.append(jnp.sum(x * wp_ref[...], axis=0, keepdims=True))
        o_ref[0] = jnp.concatenate(rows, axis=0)

    def page_spec(j):
        return pl.BlockSpec((1, PAGE, 256), lambda b, g, pt, j=j: (page_base + pt[b, g * pg + j], 0, 0))

    grid_spec = pltpu.PrefetchScalarGridSpec(
        num_scalar_prefetch=1, grid=(db, ng),
        in_specs=[page_spec(j) for j in range(pg)] + [
            pl.BlockSpec((CMP_BLOCK, 256), lambda b, g, pt: (0, 0))],
        out_specs=pl.BlockSpec((1, pg * per, 256), lambda b, g, pt: (b, g, 0)))
    return pl.pallas_call(
        body, grid_spec=grid_spec,
        out_shape=jax.ShapeDtypeStruct((db, n_pages * per, 256), F32),
        compiler_params=_params("parallel", "parallel"),
        name="b_pool_pages")(page_table, *([cache2] * pg), wpb2)


def _b_select_sample(proj3, pooled, win, neww, g4, *, past_len, n_valid):
    db, tp, _ = neww.shape
    nblk = pooled.shape[1]
    n_top = min(N_SEL, nblk + 1)
    wb = win.shape[1]
    rows = H_B * tp
    scale = HEAD ** -0.5

    def body(q_ref, pool_ref, win_ref, nw_ref, bg_ref, g_ref, qs_ref, ocw_ref, g1_ref, sel_ref):
        qf = jnp.concatenate(
            [_rms(q_ref[0, :, h * LANE:(h + 1) * LANE], g_ref[0:1, :]) * scale for h in range(H_B)],
            axis=0)
        qb = qf.astype(BF16)
        qs_ref[0] = qb
        tok = _mod(lax.broadcasted_iota(jnp.int32, (rows, 1), 0), tp)
        pos = past_len + tok
        cur = _div(pos, CMP_BLOCK)
        blk = lax.broadcasted_iota(jnp.int32, (rows, nblk), 1)
        cmask = blk < cur

        kc = _rms(pool_ref[0, :, 0:128], g_ref[1:2, :])
        s = jnp.where(cmask, _cmp_scores(qf, kc), NEG)
        mx = jnp.max(s, axis=-1, keepdims=True)
        p = jnp.where(cmask, jnp.exp(s - mx), 0.0)
        pc = p / jnp.maximum(jnp.sum(p, axis=-1, keepdims=True), 1e-30)
        o_cmp = _mm(pc.astype(BF16), pool_ref[0, :, 128:256].astype(BF16))
        imp = pc[0:tp]
        for h in range(1, H_B):
            imp = imp + pc[h * tp:(h + 1) * tp]
        imp = jnp.where(cmask[0:tp], imp, -1.0)

        pad = jnp.concatenate([imp, jnp.zeros((LANE - tp, nblk), F32)], axis=0)
        imp_t = jnp.concatenate([pad[:, u * LANE:(u + 1) * LANE].T for u in range(nblk // LANE)], axis=0)
        mi = lax.broadcasted_iota(jnp.int32, (nblk, nblk), 0)
        ni = lax.broadcasted_iota(jnp.int32, (nblk, nblk), 1)
        sels = []
        for tkn in range(tp):
            r = imp[tkn:tkn + 1, :]
            c = imp_t[:, tkn:tkn + 1]
            ahead = (c > r) | ((c == r) & (mi < ni))
            rank = jnp.sum(ahead.astype(F32), axis=0, keepdims=True)
            sels.append(((rank < n_top) & (r >= 0.0)).astype(F32))
        sel_ref[0] = jnp.concatenate(sels, axis=0)

        kw = win_ref[0, :, 0:128].astype(BF16)
        vw = win_ref[0, :, 128:256].astype(BF16)
        zpad = jnp.zeros((LANE - tp, LANE), F32)
        kn = jnp.concatenate([nw_ref[0, :, 0:128], zpad], axis=0).astype(BF16)
        vn = jnp.concatenate([nw_ref[0, :, 128:256], zpad], axis=0).astype(BF16)
        jw = lax.broadcasted_iota(jnp.int32, (rows, wb), 1)
        dw = pos - (past_len - wb + jw)
        jn = lax.broadcasted_iota(jnp.int32, (rows, LANE), 1)
        dn = tok - jn
        wmask = jnp.concatenate([(dw >= 0) & (dw < WINDOW), (dn >= 0) & (dn < WINDOW) & (jn < n_valid)],
                                axis=1)
        sw = jnp.where(wmask, jnp.concatenate([_nt(qb, kw), _nt(qb, kn)], axis=1), NEG)
        mw = jnp.max(sw, axis=-1, keepdims=True)
        pw = jnp.where(wmask, jnp.exp(sw - mw), 0.0)
        o_win = (_mm(pw[:, 0:wb].astype(BF16), vw) + _mm(pw[:, wb:].astype(BF16), vn)) \
            / jnp.maximum(jnp.sum(pw, axis=-1, keepdims=True), 1e-30)

        gate = _sigmoid(bg_ref[0])
        g0 = jnp.concatenate([gate[:, 3 * h:3 * h + 1] for h in range(H_B)], axis=0)
        g1 = jnp.concatenate([gate[:, 3 * h + 1:3 * h + 2] for h in range(H_B)], axis=0)
        g2 = jnp.concatenate([gate[:, 3 * h + 2:3 * h + 3] for h in range(H_B)], axis=0)
        ocw_ref[0] = g0 * o_cmp + g2 * o_win
        g1_ref[0] = jnp.broadcast_to(g1, (rows, LANE))

    per_b = lambda r, w: pl.BlockSpec((1, r, w), lambda b: (b, 0, 0))
    return pl.pallas_call(
        body, grid=(db,),
        in_specs=[pl.BlockSpec((1, tp, 512), lambda b: (b, 0, _UNIT["bq"] // 4)),
                  per_b(nblk, 256), per_b(wb, 256), per_b(tp, 256),
                  pl.BlockSpec((1, tp, LANE), lambda b: (b, 0, _UNIT["bg"])),
                  pl.BlockSpec((4, LANE), lambda b: (0, 0))],
        out_specs=[per_b(rows, LANE), per_b(rows, LANE), per_b(rows, LANE), per_b(tp, nblk)],
        out_shape=[jax.ShapeDtypeStruct((db, rows, LANE), BF16),
                   jax.ShapeDtypeStruct((db, rows, LANE), F32),
                   jax.ShapeDtypeStruct((db, rows, LANE), F32),
                   jax.ShapeDtypeStruct((db, tp, nblk), F32)],
        compiler_params=_params("parallel"),
        name="b_select_sample")(proj3, pooled, win, neww, proj3, g4)


def _b_selected_sample(page_table, cache2, qs, selg, bkvs, ocw, g1b, proj3, page_base,
                       *, n_valid, pg):
    db, n_pages = page_table.shape
    ng = n_pages // pg
    rows = qs.shape[1]
    tp = rows // H_B
    per = PAGE // CMP_BLOCK
    nbg = pg * per
    width = pg * PAGE

    def body(pt_ref, *refs):
        pages = refs[:pg]
        q_ref, sel_ref, new_ref, ocw_ref, g1_ref, z_ref, o_ref, m_scr, l_scr, acc_scr = refs[pg:]
        gi = pl.program_id(1)

        @pl.when(gi == 0)
        def _():
            m_scr[...] = jnp.full(m_scr.shape, NEG, F32)
            l_scr[...] = jnp.zeros(l_scr.shape, F32)
            acc_scr[...] = jnp.zeros(acc_scr.shape, F32)

        q = q_ref[0]

        def update(s, mask, vs):
            m_old = m_scr[...]
            m_new = jnp.maximum(m_old, jnp.max(s, axis=-1, keepdims=True))
            alpha = jnp.exp(m_old - m_new)
            p = jnp.where(mask, jnp.exp(s - m_new), 0.0)
            pv = None
            for j, v in enumerate(vs):
                d = _mm(p[:, j * PAGE:(j + 1) * PAGE].astype(BF16), v)
                pv = d if pv is None else pv + d
            l_scr[...] = alpha * l_scr[...] + jnp.sum(p, axis=-1, keepdims=True)
            acc_scr[...] = alpha * acc_scr[...] + pv
            m_scr[...] = m_new

        e = (_div(lax.broadcasted_iota(jnp.int32, (nbg, width), 1), CMP_BLOCK)
             == lax.broadcasted_iota(jnp.int32, (nbg, width), 0)).astype(BF16)
        mk = _mm(sel_ref[0, 0].astype(BF16), e) > 0.5
        mask = jnp.concatenate([mk] * H_B, axis=0)
        ss, vs = [], []
        for j in range(pg):
            ss.append(_nt(q, pages[j][0, :, 0:128].astype(BF16)))
            vs.append(pages[j][0, :, 128:256].astype(BF16))
        update(jnp.where(mask, jnp.concatenate(ss, axis=1), NEG), mask, vs)

        @pl.when(gi == ng - 1)
        def _():
            zpad = jnp.zeros((PAGE - tp, LANE), F32)
            kn = jnp.concatenate([new_ref[0, :, 256:384], zpad], axis=0).astype(BF16)
            vn = jnp.concatenate([new_ref[0, :, 384:512], zpad], axis=0).astype(BF16)
            tok = _mod(lax.broadcasted_iota(jnp.int32, (rows, PAGE), 0), tp)
            col = lax.broadcasted_iota(jnp.int32, (rows, PAGE), 1)
            nmask = (col <= tok) & (col < n_valid)
            update(jnp.where(nmask, _nt(q, kn), NEG), nmask, [vn])
            o_sel = acc_scr[...] / jnp.maximum(l_scr[...], 1e-30)
            ob = ocw_ref[0] + g1_ref[0] * o_sel
            for h in range(H_B):
                y = ob[h * tp:(h + 1) * tp] * _silu(z_ref[0, :, h * LANE:(h + 1) * LANE])
                o_ref[0, :, h * LANE:(h + 1) * LANE] = y.astype(BF16)

    def page_spec(j):
        return pl.BlockSpec((1, PAGE, 256), lambda b, g, pt, j=j: (page_base + pt[b, g * pg + j], 0, 1))

    per_b = lambda r, w: pl.BlockSpec((1, r, w), lambda b, g, pt: (b, 0, 0))
    grid_spec = pltpu.PrefetchScalarGridSpec(
        num_scalar_prefetch=1, grid=(db, ng),
        in_specs=[page_spec(j) for j in range(pg)] + [
            per_b(rows, LANE),
            pl.BlockSpec((1, 1, tp, nbg), lambda b, g, pt: (b, g, 0, 0)),
            per_b(tp, 512), per_b(rows, LANE), per_b(rows, LANE),
            pl.BlockSpec((1, tp, 512), lambda b, g, pt: (b, 0, _UNIT["bz"] // 4))],
        out_specs=per_b(tp, 512),
        scratch_shapes=[pltpu.VMEM((rows, 1), F32), pltpu.VMEM((rows, 1), F32),
                        pltpu.VMEM((rows, LANE), F32)])
    return pl.pallas_call(
        body, grid_spec=grid_spec,
        out_shape=jax.ShapeDtypeStruct((db, tp, 512), BF16),
        compiler_params=_params("parallel", "arbitrary"),
        name="b_selected_sample")(page_table, *([cache2] * pg), qs, selg, bkvs, ocw, g1b, proj3)


def _mlstm(proj3, conv_w, conv_b, gate_b, out_g, conv0, c0, n0, m0, *, lc, n_valid):
    b_, t, _ = proj3.shape
    nchunk = t // lc
    dqk = H_C * DQK_C
    tail = CONV_W - 1
    base = 8
    tsq = max(lc, LANE)

    def body(qk_ref, v_ref, if_ref, co_ref, cz_ref, cw_ref, cb_ref, gb_ref, g_ref,
             conv0_ref, c0_ref, n0_ref, m0_ref,
             y_ref, conv_ref, c_ref, n_ref, m_ref, xbuf, c_scr, n_scr, m_scr):
        ci = pl.program_id(1)

        @pl.when(ci == 0)
        def _():
            xbuf[base - tail:base, :] = conv0_ref[0]
            c_scr[...] = c0_ref[0]
            n_scr[...] = n0_ref[0]
            m_scr[...] = m0_ref[0]

        xbuf[base:base + lc, :] = qk_ref[0]
        acc = cb_ref[...]
        for j in range(CONV_W):
            acc = acc + xbuf[base - tail + j:base - tail + j + lc, :] * cw_ref[j:j + 1, :]
        new_tail = xbuf[base + n_valid - tail:base + n_valid, :]
        xbuf[base - tail:base, :] = new_tail
        conv_ref[0] = new_tail
        qk = _silu(acc)

        gt = if_ref[0] + gb_ref[...]
        lf = jnp.minimum(gt, 0.0) - jnp.log(1.0 + jnp.exp(-jnp.abs(gt)))
        ti = lax.broadcasted_iota(jnp.int32, (lc, lc), 0)
        si = lax.broadcasted_iota(jnp.int32, (lc, lc), 1)
        tri = (si <= ti).astype(BF16)
        l1, l2, l3 = _split3(lf)
        bcum = _mm(tri, l1) + _mm(tri, l2) + _mm(tri, l3)
        zrow = jnp.zeros((tsq - lc, LANE), F32)
        gt_t = (jnp.concatenate([gt, zrow], axis=0) if tsq > lc else gt).T
        b_t = (jnp.concatenate([bcum, zrow], axis=0) if tsq > lc else bcum).T
        dmask = (si <= ti) & (si < n_valid)
        svalid = lax.broadcasted_iota(jnp.int32, (lc, 1), 0) < n_valid

        for h in range(H_C):
            qh = qk[:, h * DQK_C:(h + 1) * DQK_C] * (DQK_C ** -0.5)
            kh = qk[:, dqk + h * DQK_C:dqk + (h + 1) * DQK_C]
            vh = v_ref[0, :, h * HEAD:(h + 1) * HEAD]
            qhb, khb = qh.astype(BF16), kh.astype(BF16)
            bcol = bcum[:, H_C + h:H_C + h + 1]
            icol = gt[:, h:h + 1]
            brow = b_t[H_C + h:H_C + h + 1, 0:lc]
            irow = gt_t[h:h + 1, 0:lc]
            m_h = m_scr[h:h + 1, 0:1]
            ch = c_scr[h]
            nh = n_scr[h:h + 1, :]

            d = jnp.where(dmask, bcol - brow + irow, NEG)
            inter = bcol + m_h
            m_t = jnp.maximum(inter, jnp.max(d, axis=1, keepdims=True))
            w_intra = jnp.where(dmask, jnp.exp(d - m_t), 0.0)
            w_inter = jnp.exp(inter - m_t)
            sqk = _nt(qhb, khb) * w_intra
            num = w_inter * _nt(qhb, ch.astype(BF16)) + _mm(sqk.astype(BF16), vh.astype(BF16))
            den = w_inter * jnp.sum(qh * nh, axis=1, keepdims=True) + jnp.sum(sqk, axis=1, keepdims=True)
            hh = num / jnp.maximum(jnp.abs(den), jnp.exp(-m_t))

            b_last = bcum[n_valid - 1:n_valid, H_C + h:H_C + h + 1]
            dec = jnp.where(svalid, b_last - bcol + icol, NEG)
            m_new = jnp.maximum(b_last + m_h, jnp.max(dec, axis=0, keepdims=True))
            wk = jnp.where(svalid, jnp.exp(dec - m_new), 0.0)
            carry = jnp.exp(b_last + m_h - m_new)
            c_new = carry * ch + _tn((wk * vh).astype(BF16), khb)
            n_new = carry * nh + jnp.sum(wk * kh, axis=0, keepdims=True)
            c_scr[h] = c_new
            n_scr[h:h + 1, :] = n_new
            m_scr[h:h + 1, :] = jnp.broadcast_to(m_new, (1, LANE))

            gate = _sigmoid(co_ref[0, :, h * HEAD:(h + 1) * HEAD]) * _silu(cz_ref[0, :, h * HEAD:(h + 1) * HEAD])
            y_ref[0, :, h * HEAD:(h + 1) * HEAD] = (_rms(hh, g_ref[...]) * gate).astype(BF16)

        c_ref[0] = c_scr[...]
        n_ref[0] = n_scr[...]
        m_ref[0] = m_scr[...]

    col = lambda u: pl.BlockSpec((1, lc, 1024), lambda b, c, u=u: (b, c, u))
    const = lambda *shape: pl.BlockSpec(shape, lambda b, c: (0,) * len(shape))
    per_b = lambda *shape: pl.BlockSpec((1,) + shape, lambda b, c: (b,) + (0,) * len(shape))
    return pl.pallas_call(
        body, grid=(b_, nchunk),
        in_specs=[col(_UNIT["cqk"] // 8), col(_UNIT["cv"] // 8),
                  pl.BlockSpec((1, lc, LANE), lambda b, c: (b, c, _UNIT["cif"])),
                  col(_UNIT["co"] // 8), col(_UNIT["cz"] // 8),
                  const(CONV_W, 1024), const(1, 1024), const(1, LANE), const(1, LANE),
                  per_b(tail, 1024), per_b(H_C, HEAD, DQK_C), per_b(H_C, DQK_C), per_b(H_C, LANE)],
        out_specs=[pl.BlockSpec((1, lc, 1024), lambda b, c: (b, c, 0)),
                   per_b(tail, 1024), per_b(H_C, HEAD, DQK_C), per_b(H_C, DQK_C), per_b(H_C, LANE)],
        out_shape=[jax.ShapeDtypeStruct((b_, t, 1024), BF16),
                   jax.ShapeDtypeStruct((b_, tail, 1024), F32),
                   jax.ShapeDtypeStruct((b_, H_C, HEAD, DQK_C), F32),
                   jax.ShapeDtypeStruct((b_, H_C, DQK_C), F32),
                   jax.ShapeDtypeStruct((b_, H_C, LANE), F32)],
        scratch_shapes=[pltpu.VMEM((base + lc, 1024), F32), pltpu.VMEM((H_C, HEAD, DQK_C), F32),
                        pltpu.VMEM((H_C, DQK_C), F32), pltpu.VMEM((H_C, LANE), F32)],
        compiler_params=_params("parallel", "arbitrary"),
        name="mlstm")(proj3, proj3, proj3, proj3, proj3, conv_w, conv_b, gate_b, out_g,
                      conv0, c0, n0, m0)


def _relayout_w_in(w):
    segs, off = {}, 0
    for name, width in _SRC:
        segs[name] = w[:, off:off + width]
        off += width
    cols = []
    for name in _DST_ORDER:
        s = segs[name]
        padw = -s.shape[1] % LANE
        cols.append(jnp.pad(s, ((0, 0), (0, padw))) if padw else s)
    return jnp.concatenate(cols, axis=1).astype(BF16)


def _pick(n, prefs):
    for p in prefs:
        if n % p == 0:
            return p
    return n


def kernel(x_prompt, x_sample, cache_a_kv, cache_b_kv, state_b_win, state_c_conv, state_c_C,
           state_c_n, state_c_m, page_table, norm_g, w_in, w_out, a_qk_g, a_lambda, a_out_g,
           b_qk_g, b_cmp_w, c_conv_w, c_conv_b, c_gate_b, c_out_g):
    bp, t, d = x_prompt.shape
    db, ts, _ = x_sample.shape
    depth = norm_g.shape[0]
    n_pool = cache_a_kv.shape[1]
    n_pages = page_table.shape[1]
    past_len = n_pages * PAGE
    wb = state_b_win.shape[2]
    tp = 8
    assert ts <= tp and t % MLSTM_CHUNK == 0 and t % Q_BLOCK == 0 and d == 2048
    mp, ms = bp * t, db * tp
    pg = _pick(n_pages, (8, 4, 2, 1))

    cache_a2 = cache_a_kv.reshape(depth * n_pool, PAGE, 2 * H_A * HEAD)
    cache_b2 = cache_b_kv.reshape(depth * n_pool, PAGE, 4 * HEAD)
    hp = x_prompt.reshape(mp, d)
    hs = jnp.pad(x_sample, ((0, 0), (0, tp - ts), (0, 0))).reshape(ms, d)

    tm_p = _pick(mp, (1024, 512, 256, 128))
    tq_a = _pick(t, (512, 256, 128))
    outs_p = [[] for _ in range(7)]
    outs_s = [[] for _ in range(7)]

    for l in range(depth):
        lam_init = 0.8 - 0.6 * math.exp(-0.3 * l)
        w_l = _relayout_w_in(w_in[l])
        w_o = w_out[l].astype(BF16)
        g_a = jnp.tile(a_qk_g[l], (1, 2))
        g_b = b_qk_g[l]
        wpb = jnp.broadcast_to(b_cmp_w[l][:, :, None], (2, CMP_BLOCK, LANE))
        wpb2 = jnp.concatenate([wpb[0], wpb[1]], axis=1)
        gate_b = jnp.pad(c_gate_b[l].reshape(1, 2 * H_C), ((0, 0), (0, LANE - 2 * H_C)))
        conv_b = c_conv_b[l].reshape(1, -1)
        out_gc = c_out_g[l].reshape(1, LANE)

        proj = _norm_matmul(hp, norm_g[l], w_l, tm=tm_p, tn=1024)
        proj3 = proj.reshape(bp, t, N_PROJ)
        qn, kn, vb, akv = _a_prep(proj, g_a, tm=_pick(mp, (512, 256, 128)))
        ya = _a_attn_prompt(qn.reshape(bp, t, 512), kn.reshape(bp, t, 512), vb.reshape(bp, t, 512),
                            proj3, a_lambda[l], a_out_g[l], lam_init, tq=tq_a)
        bkvs, neww, kvb, kcb, vcb = _b_prep(proj, g_b, wpb, tm=_pick(mp, (512, 256, 128)), pool=True)
        nblk = t // CMP_BLOCK
        yb = _b_attn_prompt(proj3, kcb.reshape(bp, nblk, LANE), vcb.reshape(bp, nblk, LANE),
                            kvb.reshape(bp, t, 512), g_b, t=t)
        yc, conv_p, c_p, n_p, m_p = _mlstm(
            proj3, c_conv_w[l], conv_b, gate_b, out_gc,
            jnp.zeros((bp, CONV_W - 1, 2 * H_C * DQK_C), F32), jnp.zeros((bp, H_C, HEAD, DQK_C), F32),
            jnp.zeros((bp, H_C, DQK_C), F32), jnp.zeros((bp, H_C, LANE), F32),
            lc=MLSTM_CHUNK, n_valid=MLSTM_CHUNK)
        hp = _out_proj(hp, ya.reshape(mp, 512), yb.reshape(mp, 512), yc.reshape(mp, 1024), w_o,
                       tm=tm_p, tn=1024)
        win_p = jnp.concatenate([jnp.zeros((bp, wb, 256), F32), neww.reshape(bp, t, 256)], axis=1)[:, -wb:]
        for lst, a in zip(outs_p, (akv.reshape(bp, t, 2, H_A, HEAD), bkvs.reshape(bp, t, 4, HEAD),
                                   win_p.reshape(bp, wb, 2, HEAD), conv_p, c_p, n_p, m_p[:, :, 0])):
            lst.append(a)

        base = l * n_pool
        sproj = _norm_matmul(hs, norm_g[l], w_l, tm=ms, tn=1024)
        sproj3 = sproj.reshape(db, tp, N_PROJ)
        sqn, skn, svb, sakv = _a_prep(sproj, g_a, tm=ms)
        kvnew = jnp.concatenate([skn.reshape(db, tp, 512), svb.reshape(db, tp, 512)], axis=2)
        kvnew = jnp.pad(kvnew, ((0, 0), (0, PAGE - tp), (0, 0)))
        sya = _a_attn_sample(page_table, cache_a2, sqn.reshape(db, tp, 512), kvnew, sproj3,
                             a_lambda[l], a_out_g[l], lam_init, base, n_valid=ts, pg=pg)
        sbkvs, sneww, _ = _b_prep(sproj, g_b, wpb, tm=ms, pool=False)
        pooled = _b_pool_pages(page_table, cache_b2, wpb2, base, pg=pg)
        qs, ocw, g1b, sel = _b_select_sample(
            sproj3, pooled, state_b_win[l].reshape(db, wb, 256), sneww.reshape(db, tp, 256), g_b,
            past_len=past_len, n_valid=ts)
        nbg = pg * (PAGE // CMP_BLOCK)
        selg = sel.reshape(db, tp, n_pages // pg, nbg).transpose(0, 2, 1, 3)
        syb = _b_selected_sample(page_table, cache_b2, qs, selg, sbkvs.reshape(db, tp, 512), ocw, g1b,
                                 sproj3, base, n_valid=ts, pg=pg)
        m0 = jnp.broadcast_to(state_c_m[l][:, :, None], (db, H_C, LANE))
        syc, conv_s, c_s, n_s, m_s = _mlstm(
            sproj3, c_conv_w[l], conv_b, gate_b, out_gc, state_c_conv[l], state_c_C[l], state_c_n[l], m0,
            lc=tp, n_valid=ts)
        hs = _out_proj(hs, sya.reshape(ms, 512), syb.reshape(ms, 512), syc.reshape(ms, 1024), w_o,
                       tm=ms, tn=1024)
        win_s = jnp.concatenate([state_b_win[l].reshape(db, wb, 256),
                                 sneww.reshape(db, tp, 256)[:, :ts]], axis=1)[:, -wb:]
        for lst, a in zip(outs_s, (sakv.reshape(db, tp, 2, H_A, HEAD)[:, :ts],
                                   sbkvs.reshape(db, tp, 4, HEAD)[:, :ts],
                                   win_s.reshape(db, wb, 2, HEAD), conv_s, c_s, n_s, m_s[:, :, 0])):
            lst.append(a)

    y_p = hp.reshape(bp, t, d)
    y_s = hs.reshape(db, tp, d)[:, :ts]
    sp = [jnp.stack(x) for x in outs_p]
    ss = [jnp.stack(x) for x in outs_s]
    return (y_p, y_s, sp[0], ss[0], sp[1], ss[1], sp[2], ss[2], sp[3], ss[3],
            sp[4], ss[4], sp[5], ss[5], sp[6], ss[6])
```

```python
import functools
import math

import jax
import jax.numpy as jnp
from jax import lax
from jax.experimental import pallas as pl
from jax.experimental.pallas import tpu as pltpu

F32 = jnp.float32
BF16 = jnp.bfloat16

EPS = 1e-6
LANE = 128
HEAD = 128
H_A = 4
DQ_A = HEAD // 2
H_B = 4
H_C = 8
DQK_C = HEAD // 2
CMP_BLOCK = 64
N_SEL = 16
WINDOW = 512
CONV_W = 4
MLSTM_CHUNK = 64
PAGE = 128
Q_BLOCK = 128
NEG = -1e30
VMEM_LIMIT = 56 * 1024 * 1024

_SRC = (("aq", 512), ("ak", 512), ("av", 512), ("az", 512),
        ("bq", 512), ("bkv", 768), ("bg", 12), ("bz", 512),
        ("cqk", 1024), ("cv", 1024), ("cif", 16), ("co", 1024), ("cz", 1024))
_DST_ORDER = ("aq", "ak", "av", "az", "bq", "bz", "bkv", "bg", "cif", "cqk", "cv", "co", "cz")
_UNIT = {"aq": 0, "ak": 4, "av": 8, "az": 12, "bq": 16, "bz": 20, "bkv": 24, "bg": 30,
         "cif": 31, "cqk": 32, "cv": 40, "co": 48, "cz": 56}
N_PROJ = 64 * LANE


def _nt(a, b):
    return lax.dot_general(a, b, (((1,), (1,)), ((), ())), preferred_element_type=F32)


def _tn(a, b):
    return lax.dot_general(a, b, (((0,), (0,)), ((), ())), preferred_element_type=F32)


def _mm(a, b):
    return jnp.dot(a, b, preferred_element_type=F32)


def _split2(x):
    hi = x.astype(BF16)
    lo = (x - hi.astype(F32)).astype(BF16)
    return hi, lo


def _split3(x):
    hi = x.astype(BF16)
    r = x - hi.astype(F32)
    mid = r.astype(BF16)
    lo = (r - mid.astype(F32)).astype(BF16)
    return hi, mid, lo


def _sigmoid(z):
    return 1.0 / (1.0 + jnp.exp(-z))


def _silu(z):
    return z * _sigmoid(z)


def _rms(x, g):
    return x * lax.rsqrt(jnp.mean(x * x, axis=-1, keepdims=True) + EPS) * g


def _div(x, n):
    return lax.shift_right_logical(x, int(math.log2(n)))


def _mod(x, n):
    return x & (n - 1)


def _params(*sem):
    return pltpu.CompilerParams(dimension_semantics=sem, vmem_limit_bytes=VMEM_LIMIT)


def _norm_matmul(x, g, w, layer, *, tm, tn):
    m, d = x.shape
    n = w.shape[2]
    rc = min(tm, 256)

    def body(x_ref, g_ref, w_ref, o_ref, h_scr):
        @pl.when(pl.program_id(1) == 0)
        def _():
            def chunk(c, carry):
                r = pl.ds(pl.multiple_of(c * rc, rc), rc)
                h_scr[r, :] = _rms(x_ref[r, :], g_ref[...]).astype(BF16)
                return carry
            lax.fori_loop(0, tm // rc, chunk, 0)
        o_ref[...] = _mm(h_scr[...], w_ref[0])

    return pl.pallas_call(
        body, grid=(m // tm, n // tn),
        in_specs=[pl.BlockSpec((tm, d), lambda i, j: (i, 0)),
                  pl.BlockSpec((1, d), lambda i, j: (0, 0)),
                  pl.BlockSpec((1, d, tn), lambda i, j: (layer, 0, j))],
        out_specs=pl.BlockSpec((tm, tn), lambda i, j: (i, j)),
        out_shape=jax.ShapeDtypeStruct((m, n), F32),
        scratch_shapes=[pltpu.VMEM((tm, d), BF16)],
        compiler_params=_params("parallel", "arbitrary"),
        name="norm_matmul")(x, g.reshape(1, d), w)


def _out_proj(x, ya, yb, yc, w, *, tm, tn):
    m, d = x.shape
    da, db, dc = ya.shape[1], yb.shape[1], yc.shape[1]

    def body(x_ref, a_ref, b_ref, c_ref, wa_ref, wb_ref, wc_ref, o_ref):
        acc = _mm(a_ref[...], wa_ref[...])
        acc += _mm(b_ref[...], wb_ref[...])
        acc += _mm(c_ref[...], wc_ref[...])
        o_ref[...] = x_ref[...] + acc

    return pl.pallas_call(
        body, grid=(m // tm, d // tn),
        in_specs=[pl.BlockSpec((tm, tn), lambda i, j: (i, j)),
                  pl.BlockSpec((tm, da), lambda i, j: (i, 0)),
                  pl.BlockSpec((tm, db), lambda i, j: (i, 0)),
                  pl.BlockSpec((tm, dc), lambda i, j: (i, 0)),
                  pl.BlockSpec((da, tn), lambda i, j: (0, j)),
                  pl.BlockSpec((db, tn), lambda i, j: (da // db, j)),
                  pl.BlockSpec((dc, tn), lambda i, j: ((da + db) // dc, j))],
        out_specs=pl.BlockSpec((tm, tn), lambda i, j: (i, j)),
        out_shape=jax.ShapeDtypeStruct((m, d), F32),
        compiler_params=_params("parallel", "parallel"),
        name="out_proj")(x, ya, yb, yc, w, w, w)


def _a_prep(proj, g2, *, tm):
    m = proj.shape[0]

    def body(q_ref, k_ref, v_ref, g_ref, qn_ref, kn_ref, vb_ref, akv_ref):
        r = _div(lax.broadcasted_iota(jnp.int32, (LANE, LANE), 0), DQ_A)
        c = _div(lax.broadcasted_iota(jnp.int32, (LANE, LANE), 1), DQ_A)
        seg = (r == c).astype(BF16)

        def segnorm(x, g):
            outs = []
            for u in range(4):
                xc = x[:, u * LANE:(u + 1) * LANE]
                hi, lo = _split2(xc * xc)
                s = _mm(hi, seg) + _mm(lo, seg)
                outs.append(xc * lax.rsqrt(s * (1.0 / DQ_A) + EPS) * g)
            return jnp.concatenate(outs, axis=1)

        qn = segnorm(q_ref[...], g_ref[0:1, :]) * (DQ_A ** -0.5)
        kn = segnorm(k_ref[...], g_ref[1:2, :])
        v = v_ref[...]
        qn_ref[...] = qn.astype(BF16)
        kn_ref[...] = kn.astype(BF16)
        vb_ref[...] = v.astype(BF16)
        for h in range(H_A):
            akv_ref[pl.ds(h, tm, stride=2 * H_A), :] = kn[:, h * LANE:(h + 1) * LANE]
            akv_ref[pl.ds(H_A + h, tm, stride=2 * H_A), :] = v[:, h * LANE:(h + 1) * LANE]

    blk = lambda u: pl.BlockSpec((tm, 512), lambda i, u=u: (i, u))
    row = lambda w: pl.BlockSpec((tm, w), lambda i: (i, 0))
    return pl.pallas_call(
        body, grid=(m // tm,),
        in_specs=[blk(_UNIT["aq"] // 4), blk(_UNIT["ak"] // 4), blk(_UNIT["av"] // 4),
                  pl.BlockSpec((2, LANE), lambda i: (0, 0))],
        out_specs=[row(512), row(512), row(512), pl.BlockSpec((tm * 2 * H_A, LANE), lambda i: (i, 0))],
        out_shape=[jax.ShapeDtypeStruct((m, 512), BF16)] * 3
        + [jax.ShapeDtypeStruct((m * 2 * H_A, LANE), F32)],
        compiler_params=_params("parallel"),
        name="a_prep")(proj, proj, proj, g2)


def _diff_lambda(lp, lam_init):
    a = jnp.sum(lp[0:1, :] * lp[1:2, :], axis=-1, keepdims=True)
    b = jnp.sum(lp[2:3, :] * lp[3:4, :], axis=-1, keepdims=True)
    return jnp.exp(a) - jnp.exp(b) + lam_init


def _a_finish(acc1, l1, acc2, l2, lam, g, z, lam_init):
    o = acc1 / l1 - lam * (acc2 / l2)
    return _rms(o, g) * (1.0 - lam_init) * _silu(z)


def _a_attn_prompt(qn, kn, vb, proj3, lam_p, out_g, lam_init, *, tq):
    b_, t, _ = qn.shape
    tk = tq

    def body(q_ref, k_ref, v_ref, z_ref, lam_ref, g_ref, o_ref, m_scr, l_scr, acc_scr):
        qi = pl.program_id(2)
        q = q_ref[0]
        lane = lax.broadcasted_iota(jnp.int32, (tq, LANE), 1)
        zero = jnp.zeros_like(q)
        qs = (jnp.where(lane < DQ_A, q, zero), jnp.where(lane >= DQ_A, q, zero))
        m_scr[...] = jnp.full(m_scr.shape, NEG, F32)
        l_scr[...] = jnp.zeros(l_scr.shape, F32)
        acc_scr[...] = jnp.zeros(acc_scr.shape, F32)
        qpos = qi * tq + lax.broadcasted_iota(jnp.int32, (tq, tk), 0)

        def chunk(kc, carry):
            ks = pl.ds(pl.multiple_of(kc * tk, tk), tk)
            k = k_ref[0, ks, :]
            v = v_ref[0, ks, :]
            kpos = kc * tk + lax.broadcasted_iota(jnp.int32, (tq, tk), 1)
            mask = kpos <= qpos
            for c in range(2):
                s = jnp.where(mask, _nt(qs[c], k), NEG)
                m_old = m_scr[c]
                m_new = jnp.maximum(m_old, jnp.max(s, axis=-1, keepdims=True))
                alpha = jnp.exp(m_old - m_new)
                p = jnp.exp(s - m_new)
                l_scr[c] = alpha * l_scr[c] + jnp.sum(p, axis=-1, keepdims=True)
                acc_scr[c] = alpha * acc_scr[c] + _mm(p.astype(BF16), v)
                m_scr[c] = m_new
            return carry

        lax.fori_loop(0, qi + 1, chunk, 0)
        lam = _diff_lambda(lam_ref[...], lam_init)
        y = _a_finish(acc_scr[0], l_scr[0], acc_scr[1], l_scr[1], lam, g_ref[...], z_ref[0], lam_init)
        o_ref[0] = y.astype(BF16)

    head = lambda rows, qdep: pl.BlockSpec(
        (1, rows, LANE), (lambda b, h, i: (b, i, h)) if qdep else (lambda b, h, i: (b, 0, h)))
    return pl.pallas_call(
        body, grid=(b_, H_A, t // tq),
        in_specs=[head(tq, True), head(t, False), head(t, False),
                  pl.BlockSpec((1, tq, LANE), lambda b, h, i: (b, i, _UNIT["az"] + h)),
                  pl.BlockSpec((4, DQ_A), lambda b, h, i: (0, 0)),
                  pl.BlockSpec((1, LANE), lambda b, h, i: (0, 0))],
        out_specs=head(tq, True),
        out_shape=jax.ShapeDtypeStruct((b_, t, 512), BF16),
        scratch_shapes=[pltpu.VMEM((2, tq, 1), F32), pltpu.VMEM((2, tq, 1), F32),
                        pltpu.VMEM((2, tq, LANE), F32)],
        compiler_params=_params("parallel", "parallel", "arbitrary"),
        name="a_attn_prompt")(qn, kn, vb, proj3, lam_p, out_g.reshape(1, LANE))


def _a_attn_sample(page_table, cache2, qn, kvnew, proj3, lam_p, out_g, lam_init, page_base,
                   *, n_valid, pg):
    db, n_pages = page_table.shape
    tp = qn.shape[1]
    ng = n_pages // pg

    def body(pt_ref, *refs):
        pages = refs[:pg]
        q_ref, new_ref, z_ref, lam_ref, g_ref, o_ref, m_scr, l_scr, acc_scr = refs[pg:]
        gi = pl.program_id(1)

        @pl.when(gi == 0)
        def _():
            m_scr[...] = jnp.full(m_scr.shape, NEG, F32)
            l_scr[...] = jnp.zeros(l_scr.shape, F32)
            acc_scr[...] = jnp.zeros(acc_scr.shape, F32)

        lane = lax.broadcasted_iota(jnp.int32, (tp, LANE), 1)

        def qstack(h):
            q = q_ref[0, :, h * LANE:(h + 1) * LANE]
            zero = jnp.zeros_like(q)
            return jnp.concatenate([jnp.where(lane < DQ_A, q, zero),
                                    jnp.where(lane >= DQ_A, q, zero)], axis=0)

        def update(h, s, vs, mask):
            m_old = m_scr[h]
            m_new = jnp.maximum(m_old, jnp.max(s, axis=-1, keepdims=True))
            alpha = jnp.exp(m_old - m_new)
            p = jnp.exp(s - m_new)
            if mask is not None:
                p = jnp.where(mask, p, 0.0)
            pv = None
            for j, v in enumerate(vs):
                d = _mm(p[:, j * PAGE:(j + 1) * PAGE].astype(BF16), v)
                pv = d if pv is None else pv + d
            l_scr[h] = alpha * l_scr[h] + jnp.sum(p, axis=-1, keepdims=True)
            acc_scr[h] = alpha * acc_scr[h] + pv
            m_scr[h] = m_new

        for h in range(H_A):
            qh = qstack(h)
            ss, vs = [], []
            for j in range(pg):
                k = pages[j][pl.ds(h, PAGE, stride=2 * H_A), :].astype(BF16)
                vs.append(pages[j][pl.ds(H_A + h, PAGE, stride=2 * H_A), :].astype(BF16))
                ss.append(_nt(qh, k))
            update(h, jnp.concatenate(ss, axis=1), vs, None)

        @pl.when(gi == ng - 1)
        def _():
            lam = _diff_lambda(lam_ref[...], lam_init)
            row = _mod(lax.broadcasted_iota(jnp.int32, (2 * tp, PAGE), 0), tp)
            col = lax.broadcasted_iota(jnp.int32, (2 * tp, PAGE), 1)
            mask = (col <= row) & (col < n_valid)
            for h in range(H_A):
                qh = qstack(h)
                k = new_ref[0, :, h * LANE:(h + 1) * LANE]
                v = new_ref[0, :, 512 + h * LANE:512 + (h + 1) * LANE]
                s = jnp.where(mask, _nt(qh, k), NEG)
                update(h, s, [v], mask)
                acc = acc_scr[h]
                l = l_scr[h]
                y = _a_finish(acc[0:tp], l[0:tp], acc[tp:], l[tp:], lam, g_ref[...],
                              z_ref[0, :, h * LANE:(h + 1) * LANE], lam_init)
                o_ref[0, :, h * LANE:(h + 1) * LANE] = y.astype(BF16)

    def page_spec(j):
        return pl.BlockSpec((PAGE * 2 * H_A, LANE),
                            lambda b, g, pt, j=j: (page_base + pt[b, g * pg + j], 0))

    grid_spec = pltpu.PrefetchScalarGridSpec(
        num_scalar_prefetch=1, grid=(db, ng),
        in_specs=[page_spec(j) for j in range(pg)] + [
            pl.BlockSpec((1, tp, 512), lambda b, g, pt: (b, 0, 0)),
            pl.BlockSpec((1, PAGE, 1024), lambda b, g, pt: (b, 0, 0)),
            pl.BlockSpec((1, tp, 512), lambda b, g, pt: (b, 0, _UNIT["az"] // 4)),
            pl.BlockSpec((4, DQ_A), lambda b, g, pt: (0, 0)),
            pl.BlockSpec((1, LANE), lambda b, g, pt: (0, 0))],
        out_specs=pl.BlockSpec((1, tp, 512), lambda b, g, pt: (b, 0, 0)),
        scratch_shapes=[pltpu.VMEM((H_A, 2 * tp, 1), F32), pltpu.VMEM((H_A, 2 * tp, 1), F32),
                        pltpu.VMEM((H_A, 2 * tp, LANE), F32)])
    return pl.pallas_call(
        body, grid_spec=grid_spec,
        out_shape=jax.ShapeDtypeStruct((db, tp, 512), BF16),
        compiler_params=_params("parallel", "arbitrary"),
        name="a_attn_sample")(page_table, *([cache2] * pg), qn, kvnew, proj3, lam_p,
                              out_g.reshape(1, LANE))


def _b_prep(proj, g4, wpb, *, tm, pool):
    m = proj.shape[0]
    nb = tm // CMP_BLOCK

    def body(kv_ref, g_ref, wp_ref, bkv_ref, nw_ref, kvb_ref, *pool_refs):
        kc = kv_ref[:, 0:128]
        vc = kv_ref[:, 128:256]
        ks = _rms(kv_ref[:, 256:384], g_ref[2:3, :])
        vs = kv_ref[:, 384:512]
        kw = _rms(kv_ref[:, 512:640], g_ref[3:4, :])
        vw = kv_ref[:, 640:768]
        for slot, x in enumerate((kc, vc, ks, vs)):
            bkv_ref[pl.ds(slot, tm, stride=4), :] = x
        nw_ref[pl.ds(0, tm, stride=2), :] = kw
        nw_ref[pl.ds(1, tm, stride=2), :] = vw
        kvb_ref[:, 0:128] = ks.astype(BF16)
        kvb_ref[:, 128:256] = vs.astype(BF16)
        kvb_ref[:, 256:384] = kw.astype(BF16)
        kvb_ref[:, 384:512] = vw.astype(BF16)
        if pool:
            kcb_ref, vcb_ref = pool_refs
            kp = jnp.sum(kc.reshape(nb, CMP_BLOCK, LANE) * wp_ref[0][None], axis=1)
            vp = jnp.sum(vc.reshape(nb, CMP_BLOCK, LANE) * wp_ref[1][None], axis=1)
            kcb_ref[...] = _rms(kp, g_ref[1:2, :])
            vcb_ref[...] = vp

    row = lambda w: pl.BlockSpec((tm, w), lambda i: (i, 0))
    out_specs = [pl.BlockSpec((tm * 4, LANE), lambda i: (i, 0)),
                 pl.BlockSpec((tm * 2, LANE), lambda i: (i, 0)), row(512)]
    out_shape = [jax.ShapeDtypeStruct((m * 4, LANE), F32), jax.ShapeDtypeStruct((m * 2, LANE), F32),
                 jax.ShapeDtypeStruct((m, 512), BF16)]
    if pool:
        out_specs += [pl.BlockSpec((nb, LANE), lambda i: (i, 0))] * 2
        out_shape += [jax.ShapeDtypeStruct((m // CMP_BLOCK, LANE), F32)] * 2
    return pl.pallas_call(
        body, grid=(m // tm,),
        in_specs=[pl.BlockSpec((tm, 768), lambda i: (i, _UNIT["bkv"] // 6)),
                  pl.BlockSpec((4, LANE), lambda i: (0, 0)),
                  pl.BlockSpec((2, CMP_BLOCK, LANE), lambda i: (0, 0, 0))],
        out_specs=out_specs, out_shape=out_shape,
        compiler_params=_params("parallel"),
        name="b_prep")(proj, g4, wpb)


def _cmp_scores(qf, kcb):
    qh, ql = _split2(qf)
    kh, kl = _split2(kcb)
    return _nt(qh, kh) + _nt(qh, kl) + _nt(ql, kh)


def _b_attn_prompt(proj3, kcb, vcb, kvb, g4, *, t):
    b_ = proj3.shape[0]
    tq = Q_BLOCK
    nblk = t // CMP_BLOCK
    n_top = min(N_SEL, nblk)
    tk = min(512, t)
    span = min(WINDOW + tq, t)
    scale = HEAD ** -0.5

    def body(q_ref, kcb_ref, vcb_ref, kv_ref, bg_ref, bz_ref, g_ref, o_ref, m_scr, l_scr, acc_scr):
        qi = pl.program_id(1)
        qf = [_rms(q_ref[0, :, h * LANE:(h + 1) * LANE], g_ref[0:1, :]) * scale for h in range(H_B)]
        qb = [x.astype(BF16) for x in qf]
        pos = qi * tq + lax.broadcasted_iota(jnp.int32, (tq, 1), 0)
        cur = _div(pos, CMP_BLOCK)
        blk = lax.broadcasted_iota(jnp.int32, (tq, nblk), 1)
        cmask = blk < cur

        kc = kcb_ref[0]
        vch = vcb_ref[0].astype(BF16)
        o_cmp, imp = [], jnp.zeros((tq, nblk), F32)
        for h in range(H_B):
            s = jnp.where(cmask, _cmp_scores(qf[h], kc), NEG)
            mx = jnp.max(s, axis=-1, keepdims=True)
            p = jnp.where(cmask, jnp.exp(s - mx), 0.0)
            pc = p / jnp.maximum(jnp.sum(p, axis=-1, keepdims=True), 1e-30)
            o_cmp.append(_mm(pc.astype(BF16), vch))
            imp = imp + pc
        imp = jnp.where(cmask, imp, -1.0)

        rank = jnp.zeros((tq, nblk), F32)
        for mcol in range(nblk):
            cm = imp[:, mcol:mcol + 1]
            ahead = (cm > imp) | ((cm == imp) & (mcol < blk))
            rank = rank + ahead.astype(F32)
        selc = (((rank < n_top) & (imp >= 0.0)) | (blk == cur)).astype(BF16)

        m_scr[...] = jnp.full(m_scr.shape, NEG, F32)
        l_scr[...] = jnp.zeros(l_scr.shape, F32)
        acc_scr[...] = jnp.zeros(acc_scr.shape, F32)
        nk = _div(qi * tq + tq + tk - 1, tk)

        def chunk(c, carry):
            ks_ = pl.ds(pl.multiple_of(c * tk, tk), tk)
            k = kv_ref[0, ks_, 0:128]
            v = kv_ref[0, ks_, 128:256]
            kidx = c * tk + lax.broadcasted_iota(jnp.int32, (nblk, tk), 1)
            e = (_div(kidx, CMP_BLOCK) == lax.broadcasted_iota(jnp.int32, (nblk, tk), 0)).astype(BF16)
            kpos = c * tk + lax.broadcasted_iota(jnp.int32, (tq, tk), 1)
            mask = (_mm(selc, e) > 0.5) & (kpos <= pos)
            for h in range(H_B):
                s = jnp.where(mask, _nt(qb[h], k), NEG)
                m_old = m_scr[h]
                m_new = jnp.maximum(m_old, jnp.max(s, axis=-1, keepdims=True))
                alpha = jnp.exp(m_old - m_new)
                p = jnp.where(mask, jnp.exp(s - m_new), 0.0)
                l_scr[h] = alpha * l_scr[h] + jnp.sum(p, axis=-1, keepdims=True)
                acc_scr[h] = alpha * acc_scr[h] + _mm(p.astype(BF16), v)
                m_scr[h] = m_new
            return carry

        lax.fori_loop(0, nk, chunk, 0)

        start = jnp.clip(qi * tq + tq - span, 0, t - span)
        ws = pl.ds(pl.multiple_of(start, tq), span)
        kw = kv_ref[0, ws, 256:384]
        vw = kv_ref[0, ws, 384:512]
        diff = pos - (start + lax.broadcasted_iota(jnp.int32, (tq, span), 1))
        wmask = (diff >= 0) & (diff < WINDOW)

        gate = _sigmoid(bg_ref[0])
        for h in range(H_B):
            s = jnp.where(wmask, _nt(qb[h], kw), NEG)
            mx = jnp.max(s, axis=-1, keepdims=True)
            p = jnp.where(wmask, jnp.exp(s - mx), 0.0)
            o_win = _mm(p.astype(BF16), vw) / jnp.maximum(jnp.sum(p, axis=-1, keepdims=True), 1e-30)
            o_sel = acc_scr[h] / jnp.maximum(l_scr[h], 1e-30)
            ob = (gate[:, 3 * h:3 * h + 1] * o_cmp[h] + gate[:, 3 * h + 1:3 * h + 2] * o_sel
                  + gate[:, 3 * h + 2:3 * h + 3] * o_win)
            y = ob * _silu(bz_ref[0, :, h * LANE:(h + 1) * LANE])
            o_ref[0, :, h * LANE:(h + 1) * LANE] = y.astype(BF16)

    full = lambda rows, w: pl.BlockSpec((1, rows, w), lambda b, i: (b, 0, 0))
    return pl.pallas_call(
        body, grid=(b_, t // tq),
        in_specs=[pl.BlockSpec((1, tq, 512), lambda b, i: (b, i, _UNIT["bq"] // 4)),
                  full(nblk, LANE), full(nblk, LANE), full(t, 512),
                  pl.BlockSpec((1, tq, LANE), lambda b, i: (b, i, _UNIT["bg"])),
                  pl.BlockSpec((1, tq, 512), lambda b, i: (b, i, _UNIT["bz"] // 4)),
                  pl.BlockSpec((4, LANE), lambda b, i: (0, 0))],
        out_specs=pl.BlockSpec((1, tq, 512), lambda b, i: (b, i, 0)),
        out_shape=jax.ShapeDtypeStruct((b_, t, 512), BF16),
        scratch_shapes=[pltpu.VMEM((H_B, tq, 1), F32), pltpu.VMEM((H_B, tq, 1), F32),
                        pltpu.VMEM((H_B, tq, LANE), F32)],
        compiler_params=_params("parallel", "arbitrary"),
        name="b_attn_prompt")(proj3, kcb, vcb, kvb, proj3, proj3, g4)


def _b_pool_pages(page_table, cache2, wpb2, page_base, *, pg):
    db, n_pages = page_table.shape
    ng = n_pages // pg
    per = PAGE // CMP_BLOCK

    def body(pt_ref, *refs):
        pages = refs[:pg]
        wp_ref, o_ref = refs[pg:]
        rows = []
        for j in range(pg):
            for u in range(per):
                parts = []
                for kind in range(2):
                    x = pages[j][pl.ds(u * CMP_BLOCK * 4 + kind, CMP_BLOCK, stride=4), :]
                    parts.append(jnp.sum(x * wp_ref[:, kind * LANE:(kind + 1) * LANE], axis=0, keepdims=True))
                rows.append(jnp.concatenate(parts, axis=1))
        o_ref[0] = jnp.concatenate(rows, axis=0)

    def page_spec(j):
        return pl.BlockSpec((PAGE * 4, LANE), lambda b, g, pt, j=j: (page_base + pt[b, g * pg + j], 0))

    grid_spec = pltpu.PrefetchScalarGridSpec(
        num_scalar_prefetch=1, grid=(db, ng),
        in_specs=[page_spec(j) for j in range(pg)] + [
            pl.BlockSpec((CMP_BLOCK, 256), lambda b, g, pt: (0, 0))],
        out_specs=pl.BlockSpec((1, pg * per, 256), lambda b, g, pt: (b, g, 0)))
    return pl.pallas_call(
        body, grid_spec=grid_spec,
        out_shape=jax.ShapeDtypeStruct((db, n_pages * per, 256), F32),
        compiler_params=_params("parallel", "parallel"),
        name="b_pool_pages")(page_table, *([cache2] * pg), wpb2)


def _b_select_sample(proj3, pooled, win, neww, g4, *, past_len, n_valid):
    db = neww.shape[0]
    tp = neww.shape[1] // 2
    nblk = pooled.shape[1]
    n_top = min(N_SEL, nblk + 1)
    wb = win.shape[1] // 2
    rows = H_B * tp
    scale = HEAD ** -0.5

    def body(q_ref, pool_ref, win_ref, nw_ref, bg_ref, g_ref, qs_ref, ocw_ref, g1_ref, sel_ref):
        qf = jnp.concatenate(
            [_rms(q_ref[0, :, h * LANE:(h + 1) * LANE], g_ref[0:1, :]) * scale for h in range(H_B)],
            axis=0)
        qb = qf.astype(BF16)
        qs_ref[0] = qb
        tok = _mod(lax.broadcasted_iota(jnp.int32, (rows, 1), 0), tp)
        pos = past_len + tok
        cur = _div(pos, CMP_BLOCK)
        blk = lax.broadcasted_iota(jnp.int32, (rows, nblk), 1)
        cmask = blk < cur

        kc = _rms(pool_ref[0, :, 0:128], g_ref[1:2, :])
        s = jnp.where(cmask, _cmp_scores(qf, kc), NEG)
        mx = jnp.max(s, axis=-1, keepdims=True)
        p = jnp.where(cmask, jnp.exp(s - mx), 0.0)
        pc = p / jnp.maximum(jnp.sum(p, axis=-1, keepdims=True), 1e-30)
        o_cmp = _mm(pc.astype(BF16), pool_ref[0, :, 128:256].astype(BF16))
        imp = pc[0:tp]
        for h in range(1, H_B):
            imp = imp + pc[h * tp:(h + 1) * tp]
        imp = jnp.where(cmask[0:tp], imp, -1.0)

        pad = jnp.concatenate([imp, jnp.zeros((LANE - tp, nblk), F32)], axis=0)
        imp_t = jnp.concatenate([pad[:, u * LANE:(u + 1) * LANE].T for u in range(nblk // LANE)], axis=0)
        mi = lax.broadcasted_iota(jnp.int32, (nblk, nblk), 0)
        ni = lax.broadcasted_iota(jnp.int32, (nblk, nblk), 1)
        sels = []
        for tkn in range(tp):
            r = imp[tkn:tkn + 1, :]
            c = imp_t[:, tkn:tkn + 1]
            ahead = (c > r) | ((c == r) & (mi < ni))
            rank = jnp.sum(ahead.astype(F32), axis=0, keepdims=True)
            sels.append(((rank < n_top) & (r >= 0.0)).astype(F32))
        sel_ref[0] = jnp.concatenate(sels, axis=0)

        kw = win_ref[0, pl.ds(0, wb, stride=2), :].astype(BF16)
        vw = win_ref[0, pl.ds(1, wb, stride=2), :].astype(BF16)
        zpad = jnp.zeros((LANE - tp, LANE), F32)
        kn = jnp.concatenate([nw_ref[0, pl.ds(0, tp, stride=2), :], zpad], axis=0).astype(BF16)
        vn = jnp.concatenate([nw_ref[0, pl.ds(1, tp, stride=2), :], zpad], axis=0).astype(BF16)
        jw =lax.broadcasted_iota(jnp.int32, (rows, wb), 1)
        dw = pos - (past_len - wb + jw)
        jn = lax.broadcasted_iota(jnp.int32, (rows, LANE), 1)
        dn = tok - jn
        wmask = jnp.concatenate([(dw >= 0) & (dw < WINDOW), (dn >= 0) & (dn < WINDOW) & (jn < n_valid)],
                                axis=1)
        sw = jnp.where(wmask, jnp.concatenate([_nt(qb, kw), _nt(qb, kn)], axis=1), NEG)
        mw = jnp.max(sw, axis=-1, keepdims=True)
        pw = jnp.where(wmask, jnp.exp(sw - mw), 0.0)
        o_win = (_mm(pw[:, 0:wb].astype(BF16), vw) + _mm(pw[:, wb:].astype(BF16), vn)) \
            / jnp.maximum(jnp.sum(pw, axis=-1, keepdims=True), 1e-30)

        gate = _sigmoid(bg_ref[0])
        g0 = jnp.concatenate([gate[:, 3 * h:3 * h + 1] for h in range(H_B)], axis=0)
        g1 = jnp.concatenate([gate[:, 3 * h + 1:3 * h + 2] for h in range(H_B)], axis=0)
        g2 = jnp.concatenate([gate[:, 3 * h + 2:3 * h + 3] for h in range(H_B)], axis=0)
        ocw_ref[0] = g0 * o_cmp + g2 * o_win
        g1_ref[0] = jnp.broadcast_to(g1, (rows, LANE))

    per_b = lambda r, w: pl.BlockSpec((1, r, w), lambda b: (b, 0, 0))
    return pl.pallas_call(
        body, grid=(db,),
        in_specs=[pl.BlockSpec((1, tp, 512), lambda b: (b, 0, _UNIT["bq"] // 4)),
                  per_b(nblk, 256), per_b(2 * wb, LANE), per_b(2 * tp, LANE),
                  pl.BlockSpec((1, tp, LANE), lambda b: (b, 0, _UNIT["bg"])),
                  pl.BlockSpec((4, LANE), lambda b: (0, 0))],
        out_specs=[per_b(rows, LANE), per_b(rows, LANE), per_b(rows, LANE), per_b(tp, nblk)],
        out_shape=[jax.ShapeDtypeStruct((db, rows, LANE), BF16),
                   jax.ShapeDtypeStruct((db, rows, LANE), F32),
                   jax.ShapeDtypeStruct((db, rows, LANE), F32),
                   jax.ShapeDtypeStruct((db, tp, nblk), F32)],
        compiler_params=_params("parallel"),
        name="b_select_sample")(proj3, pooled, win, neww, proj3, g4)


def _b_selected_sample(page_table, cache2, qs, selg, bkvs, ocw, g1b, proj3, page_base,
                       *, n_valid, pg):
    db, n_pages = page_table.shape
    ng = n_pages // pg
    rows = qs.shape[1]
    tp = rows // H_B
    per = PAGE // CMP_BLOCK
    nbg = pg * per
    width = pg * PAGE

    def body(pt_ref, *refs):
        pages = refs[:pg]
        q_ref, sel_ref, new_ref, ocw_ref, g1_ref, z_ref, o_ref, m_scr, l_scr, acc_scr = refs[pg:]
        gi = pl.program_id(1)

        @pl.when(gi == 0)
        def _():
            m_scr[...] = jnp.full(m_scr.shape, NEG, F32)
            l_scr[...] = jnp.zeros(l_scr.shape, F32)
            acc_scr[...] = jnp.zeros(acc_scr.shape, F32)

        q = q_ref[0]

        def update(s, mask, vs):
            m_old = m_scr[...]
            m_new = jnp.maximum(m_old, jnp.max(s, axis=-1, keepdims=True))
            alpha = jnp.exp(m_old - m_new)
            p = jnp.where(mask, jnp.exp(s - m_new), 0.0)
            pv = None
            for j, v in enumerate(vs):
                d = _mm(p[:, j * PAGE:(j + 1) * PAGE].astype(BF16), v)
                pv = d if pv is None else pv + d
            l_scr[...] = alpha * l_scr[...] + jnp.sum(p, axis=-1, keepdims=True)
            acc_scr[...] = alpha * acc_scr[...] + pv
            m_scr[...] = m_new

        e = (_div(lax.broadcasted_iota(jnp.int32, (nbg, width), 1), CMP_BLOCK)
             == lax.broadcasted_iota(jnp.int32, (nbg, width), 0)).astype(BF16)
        mk = _mm(sel_ref[0, 0].astype(BF16), e) > 0.5
        mask = jnp.concatenate([mk] * H_B, axis=0)
        ss, vs = [], []
        for j in range(pg):
            ss.append(_nt(q, pages[j][pl.ds(2, PAGE, stride=4), :].astype(BF16)))
            vs.append(pages[j][pl.ds(3, PAGE, stride=4), :].astype(BF16))
        update(jnp.where(mask, jnp.concatenate(ss, axis=1), NEG), mask, vs)

        @pl.when(gi == ng - 1)
        def _():
            zpad = jnp.zeros((PAGE - tp, LANE), F32)
            kn = jnp.concatenate([new_ref[0, pl.ds(2, tp, stride=4), :], zpad], axis=0).astype(BF16)
            vn = jnp.concatenate([new_ref[0, pl.ds(3, tp, stride=4), :], zpad], axis=0).astype(BF16)
            tok = _mod(lax.broadcasted_iota(jnp.int32, (rows, PAGE), 0), tp)
            col = lax.broadcasted_iota(jnp.int32, (rows, PAGE), 1)
            nmask = (col <= tok) & (col < n_valid)
            update(jnp.where(nmask, _nt(q, kn), NEG), nmask, [vn])
            o_sel = acc_scr[...] / jnp.maximum(l_scr[...], 1e-30)
            ob = ocw_ref[0] + g1_ref[0] * o_sel
            for h in range(H_B):
                y = ob[h * tp:(h + 1) * tp] * _silu(z_ref[0, :, h * LANE:(h + 1) * LANE])
                o_ref[0, :, h * LANE:(h + 1) * LANE] = y.astype(BF16)

    def page_spec(j):
        return pl.BlockSpec((PAGE * 4, LANE), lambda b, g, pt, j=j: (page_base + pt[b, g * pg + j], 0))

    per_b = lambda r, w: pl.BlockSpec((1, r, w), lambda b, g, pt: (b, 0, 0))
    grid_spec = pltpu.PrefetchScalarGridSpec(
        num_scalar_prefetch=1, grid=(db, ng),
        in_specs=[page_spec(j) for j in range(pg)] + [
            per_b(rows, LANE),
            pl.BlockSpec((1, 1, tp, nbg), lambda b, g, pt: (b, g, 0, 0)),
            per_b(4 * tp, LANE), per_b(rows, LANE), per_b(rows, LANE),
            pl.BlockSpec((1, tp, 512), lambda b, g, pt: (b, 0, _UNIT["bz"] // 4))],
        out_specs=per_b(tp, 512),
        scratch_shapes=[pltpu.VMEM((rows, 1), F32), pltpu.VMEM((rows, 1), F32),
                        pltpu.VMEM((rows, LANE), F32)])
    return pl.pallas_call(
        body, grid_spec=grid_spec,
        out_shape=jax.ShapeDtypeStruct((db, tp, 512), BF16),
        compiler_params=_params("parallel", "arbitrary"),
        name="b_selected_sample")(page_table, *([cache2] * pg), qs, selg, bkvs, ocw, g1b, proj3)


def _mlstm(proj3, conv_w, conv_b, gate_b, out_g, conv0, c0, n0, m0, *, lc, n_valid):
    b_, t, _ = proj3.shape
    nchunk = t // lc
    dqk = H_C * DQK_C
    tail = CONV_W - 1
    base = 8
    tsq = max(lc, LANE)

    def body(qk_ref, v_ref, if_ref, co_ref, cz_ref, cw_ref, cb_ref, gb_ref, g_ref,
             conv0_ref, c0_ref, n0_ref, m0_ref,
             y_ref, conv_ref, c_ref, n_ref, m_ref, xbuf, c_scr, n_scr, m_scr):
        ci = pl.program_id(1)

        @pl.when(ci == 0)
        def _():
            xbuf[base - tail:base, :] = conv0_ref[0]
            c_scr[...] = c0_ref[0]
            n_scr[...] = n0_ref[0]
            m_scr[...] = m0_ref[0]

        xbuf[base:base + lc, :] = qk_ref[0]
        acc = cb_ref[...]
        for j in range(CONV_W):
            acc = acc + xbuf[base - tail + j:base - tail + j + lc, :] * cw_ref[j:j + 1, :]
        new_tail = xbuf[base + n_valid - tail:base + n_valid, :]
        xbuf[base - tail:base, :] = new_tail
        conv_ref[0] = new_tail
        qk = _silu(acc)

        gt = if_ref[0] + gb_ref[...]
        lf = jnp.minimum(gt, 0.0) - jnp.log(1.0 + jnp.exp(-jnp.abs(gt)))
        ti = lax.broadcasted_iota(jnp.int32, (lc, lc), 0)
        si = lax.broadcasted_iota(jnp.int32, (lc, lc), 1)
        tri = (si <= ti).astype(BF16)
        l1, l2, l3 = _split3(lf)
        bcum = _mm(tri, l1) + _mm(tri, l2) + _mm(tri, l3)
        zrow = jnp.zeros((tsq - lc, LANE), F32)
        gt_t = (jnp.concatenate([gt, zrow], axis=0) if tsq > lc else gt).T
        b_t = (jnp.concatenate([bcum, zrow], axis=0) if tsq > lc else bcum).T
        dmask = (si <= ti) & (si < n_valid)
        svalid = lax.broadcasted_iota(jnp.int32, (lc, 1), 0) < n_valid

        for h in range(H_C):
            qh = qk[:, h * DQK_C:(h + 1) * DQK_C] * (DQK_C ** -0.5)
            kh = qk[:, dqk + h * DQK_C:dqk + (h + 1) * DQK_C]
            vh = v_ref[0, :, h * HEAD:(h + 1) * HEAD]
            qhb, khb = qh.astype(BF16), kh.astype(BF16)
            bcol = bcum[:, H_C + h:H_C + h + 1]
            icol = gt[:, h:h + 1]
            brow = b_t[H_C + h:H_C + h + 1, 0:lc]
            irow = gt_t[h:h + 1, 0:lc]
            m_h = m_scr[h:h + 1, 0:1]
            ch = c_scr[h]
            nh = n_scr[h:h + 1, :]

            d = jnp.where(dmask, bcol - brow + irow, NEG)
            inter = bcol + m_h
            m_t = jnp.maximum(inter, jnp.max(d, axis=1, keepdims=True))
            w_intra = jnp.where(dmask, jnp.exp(d - m_t), 0.0)
            w_inter = jnp.exp(inter - m_t)
            sqk = _nt(qhb, khb) * w_intra
            num = w_inter * _nt(qhb, ch.astype(BF16)) + _mm(sqk.astype(BF16), vh.astype(BF16))
            den = w_inter * jnp.sum(qh * nh, axis=1, keepdims=True) + jnp.sum(sqk, axis=1, keepdims=True)
            hh = num / jnp.maximum(jnp.abs(den), jnp.exp(-m_t))

            b_last = bcum[n_valid - 1:n_valid, H_C + h:H_C + h + 1]
            dec = jnp.where(svalid, b_last - bcol + icol, NEG)
            m_new = jnp.maximum(b_last + m_h, jnp.max(dec, axis=0, keepdims=True))
            wk = jnp.where(svalid, jnp.exp(dec - m_new), 0.0)
            carry = jnp.exp(b_last + m_h - m_new)
            c_new = carry * ch + _tn((wk * vh).astype(BF16), khb)
            n_new = carry * nh + jnp.sum(wk * kh, axis=0, keepdims=True)
            c_scr[h] = c_new
            n_scr[h:h + 1, :] = n_new
            m_scr[h:h + 1, :] = jnp.broadcast_to(m_new, (1, LANE))

            gate = _sigmoid(co_ref[0, :, h * HEAD:(h + 1) * HEAD]) * _silu(cz_ref[0, :, h * HEAD:(h + 1) * HEAD])
            y_ref[0, :, h * HEAD:(h + 1) * HEAD] = (_rms(hh, g_ref[...]) * gate).astype(BF16)

        c_ref[0] = c_scr[...]
        n_ref[0] = n_scr[...]
        m_ref[0] = m_scr[...]

    col = lambda u: pl.BlockSpec((1, lc, 1024), lambda b, c, u=u: (b, c, u))
    const = lambda *shape: pl.BlockSpec(shape, lambda b, c: (0,) * len(shape))
    per_b = lambda *shape: pl.BlockSpec((1,) + shape, lambda b, c: (b,) + (0,) * len(shape))
    return pl.pallas_call(
        body, grid=(b_, nchunk),
        in_specs=[col(_UNIT["cqk"] // 8), col(_UNIT["cv"] // 8),
                  pl.BlockSpec((1, lc, LANE), lambda b, c: (b, c, _UNIT["cif"])),
                  col(_UNIT["co"] // 8), col(_UNIT["cz"] // 8),
                  const(CONV_W, 1024), const(1, 1024), const(1, LANE), const(1, LANE),
                  per_b(tail, 1024), per_b(H_C, HEAD, DQK_C), per_b(H_C, DQK_C), per_b(H_C, LANE)],
        out_specs=[pl.BlockSpec((1, lc, 1024), lambda b, c: (b, c, 0)),
                   per_b(tail, 1024), per_b(H_C, HEAD, DQK_C), per_b(H_C, DQK_C), per_b(H_C, LANE)],
        out_shape=[jax.ShapeDtypeStruct((b_, t, 1024), BF16),
                   jax.ShapeDtypeStruct((b_, tail, 1024), F32),
                   jax.ShapeDtypeStruct((b_, H_C, HEAD, DQK_C), F32),
                   jax.ShapeDtypeStruct((b_, H_C, DQK_C), F32),
                   jax.ShapeDtypeStruct((b_, H_C, LANE), F32)],
        scratch_shapes=[pltpu.VMEM((base + lc, 1024), F32), pltpu.VMEM((H_C, HEAD, DQK_C), F32),
                        pltpu.VMEM((H_C, DQK_C), F32), pltpu.VMEM((H_C, LANE), F32)],
        compiler_params=_params("parallel", "arbitrary"),
        name="mlstm")(proj3, proj3, proj3, proj3, proj3, conv_w, conv_b, gate_b, out_g,
                      conv0, c0, n0, m0)


def _w_prep(w_in, *, tk):
    depth, d, n_in = w_in.shape
    src_off, off = {}, 0
    for name, width in _SRC:
        src_off[name] = (off, width)
        off += width
    assert off == n_in

    def body(w_ref, o_ref):
        for name in _DST_ORDER:
            s0, width = src_off[name]
            for u in range(-(-width // LANE)):
                s = s0 + u * LANE
                valid = min(LANE, width - u * LANE)
                a = (s // LANE) * LANE
                win = w_ref[0, :, a:min(a + 2 * LANE, n_in)]
                x = win[:, s - a:s - a + valid]
                if valid < LANE:
                    x = jnp.concatenate([x, jnp.zeros((tk, LANE - valid), F32)], axis=1)
                dst = (_UNIT[name] + u) * LANE
                o_ref[0, :, dst:dst + LANE] = x.astype(BF16)

    return pl.pallas_call(
        body, grid=(depth, d // tk),
        in_specs=[pl.BlockSpec((1, tk, n_in), lambda l, i: (l, i, 0))],
        out_specs=pl.BlockSpec((1, tk, N_PROJ), lambda l, i: (l, i, 0)),
        out_shape=jax.ShapeDtypeStruct((depth, d, N_PROJ), BF16),
        compiler_params=_params("parallel", "parallel"),
        name="w_prep")(w_in)


def _pick(n, prefs):
    for p in prefs:
        if n % p == 0:
            return p
    return n


def kernel(x_prompt, x_sample, cache_a_kv, cache_b_kv, state_b_win, state_c_conv, state_c_C,
           state_c_n, state_c_m, page_table, norm_g, w_in, w_out, a_qk_g, a_lambda, a_out_g,
           b_qk_g, b_cmp_w, c_conv_w, c_conv_b, c_gate_b, c_out_g):
    bp, t, d = x_prompt.shape
    db, ts, _ = x_sample.shape
    depth = norm_g.shape[0]
    n_pool = cache_a_kv.shape[1]
    n_pages = page_table.shape[1]
    past_len = n_pages * PAGE
    wb = state_b_win.shape[2]
    tp = 8
    assert ts <= tp and t % MLSTM_CHUNK == 0 and t % Q_BLOCK == 0 and d == 2048
    mp, ms = bp * t, db * tp
    pg = _pick(n_pages, (8, 4, 2, 1))

    cache_a2 = cache_a_kv.reshape(depth * n_pool * PAGE * 2 * H_A, HEAD)
    cache_b2 = cache_b_kv.reshape(depth * n_pool * PAGE * 4, HEAD)
    win_rows = state_b_win.reshape(depth, db, wb * 2, HEAD)
    hp = x_prompt.reshape(mp, d)
    hs = jnp.pad(x_sample, ((0, 0), (0, tp - ts), (0, 0))).reshape(ms, d)

    tm_p = _pick(mp, (1024, 512, 256, 128))
    tq_a = _pick(t, (512, 256, 128))
    outs_p = [[] for _ in range(7)]
    outs_s = [[] for _ in range(7)]

    w_all = _w_prep(w_in, tk=_pick(d, (256, 128)))
    for l in range(depth):
        lam_init = 0.8 - 0.6 * math.exp(-0.3 * l)
        w_o = w_out[l].astype(BF16)
        g_a = jnp.tile(a_qk_g[l], (1, 2))
        g_b = b_qk_g[l]
        wpb = jnp.broadcast_to(b_cmp_w[l][:, :, None], (2, CMP_BLOCK, LANE))
        wpb2 = jnp.concatenate([wpb[0], wpb[1]], axis=1)
        gate_b = jnp.pad(c_gate_b[l].reshape(1, 2 * H_C), ((0, 0), (0, LANE - 2 * H_C)))
        conv_b = c_conv_b[l].reshape(1, -1)
        out_gc = c_out_g[l].reshape(1, LANE)

        proj = _norm_matmul(hp, norm_g[l], w_all, l, tm=tm_p, tn=1024)
        proj3 = proj.reshape(bp, t, N_PROJ)
        qn, kn, vb, akv = _a_prep(proj, g_a, tm=_pick(mp, (512, 256, 128)))
        ya = _a_attn_prompt(qn.reshape(bp, t, 512), kn.reshape(bp, t, 512), vb.reshape(bp, t, 512),
                            proj3, a_lambda[l], a_out_g[l], lam_init, tq=tq_a)
        bkvs, neww, kvb, kcb, vcb = _b_prep(proj, g_b, wpb, tm=_pick(mp, (512, 256, 128)), pool=True)
        nblk = t // CMP_BLOCK
        yb = _b_attn_prompt(proj3, kcb.reshape(bp, nblk, LANE), vcb.reshape(bp, nblk, LANE),
                            kvb.reshape(bp, t, 512), g_b, t=t)
        yc, conv_p, c_p, n_p, m_p = _mlstm(
            proj3, c_conv_w[l], conv_b, gate_b, out_gc,
            jnp.zeros((bp, CONV_W - 1, 2 * H_C * DQK_C), F32), jnp.zeros((bp, H_C, HEAD, DQK_C), F32),
            jnp.zeros((bp, H_C, DQK_C), F32), jnp.zeros((bp, H_C, LANE), F32),
            lc=MLSTM_CHUNK, n_valid=MLSTM_CHUNK)
        hp = _out_proj(hp, ya.reshape(mp, 512), yb.reshape(mp, 512), yc.reshape(mp, 1024), w_o,
                       tm=tm_p, tn=1024)
        win_p = jnp.concatenate([jnp.zeros((bp, wb, 2, HEAD), F32), neww.reshape(bp, t, 2, HEAD)],
                                axis=1)[:, -wb:]
        for lst, a in zip(outs_p, (akv.reshape(bp, t, 2, H_A, HEAD), bkvs.reshape(bp, t, 4, HEAD),
                                   win_p, conv_p, c_p, n_p, m_p[:, :, 0])):
            lst.append(a)

        base = l * n_pool
        sproj = _norm_matmul(hs, norm_g[l], w_all, l, tm=ms, tn=1024)
        sproj3 = sproj.reshape(db, tp, N_PROJ)
        sqn, skn, svb, sakv = _a_prep(sproj, g_a, tm=ms)
        kvnew = jnp.concatenate([skn.reshape(db, tp, 512), svb.reshape(db, tp, 512)], axis=2)
        kvnew = jnp.pad(kvnew, ((0, 0), (0, PAGE - tp), (0, 0)))
        sya = _a_attn_sample(page_table, cache_a2, sqn.reshape(db, tp, 512), kvnew, sproj3,
                             a_lambda[l], a_out_g[l], lam_init, base, n_valid=ts, pg=pg)
        sbkvs, sneww, _ = _b_prep(sproj, g_b, wpb, tm=ms, pool=False)
        pooled = _b_pool_pages(page_table, cache_b2, wpb2, base, pg=pg)
        qs, ocw, g1b, sel = _b_select_sample(
            sproj3, pooled, win_rows[l], sneww.reshape(db, tp * 2, HEAD), g_b,
            past_len=past_len, n_valid=ts)
        nbg = pg * (PAGE // CMP_BLOCK)
        selg = sel.reshape(db, tp, n_pages // pg, nbg).transpose(0, 2, 1, 3)
        syb = _b_selected_sample(page_table, cache_b2, qs, selg, sbkvs.reshape(db, tp * 4, HEAD), ocw, g1b,
                                 sproj3, base, n_valid=ts, pg=pg)
        m0 = jnp.broadcast_to(state_c_m[l][:, :, None], (db, H_C, LANE))
        syc, conv_s, c_s, n_s, m_s = _mlstm(
            sproj3, c_conv_w[l], conv_b, gate_b, out_gc, state_c_conv[l], state_c_C[l], state_c_n[l], m0,
            lc=tp, n_valid=ts)
        hs = _out_proj(hs, sya.reshape(ms, 512), syb.reshape(ms, 512), syc.reshape(ms, 1024), w_o,
                       tm=ms, tn=1024)
        win_s = jnp.concatenate([state_b_win[l], sneww.reshape(db, tp, 2, HEAD)[:, :ts]], axis=1)[:, -wb:]
        for lst, a in zip(outs_s, (sakv.reshape(db, tp, 2, H_A, HEAD)[:, :ts],
                                   sbkvs.reshape(db, tp, 4, HEAD)[:, :ts],
                                   win_s, conv_s, c_s, n_s, m_s[:, :, 0])):
            lst.append(a)

    y_p = hp.reshape(bp, t, d)
    y_s = hs.reshape(db, tp, d)[:, :ts]
    sp = [jnp.stack(x) for x in outs_p]
    ss = [jnp.stack(x) for x in outs_s]
    return (y_p, y_s, sp[0], ss[0], sp[1], ss[1], sp[2], ss[2], sp[3], ss[3],
            sp[4], ss[4], sp[5], ss[5], sp[6], ss[6])
```

```python
import functools
import math

import jax
import jax.numpy as jnp
from jax import lax
from jax.experimental import pallas as pl
from jax.experimental.pallas import tpu as pltpu

F32 = jnp.float32
BF16 = jnp.bfloat16

EPS = 1e-6
LANE = 128
HEAD = 128
H_A = 4
DQ_A = HEAD // 2
H_B = 4
H_C = 8
DQK_C = HEAD // 2
CMP_BLOCK = 64
N_SEL = 16
WINDOW = 512
CONV_W = 4
MLSTM_CHUNK = 64
PAGE = 128
Q_BLOCK = 128
NEG = -1e30
VMEM_LIMIT = 56 * 1024 * 1024

_SRC = (("aq", 512), ("ak", 512), ("av", 512), ("az", 512),
        ("bq", 512), ("bkv", 768), ("bg", 12), ("bz", 512),
        ("cqk", 1024), ("cv", 1024), ("cif", 16), ("co", 1024), ("cz", 1024))
_DST_ORDER = ("aq", "ak", "av", "az", "bq", "bz", "bkv", "bg", "cif", "cqk", "cv", "co", "cz")
_UNIT = {"aq": 0, "ak": 4, "av": 8, "az": 12, "bq": 16, "bz": 20, "bkv": 24, "bg": 30,
         "cif": 31, "cqk": 32, "cv": 40, "co": 48, "cz": 56}
N_PROJ = 64 * LANE


def _nt(a, b):
    return lax.dot_general(a, b, (((1,), (1,)), ((), ())), preferred_element_type=F32)


def _tn(a, b):
    return lax.dot_general(a, b, (((0,), (0,)), ((), ())), preferred_element_type=F32)


def _mm(a, b):
    return jnp.dot(a, b, preferred_element_type=F32)


def _split2(x):
    hi = x.astype(BF16)
    lo = (x - hi.astype(F32)).astype(BF16)
    return hi, lo


def _split3(x):
    hi = x.astype(BF16)
    r = x - hi.astype(F32)
    mid = r.astype(BF16)
    lo = (r - mid.astype(F32)).astype(BF16)
    return hi, mid, lo


def _sigmoid(z):
    return 1.0 / (1.0 + jnp.exp(-z))


def _silu(z):
    return z * _sigmoid(z)


def _rms(x, g):
    return x * lax.rsqrt(jnp.mean(x * x, axis=-1, keepdims=True) + EPS) * g


def _div(x, n):
    return lax.shift_right_logical(x, int(math.log2(n)))


def _mod(x, n):
    return x & (n - 1)


def _online_update(s, v, m_ref, l_ref, acc_ref, valid=None, pv_fn=None):
    w = s.shape[1]
    m_old = m_ref[...]
    m_new = jnp.maximum(m_old, jnp.max(s, axis=-1, keepdims=True))
    alpha = jnp.exp(m_old - m_new)
    m_rep = m_new if w == LANE else jnp.concatenate([m_new] * (w // LANE), axis=1)
    p = jnp.exp(s - m_rep)
    if valid is not None:
        p = jnp.where(valid, p, 0.0)
    l_ref[...] = alpha * l_ref[...] + jnp.sum(p, axis=-1, keepdims=True)
    pb = p.astype(BF16)
    acc_ref[...] = alpha * acc_ref[...] + (_mm(pb, v) if pv_fn is None else pv_fn(pb))
    m_ref[...] = m_new


def _params(*sem):
    return pltpu.CompilerParams(dimension_semantics=sem, vmem_limit_bytes=VMEM_LIMIT)


def _norm_matmul(x, g, w, layer, *, tm, tn):
    m, d = x.shape
    n = w.shape[2]
    rc = min(tm, 256)

    def body(x_ref, g_ref, w_ref, o_ref, h_scr):
        @pl.when(pl.program_id(1) == 0)
        def _():
            def chunk(c, carry):
                r = pl.ds(pl.multiple_of(c * rc, rc), rc)
                h_scr[r, :] = _rms(x_ref[r, :], g_ref[...]).astype(BF16)
                return carry
            lax.fori_loop(0, tm // rc, chunk, 0)
        o_ref[...] = _mm(h_scr[...], w_ref[0])

    return pl.pallas_call(
        body, grid=(m // tm, n // tn),
        in_specs=[pl.BlockSpec((tm, d), lambda i, j: (i, 0)),
                  pl.BlockSpec((1, d), lambda i, j: (0, 0)),
                  pl.BlockSpec((1, d, tn), lambda i, j: (layer, 0, j))],
        out_specs=pl.BlockSpec((tm, tn), lambda i, j: (i, j)),
        out_shape=jax.ShapeDtypeStruct((m, n), F32),
        scratch_shapes=[pltpu.VMEM((tm, d), BF16)],
        compiler_params=_params("parallel", "arbitrary"),
        name="norm_matmul")(x, g.reshape(1, d), w)


def _out_proj(x, ya, yb, yc, w, *, tm, tn):
    m, d = x.shape
    da, db, dc = ya.shape[1], yb.shape[1], yc.shape[1]

    def body(x_ref, a_ref, b_ref, c_ref, wa_ref, wb_ref, wc_ref, o_ref):
        acc = _mm(a_ref[...], wa_ref[...])
        acc += _mm(b_ref[...], wb_ref[...])
        acc += _mm(c_ref[...], wc_ref[...])
        o_ref[...] = x_ref[...] + acc

    return pl.pallas_call(
        body, grid=(m // tm, d // tn),
        in_specs=[pl.BlockSpec((tm, tn), lambda i, j: (i, j)),
                  pl.BlockSpec((tm, da), lambda i, j: (i, 0)),
                  pl.BlockSpec((tm, db), lambda i, j: (i, 0)),
                  pl.BlockSpec((tm, dc), lambda i, j: (i, 0)),
                  pl.BlockSpec((da, tn), lambda i, j: (0, j)),
                  pl.BlockSpec((db, tn), lambda i, j: (da // db, j)),
                  pl.BlockSpec((dc, tn), lambda i, j: ((da + db) // dc, j))],
        out_specs=pl.BlockSpec((tm, tn), lambda i, j: (i, j)),
        out_shape=jax.ShapeDtypeStruct((m, d), F32),
        compiler_params=_params("parallel", "parallel"),
        name="out_proj")(x, ya, yb, yc, w, w, w)


def _a_prep(proj, g2, *, tm):
    m = proj.shape[0]

    def body(q_ref, k_ref, v_ref, g_ref, qn_ref, kn_ref, vb_ref, akv_ref):
        r = _div(lax.broadcasted_iota(jnp.int32, (LANE, LANE), 0), DQ_A)
        c = _div(lax.broadcasted_iota(jnp.int32, (LANE, LANE), 1), DQ_A)
        seg = (r == c).astype(BF16)

        def segnorm(x, g):
            outs = []
            for u in range(4):
                xc = x[:, u * LANE:(u + 1) * LANE]
                hi, lo = _split2(xc * xc)
                s = _mm(hi, seg) + _mm(lo, seg)
                outs.append(xc * lax.rsqrt(s * (1.0 / DQ_A) + EPS) * g)
            return jnp.concatenate(outs, axis=1)

        qn = segnorm(q_ref[...], g_ref[0:1, :]) * (DQ_A ** -0.5)
        kn = segnorm(k_ref[...], g_ref[1:2, :])
        v = v_ref[...]
        qn_ref[...] = qn.astype(BF16)
        kn_ref[...] = kn.astype(BF16)
        vb_ref[...] = v.astype(BF16)
        for h in range(H_A):
            akv_ref[pl.ds(h, tm, stride=2 * H_A), :] = kn[:, h * LANE:(h + 1) * LANE]
            akv_ref[pl.ds(H_A + h, tm, stride=2 * H_A), :] = v[:, h * LANE:(h + 1) * LANE]

    blk = lambda u: pl.BlockSpec((tm, 512), lambda i, u=u: (i, u))
    row = lambda w: pl.BlockSpec((tm, w), lambda i: (i, 0))
    return pl.pallas_call(
        body, grid=(m // tm,),
        in_specs=[blk(_UNIT["aq"] // 4), blk(_UNIT["ak"] // 4), blk(_UNIT["av"] // 4),
                  pl.BlockSpec((2, LANE), lambda i: (0, 0))],
        out_specs=[row(512), row(512), row(512), pl.BlockSpec((tm * 2 * H_A, LANE), lambda i: (i, 0))],
        out_shape=[jax.ShapeDtypeStruct((m, 512), BF16)] * 3
        + [jax.ShapeDtypeStruct((m * 2 * H_A, LANE), F32)],
        compiler_params=_params("parallel"),
        name="a_prep")(proj, proj, proj, g2)


def _diff_lambda(lp, lam_init):
    a = jnp.sum(lp[0:1, :] * lp[1:2, :], axis=-1, keepdims=True)
    b = jnp.sum(lp[2:3, :] * lp[3:4, :], axis=-1, keepdims=True)
    return jnp.exp(a) - jnp.exp(b) + lam_init


def _a_finish(acc1, l1, acc2, l2, lam, g, z, lam_init):
    o = acc1 / l1 - lam * (acc2 / l2)
    return _rms(o, g) * (1.0 - lam_init) * _silu(z)


def _a_attn_prompt(qn, kn, vb, proj3, lam_p, out_g, lam_init, *, tq):
    b_, t, _ = qn.shape
    tk = tq

    def body(q_ref, k_ref, v_ref, z_ref, lam_ref, g_ref, o_ref, m_scr, l_scr, acc_scr):
        qi = pl.program_id(2)
        q = q_ref[0]
        lane = lax.broadcasted_iota(jnp.int32, (tq, LANE), 1)
        zero = jnp.zeros_like(q)
        qs = (jnp.where(lane < DQ_A, q, zero), jnp.where(lane >= DQ_A, q, zero))
        m_scr[...] = jnp.full(m_scr.shape, NEG, F32)
        l_scr[...] = jnp.zeros(l_scr.shape, F32)
        acc_scr[...] = jnp.zeros(acc_scr.shape, F32)

        def chunk(kc, mask):
            ks = pl.ds(pl.multiple_of(kc * tk, tk), tk)
            k = k_ref[0, ks, :]
            v = v_ref[0, ks, :]
            for c in range(2):
                s = _nt(qs[c], k)
                if mask is not None:
                    s = jnp.where(mask, s, NEG)
                _online_update(s, v, m_scr.at[c], l_scr.at[c], acc_scr.at[c])

        def full_chunk(kc, carry):
            chunk(kc, None)
            return carry

        lax.fori_loop(0, qi, full_chunk, 0)
        chunk(qi, lax.broadcasted_iota(jnp.int32, (tq, tk), 1)
              <= lax.broadcasted_iota(jnp.int32, (tq, tk), 0))
        lam = _diff_lambda(lam_ref[...], lam_init)
        y = _a_finish(acc_scr[0], l_scr[0], acc_scr[1], l_scr[1], lam, g_ref[...], z_ref[0], lam_init)
        o_ref[0] = y.astype(BF16)

    head = lambda rows, qdep: pl.BlockSpec(
        (1, rows, LANE), (lambda b, h, i: (b, i, h)) if qdep else (lambda b, h, i: (b, 0, h)))
    return pl.pallas_call(
        body, grid=(b_, H_A, t // tq),
        in_specs=[head(tq, True), head(t, False), head(t, False),
                  pl.BlockSpec((1, tq, LANE), lambda b, h, i: (b, i, _UNIT["az"] + h)),
                  pl.BlockSpec((4, DQ_A), lambda b, h, i: (0, 0)),
                  pl.BlockSpec((1, LANE), lambda b, h, i: (0, 0))],
        out_specs=head(tq, True),
        out_shape=jax.ShapeDtypeStruct((b_, t, 512), BF16),
        scratch_shapes=[pltpu.VMEM((2, tq, LANE), F32)] * 3,
        compiler_params=_params("parallel", "parallel", "arbitrary"),
        name="a_attn_prompt")(qn, kn, vb, proj3, lam_p, out_g.reshape(1, LANE))


def _a_attn_sample(page_table, cache2, qn, kvnew, proj3, lam_p, out_g, lam_init, page_base,
                   *, n_valid, pg):
    db, n_pages = page_table.shape
    tp = qn.shape[1]
    ng = n_pages // pg

    def body(pt_ref, *refs):
        pages = refs[:pg]
        q_ref, new_ref, z_ref, lam_ref, g_ref, o_ref, m_scr, l_scr, acc_scr = refs[pg:]
        gi = pl.program_id(1)

        @pl.when(gi == 0)
        def _():
            m_scr[...] = jnp.full(m_scr.shape, NEG, F32)
            l_scr[...] = jnp.zeros(l_scr.shape, F32)
            acc_scr[...] = jnp.zeros(acc_scr.shape, F32)

        lane = lax.broadcasted_iota(jnp.int32, (tp, LANE), 1)

        def qstack(h):
            q = q_ref[0, :, h * LANE:(h + 1) * LANE]
            zero = jnp.zeros_like(q)
            return jnp.concatenate([jnp.where(lane < DQ_A, q, zero),
                                    jnp.where(lane >= DQ_A, q, zero)], axis=0)

        def slot(s_):
            return jnp.concatenate([pages[j][pl.ds(s_, PAGE, stride=2 * H_A), :].astype(BF16)
                                    for j in range(pg)], axis=0)

        s = jnp.concatenate([_nt(qstack(h), slot(h)) for h in range(H_A)], axis=0)
        _online_update(s, None, m_scr, l_scr, acc_scr,
                       pv_fn=lambda p: jnp.concatenate(
                           [_mm(p[h * 2 * tp:(h + 1) * 2 * tp], slot(H_A + h)) for h in range(H_A)], axis=0))

        @pl.when(gi == ng - 1)
        def _():
            lam = _diff_lambda(lam_ref[...], lam_init)
            row = _mod(lax.broadcasted_iota(jnp.int32, (2 * tp, PAGE), 0), tp)
            col = lax.broadcasted_iota(jnp.int32, (2 * tp, PAGE), 1)
            mask = (col <= row) & (col < n_valid)
            mask4 = jnp.concatenate([mask] * H_A, axis=0)
            s = jnp.concatenate([_nt(qstack(h), new_ref[0, :, h * LANE:(h + 1) * LANE])
                                 for h in range(H_A)], axis=0)
            _online_update(jnp.where(mask4, s, NEG), None, m_scr, l_scr, acc_scr, valid=mask4,
                           pv_fn=lambda p: jnp.concatenate(
                               [_mm(p[h * 2 * tp:(h + 1) * 2 * tp],
                                    new_ref[0, :, 512 + h * LANE:512 + (h + 1) * LANE])
                                for h in range(H_A)], axis=0))
            for h in range(H_A):
                r0 = h * 2 * tp
                acc = acc_scr[r0:r0 + 2 * tp, :]
                l = l_scr[r0:r0 + 2 * tp, :]
                y = _a_finish(acc[0:tp], l[0:tp], acc[tp:], l[tp:], lam, g_ref[...],
                              z_ref[0, :, h * LANE:(h + 1) * LANE], lam_init)
                o_ref[0, :, h * LANE:(h + 1) * LANE] = y.astype(BF16)

    def page_spec(j):
        return pl.BlockSpec((PAGE * 2 * H_A, LANE),
                            lambda b, g, pt, j=j: (page_base + pt[b, g * pg + j], 0))

    grid_spec = pltpu.PrefetchScalarGridSpec(
        num_scalar_prefetch=1, grid=(db, ng),
        in_specs=[page_spec(j) for j in range(pg)] + [
            pl.BlockSpec((1, tp, 512), lambda b, g, pt: (b, 0, 0)),
            pl.BlockSpec((1, PAGE, 1024), lambda b, g, pt: (b, 0, 0)),
            pl.BlockSpec((1, tp, 512), lambda b, g, pt: (b, 0, _UNIT["az"] // 4)),
            pl.BlockSpec((4, DQ_A), lambda b, g, pt: (0, 0)),
            pl.BlockSpec((1, LANE), lambda b, g, pt: (0, 0))],
        out_specs=pl.BlockSpec((1, tp, 512), lambda b, g, pt: (b, 0, 0)),
        scratch_shapes=[pltpu.VMEM((H_A * 2 * tp, LANE), F32)] * 3)
    return pl.pallas_call(
        body, grid_spec=grid_spec,
        out_shape=jax.ShapeDtypeStruct((db, tp, 512), BF16),
        compiler_params=_params("parallel", "arbitrary"),
        name="a_attn_sample")(page_table, *([cache2] * pg), qn, kvnew, proj3, lam_p,
                              out_g.reshape(1, LANE))


def _b_prep(proj, g4, wpb, *, tm, pool):
    m = proj.shape[0]
    nb = tm // CMP_BLOCK

    def body(kv_ref, g_ref, wp_ref, bkv_ref, nw_ref, kvb_ref, *pool_refs):
        kc = kv_ref[:, 0:128]
        vc = kv_ref[:, 128:256]
        ks = _rms(kv_ref[:, 256:384], g_ref[2:3, :])
        vs = kv_ref[:, 384:512]
        kw = _rms(kv_ref[:, 512:640], g_ref[3:4, :])
        vw = kv_ref[:, 640:768]
        for slot, x in enumerate((kc, vc, ks, vs)):
            bkv_ref[pl.ds(slot, tm, stride=4), :] = x
        nw_ref[pl.ds(0, tm, stride=2), :] = kw
        nw_ref[pl.ds(1, tm, stride=2), :] = vw
        kvb_ref[:, 0:128] = ks.astype(BF16)
        kvb_ref[:, 128:256] = vs.astype(BF16)
        kvb_ref[:, 256:384] = kw.astype(BF16)
        kvb_ref[:, 384:512] = vw.astype(BF16)
        if pool:
            kcb_ref, vcb_ref = pool_refs
            kp = jnp.sum(kc.reshape(nb, CMP_BLOCK, LANE) * wp_ref[0][None], axis=1)
            vp = jnp.sum(vc.reshape(nb, CMP_BLOCK, LANE) * wp_ref[1][None], axis=1)
            kcb_ref[...] = _rms(kp, g_ref[1:2, :])
            vcb_ref[...] = vp

    row = lambda w: pl.BlockSpec((tm, w), lambda i: (i, 0))
    out_specs = [pl.BlockSpec((tm * 4, LANE), lambda i: (i, 0)),
                 pl.BlockSpec((tm * 2, LANE), lambda i: (i, 0)), row(512)]
    out_shape = [jax.ShapeDtypeStruct((m * 4, LANE), F32), jax.ShapeDtypeStruct((m * 2, LANE), F32),
                 jax.ShapeDtypeStruct((m, 512), BF16)]
    if pool:
        out_specs += [pl.BlockSpec((nb, LANE), lambda i: (i, 0))] * 2
        out_shape += [jax.ShapeDtypeStruct((m // CMP_BLOCK, LANE), F32)] * 2
    return pl.pallas_call(
        body, grid=(m // tm,),
        in_specs=[pl.BlockSpec((tm, 768), lambda i: (i, _UNIT["bkv"] // 6)),
                  pl.BlockSpec((4, LANE), lambda i: (0, 0)),
                  pl.BlockSpec((2, CMP_BLOCK, LANE), lambda i: (0, 0, 0))],
        out_specs=out_specs, out_shape=out_shape,
        compiler_params=_params("parallel"),
        name="b_prep")(proj, g4, wpb)


def _cmp_scores(qf, kcb):
    qh, ql = _split2(qf)
    kh, kl = _split2(kcb)
    return _nt(qh, kh) + _nt(qh, kl) + _nt(ql, kh)


def _b_attn_prompt(proj3, kcb, vcb, kvb, g4, *, t):
    b_ = proj3.shape[0]
    tq = Q_BLOCK
    nblk = t // CMP_BLOCK
    n_top = min(N_SEL, nblk)
    tk = min(512, t)
    span = min(WINDOW + tq, t)
    scale = HEAD ** -0.5

    def body(q_ref, kcb_ref, vcb_ref, kv_ref, bg_ref, bz_ref, g_ref, o_ref, m_scr, l_scr, acc_scr):
        qi = pl.program_id(1)
        qf = jnp.concatenate([_rms(q_ref[0, :, h * LANE:(h + 1) * LANE], g_ref[0:1, :]) * scale
                              for h in range(H_B)], axis=0)
        qb = qf.astype(BF16)
        pos = qi * tq + lax.broadcasted_iota(jnp.int32, (tq, 1), 0)

        pos_l = qi * tq + lax.broadcasted_iota(jnp.int32, (1, tq), 1)
        cur_l = _div(pos_l, CMP_BLOCK)
        blk_s = lax.broadcasted_iota(jnp.int32, (nblk, tq), 0)
        cmask = blk_s < cur_l
        kh, kl = _split2(kcb_ref[0])
        pcs, imp = [], jnp.zeros((nblk, tq), F32)
        for h in range(H_B):
            qh, ql = _split2(qf[h * tq:(h + 1) * tq])
            s = jnp.where(cmask, _nt(kh, qh) + _nt(kl, qh) + _nt(kh, ql), NEG)
            mx = jnp.max(s, axis=0, keepdims=True)
            p = jnp.where(cmask, jnp.exp(s - mx), 0.0)
            pc = p / jnp.maximum(jnp.sum(p, axis=0, keepdims=True), 1e-30)
            pcs.append(pc)
            imp = imp + pc
        imp = jnp.where(cmask, imp, -1.0)
        fill = LANE - H_B * nblk
        pcq = jnp.concatenate(pcs + ([jnp.zeros((fill, tq), F32)] if fill else []), axis=0).T.astype(BF16)
        vch = vcb_ref[0].astype(BF16)
        zblk = jnp.zeros((nblk, LANE), BF16)
        vdiag = jnp.concatenate(
            [jnp.concatenate([vch if c == h else zblk for c in range(H_B)], axis=1)
             for h in range(H_B)] + ([jnp.zeros((fill, H_B * LANE), BF16)] if fill else []), axis=0)
        o_cmp = _mm(pcq, vdiag)

        rank = jnp.zeros((nblk, tq), F32)
        for mrow in range(nblk):
            cm = imp[mrow:mrow + 1, :]
            ahead = (cm > imp) | ((cm == imp) & (mrow < blk_s))
            rank = rank + ahead.astype(F32)
        sel_t = (((rank < n_top) & (imp >= 0.0)) | (blk_s == cur_l)).astype(F32)
        selq = jnp.concatenate([sel_t, jnp.zeros((LANE - nblk, tq), F32)], axis=0).T.astype(BF16)

        m_scr[...] = jnp.full(m_scr.shape, NEG, F32)
        l_scr[...] = jnp.zeros(l_scr.shape, F32)
        acc_scr[...] = jnp.zeros(acc_scr.shape, F32)
        nk = _div(qi * tq + tq + tk - 1, tk)

        def chunk(c, carry):
            ks_ = pl.ds(pl.multiple_of(c * tk, tk), tk)
            kidx = c * tk + lax.broadcasted_iota(jnp.int32, (LANE, tk), 1)
            e = (_div(kidx, CMP_BLOCK) == lax.broadcasted_iota(jnp.int32, (LANE, tk), 0)).astype(BF16)
            kpos = c * tk + lax.broadcasted_iota(jnp.int32, (tq, tk), 1)
            mk = (_mm(selq, e) > 0.5) & (kpos <= pos)
            mask = jnp.concatenate([mk] * H_B, axis=0)
            s = jnp.where(mask, _nt(qb, kv_ref[0, ks_, 0:128]), NEG)
            _online_update(s, kv_ref[0, ks_, 128:256], m_scr, l_scr, acc_scr, valid=mask)
            return carry

        lax.fori_loop(0, nk, chunk, 0)

        start = jnp.clip(qi * tq + tq - span, 0, t - span)
        ws = pl.ds(pl.multiple_of(start, tq), span)
        diff = pos - (start + lax.broadcasted_iota(jnp.int32, (tq, span), 1))
        wmask = jnp.concatenate([(diff >= 0) & (diff < WINDOW)] * H_B, axis=0)
        s = jnp.where(wmask, _nt(qb, kv_ref[0, ws, 256:384]), NEG)
        p = jnp.where(wmask, jnp.exp(s - jnp.max(s, axis=-1, keepdims=True)), 0.0)
        o_win = _mm(p.astype(BF16), kv_ref[0, ws, 384:512]) \
            / jnp.maximum(jnp.sum(p, axis=-1, keepdims=True), 1e-30)
        o_sel = acc_scr[...] / jnp.maximum(l_scr[...], 1e-30)

        gate = _sigmoid(bg_ref[0])
        for h in range(H_B):
            rows = slice(h * tq, (h + 1) * tq)
            ob = (gate[:, 3 * h:3 * h + 1] * o_cmp[:, h * LANE:(h + 1) * LANE]
                  + gate[:, 3 * h + 1:3 * h + 2] * o_sel[rows]
                  + gate[:, 3 * h + 2:3 * h + 3] * o_win[rows])
            y = ob * _silu(bz_ref[0, :, h * LANE:(h + 1) * LANE])
            o_ref[0, :, h * LANE:(h + 1) * LANE] = y.astype(BF16)

    full = lambda rows, w: pl.BlockSpec((1, rows, w), lambda b, i: (b, 0, 0))
    return pl.pallas_call(
        body, grid=(b_, t // tq),
        in_specs=[pl.BlockSpec((1, tq, 512), lambda b, i: (b, i, _UNIT["bq"] // 4)),
                  full(nblk, LANE), full(nblk, LANE), full(t, 512),
                  pl.BlockSpec((1, tq, LANE), lambda b, i: (b, i, _UNIT["bg"])),
                  pl.BlockSpec((1, tq, 512), lambda b, i: (b, i, _UNIT["bz"] // 4)),
                  pl.BlockSpec((4, LANE), lambda b, i: (0, 0))],
        out_specs=pl.BlockSpec((1, tq, 512), lambda b, i: (b, i, 0)),
        out_shape=jax.ShapeDtypeStruct((b_, t, 512), BF16),
        scratch_shapes=[pltpu.VMEM((H_B * tq, LANE), F32)] * 3,
        compiler_params=_params("parallel", "arbitrary"),
        name="b_attn_prompt")(proj3, kcb, vcb, kvb, proj3, proj3, g4)


def _b_pool_pages(page_table, cache2, wpb2, page_base, *, pg):
    db, n_pages = page_table.shape
    ng = n_pages // pg
    per = PAGE // CMP_BLOCK

    def body(pt_ref, *refs):
        pages = refs[:pg]
        wp_ref, o_ref = refs[pg:]
        rows = []
        for j in range(pg):
            for u in range(per):
                parts = []
                for kind in range(2):
                    x = pages[j][pl.ds(u * CMP_BLOCK * 4 + kind, CMP_BLOCK, stride=4), :]
                    parts.append(jnp.sum(x * wp_ref[:, kind * LANE:(kind + 1) * LANE], axis=0, keepdims=True))
                rows.append(jnp.concatenate(parts, axis=1))
        o_ref[0] = jnp.concatenate(rows, axis=0)

    def page_spec(j):
        return pl.BlockSpec((PAGE * 4, LANE), lambda b, g, pt, j=j: (page_base + pt[b, g * pg + j], 0))

    grid_spec = pltpu.PrefetchScalarGridSpec(
        num_scalar_prefetch=1, grid=(db, ng),
        in_specs=[page_spec(j) for j in range(pg)] + [
            pl.BlockSpec((CMP_BLOCK, 256), lambda b, g, pt: (0, 0))],
        out_specs=pl.BlockSpec((1, pg * per, 256), lambda b, g, pt: (b, g, 0)))
    return pl.pallas_call(
        body, grid_spec=grid_spec,
        out_shape=jax.ShapeDtypeStruct((db, n_pages * per, 256), F32),
        compiler_params=_params("parallel", "parallel"),
        name="b_pool_pages")(page_table, *([cache2] * pg), wpb2)


def _b_select_sample(proj3, pooled, win, neww, g4, *, past_len, n_valid):
    db = neww.shape[0]
    tp = neww.shape[1] // 2
    nblk = pooled.shape[1]
    n_top = min(N_SEL, nblk + 1)
    wb = win.shape[1] // 2
    rows = H_B * tp
    scale = HEAD ** -0.5

    def body(q_ref, pool_ref, win_ref, nw_ref, bg_ref, g_ref, qs_ref, ocw_ref, g1_ref, sel_ref):
        qf = jnp.concatenate(
            [_rms(q_ref[0, :, h * LANE:(h + 1) * LANE], g_ref[0:1, :]) * scale for h in range(H_B)],
            axis=0)
        qb = qf.astype(BF16)
        qs_ref[0] = qb
        tok = _mod(lax.broadcasted_iota(jnp.int32, (rows, 1), 0), tp)
        pos = past_len + tok
        cur = _div(pos, CMP_BLOCK)
        blk = lax.broadcasted_iota(jnp.int32, (rows, nblk), 1)
        cmask = blk < cur

        kc = _rms(pool_ref[0, :, 0:128], g_ref[1:2, :])
        s = jnp.where(cmask, _cmp_scores(qf, kc), NEG)
        mx = jnp.max(s, axis=-1, keepdims=True)
        p = jnp.where(cmask, jnp.exp(s - mx), 0.0)
        pc = p / jnp.maximum(jnp.sum(p, axis=-1, keepdims=True), 1e-30)
        o_cmp = _mm(pc.astype(BF16), pool_ref[0, :, 128:256].astype(BF16))
        imp = pc[0:tp]
        for h in range(1, H_B):
            imp = imp + pc[h * tp:(h + 1) * tp]
        imp = jnp.where(cmask[0:tp], imp, -1.0)

        pad = jnp.concatenate([imp, jnp.zeros((LANE - tp, nblk), F32)], axis=0)
        imp_t = jnp.concatenate([pad[:, u * LANE:(u + 1) * LANE].T for u in range(nblk // LANE)], axis=0)
        mi = lax.broadcasted_iota(jnp.int32, (nblk, nblk), 0)
        ni = lax.broadcasted_iota(jnp.int32, (nblk, nblk), 1)
        sels = []
        for tkn in range(tp):
            r = imp[tkn:tkn + 1, :]
            c = imp_t[:, tkn:tkn + 1]
            ahead = (c > r) | ((c == r) & (mi < ni))
            rank = jnp.sum(ahead.astype(F32), axis=0, keepdims=True)
            sels.append(((rank < n_top) & (r >= 0.0)).astype(F32))
        sel_ref[0] = jnp.concatenate(sels, axis=0)

        kw = win_ref[0, pl.ds(0, wb, stride=2), :].astype(BF16)
        vw = win_ref[0, pl.ds(1, wb, stride=2), :].astype(BF16)
        zpad = jnp.zeros((LANE - tp, LANE), F32)
        kn = jnp.concatenate([nw_ref[0, pl.ds(0, tp, stride=2), :], zpad], axis=0).astype(BF16)
        vn = jnp.concatenate([nw_ref[0, pl.ds(1, tp, stride=2), :], zpad], axis=0).astype(BF16)
        jw =lax.broadcasted_iota(jnp.int32, (rows, wb), 1)
        dw = pos - (past_len - wb + jw)
        jn = lax.broadcasted_iota(jnp.int32, (rows, LANE), 1)
        dn = tok - jn
        wmask = jnp.concatenate([(dw >= 0) & (dw < WINDOW), (dn >= 0) & (dn < WINDOW) & (jn < n_valid)],
                                axis=1)
        sw = jnp.where(wmask, jnp.concatenate([_nt(qb, kw), _nt(qb, kn)], axis=1), NEG)
        mw = jnp.max(sw, axis=-1, keepdims=True)
        pw = jnp.where(wmask, jnp.exp(sw - mw), 0.0)
        o_win = (_mm(pw[:, 0:wb].astype(BF16), vw) + _mm(pw[:, wb:].astype(BF16), vn)) \
            / jnp.maximum(jnp.sum(pw, axis=-1, keepdims=True), 1e-30)

        gate = _sigmoid(bg_ref[0])
        g0 = jnp.concatenate([gate[:, 3 * h:3 * h + 1] for h in range(H_B)], axis=0)
        g1 = jnp.concatenate([gate[:, 3 * h + 1:3 * h + 2] for h in range(H_B)], axis=0)
        g2 = jnp.concatenate([gate[:, 3 * h + 2:3 * h + 3] for h in range(H_B)], axis=0)
        ocw_ref[0] = g0 * o_cmp + g2 * o_win
        g1_ref[0] = jnp.broadcast_to(g1, (rows, LANE))

    per_b = lambda r, w: pl.BlockSpec((1, r, w), lambda b: (b, 0, 0))
    return pl.pallas_call(
        body, grid=(db,),
        in_specs=[pl.BlockSpec((1, tp, 512), lambda b: (b, 0, _UNIT["bq"] // 4)),
                  per_b(nblk, 256), per_b(2 * wb, LANE), per_b(2 * tp, LANE),
                  pl.BlockSpec((1, tp, LANE), lambda b: (b, 0, _UNIT["bg"])),
                  pl.BlockSpec((4, LANE), lambda b: (0, 0))],
        out_specs=[per_b(rows, LANE), per_b(rows, LANE), per_b(rows, LANE), per_b(tp, nblk)],
        out_shape=[jax.ShapeDtypeStruct((db, rows, LANE), BF16),
                   jax.ShapeDtypeStruct((db, rows, LANE), F32),
                   jax.ShapeDtypeStruct((db, rows, LANE), F32),
                   jax.ShapeDtypeStruct((db, tp, nblk), F32)],
        compiler_params=_params("parallel"),
        name="b_select_sample")(proj3, pooled, win, neww, proj3, g4)


def _b_selected_sample(page_table, cache2, qs, selg, bkvs, ocw, g1b, proj3, page_base,
                       *, n_valid, pg):
    db, n_pages = page_table.shape
    ng = n_pages // pg
    rows = qs.shape[1]
    tp = rows // H_B
    per = PAGE // CMP_BLOCK
    nbg = pg * per
    width = pg * PAGE

    def body(pt_ref, *refs):
        pages = refs[:pg]
        q_ref, sel_ref, new_ref, ocw_ref, g1_ref, z_ref, o_ref, m_scr, l_scr, acc_scr = refs[pg:]
        gi = pl.program_id(1)

        @pl.when(gi == 0)
        def _():
            m_scr[...] = jnp.full(m_scr.shape, NEG, F32)
            l_scr[...] = jnp.zeros(l_scr.shape, F32)
            acc_scr[...] = jnp.zeros(acc_scr.shape, F32)

        q = q_ref[0]

        def update(s, mask, vs):
            m_old = m_scr[...]
            m_new = jnp.maximum(m_old, jnp.max(s, axis=-1, keepdims=True))
            alpha = jnp.exp(m_old - m_new)
            p = jnp.where(mask, jnp.exp(s - m_new), 0.0)
            pv = None
            for j, v in enumerate(vs):
                d = _mm(p[:, j * PAGE:(j + 1) * PAGE].astype(BF16), v)
                pv = d if pv is None else pv + d
            l_scr[...] = alpha * l_scr[...] + jnp.sum(p, axis=-1, keepdims=True)
            acc_scr[...] = alpha * acc_scr[...] + pv
            m_scr[...] = m_new

        e = (_div(lax.broadcasted_iota(jnp.int32, (nbg, width), 1), CMP_BLOCK)
             == lax.broadcasted_iota(jnp.int32, (nbg, width), 0)).astype(BF16)
        mk = _mm(sel_ref[0, 0].astype(BF16), e) > 0.5
        mask = jnp.concatenate([mk] * H_B, axis=0)
        ss, vs = [], []
        for j in range(pg):
            ss.append(_nt(q, pages[j][pl.ds(2, PAGE, stride=4), :].astype(BF16)))
            vs.append(pages[j][pl.ds(3, PAGE, stride=4), :].astype(BF16))
        update(jnp.where(mask, jnp.concatenate(ss, axis=1), NEG), mask, vs)

        @pl.when(gi == ng - 1)
        def _():
            zpad = jnp.zeros((PAGE - tp, LANE), F32)
            kn = jnp.concatenate([new_ref[0, pl.ds(2, tp, stride=4), :], zpad], axis=0).astype(BF16)
            vn = jnp.concatenate([new_ref[0, pl.ds(3, tp, stride=4), :], zpad], axis=0).astype(BF16)
            tok = _mod(lax.broadcasted_iota(jnp.int32, (rows, PAGE), 0), tp)
            col = lax.broadcasted_iota(jnp.int32, (rows, PAGE), 1)
            nmask = (col <= tok) & (col < n_valid)
            update(jnp.where(nmask, _nt(q, kn), NEG), nmask, [vn])
            o_sel = acc_scr[...] / jnp.maximum(l_scr[...], 1e-30)
            ob = ocw_ref[0] + g1_ref[0] * o_sel
            for h in range(H_B):
                y = ob[h * tp:(h + 1) * tp] * _silu(z_ref[0, :, h * LANE:(h + 1) * LANE])
                o_ref[0, :, h * LANE:(h + 1) * LANE] = y.astype(BF16)

    def page_spec(j):
        return pl.BlockSpec((PAGE * 4, LANE), lambda b, g, pt, j=j: (page_base + pt[b, g * pg + j], 0))

    per_b = lambda r, w: pl.BlockSpec((1, r, w), lambda b, g, pt: (b, 0, 0))
    grid_spec = pltpu.PrefetchScalarGridSpec(
        num_scalar_prefetch=1, grid=(db, ng),
        in_specs=[page_spec(j) for j in range(pg)] + [
            per_b(rows, LANE),
            pl.BlockSpec((1, 1, tp, nbg), lambda b, g, pt: (b, g, 0, 0)),
            per_b(4 * tp, LANE), per_b(rows, LANE), per_b(rows, LANE),
            pl.BlockSpec((1, tp, 512), lambda b, g, pt: (b, 0, _UNIT["bz"] // 4))],
        out_specs=per_b(tp, 512),
        scratch_shapes=[pltpu.VMEM((rows, 1), F32), pltpu.VMEM((rows, 1), F32),
                        pltpu.VMEM((rows, LANE), F32)])
    return pl.pallas_call(
        body, grid_spec=grid_spec,
        out_shape=jax.ShapeDtypeStruct((db, tp, 512), BF16),
        compiler_params=_params("parallel", "arbitrary"),
        name="b_selected_sample")(page_table, *([cache2] * pg), qs, selg, bkvs, ocw, g1b, proj3)


def _mlstm(proj3, conv_w, conv_b, gate_b, out_g, conv0, c0, n0, m0, *, lc, n_valid):
    b_, t, _ = proj3.shape
    col_head = jnp.arange(H_C * LANE)[None, :] // LANE
    lane_id = jnp.arange(LANE)[:, None]
    rsel = jnp.stack([lane_id == col_head, lane_id == col_head + H_C]).astype(BF16)
    nchunk = t // lc
    dqk = H_C * DQK_C
    tail = CONV_W - 1
    base = 8
    tsq = max(lc, LANE)

    def bcast_cols(x, sel):
        x1, x2, x3 = _split3(x)
        return _mm(x1, sel) + _mm(x2, sel) + _mm(x3, sel)

    def body(qk_ref, v_ref, if_ref, co_ref, cz_ref, cw_ref, cb_ref, gb_ref, g_ref, rsel_ref,
             conv0_ref, c0_ref, n0_ref, m0_ref,
             y_ref, conv_ref, c_ref, n_ref, m_ref, xbuf, c_scr, n_scr, m_scr):
        ci = pl.program_id(1)

        @pl.when(ci == 0)
        def _():
            xbuf[base - tail:base, :] = conv0_ref[0]
            c_scr[...] = c0_ref[0]
            n_scr[...] = n0_ref[0]
            m_scr[...] = m0_ref[0]

        xbuf[base:base + lc, :] = qk_ref[0]
        acc = cb_ref[...]
        for j in range(CONV_W):
            acc = acc + xbuf[base - tail + j:base - tail + j + lc, :] * cw_ref[j:j + 1, :]
        new_tail = xbuf[base + n_valid - tail:base + n_valid, :]
        xbuf[base - tail:base, :] = new_tail
        conv_ref[0] = new_tail
        qk = _silu(acc)

        gt = if_ref[0] + gb_ref[...]
        lf = jnp.minimum(gt, 0.0) - jnp.log(1.0 + jnp.exp(-jnp.abs(gt)))
        ti = lax.broadcasted_iota(jnp.int32, (lc, lc), 0)
        si = lax.broadcasted_iota(jnp.int32, (lc, lc), 1)
        tri = (si <= ti).astype(BF16)
        l1, l2, l3 = _split3(lf)
        bcum = _mm(tri, l1) + _mm(tri, l2) + _mm(tri, l3)
        zrow = jnp.zeros((tsq - lc, LANE), F32)
        gt_t = (jnp.concatenate([gt, zrow], axis=0) if tsq > lc else gt).T
        b_t = (jnp.concatenate([bcum, zrow], axis=0) if tsq > lc else bcum).T
        dmask = (si <= ti) & (si < n_valid)
        svalid = lax.broadcasted_iota(jnp.int32, (lc, LANE), 0) < n_valid
        icol_all = bcast_cols(gt, rsel_ref[0])
        bcol_all = bcast_cols(bcum, rsel_ref[1])
        gate_all = _sigmoid(co_ref[0]) * _silu(cz_ref[0])
        low = lax.broadcasted_iota(jnp.int32, (lc, LANE), 1) < DQK_C
        row_low = lax.broadcasted_iota(jnp.int32, (LANE, LANE), 0) < DQK_C

        c_new, n_new, m_new_all = [], [], []
        for j in range(H_C // 2):
            qp = qk[:, j * LANE:(j + 1) * LANE] * (DQK_C ** -0.5)
            kp = qk[:, dqk + j * LANE:dqk + (j + 1) * LANE]
            cp = c_scr[j]
            cpb = cp.astype(BF16)
            npair = n_scr[j:j + 1, :]
            c_upd, n_upd, carries = None, None, []
            for u in range(2):
                h = 2 * j + u
                sel = low if u == 0 else jnp.logical_not(low)
                qm = jnp.where(sel, qp, 0.0)
                km = jnp.where(sel, kp, 0.0)
                qmb, kmb = qm.astype(BF16), km.astype(BF16)
                vh = v_ref[0, :, h * HEAD:(h + 1) * HEAD]
                bcol = bcol_all[:, h * LANE:(h + 1) * LANE]
                icol = icol_all[:, h * LANE:(h + 1) * LANE]
                brow = b_t[H_C + h:H_C + h + 1, 0:lc]
                irow = gt_t[h:h + 1, 0:lc]
                m_h = m_scr[h:h + 1, :]

                d = jnp.where(dmask, bcol[:, 0:lc] - brow + irow, NEG)
                inter = bcol + m_h
                m_t = jnp.maximum(inter, jnp.max(d, axis=1, keepdims=True))
                w_intra = jnp.where(dmask, jnp.exp(d - m_t[:, 0:lc]), 0.0)
                w_inter = jnp.exp(inter - m_t)
                sqk = _nt(qmb, kmb) * w_intra
                num = w_inter * _mm(qmb, cpb) + _mm(sqk.astype(BF16), vh.astype(BF16))
                den = (w_inter * jnp.sum(qm * npair, axis=1, keepdims=True)
                       + jnp.sum(sqk, axis=1, keepdims=True))
                hh = num / jnp.maximum(jnp.abs(den), jnp.exp(-m_t))
                y_ref[0, :, h * HEAD:(h + 1) * HEAD] = (
                    _rms(hh, g_ref[...]) * gate_all[:, h * HEAD:(h + 1) * HEAD]).astype(BF16)

                b_last = bcol[n_valid - 1:n_valid, :]
                dec = jnp.where(svalid, b_last - bcol + icol, NEG)
                m_new = jnp.maximum(b_last + m_h, jnp.max(dec, axis=0, keepdims=True))
                wk = jnp.where(svalid, jnp.exp(dec - m_new), 0.0)
                carries.append(jnp.exp(b_last + m_h - m_new))
                cu = _tn(kmb, (wk * vh).astype(BF16))
                nu = jnp.sum(wk * km, axis=0, keepdims=True)
                c_upd = cu if c_upd is None else c_upd + cu
                n_upd = nu if n_upd is None else n_upd + nu
                m_new_all.append(m_new)
            c_new.append(jnp.where(row_low, carries[0], carries[1]) * cp + c_upd)
            n_new.append(jnp.where(low[0:1], carries[0], carries[1]) * npair + n_upd)

        for j in range(H_C // 2):
            c_scr[j] = c_new[j]
            n_scr[j:j + 1, :] = n_new[j]
        for h in range(H_C):
            m_scr[h:h + 1, :] = m_new_all[h]
        c_ref[0] = c_scr[...]
        n_ref[0] = n_scr[...]
        m_ref[0] = m_scr[...]

    col = lambda u: pl.BlockSpec((1, lc, 1024), lambda b, c, u=u: (b, c, u))
    const = lambda *shape: pl.BlockSpec(shape, lambda b, c: (0,) * len(shape))
    per_b = lambda *shape: pl.BlockSpec((1,) + shape, lambda b, c: (b,) + (0,) * len(shape))
    return pl.pallas_call(
        body, grid=(b_, nchunk),
        in_specs=[col(_UNIT["cqk"] // 8), col(_UNIT["cv"] // 8),
                  pl.BlockSpec((1, lc, LANE), lambda b, c: (b, c, _UNIT["cif"])),
                  col(_UNIT["co"] // 8), col(_UNIT["cz"] // 8),
                  const(CONV_W, 1024), const(1, 1024), const(1, LANE), const(1, LANE),
                  const(2, LANE, H_C * LANE),
                  per_b(tail, 1024), per_b(H_C // 2, LANE, HEAD), per_b(H_C // 2, LANE), per_b(H_C, LANE)],
        out_specs=[pl.BlockSpec((1, lc, 1024), lambda b, c: (b, c, 0)),
                   per_b(tail, 1024), per_b(H_C // 2, LANE, HEAD), per_b(H_C // 2, LANE), per_b(H_C, LANE)],
        out_shape=[jax.ShapeDtypeStruct((b_, t, 1024), BF16),
                   jax.ShapeDtypeStruct((b_, tail, 1024), F32),
                   jax.ShapeDtypeStruct((b_, H_C // 2, LANE, HEAD), F32),
                   jax.ShapeDtypeStruct((b_, H_C // 2, LANE), F32),
                   jax.ShapeDtypeStruct((b_, H_C, LANE), F32)],
        scratch_shapes=[pltpu.VMEM((base + lc, 1024), F32), pltpu.VMEM((H_C // 2, LANE, HEAD), F32),
                        pltpu.VMEM((H_C // 2, LANE), F32), pltpu.VMEM((H_C, LANE), F32)],
        compiler_params=_params("parallel", "arbitrary"),
        name="mlstm")(proj3, proj3, proj3, proj3, proj3, conv_w, conv_b, gate_b, out_g, rsel,
                      conv0, c0, n0, m0)


def _w_prep(w_in, *, tk):
    depth, d, n_in = w_in.shape
    src_off, off = {}, 0
    for name, width in _SRC:
        src_off[name] = (off, width)
        off += width
    assert off == n_in

    def body(w_ref, o_ref):
        for name in _DST_ORDER:
            s0, width = src_off[name]
            for u in range(-(-width // LANE)):
                s = s0 + u * LANE
                valid = min(LANE, width - u * LANE)
                a = (s // LANE) * LANE
                win = w_ref[0, :, a:min(a + 2 * LANE, n_in)]
                x = win[:, s - a:s - a + valid]
                if valid < LANE:
                    x = jnp.concatenate([x, jnp.zeros((tk, LANE - valid), F32)], axis=1)
                dst = (_UNIT[name] + u) * LANE
                o_ref[0, :, dst:dst + LANE] = x.astype(BF16)

    return pl.pallas_call(
        body, grid=(depth, d // tk),
        in_specs=[pl.BlockSpec((1, tk, n_in), lambda l, i: (l, i, 0))],
        out_specs=pl.BlockSpec((1, tk, N_PROJ), lambda l, i: (l, i, 0)),
        out_shape=jax.ShapeDtypeStruct((depth, d, N_PROJ), BF16),
        compiler_params=_params("parallel", "parallel"),
        name="w_prep")(w_in)


def _c_from_pairs(c):
    b = c.shape[0]
    return jnp.swapaxes(c.reshape(b, H_C, DQK_C, HEAD), -1, -2)


def _pick(n, prefs):
    for p in prefs:
        if n % p == 0:
            return p
    return n


def kernel(x_prompt, x_sample, cache_a_kv, cache_b_kv, state_b_win, state_c_conv, state_c_C,
           state_c_n, state_c_m, page_table, norm_g, w_in, w_out, a_qk_g, a_lambda, a_out_g,
           b_qk_g, b_cmp_w, c_conv_w, c_conv_b, c_gate_b, c_out_g):
    bp, t, d = x_prompt.shape
    db, ts, _ = x_sample.shape
    depth = norm_g.shape[0]
    n_pool = cache_a_kv.shape[1]
    n_pages = page_table.shape[1]
    past_len = n_pages * PAGE
    wb = state_b_win.shape[2]
    tp = 8
    assert ts <= tp and t % MLSTM_CHUNK == 0 and t % Q_BLOCK == 0 and d == 2048
    mp, ms = bp * t, db * tp
    pg = _pick(n_pages, (16, 8, 4, 2, 1))

    cache_a2 = cache_a_kv.reshape(depth * n_pool * PAGE * 2 * H_A, HEAD)
    cache_b2 = cache_b_kv.reshape(depth * n_pool * PAGE * 4, HEAD)
    win_rows = state_b_win.reshape(depth, db, wb * 2, HEAD)
    hp = x_prompt.reshape(mp, d)
    hs = jnp.pad(x_sample, ((0, 0), (0, tp - ts), (0, 0))).reshape(ms, d)

    tm_p = _pick(mp, (1024, 512, 256, 128))
    tq_a = _pick(t, (512, 256, 128))
    outs_p = [[] for _ in range(7)]
    outs_s = [[] for _ in range(7)]

    w_all = _w_prep(w_in, tk=_pick(d, (256, 128)))
    for l in range(depth):
        lam_init = 0.8 - 0.6 * math.exp(-0.3 * l)
        w_o = w_out[l].astype(BF16)
        g_a = jnp.tile(a_qk_g[l], (1, 2))
        g_b = b_qk_g[l]
        wpb = jnp.broadcast_to(b_cmp_w[l][:, :, None], (2, CMP_BLOCK, LANE))
        wpb2 = jnp.concatenate([wpb[0], wpb[1]], axis=1)
        gate_b = jnp.pad(c_gate_b[l].reshape(1, 2 * H_C), ((0, 0), (0, LANE - 2 * H_C)))
        conv_b = c_conv_b[l].reshape(1, -1)
        out_gc = c_out_g[l].reshape(1, LANE)

        proj = _norm_matmul(hp, norm_g[l], w_all, l, tm=tm_p, tn=1024)
        proj3 = proj.reshape(bp, t, N_PROJ)
        qn, kn, vb, akv = _a_prep(proj, g_a, tm=_pick(mp, (512, 256, 128)))
        ya = _a_attn_prompt(qn.reshape(bp, t, 512), kn.reshape(bp, t, 512), vb.reshape(bp, t, 512),
                            proj3, a_lambda[l], a_out_g[l], lam_init, tq=tq_a)
        bkvs, neww, kvb, kcb, vcb = _b_prep(proj, g_b, wpb, tm=_pick(mp, (512, 256, 128)), pool=True)
        nblk = t // CMP_BLOCK
        yb = _b_attn_prompt(proj3, kcb.reshape(bp, nblk, LANE), vcb.reshape(bp, nblk, LANE),
                            kvb.reshape(bp, t, 512), g_b, t=t)
        yc, conv_p, c_p, n_p, m_p = _mlstm(
            proj3, c_conv_w[l], conv_b, gate_b, out_gc,
            jnp.zeros((bp, CONV_W - 1, 2 * H_C * DQK_C), F32), jnp.zeros((bp, H_C // 2, LANE, HEAD), F32),
            jnp.zeros((bp, H_C // 2, LANE), F32), jnp.zeros((bp, H_C, LANE), F32),
            lc=MLSTM_CHUNK, n_valid=MLSTM_CHUNK)
        c_p, n_p = _c_from_pairs(c_p), n_p.reshape(bp, H_C, DQK_C)
        hp = _out_proj(hp, ya.reshape(mp, 512), yb.reshape(mp, 512), yc.reshape(mp, 1024), w_o,
                       tm=tm_p, tn=1024)
        win_p = jnp.concatenate([jnp.zeros((bp, wb, 2, HEAD), F32), neww.reshape(bp, t, 2, HEAD)],
                                axis=1)[:, -wb:]
        for lst, a in zip(outs_p, (akv.reshape(bp, t, 2, H_A, HEAD), bkvs.reshape(bp, t, 4, HEAD),
                                   win_p, conv_p, c_p, n_p, m_p[:, :, 0])):
            lst.append(a)

        base = l * n_pool
        sproj = _norm_matmul(hs, norm_g[l], w_all, l, tm=ms, tn=1024)
        sproj3 = sproj.reshape(db, tp, N_PROJ)
        sqn, skn, svb, sakv = _a_prep(sproj, g_a, tm=ms)
        kvnew = jnp.concatenate([skn.reshape(db, tp, 512), svb.reshape(db, tp, 512)], axis=2)
        kvnew = jnp.pad(kvnew, ((0, 0), (0, PAGE - tp), (0, 0)))
        sya = _a_attn_sample(page_table, cache_a2, sqn.reshape(db, tp, 512), kvnew, sproj3,
                             a_lambda[l], a_out_g[l], lam_init, base, n_valid=ts, pg=pg)
        sbkvs, sneww, _ = _b_prep(sproj, g_b, wpb, tm=ms, pool=False)
        pooled = _b_pool_pages(page_table, cache_b2, wpb2, base, pg=pg)
        qs, ocw, g1b, sel = _b_select_sample(
            sproj3, pooled, win_rows[l], sneww.reshape(db, tp * 2, HEAD), g_b,
            past_len=past_len, n_valid=ts)
        nbg = pg * (PAGE // CMP_BLOCK)
        selg = sel.reshape(db, tp, n_pages // pg, nbg).transpose(0, 2, 1, 3)
        syb = _b_selected_sample(page_table, cache_b2, qs, selg, sbkvs.reshape(db, tp * 4, HEAD), ocw, g1b,
                                 sproj3, base, n_valid=ts, pg=pg)
        m0 = jnp.broadcast_to(state_c_m[l][:, :, None], (db, H_C, LANE))
        syc, conv_s, c_s, n_s, m_s = _mlstm(
            sproj3, c_conv_w[l], conv_b, gate_b, out_gc, state_c_conv[l],
            jnp.swapaxes(state_c_C[l], -1, -2).reshape(db, H_C // 2, LANE, HEAD),
            state_c_n[l].reshape(db, H_C // 2, LANE), m0, lc=tp, n_valid=ts)
        c_s, n_s = _c_from_pairs(c_s), n_s.reshape(db, H_C, DQK_C)
        hs = _out_proj(hs, sya.reshape(ms, 512), syb.reshape(ms, 512), syc.reshape(ms, 1024), w_o,
                       tm=ms, tn=1024)
        win_s = jnp.concatenate([state_b_win[l], sneww.reshape(db, tp, 2, HEAD)[:, :ts]], axis=1)[:, -wb:]
        for lst, a in zip(outs_s, (sakv.reshape(db, tp, 2, H_A, HEAD)[:, :ts],
                                   sbkvs.reshape(db, tp, 4, HEAD)[:, :ts],
                                   win_s, conv_s, c_s, n_s, m_s[:, :, 0])):
            lst.append(a)

    y_p = hp.reshape(bp, t, d)
    y_s = hs.reshape(db, tp, d)[:, :ts]
    sp = [jnp.stack(x) for x in outs_p]
    ss = [jnp.stack(x) for x in outs_s]
    return (y_p, y_s, sp[0], ss[0], sp[1], ss[1], sp[2], ss[2], sp[3], ss[3],
            sp[4], ss[4], sp[5], ss[5], sp[6], ss[6])
```

```python
import functools
import math

import jax
import jax.numpy as jnp
from jax import lax
from jax.experimental import pallas as pl
from jax.experimental.pallas import tpu as pltpu

F32 = jnp.float32
BF16 = jnp.bfloat16

EPS = 1e-6
LANE = 128
HEAD = 128
H_A = 4
DQ_A = HEAD // 2
H_B = 4
H_C = 8
DQK_C = HEAD // 2
CMP_BLOCK = 64
N_SEL = 16
WINDOW = 512
CONV_W = 4
MLSTM_CHUNK = 64
PAGE = 128
Q_BLOCK = 128
NEG = -1e30
VMEM_LIMIT = 56 * 1024 * 1024

_SRC = (("aq", 512), ("ak", 512), ("av", 512), ("az", 512),
        ("bq", 512), ("bkv", 768), ("bg", 12), ("bz", 512),
        ("cqk", 1024), ("cv", 1024), ("cif", 16), ("co", 1024), ("cz", 1024))
_DST_ORDER = ("aq", "ak", "av", "az", "bq", "bz", "bkv", "bg", "cif", "cqk", "cv", "co", "cz")
_UNIT = {"aq": 0, "ak": 4, "av": 8, "az": 12, "bq": 16, "bz": 20, "bkv": 24, "bg": 30,
         "cif": 31, "cqk": 32, "cv": 40, "co": 48, "cz": 56}
N_PROJ = 64 * LANE


def _nt(a, b):
    return lax.dot_general(a, b, (((1,), (1,)), ((), ())), preferred_element_type=F32)


def _tn(a, b):
    return lax.dot_general(a, b, (((0,), (0,)), ((), ())), preferred_element_type=F32)


def _mm(a, b):
    return jnp.dot(a, b, preferred_element_type=F32)


def _split2(x):
    hi = x.astype(BF16)
    lo = (x - hi.astype(F32)).astype(BF16)
    return hi, lo


def _split3(x):
    hi = x.astype(BF16)
    r = x - hi.astype(F32)
    mid = r.astype(BF16)
    lo = (r - mid.astype(F32)).astype(BF16)
    return hi, mid, lo


def _sigmoid(z):
    return 1.0 / (1.0 + jnp.exp(-z))


def _silu(z):
    return z * _sigmoid(z)


def _rms(x, g):
    return x * lax.rsqrt(jnp.mean(x * x, axis=-1, keepdims=True) + EPS) * g


def _div(x, n):
    return lax.shift_right_logical(x, int(math.log2(n)))


def _mod(x, n):
    return x & (n - 1)


def _online_update(s, v, m_ref, l_ref, acc_ref, valid=None, pv_fn=None):
    w = s.shape[1]
    m_old = m_ref[...]
    m_new = jnp.maximum(m_old, jnp.max(s, axis=-1, keepdims=True))
    alpha = jnp.exp(m_old - m_new)
    m_rep = m_new if w == LANE else jnp.concatenate([m_new] * (w // LANE), axis=1)
    p = jnp.exp(s - m_rep)
    if valid is not None:
        p = jnp.where(valid, p, 0.0)
    l_ref[...] = alpha * l_ref[...] + jnp.sum(p, axis=-1, keepdims=True)
    pb = p.astype(BF16)
    acc_ref[...] = alpha * acc_ref[...] + (_mm(pb, v) if pv_fn is None else pv_fn(pb))
    m_ref[...] = m_new


def _params(*sem):
    return pltpu.CompilerParams(dimension_semantics=sem, vmem_limit_bytes=VMEM_LIMIT)


def _norm_matmul(x, g, w, layer, *, tm, tn):
    m, d = x.shape
    n = w.shape[1]
    rc = min(tm, 256)

    def body(x_ref, g_ref, w_ref, o_ref, h_scr):
        @pl.when(pl.program_id(1) == 0)
        def _():
            def chunk(c, carry):
                r = pl.ds(pl.multiple_of(c * rc, rc), rc)
                h_scr[r, :] = _rms(x_ref[r, :], g_ref[...]).astype(BF16)
                return carry
            lax.fori_loop(0, tm // rc, chunk, 0)
        o_ref[...] = _nt(h_scr[...], w_ref[0])

    return pl.pallas_call(
        body, grid=(m // tm, n // tn),
        in_specs=[pl.BlockSpec((tm, d), lambda i, j: (i, 0)),
                  pl.BlockSpec((1, d), lambda i, j: (0, 0)),
                  pl.BlockSpec((1, tn, d), lambda i, j: (layer, j, 0))],
        out_specs=pl.BlockSpec((tm, tn), lambda i, j: (i, j)),
        out_shape=jax.ShapeDtypeStruct((m, n), F32),
        scratch_shapes=[pltpu.VMEM((tm, d), BF16)],
        compiler_params=_params("parallel", "arbitrary"),
        name="norm_matmul")(x, g.reshape(1, d), w)


def _out_proj(x, ya, yb, yc, w, *, tm, tn):
    m, d = x.shape
    da, db, dc = ya.shape[1], yb.shape[1], yc.shape[1]

    def body(x_ref, a_ref, b_ref, c_ref, wa_ref, wb_ref, wc_ref, o_ref):
        acc = _mm(a_ref[...], wa_ref[...])
        acc += _mm(b_ref[...], wb_ref[...])
        acc += _mm(c_ref[...], wc_ref[...])
        o_ref[...] = x_ref[...] + acc

    return pl.pallas_call(
        body, grid=(m // tm, d // tn),
        in_specs=[pl.BlockSpec((tm, tn), lambda i, j: (i, j)),
                  pl.BlockSpec((tm, da), lambda i, j: (i, 0)),
                  pl.BlockSpec((tm, db), lambda i, j: (i, 0)),
                  pl.BlockSpec((tm, dc), lambda i, j: (i, 0)),
                  pl.BlockSpec((da, tn), lambda i, j: (0, j)),
                  pl.BlockSpec((db, tn), lambda i, j: (da // db, j)),
                  pl.BlockSpec((dc, tn), lambda i, j: ((da + db) // dc, j))],
        out_specs=pl.BlockSpec((tm, tn), lambda i, j: (i, j)),
        out_shape=jax.ShapeDtypeStruct((m, d), F32),
        compiler_params=_params("parallel", "parallel"),
        name="out_proj")(x, ya, yb, yc, w, w, w)


def _layer_slab(rows_per_layer, block_rows, stack):
    layer, depth, prev = stack if stack is not None else (0, 1, None)
    nblk = rows_per_layer // block_rows
    spec = pl.BlockSpec((block_rows, LANE), lambda i: (layer * nblk + i, 0))
    shape = jax.ShapeDtypeStruct((depth * rows_per_layer, LANE), F32)
    return spec, shape, prev


def _a_prep(proj, g2, *, tm, stack=None):
    m = proj.shape[0]
    akv_spec, akv_shape, prev = _layer_slab(m * 2 * H_A, tm * 2 * H_A, stack)

    def body(q_ref, k_ref, v_ref, g_ref, *rest):
        qn_ref, kn_ref, vb_ref, akv_ref = rest[-4:]
        r = _div(lax.broadcasted_iota(jnp.int32, (LANE, LANE), 0), DQ_A)
        c = _div(lax.broadcasted_iota(jnp.int32, (LANE, LANE), 1), DQ_A)
        seg = (r == c).astype(BF16)

        def segnorm(x, g):
            outs = []
            for u in range(4):
                xc = x[:, u * LANE:(u + 1) * LANE]
                hi, lo = _split2(xc * xc)
                s = _mm(hi, seg) + _mm(lo, seg)
                outs.append(xc * lax.rsqrt(s * (1.0 / DQ_A) + EPS) * g)
            return jnp.concatenate(outs, axis=1)

        qn = segnorm(q_ref[...], g_ref[0:1, :]) * (DQ_A ** -0.5)
        kn = segnorm(k_ref[...], g_ref[1:2, :])
        v = v_ref[...]
        qn_ref[...] = qn.astype(BF16)
        kn_ref[...] = kn.astype(BF16)
        vb_ref[...] = v.astype(BF16)
        for h in range(H_A):
            akv_ref[pl.ds(h, tm, stride=2 * H_A), :] = kn[:, h * LANE:(h + 1) * LANE]
            akv_ref[pl.ds(H_A + h, tm, stride=2 * H_A), :] = v[:, h * LANE:(h + 1) * LANE]

    blk = lambda u: pl.BlockSpec((tm, 512), lambda i, u=u: (i, u))
    row = lambda w: pl.BlockSpec((tm, w), lambda i: (i, 0))
    return pl.pallas_call(
        body, grid=(m // tm,),
        in_specs=[blk(_UNIT["aq"] // 4), blk(_UNIT["ak"] // 4), blk(_UNIT["av"] // 4),
                  pl.BlockSpec((2, LANE), lambda i: (0, 0))]
        + ([pl.BlockSpec(memory_space=pl.ANY)] if prev is not None else []),
        out_specs=[row(512), row(512), row(512), akv_spec],
        out_shape=[jax.ShapeDtypeStruct((m, 512), BF16)] * 3 + [akv_shape],
        input_output_aliases={4: 3} if prev is not None else {},
        compiler_params=_params("parallel"),
        name="a_prep")(proj, proj, proj, g2, *([prev] if prev is not None else []))


def _diff_lambda(lp, lam_init):
    a = jnp.sum(lp[0:1, :] * lp[1:2, :], axis=-1, keepdims=True)
    b = jnp.sum(lp[2:3, :] * lp[3:4, :], axis=-1, keepdims=True)
    return jnp.exp(a) - jnp.exp(b) + lam_init


def _a_finish(acc1, l1, acc2, l2, lam, g, z, lam_init):
    o = acc1 / l1 - lam * (acc2 / l2)
    return _rms(o, g) * (1.0 - lam_init) * _silu(z)


def _a_attn_prompt(qn, kn, vb, proj3, lam_p, out_g, lam_init, *, tq):
    b_, t, _ = qn.shape
    tk = tq

    def body(q_ref, k_ref, v_ref, z_ref, lam_ref, g_ref, o_ref, m_scr, l_scr, acc_scr):
        qi = pl.program_id(2)
        q = q_ref[0]
        lane = lax.broadcasted_iota(jnp.int32, (tq, LANE), 1)
        zero = jnp.zeros_like(q)
        qs = (jnp.where(lane < DQ_A, q, zero), jnp.where(lane >= DQ_A, q, zero))
        m_scr[...] = jnp.full(m_scr.shape, NEG, F32)
        l_scr[...] = jnp.zeros(l_scr.shape, F32)
        acc_scr[...] = jnp.zeros(acc_scr.shape, F32)

        def chunk(kc, mask):
            ks = pl.ds(pl.multiple_of(kc * tk, tk), tk)
            k = k_ref[0, ks, :]
            v = v_ref[0, ks, :]
            for c in range(2):
                s = _nt(qs[c], k)
                if mask is not None:
                    s = jnp.where(mask, s, NEG)
                _online_update(s, v, m_scr.at[c], l_scr.at[c], acc_scr.at[c])

        def full_chunk(kc, carry):
            chunk(kc, None)
            return carry

        lax.fori_loop(0, qi, full_chunk, 0)
        chunk(qi, lax.broadcasted_iota(jnp.int32, (tq, tk), 1)
              <= lax.broadcasted_iota(jnp.int32, (tq, tk), 0))
        lam = _diff_lambda(lam_ref[...], lam_init)
        y = _a_finish(acc_scr[0], l_scr[0], acc_scr[1], l_scr[1], lam, g_ref[...], z_ref[0], lam_init)
        o_ref[0] = y.astype(BF16)

    head = lambda rows, qdep: pl.BlockSpec(
        (1, rows, LANE), (lambda b, h, i: (b, i, h)) if qdep else (lambda b, h, i: (b, 0, h)))
    return pl.pallas_call(
        body, grid=(b_, H_A, t // tq),
        in_specs=[head(tq, True), head(t, False), head(t, False),
                  pl.BlockSpec((1, tq, LANE), lambda b, h, i: (b, i, _UNIT["az"] + h)),
                  pl.BlockSpec((4, DQ_A), lambda b, h, i: (0, 0)),
                  pl.BlockSpec((1, LANE), lambda b, h, i: (0, 0))],
        out_specs=head(tq, True),
        out_shape=jax.ShapeDtypeStruct((b_, t, 512), BF16),
        scratch_shapes=[pltpu.VMEM((2, tq, LANE), F32)] * 3,
        compiler_params=_params("parallel", "parallel", "arbitrary"),
        name="a_attn_prompt")(qn, kn, vb, proj3, lam_p, out_g.reshape(1, LANE))


def _a_attn_sample(page_table, cache2, qn, kvnew, proj3, lam_p, out_g, lam_init, page_base,
                   *, n_valid, pg):
    db, n_pages = page_table.shape
    tp = qn.shape[1]
    ng = n_pages // pg

    def body(pt_ref, *refs):
        pages = refs[:pg]
        q_ref, new_ref, z_ref, lam_ref, g_ref, o_ref, m_scr, l_scr, acc_scr = refs[pg:]
        gi = pl.program_id(1)

        @pl.when(gi == 0)
        def _():
            m_scr[...] = jnp.full(m_scr.shape, NEG, F32)
            l_scr[...] = jnp.zeros(l_scr.shape, F32)
            acc_scr[...] = jnp.zeros(acc_scr.shape, F32)

        lane = lax.broadcasted_iota(jnp.int32, (tp, LANE), 1)

        def qstack(h):
            q = q_ref[0, :, h * LANE:(h + 1) * LANE]
            zero = jnp.zeros_like(q)
            return jnp.concatenate([jnp.where(lane < DQ_A, q, zero),
                                    jnp.where(lane >= DQ_A, q, zero)], axis=0)

        def slot(s_):
            return jnp.concatenate([pages[j][pl.ds(s_, PAGE, stride=2 * H_A), :].astype(BF16)
                                    for j in range(pg)], axis=0)

        s = jnp.concatenate([_nt(qstack(h), slot(h)) for h in range(H_A)], axis=0)
        _online_update(s, None, m_scr, l_scr, acc_scr,
                       pv_fn=lambda p: jnp.concatenate(
                           [_mm(p[h * 2 * tp:(h + 1) * 2 * tp], slot(H_A + h)) for h in range(H_A)], axis=0))

        @pl.when(gi == ng - 1)
        def _():
            lam = _diff_lambda(lam_ref[...], lam_init)
            row = _mod(lax.broadcasted_iota(jnp.int32, (2 * tp, PAGE), 0), tp)
            col = lax.broadcasted_iota(jnp.int32, (2 * tp, PAGE), 1)
            mask = (col <= row) & (col < n_valid)
            mask4 = jnp.concatenate([mask] * H_A, axis=0)
            s = jnp.concatenate([_nt(qstack(h), new_ref[0, :, h * LANE:(h + 1) * LANE])
                                 for h in range(H_A)], axis=0)
            _online_update(jnp.where(mask4, s, NEG), None, m_scr, l_scr, acc_scr, valid=mask4,
                           pv_fn=lambda p: jnp.concatenate(
                               [_mm(p[h * 2 * tp:(h + 1) * 2 * tp],
                                    new_ref[0, :, 512 + h * LANE:512 + (h + 1) * LANE])
                                for h in range(H_A)], axis=0))
            for h in range(H_A):
                r0 = h * 2 * tp
                acc = acc_scr[r0:r0 + 2 * tp, :]
                l = l_scr[r0:r0 + 2 * tp, :]
                y = _a_finish(acc[0:tp], l[0:tp], acc[tp:], l[tp:], lam, g_ref[...],
                              z_ref[0, :, h * LANE:(h + 1) * LANE], lam_init)
                o_ref[0, :, h * LANE:(h + 1) * LANE] = y.astype(BF16)

    def page_spec(j):
        return pl.BlockSpec((PAGE * 2 * H_A, LANE),
                            lambda b, g, pt, j=j: (page_base + pt[b, g * pg + j], 0))

    grid_spec = pltpu.PrefetchScalarGridSpec(
        num_scalar_prefetch=1, grid=(db, ng),
        in_specs=[page_spec(j) for j in range(pg)] + [
            pl.BlockSpec((1, tp, 512), lambda b, g, pt: (b, 0, 0)),
            pl.BlockSpec((1, PAGE, 1024), lambda b, g, pt: (b, 0, 0)),
            pl.BlockSpec((1, tp, 512), lambda b, g, pt: (b, 0, _UNIT["az"] // 4)),
            pl.BlockSpec((4, DQ_A), lambda b, g, pt: (0, 0)),
            pl.BlockSpec((1, LANE), lambda b, g, pt: (0, 0))],
        out_specs=pl.BlockSpec((1, tp, 512), lambda b, g, pt: (b, 0, 0)),
        scratch_shapes=[pltpu.VMEM((H_A * 2 * tp, LANE), F32)] * 3)
    return pl.pallas_call(
        body, grid_spec=grid_spec,
        out_shape=jax.ShapeDtypeStruct((db, tp, 512), BF16),
        compiler_params=_params("parallel", "arbitrary"),
        name="a_attn_sample")(page_table, *([cache2] * pg), qn, kvnew, proj3, lam_p,
                              out_g.reshape(1, LANE))


def _b_prep(proj, g4, wpb, *, tm, pool, stack=None):
    m = proj.shape[0]
    nb = tm // CMP_BLOCK
    bkv_spec, bkv_shape, prev = _layer_slab(m * 4, tm * 4, stack)

    def body(kv_ref, g_ref, wp_ref, *rest):
        bkv_ref, nw_ref, kvb_ref, *pool_refs = rest[1:] if prev is not None else rest
        kc = kv_ref[:, 0:128]
        vc = kv_ref[:, 128:256]
        ks = _rms(kv_ref[:, 256:384], g_ref[2:3, :])
        vs = kv_ref[:, 384:512]
        kw = _rms(kv_ref[:, 512:640], g_ref[3:4, :])
        vw = kv_ref[:, 640:768]
        for slot, x in enumerate((kc, vc, ks, vs)):
            bkv_ref[pl.ds(slot, tm, stride=4), :] = x
        nw_ref[pl.ds(0, tm, stride=2), :] = kw
        nw_ref[pl.ds(1, tm, stride=2), :] = vw
        kvb_ref[:, 0:128] = ks.astype(BF16)
        kvb_ref[:, 128:256] = vs.astype(BF16)
        kvb_ref[:, 256:384] = kw.astype(BF16)
        kvb_ref[:, 384:512] = vw.astype(BF16)
        if pool:
            kcb_ref, vcb_ref = pool_refs
            kp = jnp.sum(kc.reshape(nb, CMP_BLOCK, LANE) * wp_ref[0][None], axis=1)
            vp = jnp.sum(vc.reshape(nb, CMP_BLOCK, LANE) * wp_ref[1][None], axis=1)
            kcb_ref[...] = _rms(kp, g_ref[1:2, :])
            vcb_ref[...] = vp

    row = lambda w: pl.BlockSpec((tm, w), lambda i: (i, 0))
    out_specs = [bkv_spec, pl.BlockSpec((tm * 2, LANE), lambda i: (i, 0)), row(512)]
    out_shape = [bkv_shape, jax.ShapeDtypeStruct((m * 2, LANE), F32),
                 jax.ShapeDtypeStruct((m, 512), BF16)]
    if pool:
        out_specs += [pl.BlockSpec((nb, LANE), lambda i: (i, 0))] * 2
        out_shape += [jax.ShapeDtypeStruct((m // CMP_BLOCK, LANE), F32)] * 2
    return pl.pallas_call(
        body, grid=(m // tm,),
        in_specs=[pl.BlockSpec((tm, 768), lambda i: (i, _UNIT["bkv"] // 6)),
                  pl.BlockSpec((4, LANE), lambda i: (0, 0)),
                  pl.BlockSpec((2, CMP_BLOCK, LANE), lambda i: (0, 0, 0))]
        + ([pl.BlockSpec(memory_space=pl.ANY)] if prev is not None else []),
        out_specs=out_specs, out_shape=out_shape,
        input_output_aliases={3: 0} if prev is not None else {},
        compiler_params=_params("parallel"),
        name="b_prep")(proj, g4, wpb, *([prev] if prev is not None else []))


def _cmp_scores(qf, kcb):
    qh, ql = _split2(qf)
    kh, kl = _split2(kcb)
    return _nt(qh, kh) + _nt(qh, kl) + _nt(ql, kh)


def _b_attn_prompt(proj3, kcb, vcb, kvb, g4, *, t):
    b_ = proj3.shape[0]
    tq = Q_BLOCK
    nblk = t // CMP_BLOCK
    n_top = min(N_SEL, nblk)
    tk = min(512, t)
    span = min(WINDOW + tq, t)
    scale = HEAD ** -0.5

    def body(q_ref, kcb_ref, vcb_ref, kv_ref, bg_ref, bz_ref, g_ref, o_ref, m_scr, l_scr, acc_scr):
        qi = pl.program_id(1)
        qf = jnp.concatenate([_rms(q_ref[0, :, h * LANE:(h + 1) * LANE], g_ref[0:1, :]) * scale
                              for h in range(H_B)], axis=0)
        qb = qf.astype(BF16)
        pos = qi * tq + lax.broadcasted_iota(jnp.int32, (tq, 1), 0)

        pos_l = qi * tq + lax.broadcasted_iota(jnp.int32, (1, tq), 1)
        cur_l = _div(pos_l, CMP_BLOCK)
        blk_s = lax.broadcasted_iota(jnp.int32, (nblk, tq), 0)
        cmask = blk_s < cur_l
        kh, kl = _split2(kcb_ref[0])
        pcs, imp = [], jnp.zeros((nblk, tq), F32)
        for h in range(H_B):
            qh, ql = _split2(qf[h * tq:(h + 1) * tq])
            s = jnp.where(cmask, _nt(kh, qh) + _nt(kl, qh) + _nt(kh, ql), NEG)
            mx = jnp.max(s, axis=0, keepdims=True)
            p = jnp.where(cmask, jnp.exp(s - mx), 0.0)
            pc = p / jnp.maximum(jnp.sum(p, axis=0, keepdims=True), 1e-30)
            pcs.append(pc)
            imp = imp + pc
        imp = jnp.where(cmask, imp, -1.0)
        fill = LANE - H_B * nblk
        pcq = jnp.concatenate(pcs + ([jnp.zeros((fill, tq), F32)] if fill else []), axis=0).T.astype(BF16)
        vch = vcb_ref[0].astype(BF16)
        zblk = jnp.zeros((nblk, LANE), BF16)
        vdiag = jnp.concatenate(
            [jnp.concatenate([vch if c == h else zblk for c in range(H_B)], axis=1)
             for h in range(H_B)] + ([jnp.zeros((fill, H_B * LANE), BF16)] if fill else []), axis=0)
        o_cmp = _mm(pcq, vdiag)

        rank = jnp.zeros((nblk, tq), F32)
        for mrow in range(nblk):
            cm = imp[mrow:mrow + 1, :]
            ahead = (cm > imp) | ((cm == imp) & (mrow < blk_s))
            rank = rank + ahead.astype(F32)
        sel_t = (((rank < n_top) & (imp >= 0.0)) | (blk_s == cur_l)).astype(F32)
        selq = jnp.concatenate([sel_t, jnp.zeros((LANE - nblk, tq), F32)], axis=0).T.astype(BF16)

        m_scr[...] = jnp.full(m_scr.shape, NEG, F32)
        l_scr[...] = jnp.zeros(l_scr.shape, F32)
        acc_scr[...] = jnp.zeros(acc_scr.shape, F32)
        nk = _div(qi * tq + tq + tk - 1, tk)

        def chunk(c, carry):
            ks_ = pl.ds(pl.multiple_of(c * tk, tk), tk)
            kidx = c * tk + lax.broadcasted_iota(jnp.int32, (LANE, tk), 1)
            e = (_div(kidx, CMP_BLOCK) == lax.broadcasted_iota(jnp.int32, (LANE, tk), 0)).astype(BF16)
            kpos = c * tk + lax.broadcasted_iota(jnp.int32, (tq, tk), 1)
            mk = (_mm(selq, e) > 0.5) & (kpos <= pos)
            mask = jnp.concatenate([mk] * H_B, axis=0)
            s = jnp.where(mask, _nt(qb, kv_ref[0, ks_, 0:128]), NEG)
            _online_update(s, kv_ref[0, ks_, 128:256], m_scr, l_scr, acc_scr, valid=mask)
            return carry

        lax.fori_loop(0, nk, chunk, 0)

        start = jnp.clip(qi * tq + tq - span, 0, t - span)
        ws = pl.ds(pl.multiple_of(start, tq), span)
        diff = pos - (start + lax.broadcasted_iota(jnp.int32, (tq, span), 1))
        wmask = jnp.concatenate([(diff >= 0) & (diff < WINDOW)] * H_B, axis=0)
        s = jnp.where(wmask, _nt(qb, kv_ref[0, ws, 256:384]), NEG)
        p = jnp.where(wmask, jnp.exp(s - jnp.max(s, axis=-1, keepdims=True)), 0.0)
        o_win = _mm(p.astype(BF16), kv_ref[0, ws, 384:512]) \
            / jnp.maximum(jnp.sum(p, axis=-1, keepdims=True), 1e-30)
        o_sel = acc_scr[...] / jnp.maximum(l_scr[...], 1e-30)

        gate = _sigmoid(bg_ref[0])
        for h in range(H_B):
            rows = slice(h * tq, (h + 1) * tq)
            ob = (gate[:, 3 * h:3 * h + 1] * o_cmp[:, h * LANE:(h + 1) * LANE]
                  + gate[:, 3 * h + 1:3 * h + 2] * o_sel[rows]
                  + gate[:, 3 * h + 2:3 * h + 3] * o_win[rows])
            y = ob * _silu(bz_ref[0, :, h * LANE:(h + 1) * LANE])
            o_ref[0, :, h * LANE:(h + 1) * LANE] = y.astype(BF16)

    full = lambda rows, w: pl.BlockSpec((1, rows, w), lambda b, i: (b, 0, 0))
    return pl.pallas_call(
        body, grid=(b_, t // tq),
        in_specs=[pl.BlockSpec((1, tq, 512), lambda b, i: (b, i, _UNIT["bq"] // 4)),
                  full(nblk, LANE), full(nblk, LANE), full(t, 512),
                  pl.BlockSpec((1, tq, LANE), lambda b, i: (b, i, _UNIT["bg"])),
                  pl.BlockSpec((1, tq, 512), lambda b, i: (b, i, _UNIT["bz"] // 4)),
                  pl.BlockSpec((4, LANE), lambda b, i: (0, 0))],
        out_specs=pl.BlockSpec((1, tq, 512), lambda b, i: (b, i, 0)),
        out_shape=jax.ShapeDtypeStruct((b_, t, 512), BF16),
        scratch_shapes=[pltpu.VMEM((H_B * tq, LANE), F32)] * 3,
        compiler_params=_params("parallel", "arbitrary"),
        name="b_attn_prompt")(proj3, kcb, vcb, kvb, proj3, proj3, g4)


def _b_pool_pages(page_table, cache2, wpb2, page_base, *, pg):
    db, n_pages = page_table.shape
    ng = n_pages // pg
    per = PAGE // CMP_BLOCK

    def body(pt_ref, *refs):
        pages = refs[:pg]
        wp_ref, o_ref = refs[pg:]
        rows = []
        for j in range(pg):
            for u in range(per):
                parts = []
                for kind in range(2):
                    x = pages[j][pl.ds(u * CMP_BLOCK * 4 + kind, CMP_BLOCK, stride=4), :]
                    parts.append(jnp.sum(x * wp_ref[:, kind * LANE:(kind + 1) * LANE], axis=0, keepdims=True))
                rows.append(jnp.concatenate(parts, axis=1))
        o_ref[0] = jnp.concatenate(rows, axis=0)

    def page_spec(j):
        return pl.BlockSpec((PAGE * 4, LANE), lambda b, g, pt, j=j: (page_base + pt[b, g * pg + j], 0))

    grid_spec = pltpu.PrefetchScalarGridSpec(
        num_scalar_prefetch=1, grid=(db, ng),
        in_specs=[page_spec(j) for j in range(pg)] + [
            pl.BlockSpec((CMP_BLOCK, 256), lambda b, g, pt: (0, 0))],
        out_specs=pl.BlockSpec((1, pg * per, 256), lambda b, g, pt: (b, g, 0)))
    return pl.pallas_call(
        body, grid_spec=grid_spec,
        out_shape=jax.ShapeDtypeStruct((db, n_pages * per, 256), F32),
        compiler_params=_params("parallel", "parallel"),
        name="b_pool_pages")(page_table, *([cache2] * pg), wpb2)


def _b_select_sample(proj3, pooled, win, neww, g4, *, past_len, n_valid):
    db = neww.shape[0]
    tp = neww.shape[1] // 2
    nblk = pooled.shape[1]
    n_top = min(N_SEL, nblk + 1)
    wb = win.shape[1] // 2
    rows = H_B * tp
    scale = HEAD ** -0.5

    def body(q_ref, pool_ref, win_ref, nw_ref, bg_ref, g_ref, qs_ref, ocw_ref, g1_ref, sel_ref):
        qf = jnp.concatenate(
            [_rms(q_ref[0, :, h * LANE:(h + 1) * LANE], g_ref[0:1, :]) * scale for h in range(H_B)],
            axis=0)
        qb = qf.astype(BF16)
        qs_ref[0] = qb
        tok = _mod(lax.broadcasted_iota(jnp.int32, (rows, 1), 0), tp)
        pos = past_len + tok
        cur = _div(pos, CMP_BLOCK)
        blk = lax.broadcasted_iota(jnp.int32, (rows, nblk), 1)
        cmask = blk < cur

        kc = _rms(pool_ref[0, :, 0:128], g_ref[1:2, :])
        s = jnp.where(cmask, _cmp_scores(qf, kc), NEG)
        mx = jnp.max(s, axis=-1, keepdims=True)
        p = jnp.where(cmask, jnp.exp(s - mx), 0.0)
        pc = p / jnp.maximum(jnp.sum(p, axis=-1, keepdims=True), 1e-30)
        o_cmp = _mm(pc.astype(BF16), pool_ref[0, :, 128:256].astype(BF16))
        imp = pc[0:tp]
        for h in range(1, H_B):
            imp = imp + pc[h * tp:(h + 1) * tp]
        imp = jnp.where(cmask[0:tp], imp, -1.0)

        pad = jnp.concatenate([imp, jnp.zeros((LANE - tp, nblk), F32)], axis=0)
        imp_t = jnp.concatenate([pad[:, u * LANE:(u + 1) * LANE].T for u in range(nblk // LANE)], axis=0)
        mi = lax.broadcasted_iota(jnp.int32, (nblk, nblk), 0)
        ni = lax.broadcasted_iota(jnp.int32, (nblk, nblk), 1)
        sels = []
        for tkn in range(tp):
            r = imp[tkn:tkn + 1, :]
            c = imp_t[:, tkn:tkn + 1]
            ahead = (c > r) | ((c == r) & (mi < ni))
            rank = jnp.sum(ahead.astype(F32), axis=0, keepdims=True)
            sels.append(((rank < n_top) & (r >= 0.0)).astype(F32))
        sel_ref[0] = jnp.concatenate(sels, axis=0)

        kw = win_ref[0, pl.ds(0, wb, stride=2), :].astype(BF16)
        vw = win_ref[0, pl.ds(1, wb, stride=2), :].astype(BF16)
        zpad = jnp.zeros((LANE - tp, LANE), F32)
        kn = jnp.concatenate([nw_ref[0, pl.ds(0, tp, stride=2), :], zpad], axis=0).astype(BF16)
        vn = jnp.concatenate([nw_ref[0, pl.ds(1, tp, stride=2), :], zpad], axis=0).astype(BF16)
        jw =lax.broadcasted_iota(jnp.int32, (rows, wb), 1)
        dw = pos - (past_len - wb + jw)
        jn = lax.broadcasted_iota(jnp.int32, (rows, LANE), 1)
        dn = tok - jn
        wmask = jnp.concatenate([(dw >= 0) & (dw < WINDOW), (dn >= 0) & (dn < WINDOW) & (jn < n_valid)],
                                axis=1)
        sw = jnp.where(wmask, jnp.concatenate([_nt(qb, kw), _nt(qb, kn)], axis=1), NEG)
        mw = jnp.max(sw, axis=-1, keepdims=True)
        pw = jnp.where(wmask, jnp.exp(sw - mw), 0.0)
        o_win = (_mm(pw[:, 0:wb].astype(BF16), vw) + _mm(pw[:, wb:].astype(BF16), vn)) \
            / jnp.maximum(jnp.sum(pw, axis=-1, keepdims=True), 1e-30)

        gate = _sigmoid(bg_ref[0])
        g0 = jnp.concatenate([gate[:, 3 * h:3 * h + 1] for h in range(H_B)], axis=0)
        g1 = jnp.concatenate([gate[:, 3 * h + 1:3 * h + 2] for h in range(H_B)], axis=0)
        g2 = jnp.concatenate([gate[:, 3 * h + 2:3 * h + 3] for h in range(H_B)], axis=0)
        ocw_ref[0] = g0 * o_cmp + g2 * o_win
        g1_ref[0] = jnp.broadcast_to(g1, (rows, LANE))

    per_b = lambda r, w: pl.BlockSpec((1, r, w), lambda b: (b, 0, 0))
    return pl.pallas_call(
        body, grid=(db,),
        in_specs=[pl.BlockSpec((1, tp, 512), lambda b: (b, 0, _UNIT["bq"] // 4)),
                  per_b(nblk, 256), per_b(2 * wb, LANE), per_b(2 * tp, LANE),
                  pl.BlockSpec((1, tp, LANE), lambda b: (b, 0, _UNIT["bg"])),
                  pl.BlockSpec((4, LANE), lambda b: (0, 0))],
        out_specs=[per_b(rows, LANE), per_b(rows, LANE), per_b(rows, LANE), per_b(tp, nblk)],
        out_shape=[jax.ShapeDtypeStruct((db, rows, LANE), BF16),
                   jax.ShapeDtypeStruct((db, rows, LANE), F32),
                   jax.ShapeDtypeStruct((db, rows, LANE), F32),
                   jax.ShapeDtypeStruct((db, tp, nblk), F32)],
        compiler_params=_params("parallel"),
        name="b_select_sample")(proj3, pooled, win, neww, proj3, g4)


def _b_selected_sample(page_table, cache2, qs, selg, bkvs, ocw, g1b, proj3, page_base,
                       *, n_valid, pg):
    db, n_pages = page_table.shape
    ng = n_pages // pg
    rows = qs.shape[1]
    tp = rows // H_B
    per = PAGE // CMP_BLOCK
    nbg = pg * per
    width = pg * PAGE

    def body(pt_ref, *refs):
        pages = refs[:pg]
        q_ref, sel_ref, new_ref, ocw_ref, g1_ref, z_ref, o_ref, m_scr, l_scr, acc_scr = refs[pg:]
        gi = pl.program_id(1)

        @pl.when(gi == 0)
        def _():
            m_scr[...] = jnp.full(m_scr.shape, NEG, F32)
            l_scr[...] = jnp.zeros(l_scr.shape, F32)
            acc_scr[...] = jnp.zeros(acc_scr.shape, F32)

        q = q_ref[0]

        def update(s, mask, vs):
            m_old = m_scr[...]
            m_new = jnp.maximum(m_old, jnp.max(s, axis=-1, keepdims=True))
            alpha = jnp.exp(m_old - m_new)
            p = jnp.where(mask, jnp.exp(s - m_new), 0.0)
            pv = None
            for j, v in enumerate(vs):
                d = _mm(p[:, j * PAGE:(j + 1) * PAGE].astype(BF16), v)
                pv = d if pv is None else pv + d
            l_scr[...] = alpha * l_scr[...] + jnp.sum(p, axis=-1, keepdims=True)
            acc_scr[...] = alpha * acc_scr[...] + pv
            m_scr[...] = m_new

        e = (_div(lax.broadcasted_iota(jnp.int32, (nbg, width), 1), CMP_BLOCK)
             == lax.broadcasted_iota(jnp.int32, (nbg, width), 0)).astype(BF16)
        mk = _mm(sel_ref[0, 0].astype(BF16), e) > 0.5
        mask = jnp.concatenate([mk] * H_B, axis=0)
        ss, vs = [], []
        for j in range(pg):
            ss.append(_nt(q, pages[j][pl.ds(2, PAGE, stride=4), :].astype(BF16)))
            vs.append(pages[j][pl.ds(3, PAGE, stride=4), :].astype(BF16))
        update(jnp.where(mask, jnp.concatenate(ss, axis=1), NEG), mask, vs)

        @pl.when(gi == ng - 1)
        def _():
            zpad = jnp.zeros((PAGE - tp, LANE), F32)
            kn = jnp.concatenate([new_ref[0, pl.ds(2, tp, stride=4), :], zpad], axis=0).astype(BF16)
            vn = jnp.concatenate([new_ref[0, pl.ds(3, tp, stride=4), :], zpad], axis=0).astype(BF16)
            tok = _mod(lax.broadcasted_iota(jnp.int32, (rows, PAGE), 0), tp)
            col = lax.broadcasted_iota(jnp.int32, (rows, PAGE), 1)
            nmask = (col <= tok) & (col < n_valid)
            update(jnp.where(nmask, _nt(q, kn), NEG), nmask, [vn])
            o_sel = acc_scr[...] / jnp.maximum(l_scr[...], 1e-30)
            ob = ocw_ref[0] + g1_ref[0] * o_sel
            for h in range(H_B):
                y = ob[h * tp:(h + 1) * tp] * _silu(z_ref[0, :, h * LANE:(h + 1) * LANE])
                o_ref[0, :, h * LANE:(h + 1) * LANE] = y.astype(BF16)

    def page_spec(j):
        return pl.BlockSpec((PAGE * 4, LANE), lambda b, g, pt, j=j: (page_base + pt[b, g * pg + j], 0))

    per_b = lambda r, w: pl.BlockSpec((1, r, w), lambda b, g, pt: (b, 0, 0))
    grid_spec = pltpu.PrefetchScalarGridSpec(
        num_scalar_prefetch=1, grid=(db, ng),
        in_specs=[page_spec(j) for j in range(pg)] + [
            per_b(rows, LANE),
            pl.BlockSpec((1, 1, tp, nbg), lambda b, g, pt: (b, g, 0, 0)),
            per_b(4 * tp, LANE), per_b(rows, LANE), per_b(rows, LANE),
            pl.BlockSpec((1, tp, 512), lambda b, g, pt: (b, 0, _UNIT["bz"] // 4))],
        out_specs=per_b(tp, 512),
        scratch_shapes=[pltpu.VMEM((rows, 1), F32), pltpu.VMEM((rows, 1), F32),
                        pltpu.VMEM((rows, LANE), F32)])
    return pl.pallas_call(
        body, grid_spec=grid_spec,
        out_shape=jax.ShapeDtypeStruct((db, tp, 512), BF16),
        compiler_params=_params("parallel", "arbitrary"),
        name="b_selected_sample")(page_table, *([cache2] * pg), qs, selg, bkvs, ocw, g1b, proj3)


def _mlstm(proj3, conv_w, conv_b, gate_b, out_g, conv0, c0, n0, m0, *, lc, n_valid):
    b_, t, _ = proj3.shape
    col_head = jnp.arange(H_C * LANE)[None, :] // LANE
    lane_id = jnp.arange(LANE)[:, None]
    rsel = jnp.stack([lane_id == col_head, lane_id == col_head + H_C]).astype(BF16)
    nchunk = t // lc
    dqk = H_C * DQK_C
    tail = CONV_W - 1
    base = 8
    tsq = max(lc, LANE)

    def bcast_cols(x, sel):
        x1, x2, x3 = _split3(x)
        return _mm(x1, sel) + _mm(x2, sel) + _mm(x3, sel)

    def body(qk_ref, v_ref, if_ref, co_ref, cz_ref, cw_ref, cb_ref, gb_ref, g_ref, rsel_ref,
             conv0_ref, c0_ref, n0_ref, m0_ref,
             y_ref, conv_ref, c_ref, n_ref, m_ref, xbuf, c_scr, n_scr, m_scr):
        ci = pl.program_id(1)

        @pl.when(ci == 0)
        def _():
            xbuf[base - tail:base, :] = conv0_ref[0]
            c_scr[...] = c0_ref[0]
            n_scr[...] = n0_ref[0]
            m_scr[...] = m0_ref[0]

        xbuf[base:base + lc, :] = qk_ref[0]
        acc = cb_ref[...]
        for j in range(CONV_W):
            acc = acc + xbuf[base - tail + j:base - tail + j + lc, :] * cw_ref[j:j + 1, :]
        new_tail = xbuf[base + n_valid - tail:base + n_valid, :]
        xbuf[base - tail:base, :] = new_tail
        conv_ref[0] = new_tail
        qk = _silu(acc)

        gt = if_ref[0] + gb_ref[...]
        lf = jnp.minimum(gt, 0.0) - jnp.log(1.0 + jnp.exp(-jnp.abs(gt)))
        ti = lax.broadcasted_iota(jnp.int32, (lc, lc), 0)
        si = lax.broadcasted_iota(jnp.int32, (lc, lc), 1)
        tri = (si <= ti).astype(BF16)
        l1, l2, l3 = _split3(lf)
        bcum = _mm(tri, l1) + _mm(tri, l2) + _mm(tri, l3)
        zrow = jnp.zeros((tsq - lc, LANE), F32)
        gt_t = (jnp.concatenate([gt, zrow], axis=0) if tsq > lc else gt).T
        b_t = (jnp.concatenate([bcum, zrow], axis=0) if tsq > lc else bcum).T
        dmask = (si <= ti) & (si < n_valid)
        svalid = lax.broadcasted_iota(jnp.int32, (lc, LANE), 0) < n_valid
        icol_all = bcast_cols(gt, rsel_ref[0])
        bcol_all = bcast_cols(bcum, rsel_ref[1])
        gate_all = _sigmoid(co_ref[0]) * _silu(cz_ref[0])
        low = lax.broadcasted_iota(jnp.int32, (lc, LANE), 1) < DQK_C
        row_low = lax.broadcasted_iota(jnp.int32, (LANE, LANE), 0) < DQK_C

        c_new, n_new, m_new_all = [], [], []
        for j in range(H_C // 2):
            qp = qk[:, j * LANE:(j + 1) * LANE] * (DQK_C ** -0.5)
            kp = qk[:, dqk + j * LANE:dqk + (j + 1) * LANE]
            cp = c_scr[j]
            cpb = cp.astype(BF16)
            npair = n_scr[j:j + 1, :]
            c_upd, n_upd, carries = None, None, []
            for u in range(2):
                h = 2 * j + u
                sel = low if u == 0 else jnp.logical_not(low)
                qm = jnp.where(sel, qp, 0.0)
                km = jnp.where(sel, kp, 0.0)
                qmb, kmb = qm.astype(BF16), km.astype(BF16)
                vh = v_ref[0, :, h * HEAD:(h + 1) * HEAD]
                bcol = bcol_all[:, h * LANE:(h + 1) * LANE]
                icol = icol_all[:, h * LANE:(h + 1) * LANE]
                brow = b_t[H_C + h:H_C + h + 1, 0:lc]
                irow = gt_t[h:h + 1, 0:lc]
                m_h = m_scr[h:h + 1, :]

                d = jnp.where(dmask, bcol[:, 0:lc] - brow + irow, NEG)
                inter = bcol + m_h
                m_t = jnp.maximum(inter, jnp.max(d, axis=1, keepdims=True))
                w_intra = jnp.where(dmask, jnp.exp(d - m_t[:, 0:lc]), 0.0)
                w_inter = jnp.exp(inter - m_t)
                sqk = _nt(qmb, kmb) * w_intra
                num = w_inter * _mm(qmb, cpb) + _mm(sqk.astype(BF16), vh.astype(BF16))
                den = (w_inter * jnp.sum(qm * npair, axis=1, keepdims=True)
                       + jnp.sum(sqk, axis=1, keepdims=True))
                hh = num / jnp.maximum(jnp.abs(den), jnp.exp(-m_t))
                y_ref[0, :, h * HEAD:(h + 1) * HEAD] = (
                    _rms(hh, g_ref[...]) * gate_all[:, h * HEAD:(h + 1) * HEAD]).astype(BF16)

                b_last = bcol[n_valid - 1:n_valid, :]
                dec = jnp.where(svalid, b_last - bcol + icol, NEG)
                m_new = jnp.maximum(b_last + m_h, jnp.max(dec, axis=0, keepdims=True))
                wk = jnp.where(svalid, jnp.exp(dec - m_new), 0.0)
                carries.append(jnp.exp(b_last + m_h - m_new))
                cu = _tn(kmb, (wk * vh).astype(BF16))
                nu = jnp.sum(wk * km, axis=0, keepdims=True)
                c_upd = cu if c_upd is None else c_upd + cu
                n_upd = nu if n_upd is None else n_upd + nu
                m_new_all.append(m_new)
            c_new.append(jnp.where(row_low, carries[0], carries[1]) * cp + c_upd)
            n_new.append(jnp.where(low[0:1], carries[0], carries[1]) * npair + n_upd)

        for j in range(H_C // 2):
            c_scr[j] = c_new[j]
            n_scr[j:j + 1, :] = n_new[j]
        for h in range(H_C):
            m_scr[h:h + 1, :] = m_new_all[h]
        c_ref[0] = c_scr[...]
        n_ref[0] = n_scr[...]
        m_ref[0] = m_scr[...]

    col = lambda u: pl.BlockSpec((1, lc, 1024), lambda b, c, u=u: (b, c, u))
    const = lambda *shape: pl.BlockSpec(shape, lambda b, c: (0,) * len(shape))
    per_b = lambda *shape: pl.BlockSpec((1,) + shape, lambda b, c: (b,) + (0,) * len(shape))
    return pl.pallas_call(
        body, grid=(b_, nchunk),
        in_specs=[col(_UNIT["cqk"] // 8), col(_UNIT["cv"] // 8),
                  pl.BlockSpec((1, lc, LANE), lambda b, c: (b, c, _UNIT["cif"])),
                  col(_UNIT["co"] // 8), col(_UNIT["cz"] // 8),
                  const(CONV_W, 1024), const(1, 1024), const(1, LANE), const(1, LANE),
                  const(2, LANE, H_C * LANE),
                  per_b(tail, 1024), per_b(H_C // 2, LANE, HEAD), per_b(H_C // 2, LANE), per_b(H_C, LANE)],
        out_specs=[pl.BlockSpec((1, lc, 1024), lambda b, c: (b, c, 0)),
                   per_b(tail, 1024), per_b(H_C // 2, LANE, HEAD), per_b(H_C // 2, LANE), per_b(H_C, LANE)],
        out_shape=[jax.ShapeDtypeStruct((b_, t, 1024), BF16),
                   jax.ShapeDtypeStruct((b_, tail, 1024), F32),
                   jax.ShapeDtypeStruct((b_, H_C // 2, LANE, HEAD), F32),
                   jax.ShapeDtypeStruct((b_, H_C // 2, LANE), F32),
                   jax.ShapeDtypeStruct((b_, H_C, LANE), F32)],
        scratch_shapes=[pltpu.VMEM((base + lc, 1024), F32), pltpu.VMEM((H_C // 2, LANE, HEAD), F32),
                        pltpu.VMEM((H_C // 2, LANE), F32), pltpu.VMEM((H_C, LANE), F32)],
        compiler_params=_params("parallel", "arbitrary"),
        name="mlstm")(proj3, proj3, proj3, proj3, proj3, conv_w, conv_b, gate_b, out_g, rsel,
                      conv0, c0, n0, m0)


def _w_prep(w_in):
    depth, d, n_in = w_in.shape
    kt_n = d // LANE
    per_col = kt_n * depth
    rows = w_in.reshape(depth, kt_n, LANE, n_in).transpose(3, 1, 0, 2).reshape(n_in * per_col, LANE)
    src_off, off = {}, 0
    for name, width in _SRC:
        src_off[name] = (off, width)
        off += width
    assert off == n_in
    starts, valids = [0] * (N_PROJ // LANE), [0] * (N_PROJ // LANE)
    for name in _DST_ORDER:
        s0, width = src_off[name]
        for u in range(-(-width // LANE)):
            starts[_UNIT[name] + u] = s0 + u * LANE
            valids[_UNIT[name] + u] = min(LANE, width - u * LANE)
    table = jnp.array([starts, valids], jnp.int32)

    def body(tbl_ref, w_ref, o_ref):
        u = pl.program_id(0)
        keep = lax.broadcasted_iota(jnp.int32, (LANE, LANE), 0) < tbl_ref[1, u]
        for l in range(depth):
            for kt in range(kt_n):
                x = w_ref[pl.ds(kt * depth + l, LANE, stride=per_col), :]
                o_ref[l, :, kt * LANE:(kt + 1) * LANE] = jnp.where(keep, x, 0.0).astype(BF16)

    grid_spec = pltpu.PrefetchScalarGridSpec(
        num_scalar_prefetch=1, grid=(N_PROJ // LANE,),
        in_specs=[pl.BlockSpec((pl.Element(LANE * per_col), pl.Element(LANE)),
                               lambda u, tbl: (tbl[0, u] * per_col, 0))],
        out_specs=pl.BlockSpec((depth, LANE, d), lambda u, tbl: (0, u, 0)))
    return pl.pallas_call(
        body, grid_spec=grid_spec,
        out_shape=jax.ShapeDtypeStruct((depth, N_PROJ, d), BF16),
        compiler_params=_params("parallel"),
        name="w_prep")(table, rows)


def _c_from_pairs(c):
    b = c.shape[0]
    return jnp.swapaxes(c.reshape(b, H_C, DQK_C, HEAD), -1, -2)


def _pick(n, prefs):
    for p in prefs:
        if n % p == 0:
            return p
    return n


def kernel(x_prompt, x_sample, cache_a_kv, cache_b_kv, state_b_win, state_c_conv, state_c_C,
           state_c_n, state_c_m, page_table, norm_g, w_in, w_out, a_qk_g, a_lambda, a_out_g,
           b_qk_g, b_cmp_w, c_conv_w, c_conv_b, c_gate_b, c_out_g):
    bp, t, d = x_prompt.shape
    db, ts, _ = x_sample.shape
    depth = norm_g.shape[0]
    n_pool = cache_a_kv.shape[1]
    n_pages = page_table.shape[1]
    past_len = n_pages * PAGE
    wb = state_b_win.shape[2]
    tp = 8
    assert ts <= tp and t % MLSTM_CHUNK == 0 and t % Q_BLOCK == 0 and d == 2048
    mp, ms = bp * t, db * tp
    pg = _pick(n_pages, (16, 8, 4, 2, 1))

    cache_a2 = cache_a_kv.reshape(depth * n_pool * PAGE * 2 * H_A, HEAD)
    cache_b2 = cache_b_kv.reshape(depth * n_pool * PAGE * 4, HEAD)
    win_rows = state_b_win.reshape(depth, db, wb * 2, HEAD)
    hp = x_prompt.reshape(mp, d)
    hs = jnp.pad(x_sample, ((0, 0), (0, tp - ts), (0, 0))).reshape(ms, d)

    tm_p = _pick(mp, (1024, 512, 256, 128))
    tq_a = _pick(t, (512, 256, 128))
    outs_p = [[] for _ in range(5)]
    outs_s = [[] for _ in range(7)]
    akv_all = bkv_all = None

    w_all = _w_prep(w_in)
    for l in range(depth):
        lam_init = 0.8 - 0.6 * math.exp(-0.3 * l)
        w_o = w_out[l].astype(BF16)
        g_a = jnp.tile(a_qk_g[l], (1, 2))
        g_b = b_qk_g[l]
        wpb = jnp.broadcast_to(b_cmp_w[l][:, :, None], (2, CMP_BLOCK, LANE))
        wpb2 = jnp.concatenate([wpb[0], wpb[1]], axis=1)
        gate_b = jnp.pad(c_gate_b[l].reshape(1, 2 * H_C), ((0, 0), (0, LANE - 2 * H_C)))
        conv_b = c_conv_b[l].reshape(1, -1)
        out_gc = c_out_g[l].reshape(1, LANE)

        proj = _norm_matmul(hp, norm_g[l], w_all, l, tm=tm_p, tn=1024)
        proj3 = proj.reshape(bp, t, N_PROJ)
        qn, kn, vb, akv_all = _a_prep(proj, g_a, tm=_pick(mp, (512, 256, 128)),
                                      stack=(l, depth, akv_all))
        ya = _a_attn_prompt(qn.reshape(bp, t, 512), kn.reshape(bp, t, 512), vb.reshape(bp, t, 512),
                            proj3, a_lambda[l], a_out_g[l], lam_init, tq=tq_a)
        bkv_all, neww, kvb, kcb, vcb = _b_prep(proj, g_b, wpb, tm=_pick(mp, (512, 256, 128)), pool=True,
                                               stack=(l, depth, bkv_all))
        nblk = t // CMP_BLOCK
        yb = _b_attn_prompt(proj3, kcb.reshape(bp, nblk, LANE), vcb.reshape(bp, nblk, LANE),
                            kvb.reshape(bp, t, 512), g_b, t=t)
        yc, conv_p, c_p, n_p, m_p = _mlstm(
            proj3, c_conv_w[l], conv_b, gate_b, out_gc,
            jnp.zeros((bp, CONV_W - 1, 2 * H_C * DQK_C), F32), jnp.zeros((bp, H_C // 2, LANE, HEAD), F32),
            jnp.zeros((bp, H_C // 2, LANE), F32), jnp.zeros((bp, H_C, LANE), F32),
            lc=MLSTM_CHUNK, n_valid=MLSTM_CHUNK)
        c_p, n_p = _c_from_pairs(c_p), n_p.reshape(bp, H_C, DQK_C)
        hp = _out_proj(hp, ya.reshape(mp, 512), yb.reshape(mp, 512), yc.reshape(mp, 1024), w_o,
                       tm=tm_p, tn=1024)
        win_p = jnp.concatenate([jnp.zeros((bp, wb, 2, HEAD), F32), neww.reshape(bp, t, 2, HEAD)],
                                axis=1)[:, -wb:]
        for lst, a in zip(outs_p, (win_p, conv_p, c_p, n_p, m_p[:, :, 0])):
            lst.append(a)

        base = l * n_pool
        sproj = _norm_matmul(hs, norm_g[l], w_all, l, tm=ms, tn=1024)
        sproj3 = sproj.reshape(db, tp, N_PROJ)
        sqn, skn, svb, sakv = _a_prep(sproj, g_a, tm=ms)
        kvnew = jnp.concatenate([skn.reshape(db, tp, 512), svb.reshape(db, tp, 512)], axis=2)
        kvnew = jnp.pad(kvnew, ((0, 0), (0, PAGE - tp), (0, 0)))
        sya = _a_attn_sample(page_table, cache_a2, sqn.reshape(db, tp, 512), kvnew, sproj3,
                             a_lambda[l], a_out_g[l], lam_init, base, n_valid=ts, pg=pg)
        sbkvs, sneww, _ = _b_prep(sproj, g_b, wpb, tm=ms, pool=False)
        pooled = _b_pool_pages(page_table, cache_b2, wpb2, base, pg=pg)
        qs, ocw, g1b, sel = _b_select_sample(
            sproj3, pooled, win_rows[l], sneww.reshape(db, tp * 2, HEAD), g_b,
            past_len=past_len, n_valid=ts)
        nbg = pg * (PAGE // CMP_BLOCK)
        selg = sel.reshape(db, tp, n_pages // pg, nbg).transpose(0, 2, 1, 3)
        syb = _b_selected_sample(page_table, cache_b2, qs, selg, sbkvs.reshape(db, tp * 4, HEAD), ocw, g1b,
                                 sproj3, base, n_valid=ts, pg=pg)
        m0 = jnp.broadcast_to(state_c_m[l][:, :, None], (db, H_C, LANE))
        syc, conv_s, c_s, n_s, m_s = _mlstm(
            sproj3, c_conv_w[l], conv_b, gate_b, out_gc, state_c_conv[l],
            jnp.swapaxes(state_c_C[l], -1, -2).reshape(db, H_C // 2, LANE, HEAD),
            state_c_n[l].reshape(db, H_C // 2, LANE), m0, lc=tp, n_valid=ts)
        c_s, n_s = _c_from_pairs(c_s), n_s.reshape(db, H_C, DQK_C)
        hs = _out_proj(hs, sya.reshape(ms, 512), syb.reshape(ms, 512), syc.reshape(ms, 1024), w_o,
                       tm=ms, tn=1024)
        win_s = jnp.concatenate([state_b_win[l], sneww.reshape(db, tp, 2, HEAD)[:, :ts]], axis=1)[:, -wb:]
        for lst, a in zip(outs_s, (sakv.reshape(db, tp, 2, H_A, HEAD)[:, :ts],
                                   sbkvs.reshape(db, tp, 4, HEAD)[:, :ts],
                                   win_s, conv_s, c_s, n_s, m_s[:, :, 0])):
            lst.append(a)

    y_p = hp.reshape(bp, t, d)
    y_s = hs.reshape(db, tp, d)[:, :ts]
    sp = [akv_all.reshape(depth, bp, t, 2, H_A, HEAD), bkv_all.reshape(depth, bp, t, 4, HEAD)]
    sp += [jnp.stack(x) for x in outs_p]
    ss = [jnp.stack(x) for x in outs_s]
    return (y_p, y_s, sp[0], ss[0], sp[1], ss[1], sp[2], ss[2], sp[3], ss[3],
            sp[4], ss[4], sp[5], ss[5], sp[6], ss[6])
```

```python
import functools
import math

import jax
import jax.numpy as jnp
from jax import lax
from jax.experimental import pallas as pl
from jax.experimental.pallas import tpu as pltpu

F32 = jnp.float32
BF16 = jnp.bfloat16

EPS = 1e-6
LANE = 128
HEAD = 128
H_A = 4
DQ_A = HEAD // 2
H_B = 4
H_C = 8
DQK_C = HEAD // 2
CMP_BLOCK = 64
N_SEL = 16
WINDOW = 512
CONV_W = 4
MLSTM_CHUNK = 128
PAGE = 128
Q_BLOCK = 128
NEG = -1e30
MASKED = 2 * NEG
VMEM_LIMIT = 56 * 1024 * 1024

_SRC = (("aq", 512), ("ak", 512), ("av", 512), ("az", 512),
        ("bq", 512), ("bkv", 768), ("bg", 12), ("bz", 512),
        ("cqk", 1024), ("cv", 1024), ("cif", 16), ("co", 1024), ("cz", 1024))
_DST_ORDER = ("aq", "ak", "av", "az", "bq", "bz", "bkv", "bg", "cif", "cqk", "cv", "co", "cz")
_UNIT = {"aq": 0, "ak": 4, "av": 8, "az": 12, "bq": 16, "bz": 20, "bkv": 24, "bg": 30,
         "cif": 31, "cqk": 32, "cv": 40, "co": 48, "cz": 56}
N_PROJ = 64 * LANE


def _nt(a, b):
    return lax.dot_general(a, b, (((1,), (1,)), ((), ())), preferred_element_type=F32)


def _tn(a, b):
    return lax.dot_general(a, b, (((0,), (0,)), ((), ())), preferred_element_type=F32)


def _mm(a, b):
    return jnp.dot(a, b, preferred_element_type=F32)


def _split2(x):
    hi = x.astype(BF16)
    lo = (x - hi.astype(F32)).astype(BF16)
    return hi, lo


def _split3(x):
    hi = x.astype(BF16)
    r = x - hi.astype(F32)
    mid = r.astype(BF16)
    lo = (r - mid.astype(F32)).astype(BF16)
    return hi, mid, lo


def _sigmoid(z):
    return 1.0 / (1.0 + jnp.exp(-z))


def _silu(z):
    return z * _sigmoid(z)


def _rms(x, g):
    return x * lax.rsqrt(jnp.mean(x * x, axis=-1, keepdims=True) + EPS) * g


def _div(x, n):
    return lax.shift_right_logical(x, int(math.log2(n)))


def _mod(x, n):
    return x & (n - 1)


def _online_update(s, v, m_ref, l_ref, acc_ref, pv_fn=None):
    w = s.shape[1]
    m_old = m_ref[...]
    m_new = jnp.maximum(m_old, jnp.max(s, axis=-1, keepdims=True))
    alpha = jnp.exp(m_old - m_new)
    m_rep = m_new if w == LANE else jnp.concatenate([m_new] * (w // LANE), axis=1)
    p = jnp.exp(s - m_rep)
    l_ref[...] = alpha * l_ref[...] + jnp.sum(p, axis=-1, keepdims=True)
    pb = p.astype(BF16)
    acc_ref[...] = alpha * acc_ref[...] + (_mm(pb, v) if pv_fn is None else pv_fn(pb))
    m_ref[...] = m_new


def _params(*sem):
    return pltpu.CompilerParams(dimension_semantics=sem, vmem_limit_bytes=VMEM_LIMIT)


def _norm_matmul(x, g, w, layer, *, tm, tn):
    m, d = x.shape
    n = w.shape[1]
    rc = min(tm, 256)

    def body(x_ref, g_ref, w_ref, o_ref, h_scr):
        @pl.when(pl.program_id(1) == 0)
        def _():
            def chunk(c, carry):
                r = pl.ds(pl.multiple_of(c * rc, rc), rc)
                h_scr[r, :] = _rms(x_ref[r, :], g_ref[...]).astype(BF16)
                return carry
            lax.fori_loop(0, tm // rc, chunk, 0)
        o_ref[...] = _nt(h_scr[...], w_ref[0])

    return pl.pallas_call(
        body, grid=(m // tm, n // tn),
        in_specs=[pl.BlockSpec((tm, d), lambda i, j: (i, 0)),
                  pl.BlockSpec((1, d), lambda i, j: (0, 0)),
                  pl.BlockSpec((1, tn, d), lambda i, j: (layer, j, 0))],
        out_specs=pl.BlockSpec((tm, tn), lambda i, j: (i, j)),
        out_shape=jax.ShapeDtypeStruct((m, n), F32),
        scratch_shapes=[pltpu.VMEM((tm, d), BF16)],
        compiler_params=_params("parallel", "arbitrary"),
        name="norm_matmul")(x, g.reshape(1, d), w)


def _out_proj(x, ya, yb, yc, w, *, tm, tn):
    m, d = x.shape
    da, db, dc = ya.shape[1], yb.shape[1], yc.shape[1]

    def body(x_ref, a_ref, b_ref, c_ref, w_ref, o_ref):
        mix = jnp.concatenate([a_ref[...], b_ref[...], c_ref[...]], axis=1)
        o_ref[...] = x_ref[...] + _mm(mix, w_ref[...])

    return pl.pallas_call(
        body, grid=(m // tm, d // tn),
        in_specs=[pl.BlockSpec((tm, tn), lambda i, j: (i, j)),
                  pl.BlockSpec((tm, da), lambda i, j: (i, 0)),
                  pl.BlockSpec((tm, db), lambda i, j: (i, 0)),
                  pl.BlockSpec((tm, dc), lambda i, j: (i, 0)),
                  pl.BlockSpec((da + db + dc, tn), lambda i, j: (0, j))],
        out_specs=pl.BlockSpec((tm, tn), lambda i, j: (i, j)),
        out_shape=jax.ShapeDtypeStruct((m, d), F32),
        compiler_params=_params("parallel", "parallel"),
        name="out_proj")(x, ya, yb, yc, w)


def _a_prep(proj, g2, *, tm):
    m = proj.shape[0]

    def body(q_ref, k_ref, v_ref, g_ref, qn_ref, kn_ref, vb_ref, akv_ref):
        r = _div(lax.broadcasted_iota(jnp.int32, (LANE, LANE), 0), DQ_A)
        c = _div(lax.broadcasted_iota(jnp.int32, (LANE, LANE), 1), DQ_A)
        seg = (r == c).astype(BF16)

        def segnorm(x, g):
            outs = []
            for u in range(4):
                xc = x[:, u * LANE:(u + 1) * LANE]
                hi, lo = _split2(xc * xc)
                s = _mm(hi, seg) + _mm(lo, seg)
                outs.append(xc * lax.rsqrt(s * (1.0 / DQ_A) + EPS) * g)
            return jnp.concatenate(outs, axis=1)

        qn = segnorm(q_ref[...], g_ref[0:1, :]) * (DQ_A ** -0.5)
        kn = segnorm(k_ref[...], g_ref[1:2, :])
        v = v_ref[...]
        qn_ref[...] = qn.astype(BF16)
        kn_ref[...] = kn.astype(BF16)
        vb_ref[...] = v.astype(BF16)
        for h in range(H_A):
            akv_ref[pl.ds(h, tm, stride=2 * H_A), :] = kn[:, h * LANE:(h + 1) * LANE]
            akv_ref[pl.ds(H_A + h, tm, stride=2 * H_A), :] = v[:, h * LANE:(h + 1) * LANE]

    blk = lambda u: pl.BlockSpec((tm, 512), lambda i, u=u: (i, u))
    row = lambda w: pl.BlockSpec((tm, w), lambda i: (i, 0))
    return pl.pallas_call(
        body, grid=(m // tm,),
        in_specs=[blk(_UNIT["aq"] // 4), blk(_UNIT["ak"] // 4), blk(_UNIT["av"] // 4),
                  pl.BlockSpec((2, LANE), lambda i: (0, 0))],
        out_specs=[row(512), row(512), row(512), pl.BlockSpec((tm * 2 * H_A, LANE), lambda i: (i, 0))],
        out_shape=[jax.ShapeDtypeStruct((m, 512), BF16)] * 3
        + [jax.ShapeDtypeStruct((m * 2 * H_A, LANE), F32)],
        compiler_params=_params("parallel"),
        name="a_prep")(proj, proj, proj, g2)


def _diff_lambda(lp, lam_init):
    a = jnp.sum(lp[0:1, :] * lp[1:2, :], axis=-1, keepdims=True)
    b = jnp.sum(lp[2:3, :] * lp[3:4, :], axis=-1, keepdims=True)
    return jnp.exp(a) - jnp.exp(b) + lam_init


def _a_finish(acc1, l1, acc2, l2, lam, g, z, lam_init):
    o = acc1 / l1 - lam * (acc2 / l2)
    return _rms(o, g) * (1.0 - lam_init) * _silu(z)


def _a_attn_prompt(qn, kn, vb, proj3, lam_p, out_g, lam_init, *, tq):
    b_, t, _ = qn.shape
    tk = tq

    def body(q_ref, k_ref, v_ref, z_ref, lam_ref, g_ref, o_ref, m_scr, l_scr, acc_scr):
        qi = pl.program_id(2)
        q = q_ref[0]
        lane = lax.broadcasted_iota(jnp.int32, (tq, LANE), 1)
        zero = jnp.zeros_like(q)
        qs = (jnp.where(lane < DQ_A, q, zero), jnp.where(lane >= DQ_A, q, zero))
        m_scr[...] = jnp.full(m_scr.shape, NEG, F32)
        l_scr[...] = jnp.zeros(l_scr.shape, F32)
        acc_scr[...] = jnp.zeros(acc_scr.shape, F32)

        def chunk(kc, mask):
            ks = pl.ds(pl.multiple_of(kc * tk, tk), tk)
            k = k_ref[0, ks, :]
            v = v_ref[0, ks, :]
            for c in range(2):
                s = _nt(qs[c], k)
                if mask is not None:
                    s = jnp.where(mask, s, MASKED)
                _online_update(s, v, m_scr.at[c], l_scr.at[c], acc_scr.at[c])

        def full_chunk(kc, carry):
            chunk(kc, None)
            return carry

        lax.fori_loop(0, qi, full_chunk, 0)
        chunk(qi, lax.broadcasted_iota(jnp.int32, (tq, tk), 1)
              <= lax.broadcasted_iota(jnp.int32, (tq, tk), 0))
        lam = _diff_lambda(lam_ref[...], lam_init)
        y = _a_finish(acc_scr[0], l_scr[0], acc_scr[1], l_scr[1], lam, g_ref[...], z_ref[0], lam_init)
        o_ref[0] = y.astype(BF16)

    head = lambda rows, qdep: pl.BlockSpec(
        (1, rows, LANE), (lambda b, h, i: (b, i, h)) if qdep else (lambda b, h, i: (b, 0, h)))
    return pl.pallas_call(
        body, grid=(b_, H_A, t // tq),
        in_specs=[head(tq, True), head(t, False), head(t, False),
                  pl.BlockSpec((1, tq, LANE), lambda b, h, i: (b, i, _UNIT["az"] + h)),
                  pl.BlockSpec((4, DQ_A), lambda b, h, i: (0, 0)),
                  pl.BlockSpec((1, LANE), lambda b, h, i: (0, 0))],
        out_specs=head(tq, True),
        out_shape=jax.ShapeDtypeStruct((b_, t, 512), BF16),
        scratch_shapes=[pltpu.VMEM((2, tq, LANE), F32)] * 3,
        compiler_params=_params("parallel", "parallel", "arbitrary"),
        name="a_attn_prompt")(qn, kn, vb, proj3, lam_p, out_g.reshape(1, LANE))


def _a_attn_sample(page_table, cache2, qn, kvnew, proj3, lam_p, out_g, lam_init, page_base,
                   *, n_valid, pg):
    db, n_pages = page_table.shape
    tp = qn.shape[1]
    ng = n_pages // pg

    def body(pt_ref, *refs):
        pages = refs[:pg]
        q_ref, new_ref, z_ref, lam_ref, g_ref, o_ref, m_scr, l_scr, acc_scr = refs[pg:]
        gi = pl.program_id(1)

        @pl.when(gi == 0)
        def _():
            m_scr[...] = jnp.full(m_scr.shape, NEG, F32)
            l_scr[...] = jnp.zeros(l_scr.shape, F32)
            acc_scr[...] = jnp.zeros(acc_scr.shape, F32)

        lane = lax.broadcasted_iota(jnp.int32, (tp, LANE), 1)

        def qstack(h):
            q = q_ref[0, :, h * LANE:(h + 1) * LANE]
            zero = jnp.zeros_like(q)
            return jnp.concatenate([jnp.where(lane < DQ_A, q, zero),
                                    jnp.where(lane >= DQ_A, q, zero)], axis=0)

        pts = [jnp.swapaxes(pages[j][...].reshape(PAGE, 2 * H_A, LANE), 0, 1) for j in range(pg)]

        def slot(s_):
            return jnp.concatenate([pts[j][s_].astype(BF16) for j in range(pg)], axis=0)

        s = jnp.concatenate([_nt(qstack(h), slot(h)) for h in range(H_A)], axis=0)
        _online_update(s, None, m_scr, l_scr, acc_scr,
                       pv_fn=lambda p: jnp.concatenate(
                           [_mm(p[h * 2 * tp:(h + 1) * 2 * tp], slot(H_A + h)) for h in range(H_A)], axis=0))

        @pl.when(gi == ng - 1)
        def _():
            lam = _diff_lambda(lam_ref[...], lam_init)
            row = _mod(lax.broadcasted_iota(jnp.int32, (2 * tp, PAGE), 0), tp)
            col = lax.broadcasted_iota(jnp.int32, (2 * tp, PAGE), 1)
            mask = (col <= row) & (col < n_valid)
            mask4 = jnp.concatenate([mask] * H_A, axis=0)
            s = jnp.concatenate([_nt(qstack(h), new_ref[0, :, h * LANE:(h + 1) * LANE])
                                 for h in range(H_A)], axis=0)
            _online_update(jnp.where(mask4, s, MASKED), None, m_scr, l_scr, acc_scr,
                           pv_fn=lambda p: jnp.concatenate(
                               [_mm(p[h * 2 * tp:(h + 1) * 2 * tp],
                                    new_ref[0, :, 512 + h * LANE:512 + (h + 1) * LANE])
                                for h in range(H_A)], axis=0))
            for h in range(H_A):
                r0 = h * 2 * tp
                acc = acc_scr[r0:r0 + 2 * tp, :]
                l = l_scr[r0:r0 + 2 * tp, :]
                y = _a_finish(acc[0:tp], l[0:tp], acc[tp:], l[tp:], lam, g_ref[...],
                              z_ref[0, :, h * LANE:(h + 1) * LANE], lam_init)
                o_ref[0, :, h * LANE:(h + 1) * LANE] = y.astype(BF16)

    def page_spec(j):
        return pl.BlockSpec((PAGE * 2 * H_A, LANE),
                            lambda b, g, pt, j=j: (page_base + pt[b, g * pg + j], 0))

    in_specs = [page_spec(j) for j in range(pg)] + [
            pl.BlockSpec((1, tp, 512), lambda b, g, pt: (b, 0, 0)),
            pl.BlockSpec((1, PAGE, 1024), lambda b, g, pt: (b, 0, 0)),
            pl.BlockSpec((1, tp, 512), lambda b, g, pt: (b, 0, _UNIT["az"] // 4)),
            pl.BlockSpec((4, DQ_A), lambda b, g, pt: (0, 0)),
            pl.BlockSpec((1, LANE), lambda b, g, pt: (0, 0))]
    grid_spec = pltpu.PrefetchScalarGridSpec(
        num_scalar_prefetch=1, grid=(db, ng), in_specs=in_specs,
        out_specs=pl.BlockSpec((1, tp, 512), lambda b, g, pt: (b, 0, 0)),
        scratch_shapes=[pltpu.VMEM((H_A * 2 * tp, LANE), F32)] * 3)
    return pl.pallas_call(
        body, grid_spec=grid_spec,
        out_shape=jax.ShapeDtypeStruct((db, tp, 512), BF16),
        compiler_params=_params("parallel", "arbitrary"),
        name="a_attn_sample")(page_table, *([cache2] * pg), qn, kvnew, proj3, lam_p,
                              out_g.reshape(1, LANE))


def _b_prep(proj, g4, wpb, *, tm, pool):
    m = proj.shape[0]
    nb = tm // CMP_BLOCK

    def body(kv_ref, g_ref, wp_ref, bkv_ref, nw_ref, kvb_ref, *pool_refs):
        kc = kv_ref[:, 0:128]
        vc = kv_ref[:, 128:256]
        ks = _rms(kv_ref[:, 256:384], g_ref[2:3, :])
        vs = kv_ref[:, 384:512]
        kw = _rms(kv_ref[:, 512:640], g_ref[3:4, :])
        vw = kv_ref[:, 640:768]
        for slot, x in enumerate((kc, vc, ks, vs)):
            bkv_ref[pl.ds(slot, tm, stride=4), :] = x
        nw_ref[pl.ds(0, tm, stride=2), :] = kw
        nw_ref[pl.ds(1, tm, stride=2), :] = vw
        kvb_ref[:, 0:128] = ks.astype(BF16)
        kvb_ref[:, 128:256] = vs.astype(BF16)
        kvb_ref[:, 256:384] = kw.astype(BF16)
        kvb_ref[:, 384:512] = vw.astype(BF16)
        if pool:
            kcb_ref, vcb_ref = pool_refs
            kp = jnp.sum(kc.reshape(nb, CMP_BLOCK, LANE) * wp_ref[0][None], axis=1)
            vp = jnp.sum(vc.reshape(nb, CMP_BLOCK, LANE) * wp_ref[1][None], axis=1)
            kcb_ref[...] = _rms(kp, g_ref[1:2, :])
            vcb_ref[...] = vp

    row = lambda w: pl.BlockSpec((tm, w), lambda i: (i, 0))
    out_specs = [pl.BlockSpec((tm * 4, LANE), lambda i: (i, 0)),
                 pl.BlockSpec((tm * 2, LANE), lambda i: (i, 0)), row(512)]
    out_shape = [jax.ShapeDtypeStruct((m * 4, LANE), F32), jax.ShapeDtypeStruct((m * 2, LANE), F32),
                 jax.ShapeDtypeStruct((m, 512), BF16)]
    if pool:
        out_specs += [pl.BlockSpec((nb, LANE), lambda i: (i, 0))] * 2
        out_shape += [jax.ShapeDtypeStruct((m // CMP_BLOCK, LANE), F32)] * 2
    return pl.pallas_call(
        body, grid=(m // tm,),
        in_specs=[pl.BlockSpec((tm, 768), lambda i: (i, _UNIT["bkv"] // 6)),
                  pl.BlockSpec((4, LANE), lambda i: (0, 0)),
                  pl.BlockSpec((2, CMP_BLOCK, LANE), lambda i: (0, 0, 0))],
        out_specs=out_specs, out_shape=out_shape,
        compiler_params=_params("parallel"),
        name="b_prep")(proj, g4, wpb)


def _cmp_scores(qf, kcb):
    qh, ql = _split2(qf)
    kh, kl = _split2(kcb)
    return _nt(qh, kh) + _nt(qh, kl) + _nt(ql, kh)


def _b_attn_prompt(proj3, kcb, vcb, kvb, g4, *, t):
    b_ = proj3.shape[0]
    tq = Q_BLOCK
    nblk = t // CMP_BLOCK
    n_top = min(N_SEL, nblk)
    tk = min(512, t)
    span = min(WINDOW + tq, t)
    scale = HEAD ** -0.5

    def body(q_ref, kcb_ref, vcb_ref, kv_ref, bg_ref, bz_ref, g_ref, o_ref, m_scr, l_scr, acc_scr):
        qi = pl.program_id(1)
        qf = jnp.concatenate([_rms(q_ref[0, :, h * LANE:(h + 1) * LANE], g_ref[0:1, :]) * scale
                              for h in range(H_B)], axis=0)
        qb = qf.astype(BF16)
        pos = qi * tq + lax.broadcasted_iota(jnp.int32, (tq, 1), 0)

        pos_l = qi * tq + lax.broadcasted_iota(jnp.int32, (1, tq), 1)
        cur_l = _div(pos_l, CMP_BLOCK)
        blk_s = lax.broadcasted_iota(jnp.int32, (nblk, tq), 0)
        cmask = blk_s < cur_l
        kh, kl = _split2(kcb_ref[0])
        pcs, imp = [], jnp.zeros((nblk, tq), F32)
        for h in range(H_B):
            qh, ql = _split2(qf[h * tq:(h + 1) * tq])
            s = jnp.where(cmask, _nt(kh, qh) + _nt(kl, qh) + _nt(kh, ql), NEG)
            mx = jnp.max(s, axis=0, keepdims=True)
            p = jnp.where(cmask, jnp.exp(s - mx), 0.0)
            pc = p / jnp.maximum(jnp.sum(p, axis=0, keepdims=True), 1e-30)
            pcs.append(pc)
            imp = imp + pc
        imp = jnp.where(cmask, imp, -1.0)
        fill = LANE - H_B * nblk
        pcq = jnp.concatenate(pcs + ([jnp.zeros((fill, tq), F32)] if fill else []), axis=0).T.astype(BF16)
        vch = vcb_ref[0].astype(BF16)
        zblk = jnp.zeros((nblk, LANE), BF16)
        vdiag = jnp.concatenate(
            [jnp.concatenate([vch if c == h else zblk for c in range(H_B)], axis=1)
             for h in range(H_B)] + ([jnp.zeros((fill, H_B * LANE), BF16)] if fill else []), axis=0)
        o_cmp = _mm(pcq, vdiag)

        rank = jnp.zeros((nblk, tq), F32)
        for mrow in range(nblk):
            cm = imp[mrow:mrow + 1, :]
            ahead = (cm > imp) | ((cm == imp) & (mrow < blk_s))
            rank = rank + ahead.astype(F32)
        sel_t = (((rank < n_top) & (imp >= 0.0)) | (blk_s == cur_l)).astype(F32)
        selq = jnp.concatenate([sel_t, jnp.zeros((LANE - nblk, tq), F32)], axis=0).T.astype(BF16)

        m_scr[...] = jnp.full(m_scr.shape, NEG, F32)
        l_scr[...] = jnp.zeros(l_scr.shape, F32)
        acc_scr[...] = jnp.zeros(acc_scr.shape, F32)
        nk = _div(qi * tq + tq + tk - 1, tk)

        def chunk(c, carry):
            ks_ = pl.ds(pl.multiple_of(c * tk, tk), tk)
            kidx = c * tk + lax.broadcasted_iota(jnp.int32, (LANE, tk), 1)
            e = (_div(kidx, CMP_BLOCK) == lax.broadcasted_iota(jnp.int32, (LANE, tk), 0)).astype(BF16)
            kpos = c * tk + lax.broadcasted_iota(jnp.int32, (tq, tk), 1)
            mk = (_mm(selq, e) > 0.5) & (kpos <= pos)
            mask = jnp.concatenate([mk] * H_B, axis=0)
            s = jnp.where(mask, _nt(qb, kv_ref[0, ks_, 0:128]), MASKED)
            _online_update(s, kv_ref[0, ks_, 128:256], m_scr, l_scr, acc_scr)
            return carry

        lax.fori_loop(0, nk, chunk, 0)

        start = jnp.clip(qi * tq + tq - span, 0, t - span)
        ws = pl.ds(pl.multiple_of(start, tq), span)
        diff = pos - (start + lax.broadcasted_iota(jnp.int32, (tq, span), 1))
        wmask = jnp.concatenate([(diff >= 0) & (diff < WINDOW)] * H_B, axis=0)
        s = jnp.where(wmask, _nt(qb, kv_ref[0, ws, 256:384]), NEG)
        p = jnp.where(wmask, jnp.exp(s - jnp.max(s, axis=-1, keepdims=True)), 0.0)
        o_win = _mm(p.astype(BF16), kv_ref[0, ws, 384:512]) \
            / jnp.maximum(jnp.sum(p, axis=-1, keepdims=True), 1e-30)
        o_sel = acc_scr[...] / jnp.maximum(l_scr[...], 1e-30)

        gate = _sigmoid(bg_ref[0])
        for h in range(H_B):
            rows = slice(h * tq, (h + 1) * tq)
            ob = (gate[:, 3 * h:3 * h + 1] * o_cmp[:, h * LANE:(h + 1) * LANE]
                  + gate[:, 3 * h + 1:3 * h + 2] * o_sel[rows]
                  + gate[:, 3 * h + 2:3 * h + 3] * o_win[rows])
            y = ob * _silu(bz_ref[0, :, h * LANE:(h + 1) * LANE])
            o_ref[0, :, h * LANE:(h + 1) * LANE] = y.astype(BF16)

    full = lambda rows, w: pl.BlockSpec((1, rows, w), lambda b, i: (b, 0, 0))
    return pl.pallas_call(
        body, grid=(b_, t // tq),
        in_specs=[pl.BlockSpec((1, tq, 512), lambda b, i: (b, i, _UNIT["bq"] // 4)),
                  full(nblk, LANE), full(nblk, LANE), full(t, 512),
                  pl.BlockSpec((1, tq, LANE), lambda b, i: (b, i, _UNIT["bg"])),
                  pl.BlockSpec((1, tq, 512), lambda b, i: (b, i, _UNIT["bz"] // 4)),
                  pl.BlockSpec((4, LANE), lambda b, i: (0, 0))],
        out_specs=pl.BlockSpec((1, tq, 512), lambda b, i: (b, i, 0)),
        out_shape=jax.ShapeDtypeStruct((b_, t, 512), BF16),
        scratch_shapes=[pltpu.VMEM((H_B * tq, LANE), F32)] * 3,
        compiler_params=_params("parallel", "arbitrary"),
        name="b_attn_prompt")(proj3, kcb, vcb, kvb, proj3, proj3, g4)


def _b_pool_pages(page_table, cache2, wpb2, page_base, *, pg):
    db, n_pages = page_table.shape
    ng = n_pages // pg
    per = PAGE // CMP_BLOCK

    def body(pt_ref, *refs):
        pages = refs[:pg]
        wp_ref, o_ref = refs[pg:]
        rows = []
        for j in range(pg):
            for u in range(per):
                parts = []
                for kind in range(2):
                    x = pages[j][pl.ds(u * CMP_BLOCK * 4 + kind, CMP_BLOCK, stride=4), :]
                    parts.append(jnp.sum(x * wp_ref[:, kind * LANE:(kind + 1) * LANE], axis=0, keepdims=True))
                rows.append(jnp.concatenate(parts, axis=1))
        o_ref[0] = jnp.concatenate(rows, axis=0)

    def page_spec(j):
        return pl.BlockSpec((PAGE * 4, LANE), lambda b, g, pt, j=j: (page_base + pt[b, g * pg + j], 0))

    grid_spec = pltpu.PrefetchScalarGridSpec(
        num_scalar_prefetch=1, grid=(db, ng),
        in_specs=[page_spec(j) for j in range(pg)] + [
            pl.BlockSpec((CMP_BLOCK, 256), lambda b, g, pt: (0, 0))],
        out_specs=pl.BlockSpec((1, pg * per, 256), lambda b, g, pt: (b, g, 0)))
    return pl.pallas_call(
        body, grid_spec=grid_spec,
        out_shape=jax.ShapeDtypeStruct((db, n_pages * per, 256), F32),
        compiler_params=_params("parallel", "parallel"),
        name="b_pool_pages")(page_table, *([cache2] * pg), wpb2)


def _b_select_sample(proj3, pooled, win, neww, g4, *, past_len, n_valid):
    db = neww.shape[0]
    tp = neww.shape[1] // 2
    nblk = pooled.shape[1]
    n_top = min(N_SEL, nblk + 1)
    wb = win.shape[1] // 2
    rows = H_B * tp
    scale = HEAD ** -0.5

    def body(q_ref, pool_ref, win_ref, nw_ref, bg_ref, g_ref, qs_ref, ocw_ref, g1_ref, sel_ref):
        qf = jnp.concatenate(
            [_rms(q_ref[0, :, h * LANE:(h + 1) * LANE], g_ref[0:1, :]) * scale for h in range(H_B)],
            axis=0)
        qb = qf.astype(BF16)
        qs_ref[0] = qb
        tok = _mod(lax.broadcasted_iota(jnp.int32, (rows, 1), 0), tp)
        pos = past_len + tok
        cur = _div(pos, CMP_BLOCK)
        blk = lax.broadcasted_iota(jnp.int32, (rows, nblk), 1)
        cmask = blk < cur

        kc = _rms(pool_ref[0, :, 0:128], g_ref[1:2, :])
        s = jnp.where(cmask, _cmp_scores(qf, kc), NEG)
        mx = jnp.max(s, axis=-1, keepdims=True)
        p = jnp.where(cmask, jnp.exp(s - mx), 0.0)
        pc = p / jnp.maximum(jnp.sum(p, axis=-1, keepdims=True), 1e-30)
        o_cmp = _mm(pc.astype(BF16), pool_ref[0, :, 128:256].astype(BF16))
        imp = pc[0:tp]
        for h in range(1, H_B):
            imp = imp + pc[h * tp:(h + 1) * tp]
        imp = jnp.where(cmask[0:tp], imp, -1.0)

        pad = jnp.concatenate([imp, jnp.zeros((LANE - tp, nblk), F32)], axis=0)
        imp_t = jnp.concatenate([pad[:, u * LANE:(u + 1) * LANE].T for u in range(nblk // LANE)], axis=0)
        mi = lax.broadcasted_iota(jnp.int32, (nblk, nblk), 0)
        ni = lax.broadcasted_iota(jnp.int32, (nblk, nblk), 1)
        sels = []
        for tkn in range(tp):
            r = imp[tkn:tkn + 1, :]
            c = imp_t[:, tkn:tkn + 1]
            ahead = (c > r) | ((c == r) & (mi < ni))
            rank = jnp.sum(ahead.astype(F32), axis=0, keepdims=True)
            sels.append(((rank < n_top) & (r >= 0.0)).astype(F32))
        sel_ref[0] = jnp.concatenate(sels, axis=0)

        kw = win_ref[0, pl.ds(0, wb, stride=2), :].astype(BF16)
        vw = win_ref[0, pl.ds(1, wb, stride=2), :].astype(BF16)
        zpad = jnp.zeros((LANE - tp, LANE), F32)
        kn = jnp.concatenate([nw_ref[0, pl.ds(0, tp, stride=2), :], zpad], axis=0).astype(BF16)
        vn = jnp.concatenate([nw_ref[0, pl.ds(1, tp, stride=2), :], zpad], axis=0).astype(BF16)
        jw =lax.broadcasted_iota(jnp.int32, (rows, wb), 1)
        dw = pos - (past_len - wb + jw)
        jn = lax.broadcasted_iota(jnp.int32, (rows, LANE), 1)
        dn = tok - jn
        wmask = jnp.concatenate([(dw >= 0) & (dw < WINDOW), (dn >= 0) & (dn < WINDOW) & (jn < n_valid)],
                                axis=1)
        sw = jnp.where(wmask, jnp.concatenate([_nt(qb, kw), _nt(qb, kn)], axis=1), NEG)
        mw = jnp.max(sw, axis=-1, keepdims=True)
        pw = jnp.where(wmask, jnp.exp(sw - mw), 0.0)
        o_win = (_mm(pw[:, 0:wb].astype(BF16), vw) + _mm(pw[:, wb:].astype(BF16), vn)) \
            / jnp.maximum(jnp.sum(pw, axis=-1, keepdims=True), 1e-30)

        gate = _sigmoid(bg_ref[0])
        g0 = jnp.concatenate([gate[:, 3 * h:3 * h + 1] for h in range(H_B)], axis=0)
        g1 = jnp.concatenate([gate[:, 3 * h + 1:3 * h + 2] for h in range(H_B)], axis=0)
        g2 = jnp.concatenate([gate[:, 3 * h + 2:3 * h + 3] for h in range(H_B)], axis=0)
        ocw_ref[0] = g0 * o_cmp + g2 * o_win
        g1_ref[0] = jnp.broadcast_to(g1, (rows, LANE))

    per_b = lambda r, w: pl.BlockSpec((1, r, w), lambda b: (b, 0, 0))
    return pl.pallas_call(
        body, grid=(db,),
        in_specs=[pl.BlockSpec((1, tp, 512), lambda b: (b, 0, _UNIT["bq"] // 4)),
                  per_b(nblk, 256), per_b(2 * wb, LANE), per_b(2 * tp, LANE),
                  pl.BlockSpec((1, tp, LANE), lambda b: (b, 0, _UNIT["bg"])),
                  pl.BlockSpec((4, LANE), lambda b: (0, 0))],
        out_specs=[per_b(rows, LANE), per_b(rows, LANE), per_b(rows, LANE), per_b(tp, nblk)],
        out_shape=[jax.ShapeDtypeStruct((db, rows, LANE), BF16),
                   jax.ShapeDtypeStruct((db, rows, LANE), F32),
                   jax.ShapeDtypeStruct((db, rows, LANE), F32),
                   jax.ShapeDtypeStruct((db, tp, nblk), F32)],
        compiler_params=_params("parallel"),
        name="b_select_sample")(proj3, pooled, win, neww, proj3, g4)


def _b_selected_sample(page_table, cache2, qs, selg, bkvs, ocw, g1b, proj3, page_base,
                       *, n_valid, pg):
    db, n_pages = page_table.shape
    ng = n_pages // pg
    rows = qs.shape[1]
    tp = rows // H_B
    per = PAGE // CMP_BLOCK
    nbg = pg * per
    width = pg * PAGE

    def body(pt_ref, *refs):
        pages = refs[:pg]
        q_ref, sel_ref, new_ref, ocw_ref, g1_ref, z_ref, o_ref, m_scr, l_scr, acc_scr = refs[pg:]
        gi = pl.program_id(1)

        @pl.when(gi == 0)
        def _():
            m_scr[...] = jnp.full(m_scr.shape, NEG, F32)
            l_scr[...] = jnp.zeros(l_scr.shape, F32)
            acc_scr[...] = jnp.zeros(acc_scr.shape, F32)

        q = q_ref[0]

        def slot(s_):
            return jnp.concatenate([pages[j][pl.ds(s_, PAGE, stride=4), :].astype(BF16)
                                    for j in range(pg)], axis=0)

        e = (_div(lax.broadcasted_iota(jnp.int32, (nbg, width), 1), CMP_BLOCK)
             == lax.broadcasted_iota(jnp.int32, (nbg, width), 0)).astype(BF16)
        mk = _mm(sel_ref[0, 0].astype(BF16), e) > 0.5
        mask = jnp.concatenate([mk] * H_B, axis=0)
        _online_update(jnp.where(mask, _nt(q, slot(2)), MASKED), slot(3), m_scr, l_scr, acc_scr)

        @pl.when(gi == ng - 1)
        def _():
            zpad = jnp.zeros((PAGE - tp, LANE), F32)
            kn = jnp.concatenate([new_ref[0, pl.ds(2, tp, stride=4), :], zpad], axis=0).astype(BF16)
            vn = jnp.concatenate([new_ref[0, pl.ds(3, tp, stride=4), :], zpad], axis=0).astype(BF16)
            tok = _mod(lax.broadcasted_iota(jnp.int32, (rows, PAGE), 0), tp)
            col = lax.broadcasted_iota(jnp.int32, (rows, PAGE), 1)
            nmask = (col <= tok) & (col < n_valid)
            _online_update(jnp.where(nmask, _nt(q, kn), MASKED), vn, m_scr, l_scr, acc_scr)
            o_sel = acc_scr[...] / jnp.maximum(l_scr[...], 1e-30)
            ob = ocw_ref[0] + g1_ref[0] * o_sel
            for h in range(H_B):
                y = ob[h * tp:(h + 1) * tp] * _silu(z_ref[0, :, h * LANE:(h + 1) * LANE])
                o_ref[0, :, h * LANE:(h + 1) * LANE] = y.astype(BF16)

    def page_spec(j):
        return pl.BlockSpec((PAGE * 4, LANE), lambda b, g, pt, j=j: (page_base + pt[b, g * pg + j], 0))

    per_b = lambda r, w: pl.BlockSpec((1, r, w), lambda b, g, pt: (b, 0, 0))
    grid_spec = pltpu.PrefetchScalarGridSpec(
        num_scalar_prefetch=1, grid=(db, ng),
        in_specs=[page_spec(j) for j in range(pg)] + [
            per_b(rows, LANE),
            pl.BlockSpec((1, 1, tp, nbg), lambda b, g, pt: (b, g, 0, 0)),
            per_b(4 * tp, LANE), per_b(rows, LANE), per_b(rows, LANE),
            pl.BlockSpec((1, tp, 512), lambda b, g, pt: (b, 0, _UNIT["bz"] // 4))],
        out_specs=per_b(tp, 512),
        scratch_shapes=[pltpu.VMEM((rows, LANE), F32)] * 3)
    return pl.pallas_call(
        body, grid_spec=grid_spec,
        out_shape=jax.ShapeDtypeStruct((db, tp, 512), BF16),
        compiler_params=_params("parallel", "arbitrary"),
        name="b_selected_sample")(page_table, *([cache2] * pg), qs, selg, bkvs, ocw, g1b, proj3)


def _mlstm(proj3, conv_w, conv_b, gate_b, out_g, conv0, c0, n0, m0, *, lc, n_valid):
    b_, t, _ = proj3.shape
    col_head = jnp.arange(H_C * LANE)[None, :] // LANE
    lane_id = jnp.arange(LANE)[:, None]
    rsel = jnp.stack([lane_id == col_head, lane_id == col_head + H_C]).astype(BF16)
    nchunk = t // lc
    dqk = H_C * DQK_C
    tail = CONV_W - 1
    base = 8
    tsq = max(lc, LANE)

    def bcast_cols(x, sel):
        x1, x2, x3 = _split3(x)
        return _mm(x1, sel) + _mm(x2, sel) + _mm(x3, sel)

    def body(qk_ref, v_ref, if_ref, co_ref, cz_ref, cw_ref, cb_ref, gb_ref, g_ref, rsel_ref,
             conv0_ref, c0_ref, n0_ref, m0_ref,
             y_ref, conv_ref, c_ref, n_ref, m_ref, xbuf, c_scr, n_scr, m_scr):
        ci = pl.program_id(1)

        @pl.when(ci == 0)
        def _():
            xbuf[base - tail:base, :] = conv0_ref[0]
            c_scr[...] = c0_ref[0]
            n_scr[...] = n0_ref[0]
            m_scr[...] = m0_ref[0]

        xbuf[base:base + lc, :] = qk_ref[0]
        acc = cb_ref[...]
        for j in range(CONV_W):
            acc = acc + xbuf[base - tail + j:base - tail + j + lc, :] * cw_ref[j:j + 1, :]
        new_tail = xbuf[base + n_valid - tail:base + n_valid, :]
        xbuf[base - tail:base, :] = new_tail
        conv_ref[0] = new_tail
        qk = _silu(acc)

        gt = if_ref[0] + gb_ref[...]
        lf = jnp.minimum(gt, 0.0) - jnp.log(1.0 + jnp.exp(-jnp.abs(gt)))
        ti = lax.broadcasted_iota(jnp.int32, (lc, lc), 0)
        si = lax.broadcasted_iota(jnp.int32, (lc, lc), 1)
        tri = (si <= ti).astype(BF16)
        l1, l2, l3 = _split3(lf)
        bcum = _mm(tri, l1) + _mm(tri, l2) + _mm(tri, l3)
        zrow = jnp.zeros((tsq - lc, LANE), F32)
        gt_t = (jnp.concatenate([gt, zrow], axis=0) if tsq > lc else gt).T
        b_t = (jnp.concatenate([bcum, zrow], axis=0) if tsq > lc else bcum).T
        dmask = (si <= ti) & (si < n_valid)
        svalid = lax.broadcasted_iota(jnp.int32, (lc, LANE), 0) < n_valid
        icol_all = bcast_cols(gt, rsel_ref[0])
        bcol_all = bcast_cols(bcum, rsel_ref[1])
        gate_all = _sigmoid(co_ref[0]) * _silu(cz_ref[0])
        low = lax.broadcasted_iota(jnp.int32, (lc, LANE), 1) < DQK_C
        row_low = lax.broadcasted_iota(jnp.int32, (LANE, LANE), 0) < DQK_C

        c_new, n_new, m_new_all = [], [], []
        for j in range(H_C // 2):
            qp = qk[:, j * LANE:(j + 1) * LANE] * (DQK_C ** -0.5)
            kp = qk[:, dqk + j * LANE:dqk + (j + 1) * LANE]
            cp = c_scr[j]
            cpb = cp.astype(BF16)
            npair = n_scr[j:j + 1, :]
            c_upd, n_upd, carries = None, None, []
            for u in range(2):
                h = 2 * j + u
                sel = low if u == 0 else jnp.logical_not(low)
                qm = jnp.where(sel, qp, 0.0)
                km = jnp.where(sel, kp, 0.0)
                qmb, kmb = qm.astype(BF16), km.astype(BF16)
                vh = v_ref[0, :, h * HEAD:(h + 1) * HEAD]
                bcol = bcol_all[:, h * LANE:(h + 1) * LANE]
                icol = icol_all[:, h * LANE:(h + 1) * LANE]
                brow = b_t[H_C + h:H_C + h + 1, 0:lc]
                irow = gt_t[h:h + 1, 0:lc]
                m_h = m_scr[h:h + 1, :]

                d = jnp.where(dmask, bcol[:, 0:lc] - brow + irow, NEG)
                inter = bcol + m_h
                m_t = jnp.maximum(inter, jnp.max(d, axis=1, keepdims=True))
                w_intra = jnp.exp(d - m_t[:, 0:lc])
                w_inter = jnp.exp(inter - m_t)
                sqk = _nt(qmb, kmb) * w_intra
                num = w_inter * _mm(qmb, cpb) + _mm(sqk.astype(BF16), vh.astype(BF16))
                den = (w_inter * jnp.sum(qm * npair, axis=1, keepdims=True)
                       + jnp.sum(sqk, axis=1, keepdims=True))
                hh = num / jnp.maximum(jnp.abs(den), jnp.exp(-m_t))
                y_ref[0, :, h * HEAD:(h + 1) * HEAD] = (
                    _rms(hh, g_ref[...]) * gate_all[:, h * HEAD:(h + 1) * HEAD]).astype(BF16)

                b_last = bcol[n_valid - 1:n_valid, :]
                dec = jnp.where(svalid, b_last - bcol + icol, NEG)
                m_new = jnp.maximum(b_last + m_h, jnp.max(dec, axis=0, keepdims=True))
                wk = jnp.exp(dec - m_new)
                carries.append(jnp.exp(b_last + m_h - m_new))
                cu = _tn(kmb, (wk * vh).astype(BF16))
                nu = jnp.sum(wk * km, axis=0, keepdims=True)
                c_upd = cu if c_upd is None else c_upd + cu
                n_upd = nu if n_upd is None else n_upd + nu
                m_new_all.append(m_new)
            c_new.append(jnp.where(row_low, carries[0], carries[1]) * cp + c_upd)
            n_new.append(jnp.where(low[0:1], carries[0], carries[1]) * npair + n_upd)

        for j in range(H_C // 2):
            c_scr[j] = c_new[j]
            n_scr[j:j + 1, :] = n_new[j]
        for h in range(H_C):
            m_scr[h:h + 1, :] = m_new_all[h]
        c_ref[0] = c_scr[...]
        n_ref[0] = n_scr[...]
        m_ref[0] = m_scr[...]

    col = lambda u: pl.BlockSpec((1, lc, 1024), lambda b, c, u=u: (b, c, u))
    const = lambda *shape: pl.BlockSpec(shape, lambda b, c: (0,) * len(shape))
    per_b = lambda *shape: pl.BlockSpec((1,) + shape, lambda b, c: (b,) + (0,) * len(shape))
    return pl.pallas_call(
        body, grid=(b_, nchunk),
        in_specs=[col(_UNIT["cqk"] // 8), col(_UNIT["cv"] // 8),
                  pl.BlockSpec((1, lc, LANE), lambda b, c: (b, c, _UNIT["cif"])),
                  col(_UNIT["co"] // 8), col(_UNIT["cz"] // 8),
                  const(CONV_W, 1024), const(1, 1024), const(1, LANE), const(1, LANE),
                  const(2, LANE, H_C * LANE),
                  per_b(tail, 1024), per_b(H_C // 2, LANE, HEAD), per_b(H_C // 2, LANE), per_b(H_C, LANE)],
        out_specs=[pl.BlockSpec((1, lc, 1024), lambda b, c: (b, c, 0)),
                   per_b(tail, 1024), per_b(H_C // 2, LANE, HEAD), per_b(H_C // 2, LANE), per_b(H_C, LANE)],
        out_shape=[jax.ShapeDtypeStruct((b_, t, 1024), BF16),
                   jax.ShapeDtypeStruct((b_, tail, 1024), F32),
                   jax.ShapeDtypeStruct((b_, H_C // 2, LANE, HEAD), F32),
                   jax.ShapeDtypeStruct((b_, H_C // 2, LANE), F32),
                   jax.ShapeDtypeStruct((b_, H_C, LANE), F32)],
        scratch_shapes=[pltpu.VMEM((base + lc, 1024), F32), pltpu.VMEM((H_C // 2, LANE, HEAD), F32),
                        pltpu.VMEM((H_C // 2, LANE), F32), pltpu.VMEM((H_C, LANE), F32)],
        compiler_params=_params("parallel", "arbitrary"),
        name="mlstm")(proj3, proj3, proj3, proj3, proj3, conv_w, conv_b, gate_b, out_g, rsel,
                      conv0, c0, n0, m0)


def _w_prep(w_in):
    depth, d, n_in = w_in.shape
    kt_n = d // LANE
    per_col = kt_n * depth
    rows = w_in.reshape(depth, kt_n, LANE, n_in).transpose(3, 1, 0, 2).reshape(n_in * per_col, LANE)
    src_off, off = {}, 0
    for name, width in _SRC:
        src_off[name] = (off, width)
        off += width
    assert off == n_in
    starts, valids = [0] * (N_PROJ // LANE), [0] * (N_PROJ // LANE)
    for name in _DST_ORDER:
        s0, width = src_off[name]
        for u in range(-(-width // LANE)):
            starts[_UNIT[name] + u] = s0 + u * LANE
            valids[_UNIT[name] + u] = min(LANE, width - u * LANE)
    table = jnp.array([starts, valids], jnp.int32)

    def body(tbl_ref, w_ref, o_ref):
        u = pl.program_id(0)
        keep = lax.broadcasted_iota(jnp.int32, (LANE, LANE), 0) < tbl_ref[1, u]
        xt = jnp.swapaxes(w_ref[...].reshape(LANE, per_col, LANE), 0, 1)
        for l in range(depth):
            for kt in range(kt_n):
                o_ref[l, :, kt * LANE:(kt + 1) * LANE] = jnp.where(keep, xt[kt * depth + l], 0.0).astype(BF16)

    grid_spec = pltpu.PrefetchScalarGridSpec(
        num_scalar_prefetch=1, grid=(N_PROJ // LANE,),
        in_specs=[pl.BlockSpec((pl.Element(LANE * per_col), pl.Element(LANE)),
                               lambda u, tbl: (tbl[0, u] * per_col, 0))],
        out_specs=pl.BlockSpec((depth, LANE, d), lambda u, tbl: (0, u, 0)))
    return pl.pallas_call(
        body, grid_spec=grid_spec,
        out_shape=jax.ShapeDtypeStruct((depth, N_PROJ, d), BF16),
        compiler_params=_params("parallel"),
        name="w_prep")(table, rows)


def _c_from_pairs(c):
    b = c.shape[0]
    return jnp.swapaxes(c.reshape(b, H_C, DQK_C, HEAD), -1, -2)


def _pick(n, prefs):
    for p in prefs:
        if n % p == 0:
            return p
    return n


def kernel(x_prompt, x_sample, cache_a_kv, cache_b_kv, state_b_win, state_c_conv, state_c_C,
           state_c_n, state_c_m, page_table, norm_g, w_in, w_out, a_qk_g, a_lambda, a_out_g,
           b_qk_g, b_cmp_w, c_conv_w, c_conv_b, c_gate_b, c_out_g):
    bp, t, d = x_prompt.shape
    db, ts, _ = x_sample.shape
    depth = norm_g.shape[0]
    n_pool = cache_a_kv.shape[1]
    n_pages = page_table.shape[1]
    past_len = n_pages * PAGE
    wb = state_b_win.shape[2]
    tp = 8
    assert ts <= tp and t % MLSTM_CHUNK == 0 and t % Q_BLOCK == 0 and d == 2048
    mp, ms = bp * t, db * tp
    pg = _pick(n_pages, (16, 8, 4, 2, 1))

    cache_a2 = cache_a_kv.reshape(depth * n_pool * PAGE * 2 * H_A, HEAD)
    cache_b2 = cache_b_kv.reshape(depth * n_pool * PAGE * 4, HEAD)
    win_rows = state_b_win.reshape(depth, db, wb * 2, HEAD)
    hp = x_prompt.reshape(mp, d)
    hs = jnp.pad(x_sample, ((0, 0), (0, tp - ts), (0, 0))).reshape(ms, d)

    tm_p = _pick(mp, (1024, 512, 256, 128))
    tq_a = _pick(t, (512, 256, 128))
    outs_p = [[] for _ in range(7)]
    outs_s = [[] for _ in range(7)]

    w_all = _w_prep(w_in)
    for l in range(depth):
        lam_init = 0.8 - 0.6 * math.exp(-0.3 * l)
        w_o = w_out[l].astype(BF16)
        g_a = jnp.tile(a_qk_g[l], (1, 2))
        g_b = b_qk_g[l]
        wpb = jnp.broadcast_to(b_cmp_w[l][:, :, None], (2, CMP_BLOCK, LANE))
        wpb2 = jnp.concatenate([wpb[0], wpb[1]], axis=1)
        gate_b = jnp.pad(c_gate_b[l].reshape(1, 2 * H_C), ((0, 0), (0, LANE - 2 * H_C)))
        conv_b = c_conv_b[l].reshape(1, -1)
        out_gc = c_out_g[l].reshape(1, LANE)

        proj = _norm_matmul(hp, norm_g[l], w_all, l, tm=tm_p, tn=1024)
        proj3 = proj.reshape(bp, t, N_PROJ)
        qn, kn, vb, akv = _a_prep(proj, g_a, tm=_pick(mp, (512, 256, 128)))
        ya = _a_attn_prompt(qn.reshape(bp, t, 512), kn.reshape(bp, t, 512), vb.reshape(bp, t, 512),
                            proj3, a_lambda[l], a_out_g[l], lam_init, tq=tq_a)
        bkvs, neww, kvb, kcb, vcb = _b_prep(proj, g_b, wpb, tm=_pick(mp, (512, 256, 128)), pool=True)
        nblk = t // CMP_BLOCK
        yb = _b_attn_prompt(proj3, kcb.reshape(bp, nblk, LANE), vcb.reshape(bp, nblk, LANE),
                            kvb.reshape(bp, t, 512), g_b, t=t)
        yc, conv_p, c_p, n_p, m_p = _mlstm(
            proj3, c_conv_w[l], conv_b, gate_b, out_gc,
            jnp.zeros((bp, CONV_W - 1, 2 * H_C * DQK_C), F32), jnp.zeros((bp, H_C // 2, LANE, HEAD), F32),
            jnp.zeros((bp, H_C // 2, LANE), F32), jnp.zeros((bp, H_C, LANE), F32),
            lc=MLSTM_CHUNK, n_valid=MLSTM_CHUNK)
        c_p, n_p = _c_from_pairs(c_p), n_p.reshape(bp, H_C, DQK_C)
        hp = _out_proj(hp, ya.reshape(mp, 512), yb.reshape(mp, 512), yc.reshape(mp, 1024), w_o,
                       tm=tm_p, tn=1024)
        win_p = jnp.concatenate([jnp.zeros((bp, wb, 2, HEAD), F32), neww.reshape(bp, t, 2, HEAD)],
                                axis=1)[:, -wb:]
        for lst, a in zip(outs_p, (akv.reshape(bp, t, 2, H_A, HEAD), bkvs.reshape(bp, t, 4, HEAD),
                                   win_p, conv_p, c_p, n_p, m_p[:, :, 0])):
            lst.append(a)

        base = l * n_pool
        sproj = _norm_matmul(hs, norm_g[l], w_all, l, tm=ms, tn=1024)
        sproj3 = sproj.reshape(db, tp, N_PROJ)
        sqn, skn, svb, sakv = _a_prep(sproj, g_a, tm=ms)
        kvnew = jnp.concatenate([skn.reshape(db, tp, 512), svb.reshape(db, tp, 512)], axis=2)
        kvnew = jnp.pad(kvnew, ((0, 0), (0, PAGE - tp), (0, 0)))
        sya = _a_attn_sample(page_table, cache_a2, sqn.reshape(db, tp, 512), kvnew, sproj3,
                             a_lambda[l], a_out_g[l], lam_init, base, n_valid=ts, pg=pg)
        sbkvs, sneww, _ = _b_prep(sproj, g_b, wpb, tm=ms, pool=False)
        pooled = _b_pool_pages(page_table, cache_b2, wpb2, base, pg=pg)
        qs, ocw, g1b, sel = _b_select_sample(
            sproj3, pooled, win_rows[l], sneww.reshape(db, tp * 2, HEAD), g_b,
            past_len=past_len, n_valid=ts)
        nbg = pg * (PAGE // CMP_BLOCK)
        selg = sel.reshape(db, tp, n_pages // pg, nbg).transpose(0, 2, 1, 3)
        syb = _b_selected_sample(page_table, cache_b2, qs, selg, sbkvs.reshape(db, tp * 4, HEAD), ocw, g1b,
                                 sproj3, base, n_valid=ts, pg=pg)
        m0 = jnp.broadcast_to(state_c_m[l][:, :, None], (db, H_C, LANE))
        syc, conv_s, c_s, n_s, m_s = _mlstm(
            sproj3, c_conv_w[l], conv_b, gate_b, out_gc, state_c_conv[l],
            jnp.swapaxes(state_c_C[l], -1, -2).reshape(db, H_C // 2, LANE, HEAD),
            state_c_n[l].reshape(db, H_C // 2, LANE), m0, lc=tp, n_valid=ts)
        c_s, n_s = _c_from_pairs(c_s), n_s.reshape(db, H_C, DQK_C)
        hs = _out_proj(hs, sya.reshape(ms, 512), syb.reshape(ms, 512), syc.reshape(ms, 1024), w_o,
                       tm=ms, tn=1024)
        win_s = jnp.concatenate([state_b_win[l], sneww.reshape(db, tp, 2, HEAD)[:, :ts]], axis=1)[:, -wb:]
        for lst, a in zip(outs_s, (sakv.reshape(db, tp, 2, H_A, HEAD)[:, :ts],
                                   sbkvs.reshape(db, tp, 4, HEAD)[:, :ts],
                                   win_s, conv_s, c_s, n_s, m_s[:, :, 0])):
            lst.append(a)

    y_p = hp.reshape(bp, t, d)
    y_s = hs.reshape(db, tp, d)[:, :ts]
    sp = [jnp.stack(x) for x in outs_p]
    ss = [jnp.stack(x) for x in outs_s]
    return (y_p, y_s, sp[0], ss[0], sp[1], ss[1], sp[2], ss[2], sp[3], ss[3],
            sp[4], ss[4], sp[5], ss[5], sp[6], ss[6])
```

```python
import functools
import math

import jax
import jax.numpy as jnp
from jax import lax
from jax.experimental import pallas as pl
from jax.experimental.pallas import tpu as pltpu

F32 = jnp.float32
BF16 = jnp.bfloat16

EPS = 1e-6
LANE = 128
HEAD = 128
H_A = 4
DQ_A = HEAD // 2
H_B = 4
H_C = 8
DQK_C = HEAD // 2
CMP_BLOCK = 64
N_SEL = 16
WINDOW = 512
CONV_W = 4
MLSTM_CHUNK = 128
PAGE = 128
Q_BLOCK = 128
NEG = -1e30
MASKED = 2 * NEG
SCORE_LOG2E = math.log2(math.e)
VMEM_LIMIT = 56 * 1024 * 1024

_SRC = (("aq", 512), ("ak", 512), ("av", 512), ("az", 512),
        ("bq", 512), ("bkv", 768), ("bg", 12), ("bz", 512),
        ("cqk", 1024), ("cv", 1024), ("cif", 16), ("co", 1024), ("cz", 1024))
_DST_ORDER = ("aq", "ak", "av", "az", "bq", "bz", "bkv", "bg", "cif", "cqk", "cv", "co", "cz")
_UNIT = {"aq": 0, "ak": 4, "av": 8, "az": 12, "bq": 16, "bz": 20, "bkv": 24, "bg": 30,
         "cif": 31, "cqk": 32, "cv": 40, "co": 48, "cz": 56}
N_PROJ = 64 * LANE


def _nt(a, b):
    return lax.dot_general(a, b, (((1,), (1,)), ((), ())), preferred_element_type=F32)


def _tn(a, b):
    return lax.dot_general(a, b, (((0,), (0,)), ((), ())), preferred_element_type=F32)


def _mm(a, b):
    return jnp.dot(a, b, preferred_element_type=F32)


def _split2(x):
    hi = x.astype(BF16)
    lo = (x - hi.astype(F32)).astype(BF16)
    return hi, lo


def _split3(x):
    hi = x.astype(BF16)
    r = x - hi.astype(F32)
    mid = r.astype(BF16)
    lo = (r - mid.astype(F32)).astype(BF16)
    return hi, mid, lo


def _sigmoid(z):
    return 1.0 / (1.0 + jnp.exp(-z))


def _silu(z):
    return z * _sigmoid(z)


def _rms(x, g):
    return x * lax.rsqrt(jnp.mean(x * x, axis=-1, keepdims=True) + EPS) * g


def _div(x, n):
    return lax.shift_right_logical(x, int(math.log2(n)))


def _mod(x, n):
    return x & (n - 1)


def _online_update(s, v, m_ref, l_ref, acc_ref, pv_fn=None):
    w = s.shape[1]
    m_old = m_ref[...]
    m_new = jnp.maximum(m_old, jnp.max(s, axis=-1, keepdims=True))
    alpha = jnp.exp2(m_old - m_new)
    m_rep = m_new if w == LANE else jnp.concatenate([m_new] * (w // LANE), axis=1)
    p = jnp.exp2(s - m_rep)
    l_ref[...] = alpha * l_ref[...] + jnp.sum(p, axis=-1, keepdims=True)
    pb = p.astype(BF16)
    acc_ref[...] = alpha * acc_ref[...] + (_mm(pb, v) if pv_fn is None else pv_fn(pb))
    m_ref[...] = m_new


def _params(*sem):
    return pltpu.CompilerParams(dimension_semantics=sem, vmem_limit_bytes=VMEM_LIMIT)


def _norm_matmul(x, g, w, layer, *, tm, tn):
    m, d = x.shape
    n = w.shape[1]
    rc = min(tm, 256)

    def body(x_ref, g_ref, w_ref, o_ref, h_scr):
        @pl.when(pl.program_id(1) == 0)
        def _():
            def chunk(c, carry):
                r = pl.ds(pl.multiple_of(c * rc, rc), rc)
                h_scr[r, :] = _rms(x_ref[r, :], g_ref[...]).astype(BF16)
                return carry
            lax.fori_loop(0, tm // rc, chunk, 0)
        o_ref[...] = _nt(h_scr[...], w_ref[0])

    return pl.pallas_call(
        body, grid=(m // tm, n // tn),
        in_specs=[pl.BlockSpec((tm, d), lambda i, j: (i, 0)),
                  pl.BlockSpec((1, d), lambda i, j: (0, 0)),
                  pl.BlockSpec((1, tn, d), lambda i, j: (layer, j, 0))],
        out_specs=pl.BlockSpec((tm, tn), lambda i, j: (i, j)),
        out_shape=jax.ShapeDtypeStruct((m, n), F32),
        scratch_shapes=[pltpu.VMEM((tm, d), BF16)],
        compiler_params=_params("parallel", "arbitrary"),
        name="norm_matmul")(x, g.reshape(1, d), w)


def _out_proj(x, ya, yb, yc, w, *, tm, tn):
    m, d = x.shape
    da, db, dc = ya.shape[1], yb.shape[1], yc.shape[1]

    def body(x_ref, a_ref, b_ref, c_ref, w_ref, o_ref):
        mix = jnp.concatenate([a_ref[...], b_ref[...], c_ref[...]], axis=1)
        o_ref[...] = x_ref[...] + _mm(mix, w_ref[...])

    return pl.pallas_call(
        body, grid=(m // tm, d // tn),
        in_specs=[pl.BlockSpec((tm, tn), lambda i, j: (i, j)),
                  pl.BlockSpec((tm, da), lambda i, j: (i, 0)),
                  pl.BlockSpec((tm, db), lambda i, j: (i, 0)),
                  pl.BlockSpec((tm, dc), lambda i, j: (i, 0)),
                  pl.BlockSpec((da + db + dc, tn), lambda i, j: (0, j))],
        out_specs=pl.BlockSpec((tm, tn), lambda i, j: (i, j)),
        out_shape=jax.ShapeDtypeStruct((m, d), F32),
        compiler_params=_params("parallel", "parallel"),
        name="out_proj")(x, ya, yb, yc, w)


def _a_prep(proj, g2, *, tm):
    m = proj.shape[0]

    def body(q_ref, k_ref, v_ref, g_ref, qn_ref, kn_ref, vb_ref, akv_ref):
        r = _div(lax.broadcasted_iota(jnp.int32, (LANE, LANE), 0), DQ_A)
        c = _div(lax.broadcasted_iota(jnp.int32, (LANE, LANE), 1), DQ_A)
        seg = (r == c).astype(BF16)

        def segnorm(x, g):
            outs = []
            for u in range(4):
                xc = x[:, u * LANE:(u + 1) * LANE]
                hi, lo = _split2(xc * xc)
                s = _mm(hi, seg) + _mm(lo, seg)
                outs.append(xc * lax.rsqrt(s * (1.0 / DQ_A) + EPS) * g)
            return jnp.concatenate(outs, axis=1)

        qn = segnorm(q_ref[...], g_ref[0:1, :]) * (DQ_A ** -0.5 * SCORE_LOG2E)
        kn = segnorm(k_ref[...], g_ref[1:2, :])
        v = v_ref[...]
        qn_ref[...] = qn.astype(BF16)
        kn_ref[...] = kn.astype(BF16)
        vb_ref[...] = v.astype(BF16)
        for h in range(H_A):
            akv_ref[pl.ds(h, tm, stride=2 * H_A), :] = kn[:, h * LANE:(h + 1) * LANE]
            akv_ref[pl.ds(H_A + h, tm, stride=2 * H_A), :] = v[:, h * LANE:(h + 1) * LANE]

    blk = lambda u: pl.BlockSpec((tm, 512), lambda i, u=u: (i, u))
    row = lambda w: pl.BlockSpec((tm, w), lambda i: (i, 0))
    return pl.pallas_call(
        body, grid=(m // tm,),
        in_specs=[blk(_UNIT["aq"] // 4), blk(_UNIT["ak"] // 4), blk(_UNIT["av"] // 4),
                  pl.BlockSpec((2, LANE), lambda i: (0, 0))],
        out_specs=[row(512), row(512), row(512), pl.BlockSpec((tm * 2 * H_A, LANE), lambda i: (i, 0))],
        out_shape=[jax.ShapeDtypeStruct((m, 512), BF16)] * 3
        + [jax.ShapeDtypeStruct((m * 2 * H_A, LANE), F32)],
        compiler_params=_params("parallel"),
        name="a_prep")(proj, proj, proj, g2)


def _diff_lambda(lp, lam_init):
    a = jnp.sum(lp[0:1, :] * lp[1:2, :], axis=-1, keepdims=True)
    b = jnp.sum(lp[2:3, :] * lp[3:4, :], axis=-1, keepdims=True)
    return jnp.exp(a) - jnp.exp(b) + lam_init


def _a_finish(acc1, l1, acc2, l2, lam, g, z, lam_init):
    o = acc1 / l1 - lam * (acc2 / l2)
    return _rms(o, g) * (1.0 - lam_init) * _silu(z)


def _a_attn_prompt(qn, kn, vb, proj3, lam_p, out_g, lam_init, *, tq):
    b_, t, _ = qn.shape
    tk = tq

    def body(q_ref, k_ref, v_ref, z_ref, lam_ref, g_ref, o_ref, m_scr, l_scr, acc_scr):
        qi = pl.program_id(2)
        q = q_ref[0]
        lane = lax.broadcasted_iota(jnp.int32, (tq, LANE), 1)
        zero = jnp.zeros_like(q)
        qs = (jnp.where(lane < DQ_A, q, zero), jnp.where(lane >= DQ_A, q, zero))
        m_scr[...] = jnp.full(m_scr.shape, NEG, F32)
        l_scr[...] = jnp.zeros(l_scr.shape, F32)
        acc_scr[...] = jnp.zeros(acc_scr.shape, F32)

        def chunk(kc, mask):
            ks = pl.ds(pl.multiple_of(kc * tk, tk), tk)
            k = k_ref[0, ks, :]
            v = v_ref[0, ks, :]
            for c in range(2):
                s = _nt(qs[c], k)
                if mask is not None:
                    s = jnp.where(mask, s, MASKED)
                _online_update(s, v, m_scr.at[c], l_scr.at[c], acc_scr.at[c])

        def full_chunk(kc, carry):
            chunk(kc, None)
            return carry

        lax.fori_loop(0, qi, full_chunk, 0)
        chunk(qi, lax.broadcasted_iota(jnp.int32, (tq, tk), 1)
              <= lax.broadcasted_iota(jnp.int32, (tq, tk), 0))
        lam = _diff_lambda(lam_ref[...], lam_init)
        y = _a_finish(acc_scr[0], l_scr[0], acc_scr[1], l_scr[1], lam, g_ref[...], z_ref[0], lam_init)
        o_ref[0] = y.astype(BF16)

    head = lambda rows, qdep: pl.BlockSpec(
        (1, rows, LANE), (lambda b, h, i: (b, i, h)) if qdep else (lambda b, h, i: (b, 0, h)))
    return pl.pallas_call(
        body, grid=(b_, H_A, t // tq),
        in_specs=[head(tq, True), head(t, False), head(t, False),
                  pl.BlockSpec((1, tq, LANE), lambda b, h, i: (b, i, _UNIT["az"] + h)),
                  pl.BlockSpec((4, DQ_A), lambda b, h, i: (0, 0)),
                  pl.BlockSpec((1, LANE), lambda b, h, i: (0, 0))],
        out_specs=head(tq, True),
        out_shape=jax.ShapeDtypeStruct((b_, t, 512), BF16),
        scratch_shapes=[pltpu.VMEM((2, tq, LANE), F32)] * 3,
        compiler_params=_params("parallel", "parallel", "arbitrary"),
        name="a_attn_prompt")(qn, kn, vb, proj3, lam_p, out_g.reshape(1, LANE))


def _a_attn_sample(page_table, cache2, qn, kvnew, proj3, lam_p, out_g, lam_init, page_base,
                   *, n_valid, pg):
    db, n_pages = page_table.shape
    tp = qn.shape[1]
    ng = n_pages // pg

    def body(pt_ref, *refs):
        pages = refs[:pg]
        q_ref, new_ref, z_ref, lam_ref, g_ref, o_ref, m_scr, l_scr, acc_scr = refs[pg:]
        gi = pl.program_id(1)

        @pl.when(gi == 0)
        def _():
            m_scr[...] = jnp.full(m_scr.shape, NEG, F32)
            l_scr[...] = jnp.zeros(l_scr.shape, F32)
            acc_scr[...] = jnp.zeros(acc_scr.shape, F32)

        lane = lax.broadcasted_iota(jnp.int32, (tp, LANE), 1)

        def qstack(h):
            q = q_ref[0, :, h * LANE:(h + 1) * LANE]
            zero = jnp.zeros_like(q)
            return jnp.concatenate([jnp.where(lane < DQ_A, q, zero),
                                    jnp.where(lane >= DQ_A, q, zero)], axis=0)

        pts = [jnp.swapaxes(pages[j][...].reshape(PAGE, 2 * H_A, LANE), 0, 1) for j in range(pg)]

        def slot(s_):
            return jnp.concatenate([pts[j][s_].astype(BF16) for j in range(pg)], axis=0)

        s = jnp.concatenate([_nt(qstack(h), slot(h)) for h in range(H_A)], axis=0)
        _online_update(s, None, m_scr, l_scr, acc_scr,
                       pv_fn=lambda p: jnp.concatenate(
                           [_mm(p[h * 2 * tp:(h + 1) * 2 * tp], slot(H_A + h)) for h in range(H_A)], axis=0))

        @pl.when(gi == ng - 1)
        def _():
            lam = _diff_lambda(lam_ref[...], lam_init)
            row = _mod(lax.broadcasted_iota(jnp.int32, (2 * tp, PAGE), 0), tp)
            col = lax.broadcasted_iota(jnp.int32, (2 * tp, PAGE), 1)
            mask = (col <= row) & (col < n_valid)
            mask4 = jnp.concatenate([mask] * H_A, axis=0)
            s = jnp.concatenate([_nt(qstack(h), new_ref[0, :, h * LANE:(h + 1) * LANE])
                                 for h in range(H_A)], axis=0)
            _online_update(jnp.where(mask4, s, MASKED), None, m_scr, l_scr, acc_scr,
                           pv_fn=lambda p: jnp.concatenate(
                               [_mm(p[h * 2 * tp:(h + 1) * 2 * tp],
                                    new_ref[0, :, 512 + h * LANE:512 + (h + 1) * LANE])
                                for h in range(H_A)], axis=0))
            for h in range(H_A):
                r0 = h * 2 * tp
                acc = acc_scr[r0:r0 + 2 * tp, :]
                l = l_scr[r0:r0 + 2 * tp, :]
                y = _a_finish(acc[0:tp], l[0:tp], acc[tp:], l[tp:], lam, g_ref[...],
                              z_ref[0, :, h * LANE:(h + 1) * LANE], lam_init)
                o_ref[0, :, h * LANE:(h + 1) * LANE] = y.astype(BF16)

    def page_spec(j):
        return pl.BlockSpec((PAGE * 2 * H_A, LANE),
                            lambda b, g, pt, j=j: (page_base + pt[b, g * pg + j], 0))

    in_specs = [page_spec(j) for j in range(pg)] + [
            pl.BlockSpec((1, tp, 512), lambda b, g, pt: (b, 0, 0)),
            pl.BlockSpec((1, PAGE, 1024), lambda b, g, pt: (b, 0, 0)),
            pl.BlockSpec((1, tp, 512), lambda b, g, pt: (b, 0, _UNIT["az"] // 4)),
            pl.BlockSpec((4, DQ_A), lambda b, g, pt: (0, 0)),
            pl.BlockSpec((1, LANE), lambda b, g, pt: (0, 0))]
    grid_spec = pltpu.PrefetchScalarGridSpec(
        num_scalar_prefetch=1, grid=(db, ng), in_specs=in_specs,
        out_specs=pl.BlockSpec((1, tp, 512), lambda b, g, pt: (b, 0, 0)),
        scratch_shapes=[pltpu.VMEM((H_A * 2 * tp, LANE), F32)] * 3)
    return pl.pallas_call(
        body, grid_spec=grid_spec,
        out_shape=jax.ShapeDtypeStruct((db, tp, 512), BF16),
        compiler_params=_params("parallel", "arbitrary"),
        name="a_attn_sample")(page_table, *([cache2] * pg), qn, kvnew, proj3, lam_p,
                              out_g.reshape(1, LANE))


def _b_prep(proj, g4, wpb, *, tm, pool):
    m = proj.shape[0]
    nb = tm // CMP_BLOCK

    def body(kv_ref, g_ref, wp_ref, bkv_ref, nw_ref, kvb_ref, *pool_refs):
        kc = kv_ref[:, 0:128]
        vc = kv_ref[:, 128:256]
        ks = _rms(kv_ref[:, 256:384], g_ref[2:3, :])
        vs = kv_ref[:, 384:512]
        kw = _rms(kv_ref[:, 512:640], g_ref[3:4, :])
        vw = kv_ref[:, 640:768]
        for slot, x in enumerate((kc, vc, ks, vs)):
            bkv_ref[pl.ds(slot, tm, stride=4), :] = x
        nw_ref[pl.ds(0, tm, stride=2), :] = kw
        nw_ref[pl.ds(1, tm, stride=2), :] = vw
        kvb_ref[:, 0:128] = ks.astype(BF16)
        kvb_ref[:, 128:256] = vs.astype(BF16)
        kvb_ref[:, 256:384] = kw.astype(BF16)
        kvb_ref[:, 384:512] = vw.astype(BF16)
        if pool:
            kcb_ref, vcb_ref = pool_refs
            kp = jnp.sum(kc.reshape(nb, CMP_BLOCK, LANE) * wp_ref[0][None], axis=1)
            vp = jnp.sum(vc.reshape(nb, CMP_BLOCK, LANE) * wp_ref[1][None], axis=1)
            kcb_ref[...] = _rms(kp, g_ref[1:2, :])
            vcb_ref[...] = vp

    row = lambda w: pl.BlockSpec((tm, w), lambda i: (i, 0))
    out_specs = [pl.BlockSpec((tm * 4, LANE), lambda i: (i, 0)),
                 pl.BlockSpec((tm * 2, LANE), lambda i: (i, 0)), row(512)]
    out_shape = [jax.ShapeDtypeStruct((m * 4, LANE), F32), jax.ShapeDtypeStruct((m * 2, LANE), F32),
                 jax.ShapeDtypeStruct((m, 512), BF16)]
    if pool:
        out_specs += [pl.BlockSpec((nb, LANE), lambda i: (i, 0))] * 2
        out_shape += [jax.ShapeDtypeStruct((m // CMP_BLOCK, LANE), F32)] * 2
    return pl.pallas_call(
        body, grid=(m // tm,),
        in_specs=[pl.BlockSpec((tm, 768), lambda i: (i, _UNIT["bkv"] // 6)),
                  pl.BlockSpec((4, LANE), lambda i: (0, 0)),
                  pl.BlockSpec((2, CMP_BLOCK, LANE), lambda i: (0, 0, 0))],
        out_specs=out_specs, out_shape=out_shape,
        compiler_params=_params("parallel"),
        name="b_prep")(proj, g4, wpb)


def _cmp_scores(qf, kcb):
    qh, ql = _split2(qf)
    kh, kl = _split2(kcb)
    return _nt(qh, kh) + _nt(qh, kl) + _nt(ql, kh)


def _b_attn_prompt(proj3, kcb, vcb, kvb, g4, *, t):
    b_ = proj3.shape[0]
    tq = Q_BLOCK
    nblk = t // CMP_BLOCK
    n_top = min(N_SEL, nblk)
    tk = min(512, t)
    span = min(WINDOW + tq, t)
    scale = HEAD ** -0.5 * SCORE_LOG2E

    def body(q_ref, kcb_ref, vcb_ref, kv_ref, bg_ref, bz_ref, g_ref, o_ref, m_scr, l_scr, acc_scr):
        qi = pl.program_id(1)
        qf = jnp.concatenate([_rms(q_ref[0, :, h * LANE:(h + 1) * LANE], g_ref[0:1, :]) * scale
                              for h in range(H_B)], axis=0)
        qb = qf.astype(BF16)
        pos = qi * tq + lax.broadcasted_iota(jnp.int32, (tq, 1), 0)

        pos_l = qi * tq + lax.broadcasted_iota(jnp.int32, (1, tq), 1)
        cur_l = _div(pos_l, CMP_BLOCK)
        blk_s = lax.broadcasted_iota(jnp.int32, (nblk, tq), 0)
        cmask = blk_s < cur_l
        kh, kl = _split2(kcb_ref[0])
        qh, ql = _split2(qf)
        cmask4 = jnp.concatenate([cmask] * H_B, axis=1)
        s = jnp.where(cmask4, _nt(kh, qh) + _nt(kl, qh) + _nt(kh, ql), NEG)
        mx = jnp.max(s, axis=0, keepdims=True)
        p = jnp.where(cmask4, jnp.exp2(s - mx), 0.0)
        pc = p / jnp.maximum(jnp.sum(p, axis=0, keepdims=True), 1e-30)
        pcs = [pc[:, h * tq:(h + 1) * tq] for h in range(H_B)]
        imp = jnp.where(cmask, pcs[0] + pcs[1] + pcs[2] + pcs[3], -1.0)
        fill = LANE - H_B * nblk
        pcq = jnp.concatenate(pcs + ([jnp.zeros((fill, tq), F32)] if fill else []), axis=0).T.astype(BF16)
        vch = vcb_ref[0].astype(BF16)
        zblk = jnp.zeros((nblk, LANE), BF16)
        vdiag = jnp.concatenate(
            [jnp.concatenate([vch if c == h else zblk for c in range(H_B)], axis=1)
             for h in range(H_B)] + ([jnp.zeros((fill, H_B * LANE), BF16)] if fill else []), axis=0)
        o_cmp = _mm(pcq, vdiag)

        rank = jnp.zeros((nblk, tq), F32)
        for mrow in range(nblk):
            cm = imp[mrow:mrow + 1, :]
            ahead = (cm > imp) | ((cm == imp) & (mrow < blk_s))
            rank = rank + ahead.astype(F32)
        sel_t = (((rank < n_top) & (imp >= 0.0)) | (blk_s == cur_l)).astype(F32)
        selq = jnp.concatenate([sel_t, jnp.zeros((LANE - nblk, tq), F32)], axis=0).T.astype(BF16)

        m_scr[...] = jnp.full(m_scr.shape, NEG, F32)
        l_scr[...] = jnp.zeros(l_scr.shape, F32)
        acc_scr[...] = jnp.zeros(acc_scr.shape, F32)
        nk = _div(qi * tq + tq + tk - 1, tk)

        def chunk(c, carry):
            ks_ = pl.ds(pl.multiple_of(c * tk, tk), tk)
            kidx = c * tk + lax.broadcasted_iota(jnp.int32, (LANE, tk), 1)
            e = (_div(kidx, CMP_BLOCK) == lax.broadcasted_iota(jnp.int32, (LANE, tk), 0)).astype(BF16)
            kpos = c * tk + lax.broadcasted_iota(jnp.int32, (tq, tk), 1)
            mk = (_mm(selq, e) > 0.5) & (kpos <= pos)
            mask = jnp.concatenate([mk] * H_B, axis=0)
            s = jnp.where(mask, _nt(qb, kv_ref[0, ks_, 0:128]), MASKED)
            _online_update(s, kv_ref[0, ks_, 128:256], m_scr, l_scr, acc_scr)
            return carry

        lax.fori_loop(0, nk, chunk, 0)

        start = jnp.clip(qi * tq + tq - span, 0, t - span)
        ws = pl.ds(pl.multiple_of(start, tq), span)
        diff = pos - (start + lax.broadcasted_iota(jnp.int32, (tq, span), 1))
        wmask = jnp.concatenate([(diff >= 0) & (diff < WINDOW)] * H_B, axis=0)
        s = jnp.where(wmask, _nt(qb, kv_ref[0, ws, 256:384]), MASKED)
        p = jnp.exp2(s - jnp.maximum(jnp.max(s, axis=-1, keepdims=True), NEG))
        o_win = _mm(p.astype(BF16), kv_ref[0, ws, 384:512]) \
            / jnp.maximum(jnp.sum(p, axis=-1, keepdims=True), 1e-30)
        o_sel = acc_scr[...] / jnp.maximum(l_scr[...], 1e-30)

        gate = _sigmoid(bg_ref[0])
        for h in range(H_B):
            rows = slice(h * tq, (h + 1) * tq)
            ob = (gate[:, 3 * h:3 * h + 1] * o_cmp[:, h * LANE:(h + 1) * LANE]
                  + gate[:, 3 * h + 1:3 * h + 2] * o_sel[rows]
                  + gate[:, 3 * h + 2:3 * h + 3] * o_win[rows])
            y = ob * _silu(bz_ref[0, :, h * LANE:(h + 1) * LANE])
            o_ref[0, :, h * LANE:(h + 1) * LANE] = y.astype(BF16)

    full = lambda rows, w: pl.BlockSpec((1, rows, w), lambda b, i: (b, 0, 0))
    return pl.pallas_call(
        body, grid=(b_, t // tq),
        in_specs=[pl.BlockSpec((1, tq, 512), lambda b, i: (b, i, _UNIT["bq"] // 4)),
                  full(nblk, LANE), full(nblk, LANE), full(t, 512),
                  pl.BlockSpec((1, tq, LANE), lambda b, i: (b, i, _UNIT["bg"])),
                  pl.BlockSpec((1, tq, 512), lambda b, i: (b, i, _UNIT["bz"] // 4)),
                  pl.BlockSpec((4, LANE), lambda b, i: (0, 0))],
        out_specs=pl.BlockSpec((1, tq, 512), lambda b, i: (b, i, 0)),
        out_shape=jax.ShapeDtypeStruct((b_, t, 512), BF16),
        scratch_shapes=[pltpu.VMEM((H_B * tq, LANE), F32)] * 3,
        compiler_params=_params("parallel", "arbitrary"),
        name="b_attn_prompt")(proj3, kcb, vcb, kvb, proj3, proj3, g4)


def _b_pool_pages(page_table, cache2, wpb2, page_base, *, pg):
    db, n_pages = page_table.shape
    ng = n_pages // pg
    per = PAGE // CMP_BLOCK

    def body(pt_ref, *refs):
        pages = refs[:pg]
        wp_ref, o_ref = refs[pg:]
        rows = []
        for j in range(pg):
            for u in range(per):
                parts = []
                for kind in range(2):
                    x = pages[j][pl.ds(u * CMP_BLOCK * 4 + kind, CMP_BLOCK, stride=4), :]
                    parts.append(jnp.sum(x * wp_ref[:, kind * LANE:(kind + 1) * LANE], axis=0, keepdims=True))
                rows.append(jnp.concatenate(parts, axis=1))
        o_ref[0] = jnp.concatenate(rows, axis=0)

    def page_spec(j):
        return pl.BlockSpec((PAGE * 4, LANE), lambda b, g, pt, j=j: (page_base + pt[b, g * pg + j], 0))

    grid_spec = pltpu.PrefetchScalarGridSpec(
        num_scalar_prefetch=1, grid=(db, ng),
        in_specs=[page_spec(j) for j in range(pg)] + [
            pl.BlockSpec((CMP_BLOCK, 256), lambda b, g, pt: (0, 0))],
        out_specs=pl.BlockSpec((1, pg * per, 256), lambda b, g, pt: (b, g, 0)))
    return pl.pallas_call(
        body, grid_spec=grid_spec,
        out_shape=jax.ShapeDtypeStruct((db, n_pages * per, 256), F32),
        compiler_params=_params("parallel", "parallel"),
        name="b_pool_pages")(page_table, *([cache2] * pg), wpb2)


def _b_select_sample(proj3, pooled, win, neww, g4, *, past_len, n_valid):
    db = neww.shape[0]
    tp = neww.shape[1] // 2
    nblk = pooled.shape[1]
    n_top = min(N_SEL, nblk + 1)
    wb = win.shape[1] // 2
    rows = H_B * tp
    scale = HEAD ** -0.5 * SCORE_LOG2E

    def body(q_ref, pool_ref, win_ref, nw_ref, bg_ref, g_ref, qs_ref, ocw_ref, g1_ref, sel_ref):
        qf = jnp.concatenate(
            [_rms(q_ref[0, :, h * LANE:(h + 1) * LANE], g_ref[0:1, :]) * scale for h in range(H_B)],
            axis=0)
        qb = qf.astype(BF16)
        qs_ref[0] = qb
        tok = _mod(lax.broadcasted_iota(jnp.int32, (rows, 1), 0), tp)
        pos = past_len + tok
        cur = _div(pos, CMP_BLOCK)
        blk = lax.broadcasted_iota(jnp.int32, (rows, nblk), 1)
        cmask = blk < cur

        kc = _rms(pool_ref[0, :, 0:128], g_ref[1:2, :])
        s = jnp.where(cmask, _cmp_scores(qf, kc), NEG)
        mx = jnp.max(s, axis=-1, keepdims=True)
        p = jnp.where(cmask, jnp.exp2(s - mx), 0.0)
        pc = p / jnp.maximum(jnp.sum(p, axis=-1, keepdims=True), 1e-30)
        o_cmp = _mm(pc.astype(BF16), pool_ref[0, :, 128:256].astype(BF16))
        imp = pc[0:tp]
        for h in range(1, H_B):
            imp = imp + pc[h * tp:(h + 1) * tp]
        imp = jnp.where(cmask[0:tp], imp, -1.0)

        pad = jnp.concatenate([imp, jnp.zeros((LANE - tp, nblk), F32)], axis=0)
        imp_t = jnp.concatenate([pad[:, u * LANE:(u + 1) * LANE].T for u in range(nblk // LANE)], axis=0)
        mi = lax.broadcasted_iota(jnp.int32, (nblk, nblk), 0)
        ni = lax.broadcasted_iota(jnp.int32, (nblk, nblk), 1)
        sels = []
        for tkn in range(tp):
            r = imp[tkn:tkn + 1, :]
            c = imp_t[:, tkn:tkn + 1]
            ahead = (c > r) | ((c == r) & (mi < ni))
            rank = jnp.sum(ahead.astype(F32), axis=0, keepdims=True)
            sels.append(((rank < n_top) & (r >= 0.0)).astype(F32))
        sel_ref[0] = jnp.concatenate(sels, axis=0)

        kw = win_ref[0, pl.ds(0, wb, stride=2), :].astype(BF16)
        vw = win_ref[0, pl.ds(1, wb, stride=2), :].astype(BF16)
        zpad = jnp.zeros((LANE - tp, LANE), F32)
        kn = jnp.concatenate([nw_ref[0, pl.ds(0, tp, stride=2), :], zpad], axis=0).astype(BF16)
        vn = jnp.concatenate([nw_ref[0, pl.ds(1, tp, stride=2), :], zpad], axis=0).astype(BF16)
        jw =lax.broadcasted_iota(jnp.int32, (rows, wb), 1)
        dw = pos - (past_len - wb + jw)
        jn = lax.broadcasted_iota(jnp.int32, (rows, LANE), 1)
        dn = tok - jn
        wmask = jnp.concatenate([(dw >= 0) & (dw < WINDOW), (dn >= 0) & (dn < WINDOW) & (jn < n_valid)],
                                axis=1)
        sw = jnp.where(wmask, jnp.concatenate([_nt(qb, kw), _nt(qb, kn)], axis=1), NEG)
        mw = jnp.max(sw, axis=-1, keepdims=True)
        pw = jnp.where(wmask, jnp.exp2(sw - mw), 0.0)
        o_win = (_mm(pw[:, 0:wb].astype(BF16), vw) + _mm(pw[:, wb:].astype(BF16), vn)) \
            / jnp.maximum(jnp.sum(pw, axis=-1, keepdims=True), 1e-30)

        gate = _sigmoid(bg_ref[0])
        g0 = jnp.concatenate([gate[:, 3 * h:3 * h + 1] for h in range(H_B)], axis=0)
        g1 = jnp.concatenate([gate[:, 3 * h + 1:3 * h + 2] for h in range(H_B)], axis=0)
        g2 = jnp.concatenate([gate[:, 3 * h + 2:3 * h + 3] for h in range(H_B)], axis=0)
        ocw_ref[0] = g0 * o_cmp + g2 * o_win
        g1_ref[0] = jnp.broadcast_to(g1, (rows, LANE))

    per_b = lambda r, w: pl.BlockSpec((1, r, w), lambda b: (b, 0, 0))
    return pl.pallas_call(
        body, grid=(db,),
        in_specs=[pl.BlockSpec((1, tp, 512), lambda b: (b, 0, _UNIT["bq"] // 4)),
                  per_b(nblk, 256), per_b(2 * wb, LANE), per_b(2 * tp, LANE),
                  pl.BlockSpec((1, tp, LANE), lambda b: (b, 0, _UNIT["bg"])),
                  pl.BlockSpec((4, LANE), lambda b: (0, 0))],
        out_specs=[per_b(rows, LANE), per_b(rows, LANE), per_b(rows, LANE), per_b(tp, nblk)],
        out_shape=[jax.ShapeDtypeStruct((db, rows, LANE), BF16),
                   jax.ShapeDtypeStruct((db, rows, LANE), F32),
                   jax.ShapeDtypeStruct((db, rows, LANE), F32),
                   jax.ShapeDtypeStruct((db, tp, nblk), F32)],
        compiler_params=_params("parallel"),
        name="b_select_sample")(proj3, pooled, win, neww, proj3, g4)


def _b_selected_sample(page_table, cache2, qs, selg, bkvs, ocw, g1b, proj3, page_base,
                       *, n_valid, pg):
    db, n_pages = page_table.shape
    ng = n_pages // pg
    rows = qs.shape[1]
    tp = rows // H_B
    per = PAGE // CMP_BLOCK
    nbg = pg * per
    width = pg * PAGE

    def body(pt_ref, *refs):
        pages = refs[:pg]
        q_ref, sel_ref, new_ref, ocw_ref, g1_ref, z_ref, o_ref, m_scr, l_scr, acc_scr = refs[pg:]
        gi = pl.program_id(1)

        @pl.when(gi == 0)
        def _():
            m_scr[...] = jnp.full(m_scr.shape, NEG, F32)
            l_scr[...] = jnp.zeros(l_scr.shape, F32)
            acc_scr[...] = jnp.zeros(acc_scr.shape, F32)

        q = q_ref[0]

        def slot(s_):
            return jnp.concatenate([pages[j][pl.ds(s_, PAGE, stride=4), :].astype(BF16)
                                    for j in range(pg)], axis=0)

        e = (_div(lax.broadcasted_iota(jnp.int32, (nbg, width), 1), CMP_BLOCK)
             == lax.broadcasted_iota(jnp.int32, (nbg, width), 0)).astype(BF16)
        mk = _mm(sel_ref[0, 0].astype(BF16), e) > 0.5
        mask = jnp.concatenate([mk] * H_B, axis=0)
        _online_update(jnp.where(mask, _nt(q, slot(2)), MASKED), slot(3), m_scr, l_scr, acc_scr)

        @pl.when(gi == ng - 1)
        def _():
            zpad = jnp.zeros((PAGE - tp, LANE), F32)
            kn = jnp.concatenate([new_ref[0, pl.ds(2, tp, stride=4), :], zpad], axis=0).astype(BF16)
            vn = jnp.concatenate([new_ref[0, pl.ds(3, tp, stride=4), :], zpad], axis=0).astype(BF16)
            tok = _mod(lax.broadcasted_iota(jnp.int32, (rows, PAGE), 0), tp)
            col = lax.broadcasted_iota(jnp.int32, (rows, PAGE), 1)
            nmask = (col <= tok) & (col < n_valid)
            _online_update(jnp.where(nmask, _nt(q, kn), MASKED), vn, m_scr, l_scr, acc_scr)
            o_sel = acc_scr[...] / jnp.maximum(l_scr[...], 1e-30)
            ob = ocw_ref[0] + g1_ref[0] * o_sel
            for h in range(H_B):
                y = ob[h * tp:(h + 1) * tp] * _silu(z_ref[0, :, h * LANE:(h + 1) * LANE])
                o_ref[0, :, h * LANE:(h + 1) * LANE] = y.astype(BF16)

    def page_spec(j):
        return pl.BlockSpec((PAGE * 4, LANE), lambda b, g, pt, j=j: (page_base + pt[b, g * pg + j], 0))

    per_b = lambda r, w: pl.BlockSpec((1, r, w), lambda b, g, pt: (b, 0, 0))
    grid_spec = pltpu.PrefetchScalarGridSpec(
        num_scalar_prefetch=1, grid=(db, ng),
        in_specs=[page_spec(j) for j in range(pg)] + [
            per_b(rows, LANE),
            pl.BlockSpec((1, 1, tp, nbg), lambda b, g, pt: (b, g, 0, 0)),
            per_b(4 * tp, LANE), per_b(rows, LANE), per_b(rows, LANE),
            pl.BlockSpec((1, tp, 512), lambda b, g, pt: (b, 0, _UNIT["bz"] // 4))],
        out_specs=per_b(tp, 512),
        scratch_shapes=[pltpu.VMEM((rows, LANE), F32)] * 3)
    return pl.pallas_call(
        body, grid_spec=grid_spec,
        out_shape=jax.ShapeDtypeStruct((db, tp, 512), BF16),
        compiler_params=_params("parallel", "arbitrary"),
        name="b_selected_sample")(page_table, *([cache2] * pg), qs, selg, bkvs, ocw, g1b, proj3)


def _mlstm(proj3, conv_w, conv_b, gate_b, out_g, conv0, c0, n0, m0, *, lc, n_valid):
    b_, t, _ = proj3.shape
    col_head = jnp.arange(H_C * LANE)[None, :] // LANE
    lane_id = jnp.arange(LANE)[:, None]
    rsel = jnp.stack([lane_id == col_head, lane_id == col_head + H_C]).astype(BF16)
    nchunk = t // lc
    dqk = H_C * DQK_C
    tail = CONV_W - 1
    base = 8
    tsq = max(lc, LANE)

    def bcast_cols(x, sel):
        hi, lo = _split2(x)
        return _mm(hi, sel) + _mm(lo, sel)

    def body(qk_ref, v_ref, if_ref, co_ref, cz_ref, cw_ref, cb_ref, gb_ref, g_ref, rsel_ref,
             conv0_ref, c0_ref, n0_ref, m0_ref,
             y_ref, conv_ref, c_ref, n_ref, m_ref, xbuf, c_scr, n_scr, m_scr):
        ci = pl.program_id(1)

        @pl.when(ci == 0)
        def _():
            xbuf[base - tail:base, :] = conv0_ref[0]
            c_scr[...] = c0_ref[0]
            n_scr[...] = n0_ref[0]
            m_scr[...] = m0_ref[0]

        xbuf[base:base + lc, :] = qk_ref[0]
        acc = cb_ref[...]
        for j in range(CONV_W):
            acc = acc + xbuf[base - tail + j:base - tail + j + lc, :] * cw_ref[j:j + 1, :]
        new_tail = xbuf[base + n_valid - tail:base + n_valid, :]
        xbuf[base - tail:base, :] = new_tail
        conv_ref[0] = new_tail
        qk = _silu(acc)

        gt = if_ref[0] + gb_ref[...]
        lf = jnp.minimum(gt, 0.0) - jnp.log(1.0 + jnp.exp(-jnp.abs(gt)))
        ti = lax.broadcasted_iota(jnp.int32, (lc, lc), 0)
        si = lax.broadcasted_iota(jnp.int32, (lc, lc), 1)
        tri = (si <= ti).astype(BF16)
        l1, l2, l3 = _split3(lf)
        bcum = _mm(tri, l1) + _mm(tri, l2) + _mm(tri, l3)
        zrow = jnp.zeros((tsq - lc, LANE), F32)
        gt_t = (jnp.concatenate([gt, zrow], axis=0) if tsq > lc else gt).T
        b_t = (jnp.concatenate([bcum, zrow], axis=0) if tsq > lc else bcum).T
        dmask = (si <= ti) & (si < n_valid)
        svalid = lax.broadcasted_iota(jnp.int32, (lc, LANE), 0) < n_valid
        icol_all = bcast_cols(gt, rsel_ref[0])
        bcol_all = bcast_cols(bcum, rsel_ref[1])
        gate_all = _sigmoid(co_ref[0]) * _silu(cz_ref[0])
        low = lax.broadcasted_iota(jnp.int32, (lc, LANE), 1) < DQK_C
        row_low = lax.broadcasted_iota(jnp.int32, (LANE, LANE), 0) < DQK_C

        c_new, n_new, m_new_all = [], [], []
        for j in range(H_C // 2):
            qp = qk[:, j * LANE:(j + 1) * LANE] * (DQK_C ** -0.5)
            kp = qk[:, dqk + j * LANE:dqk + (j + 1) * LANE]
            cp = c_scr[j]
            cpb = cp.astype(BF16)
            npair = n_scr[j:j + 1, :]
            c_upd, n_upd, carries = None, None, []
            for u in range(2):
                h = 2 * j + u
                sel = low if u == 0 else jnp.logical_not(low)
                qm = jnp.where(sel, qp, 0.0)
                km = jnp.where(sel, kp, 0.0)
                qmb, kmb = qm.astype(BF16), km.astype(BF16)
                vh = v_ref[0, :, h * HEAD:(h + 1) * HEAD]
                bcol = bcol_all[:, h * LANE:(h + 1) * LANE]
                icol = icol_all[:, h * LANE:(h + 1) * LANE]
                brow = b_t[H_C + h:H_C + h + 1, 0:lc]
                irow = gt_t[h:h + 1, 0:lc]
                m_h = m_scr[h:h + 1, :]

                d = jnp.where(dmask, bcol[:, 0:lc] - brow + irow, NEG)
                inter = bcol + m_h
                m_t = jnp.maximum(inter, jnp.max(d, axis=1, keepdims=True))
                w_intra = jnp.exp(d - m_t[:, 0:lc])
                w_inter = jnp.exp(inter - m_t)
                sqk = _nt(qmb, kmb) * w_intra
                num = w_inter * _mm(qmb, cpb) + _mm(sqk.astype(BF16), vh.astype(BF16))
                den = (w_inter * jnp.sum(qm * npair, axis=1, keepdims=True)
                       + jnp.sum(sqk, axis=1, keepdims=True))
                hh = num / jnp.maximum(jnp.abs(den), jnp.exp(-m_t))
                y_ref[0, :, h * HEAD:(h + 1) * HEAD] = (
                    _rms(hh, g_ref[...]) * gate_all[:, h * HEAD:(h + 1) * HEAD]).astype(BF16)

                b_last = bcol[n_valid - 1:n_valid, :]
                dec = jnp.where(svalid, b_last - bcol + icol, NEG)
                m_new = jnp.maximum(b_last + m_h, jnp.max(dec, axis=0, keepdims=True))
                wk = jnp.exp(dec - m_new)
                carries.append(jnp.exp(b_last + m_h - m_new))
                cu = _tn(kmb, (wk * vh).astype(BF16))
                nu = jnp.sum(wk * km, axis=0, keepdims=True)
                c_upd = cu if c_upd is None else c_upd + cu
                n_upd = nu if n_upd is None else n_upd + nu
                m_new_all.append(m_new)
            c_new.append(jnp.where(row_low, carries[0], carries[1]) * cp + c_upd)
            n_new.append(jnp.where(low[0:1], carries[0], carries[1]) * npair + n_upd)

        for j in range(H_C // 2):
            c_scr[j] = c_new[j]
            n_scr[j:j + 1, :] = n_new[j]
        for h in range(H_C):
            m_scr[h:h + 1, :] = m_new_all[h]
        c_ref[0] = c_scr[...]
        n_ref[0] = n_scr[...]
        m_ref[0] = m_scr[...]

    col = lambda u: pl.BlockSpec((1, lc, 1024), lambda b, c, u=u: (b, c, u))
    const = lambda *shape: pl.BlockSpec(shape, lambda b, c: (0,) * len(shape))
    per_b = lambda *shape: pl.BlockSpec((1,) + shape, lambda b, c: (b,) + (0,) * len(shape))
    return pl.pallas_call(
        body, grid=(b_, nchunk),
        in_specs=[col(_UNIT["cqk"] // 8), col(_UNIT["cv"] // 8),
                  pl.BlockSpec((1, lc, LANE), lambda b, c: (b, c, _UNIT["cif"])),
                  col(_UNIT["co"] // 8), col(_UNIT["cz"] // 8),
                  const(CONV_W, 1024), const(1, 1024), const(1, LANE), const(1, LANE),
                  const(2, LANE, H_C * LANE),
                  per_b(tail, 1024), per_b(H_C // 2, LANE, HEAD), per_b(H_C // 2, LANE), per_b(H_C, LANE)],
        out_specs=[pl.BlockSpec((1, lc, 1024), lambda b, c: (b, c, 0)),
                   per_b(tail, 1024), per_b(H_C // 2, LANE, HEAD), per_b(H_C // 2, LANE), per_b(H_C, LANE)],
        out_shape=[jax.ShapeDtypeStruct((b_, t, 1024), BF16),
                   jax.ShapeDtypeStruct((b_, tail, 1024), F32),
                   jax.ShapeDtypeStruct((b_, H_C // 2, LANE, HEAD), F32),
                   jax.ShapeDtypeStruct((b_, H_C // 2, LANE), F32),
                   jax.ShapeDtypeStruct((b_, H_C, LANE), F32)],
        scratch_shapes=[pltpu.VMEM((base + lc, 1024), F32), pltpu.VMEM((H_C // 2, LANE, HEAD), F32),
                        pltpu.VMEM((H_C // 2, LANE), F32), pltpu.VMEM((H_C, LANE), F32)],
        compiler_params=_params("parallel", "arbitrary"),
        name="mlstm")(proj3, proj3, proj3, proj3, proj3, conv_w, conv_b, gate_b, out_g, rsel,
                      conv0, c0, n0, m0)


def _w_prep(w_in):
    depth, d, n_in = w_in.shape
    kt_n = d // LANE
    per_col = kt_n * depth
    rows = w_in.reshape(depth, kt_n, LANE, n_in).transpose(3, 1, 0, 2).reshape(n_in * per_col, LANE)
    src_off, off = {}, 0
    for name, width in _SRC:
        src_off[name] = (off, width)
        off += width
    assert off == n_in
    starts, valids = [0] * (N_PROJ // LANE), [0] * (N_PROJ // LANE)
    for name in _DST_ORDER:
        s0, width = src_off[name]
        for u in range(-(-width // LANE)):
            starts[_UNIT[name] + u] = s0 + u * LANE
            valids[_UNIT[name] + u] = min(LANE, width - u * LANE)
    table = jnp.array([starts, valids], jnp.int32)

    def body(tbl_ref, w_ref, o_ref):
        u = pl.program_id(0)
        keep = lax.broadcasted_iota(jnp.int32, (LANE, LANE), 0) < tbl_ref[1, u]
        xt = jnp.swapaxes(w_ref[...].reshape(LANE, per_col, LANE), 0, 1)
        for l in range(depth):
            for kt in range(kt_n):
                o_ref[l, :, kt * LANE:(kt + 1) * LANE] = jnp.where(keep, xt[kt * depth + l], 0.0).astype(BF16)

    grid_spec = pltpu.PrefetchScalarGridSpec(
        num_scalar_prefetch=1, grid=(N_PROJ // LANE,),
        in_specs=[pl.BlockSpec((pl.Element(LANE * per_col), pl.Element(LANE)),
                               lambda u, tbl: (tbl[0, u] * per_col, 0))],
        out_specs=pl.BlockSpec((depth, LANE, d), lambda u, tbl: (0, u, 0)))
    return pl.pallas_call(
        body, grid_spec=grid_spec,
        out_shape=jax.ShapeDtypeStruct((depth, N_PROJ, d), BF16),
        compiler_params=_params("parallel"),
        name="w_prep")(table, rows)


def _c_from_pairs(c):
    b = c.shape[0]
    return jnp.swapaxes(c.reshape(b, H_C, DQK_C, HEAD), -1, -2)


def _pick(n, prefs):
    for p in prefs:
        if n % p == 0:
            return p
    return n


def kernel(x_prompt, x_sample, cache_a_kv, cache_b_kv, state_b_win, state_c_conv, state_c_C,
           state_c_n, state_c_m, page_table, norm_g, w_in, w_out, a_qk_g, a_lambda, a_out_g,
           b_qk_g, b_cmp_w, c_conv_w, c_conv_b, c_gate_b, c_out_g):
    bp, t, d = x_prompt.shape
    db, ts, _ = x_sample.shape
    depth = norm_g.shape[0]
    n_pool = cache_a_kv.shape[1]
    n_pages = page_table.shape[1]
    past_len = n_pages * PAGE
    wb = state_b_win.shape[2]
    tp = 8
    assert ts <= tp and t % MLSTM_CHUNK == 0 and t % Q_BLOCK == 0 and d == 2048
    mp, ms = bp * t, db * tp
    pg = _pick(n_pages, (16, 8, 4, 2, 1))

    cache_a2 = cache_a_kv.reshape(depth * n_pool * PAGE * 2 * H_A, HEAD)
    cache_b2 = cache_b_kv.reshape(depth * n_pool * PAGE * 4, HEAD)
    win_rows = state_b_win.reshape(depth, db, wb * 2, HEAD)
    hp = x_prompt.reshape(mp, d)
    hs = jnp.pad(x_sample, ((0, 0), (0, tp - ts), (0, 0))).reshape(ms, d)

    tm_p = _pick(mp, (1024, 512, 256, 128))
    tq_a = _pick(t, (512, 256, 128))
    outs_p = [[] for _ in range(7)]
    outs_s = [[] for _ in range(7)]

    w_all = _w_prep(w_in)
    for l in range(depth):
        lam_init = 0.8 - 0.6 * math.exp(-0.3 * l)
        w_o = w_out[l].astype(BF16)
        g_a = jnp.tile(a_qk_g[l], (1, 2))
        g_b = b_qk_g[l]
        wpb = jnp.broadcast_to(b_cmp_w[l][:, :, None], (2, CMP_BLOCK, LANE))
        wpb2 = jnp.concatenate([wpb[0], wpb[1]], axis=1)
        gate_b = jnp.pad(c_gate_b[l].reshape(1, 2 * H_C), ((0, 0), (0, LANE - 2 * H_C)))
        conv_b = c_conv_b[l].reshape(1, -1)
        out_gc = c_out_g[l].reshape(1, LANE)

        proj = _norm_matmul(hp, norm_g[l], w_all, l, tm=tm_p, tn=1024)
        proj3 = proj.reshape(bp, t, N_PROJ)
        qn, kn, vb, akv = _a_prep(proj, g_a, tm=_pick(mp, (512, 256, 128)))
        ya = _a_attn_prompt(qn.reshape(bp, t, 512), kn.reshape(bp, t, 512), vb.reshape(bp, t, 512),
                            proj3, a_lambda[l], a_out_g[l], lam_init, tq=tq_a)
        bkvs, neww, kvb, kcb, vcb = _b_prep(proj, g_b, wpb, tm=_pick(mp, (512, 256, 128)), pool=True)
        nblk = t // CMP_BLOCK
        yb = _b_attn_prompt(proj3, kcb.reshape(bp, nblk, LANE), vcb.reshape(bp, nblk, LANE),
                            kvb.reshape(bp, t, 512), g_b, t=t)
        yc, conv_p, c_p, n_p, m_p = _mlstm(
            proj3, c_conv_w[l], conv_b, gate_b, out_gc,
            jnp.zeros((bp, CONV_W - 1, 2 * H_C * DQK_C), F32), jnp.zeros((bp, H_C // 2, LANE, HEAD), F32),
            jnp.zeros((bp, H_C // 2, LANE), F32), jnp.zeros((bp, H_C, LANE), F32),
            lc=MLSTM_CHUNK, n_valid=MLSTM_CHUNK)
        c_p, n_p = _c_from_pairs(c_p), n_p.reshape(bp, H_C, DQK_C)
        hp = _out_proj(hp, ya.reshape(mp, 512), yb.reshape(mp, 512), yc.reshape(mp, 1024), w_o,
                       tm=tm_p, tn=1024)
        win_p = jnp.concatenate([jnp.zeros((bp, wb, 2, HEAD), F32), neww.reshape(bp, t, 2, HEAD)],
                                axis=1)[:, -wb:]
        for lst, a in zip(outs_p, (akv.reshape(bp, t, 2, H_A, HEAD), bkvs.reshape(bp, t, 4, HEAD),
                                   win_p, conv_p, c_p, n_p, m_p[:, :, 0])):
            lst.append(a)

        base = l * n_pool
        sproj = _norm_matmul(hs, norm_g[l], w_all, l, tm=ms, tn=1024)
        sproj3 = sproj.reshape(db, tp, N_PROJ)
        sqn, skn, svb, sakv = _a_prep(sproj, g_a, tm=ms)
        kvnew = jnp.concatenate([skn.reshape(db, tp, 512), svb.reshape(db, tp, 512)], axis=2)
        kvnew = jnp.pad(kvnew, ((0, 0), (0, PAGE - tp), (0, 0)))
        sya = _a_attn_sample(page_table, cache_a2, sqn.reshape(db, tp, 512), kvnew, sproj3,
                             a_lambda[l], a_out_g[l], lam_init, base, n_valid=ts, pg=pg)
        sbkvs, sneww, _ = _b_prep(sproj, g_b, wpb, tm=ms, pool=False)
        pooled = _b_pool_pages(page_table, cache_b2, wpb2, base, pg=pg)
        qs, ocw, g1b, sel = _b_select_sample(
            sproj3, pooled, win_rows[l], sneww.reshape(db, tp * 2, HEAD), g_b,
            past_len=past_len, n_valid=ts)
        nbg = pg * (PAGE // CMP_BLOCK)
        selg = sel.reshape(db, tp, n_pages // pg, nbg).transpose(0, 2, 1, 3)
        syb = _b_selected_sample(page_table, cache_b2, qs, selg, sbkvs.reshape(db, tp * 4, HEAD), ocw, g1b,
                                 sproj3, base, n_valid=ts, pg=pg)
        m0 = jnp.broadcast_to(state_c_m[l][:, :, None], (db, H_C, LANE))
        syc, conv_s, c_s, n_s, m_s = _mlstm(
            sproj3, c_conv_w[l], conv_b, gate_b, out_gc, state_c_conv[l],
            jnp.swapaxes(state_c_C[l], -1, -2).reshape(db, H_C // 2, LANE, HEAD),
            state_c_n[l].reshape(db, H_C // 2, LANE), m0, lc=tp, n_valid=ts)
        c_s, n_s = _c_from_pairs(c_s), n_s.reshape(db, H_C, DQK_C)
        hs = _out_proj(hs, sya.reshape(ms, 512), syb.reshape(ms, 512), syc.reshape(ms, 1024), w_o,
                       tm=ms, tn=1024)
        win_s = jnp.concatenate([state_b_win[l], sneww.reshape(db, tp, 2, HEAD)[:, :ts]], axis=1)[:, -wb:]
        for lst, a in zip(outs_s, (sakv.reshape(db, tp, 2, H_A, HEAD)[:, :ts],
                                   sbkvs.reshape(db, tp, 4, HEAD)[:, :ts],
                                   win_s, conv_s, c_s, n_s, m_s[:, :, 0])):
            lst.append(a)

    y_p = hp.reshape(bp, t, d)
    y_s = hs.reshape(db, tp, d)[:, :ts]
    sp = [jnp.stack(x) for x in outs_p]
    ss = [jnp.stack(x) for x in outs_s]
    return (y_p, y_s, sp[0], ss[0], sp[1], ss[1], sp[2], ss[2], sp[3], ss[3],
            sp[4], ss[4], sp[5], ss[5], sp[6], ss[6])
```

```python
import functools
import math

import jax
import jax.numpy as jnp
from jax import lax
from jax.experimental import pallas as pl
from jax.experimental.pallas import tpu as pltpu

F32 = jnp.float32
BF16 = jnp.bfloat16

EPS = 1e-6
LANE = 128
HEAD = 128
H_A = 4
DQ_A = HEAD // 2
H_B = 4
H_C = 8
DQK_C = HEAD // 2
CMP_BLOCK = 64
N_SEL = 16
WINDOW = 512
CONV_W = 4
MLSTM_CHUNK = 128
PAGE = 128
Q_BLOCK = 128
NEG = -1e30
MASKED = 2 * NEG
SCORE_LOG2E = math.log2(math.e)
VMEM_LIMIT = 56 * 1024 * 1024

_SRC = (("aq", 512), ("ak", 512), ("av", 512), ("az", 512),
        ("bq", 512), ("bkv", 768), ("bg", 12), ("bz", 512),
        ("cqk", 1024), ("cv", 1024), ("cif", 16), ("co", 1024), ("cz", 1024))
_DST_ORDER = ("aq", "ak", "av", "az", "bq", "bz", "bkv", "bg", "cif", "cqk", "cv", "co", "cz")
_UNIT = {"aq": 0, "ak": 4, "av": 8, "az": 12, "bq": 16, "bz": 20, "bkv": 24, "bg": 30,
         "cif": 31, "cqk": 32, "cv": 40, "co": 48, "cz": 56}
N_PROJ = 64 * LANE


def _nt(a, b):
    return lax.dot_general(a, b, (((1,), (1,)), ((), ())), preferred_element_type=F32)


def _tn(a, b):
    return lax.dot_general(a, b, (((0,), (0,)), ((), ())), preferred_element_type=F32)


def _mm(a, b):
    return jnp.dot(a, b, preferred_element_type=F32)


def _split2(x):
    hi = x.astype(BF16)
    lo = (x - hi.astype(F32)).astype(BF16)
    return hi, lo


def _split3(x):
    hi = x.astype(BF16)
    r = x - hi.astype(F32)
    mid = r.astype(BF16)
    lo = (r - mid.astype(F32)).astype(BF16)
    return hi, mid, lo


def _sigmoid(z):
    return 1.0 / (1.0 + jnp.exp(-z))


def _silu(z):
    return z * _sigmoid(z)


def _rms(x, g):
    return x * lax.rsqrt(jnp.mean(x * x, axis=-1, keepdims=True) + EPS) * g


def _div(x, n):
    return lax.shift_right_logical(x, int(math.log2(n)))


def _mod(x, n):
    return x & (n - 1)


def _online_update(s, v, m_ref, l_ref, acc_ref, pv_fn=None):
    w = s.shape[1]
    m_old = m_ref[...]
    m_new = jnp.maximum(m_old, jnp.max(s, axis=-1, keepdims=True))
    alpha = jnp.exp2(m_old - m_new)
    m_rep = m_new if w == LANE else jnp.concatenate([m_new] * (w // LANE), axis=1)
    p = jnp.exp2(s - m_rep)
    l_ref[...] = alpha * l_ref[...] + jnp.sum(p, axis=-1, keepdims=True)
    pb = p.astype(BF16)
    acc_ref[...] = alpha * acc_ref[...] + (_mm(pb, v) if pv_fn is None else pv_fn(pb))
    m_ref[...] = m_new


def _params(*sem):
    return pltpu.CompilerParams(dimension_semantics=sem, vmem_limit_bytes=VMEM_LIMIT)


def _norm_matmul(x, g, w, layer, *, tm, tn):
    m, d = x.shape
    n = w.shape[1]
    rc = min(tm, 256)

    def body(x_ref, g_ref, w_ref, o_ref, h_scr):
        @pl.when(pl.program_id(1) == 0)
        def _():
            def chunk(c, carry):
                r = pl.ds(pl.multiple_of(c * rc, rc), rc)
                h_scr[r, :] = _rms(x_ref[r, :], g_ref[...]).astype(BF16)
                return carry
            lax.fori_loop(0, tm // rc, chunk, 0)
        o_ref[...] = _nt(h_scr[...], w_ref[0])

    return pl.pallas_call(
        body, grid=(m // tm, n // tn),
        in_specs=[pl.BlockSpec((tm, d), lambda i, j: (i, 0)),
                  pl.BlockSpec((1, d), lambda i, j: (0, 0)),
                  pl.BlockSpec((1, tn, d), lambda i, j: (layer, j, 0))],
        out_specs=pl.BlockSpec((tm, tn), lambda i, j: (i, j)),
        out_shape=jax.ShapeDtypeStruct((m, n), F32),
        scratch_shapes=[pltpu.VMEM((tm, d), BF16)],
        compiler_params=_params("parallel", "arbitrary"),
        name="norm_matmul")(x, g.reshape(1, d), w)


def _out_proj(x, ya, yb, yc, w, *, tm, tn):
    m, d = x.shape
    da, db, dc = ya.shape[1], yb.shape[1], yc.shape[1]

    def body(x_ref, a_ref, b_ref, c_ref, w_ref, o_ref):
        mix = jnp.concatenate([a_ref[...], b_ref[...], c_ref[...]], axis=1)
        o_ref[...] = x_ref[...] + _mm(mix, w_ref[...])

    return pl.pallas_call(
        body, grid=(m // tm, d // tn),
        in_specs=[pl.BlockSpec((tm, tn), lambda i, j: (i, j)),
                  pl.BlockSpec((tm, da), lambda i, j: (i, 0)),
                  pl.BlockSpec((tm, db), lambda i, j: (i, 0)),
                  pl.BlockSpec((tm, dc), lambda i, j: (i, 0)),
                  pl.BlockSpec((da + db + dc, tn), lambda i, j: (0, j))],
        out_specs=pl.BlockSpec((tm, tn), lambda i, j: (i, j)),
        out_shape=jax.ShapeDtypeStruct((m, d), F32),
        compiler_params=_params("parallel", "parallel"),
        name="out_proj")(x, ya, yb, yc, w)


def _a_prep(proj, g2, *, tm):
    m = proj.shape[0]

    def body(q_ref, k_ref, v_ref, g_ref, qn_ref, kn_ref, vb_ref, akv_ref):
        r = _div(lax.broadcasted_iota(jnp.int32, (LANE, LANE), 0), DQ_A)
        c = _div(lax.broadcasted_iota(jnp.int32, (LANE, LANE), 1), DQ_A)
        seg = (r == c).astype(BF16)

        def segnorm(x, g):
            outs = []
            for u in range(4):
                xc = x[:, u * LANE:(u + 1) * LANE]
                hi, lo = _split2(xc * xc)
                s = _mm(hi, seg) + _mm(lo, seg)
                outs.append(xc * lax.rsqrt(s * (1.0 / DQ_A) + EPS) * g)
            return jnp.concatenate(outs, axis=1)

        qn = segnorm(q_ref[...], g_ref[0:1, :]) * (DQ_A ** -0.5 * SCORE_LOG2E)
        kn = segnorm(k_ref[...], g_ref[1:2, :])
        v = v_ref[...]
        qn_ref[...] = qn.astype(BF16)
        kn_ref[...] = kn.astype(BF16)
        vb_ref[...] = v.astype(BF16)
        for h in range(H_A):
            akv_ref[pl.ds(h, tm, stride=2 * H_A), :] = kn[:, h * LANE:(h + 1) * LANE]
            akv_ref[pl.ds(H_A + h, tm, stride=2 * H_A), :] = v[:, h * LANE:(h + 1) * LANE]

    blk = lambda u: pl.BlockSpec((tm, 512), lambda i, u=u: (i, u))
    row = lambda w: pl.BlockSpec((tm, w), lambda i: (i, 0))
    return pl.pallas_call(
        body, grid=(m // tm,),
        in_specs=[blk(_UNIT["aq"] // 4), blk(_UNIT["ak"] // 4), blk(_UNIT["av"] // 4),
                  pl.BlockSpec((2, LANE), lambda i: (0, 0))],
        out_specs=[row(512), row(512), row(512), pl.BlockSpec((tm * 2 * H_A, LANE), lambda i: (i, 0))],
        out_shape=[jax.ShapeDtypeStruct((m, 512), BF16)] * 3
        + [jax.ShapeDtypeStruct((m * 2 * H_A, LANE), F32)],
        compiler_params=_params("parallel"),
        name="a_prep")(proj, proj, proj, g2)


def _diff_lambda(lp, lam_init):
    a = jnp.sum(lp[0:1, :] * lp[1:2, :], axis=-1, keepdims=True)
    b = jnp.sum(lp[2:3, :] * lp[3:4, :], axis=-1, keepdims=True)
    return jnp.exp(a) - jnp.exp(b) + lam_init


def _a_finish(acc1, l1, acc2, l2, lam, g, z, lam_init):
    o = acc1 / l1 - lam * (acc2 / l2)
    return _rms(o, g) * (1.0 - lam_init) * _silu(z)


def _a_attn_prompt(qn, kn, vb, proj3, lam_p, out_g, lam_init, *, tq):
    b_, t, _ = qn.shape
    tk = tq

    def body(q_ref, k_ref, v_ref, z_ref, lam_ref, g_ref, o_ref, m_scr, l_scr, acc_scr):
        qi = pl.program_id(2)
        q = q_ref[0]
        lane = lax.broadcasted_iota(jnp.int32, (tq, LANE), 1)
        zero = jnp.zeros_like(q)
        qs = (jnp.where(lane < DQ_A, q, zero), jnp.where(lane >= DQ_A, q, zero))
        m_scr[...] = jnp.full(m_scr.shape, NEG, F32)
        l_scr[...] = jnp.zeros(l_scr.shape, F32)
        acc_scr[...] = jnp.zeros(acc_scr.shape, F32)

        def chunk(kc, mask):
            ks = pl.ds(pl.multiple_of(kc * tk, tk), tk)
            k = k_ref[0, ks, :]
            v = v_ref[0, ks, :]
            for c in range(2):
                s = _nt(qs[c], k)
                if mask is not None:
                    s = jnp.where(mask, s, MASKED)
                _online_update(s, v, m_scr.at[c], l_scr.at[c], acc_scr.at[c])

        def full_chunk(kc, carry):
            chunk(kc, None)
            return carry

        lax.fori_loop(0, qi, full_chunk, 0)
        chunk(qi, lax.broadcasted_iota(jnp.int32, (tq, tk), 1)
              <= lax.broadcasted_iota(jnp.int32, (tq, tk), 0))
        lam = _diff_lambda(lam_ref[...], lam_init)
        y = _a_finish(acc_scr[0], l_scr[0], acc_scr[1], l_scr[1], lam, g_ref[...], z_ref[0], lam_init)
        o_ref[0] = y.astype(BF16)

    head = lambda rows, qdep: pl.BlockSpec(
        (1, rows, LANE), (lambda b, h, i: (b, i, h)) if qdep else (lambda b, h, i: (b, 0, h)))
    return pl.pallas_call(
        body, grid=(b_, H_A, t // tq),
        in_specs=[head(tq, True), head(t, False), head(t, False),
                  pl.BlockSpec((1, tq, LANE), lambda b, h, i: (b, i, _UNIT["az"] + h)),
                  pl.BlockSpec((4, DQ_A), lambda b, h, i: (0, 0)),
                  pl.BlockSpec((1, LANE), lambda b, h, i: (0, 0))],
        out_specs=head(tq, True),
        out_shape=jax.ShapeDtypeStruct((b_, t, 512), BF16),
        scratch_shapes=[pltpu.VMEM((2, tq, LANE), F32)] * 3,
        compiler_params=_params("parallel", "parallel", "arbitrary"),
        name="a_attn_prompt")(qn, kn, vb, proj3, lam_p, out_g.reshape(1, LANE))


def _a_attn_sample(page_table, cache2, qn, kvnew, proj3, lam_p, out_g, lam_init, page_base,
                   *, n_valid, pg):
    db, n_pages = page_table.shape
    tp = qn.shape[1]
    ng = n_pages // pg

    def body(pt_ref, *refs):
        pages = refs[:pg]
        q_ref, new_ref, z_ref, lam_ref, g_ref, o_ref, m_scr, l_scr, acc_scr = refs[pg:]
        gi = pl.program_id(1)

        @pl.when(gi == 0)
        def _():
            m_scr[...] = jnp.full(m_scr.shape, NEG, F32)
            l_scr[...] = jnp.zeros(l_scr.shape, F32)
            acc_scr[...] = jnp.zeros(acc_scr.shape, F32)

        lane = lax.broadcasted_iota(jnp.int32, (tp, LANE), 1)

        def qstack(h):
            q = q_ref[0, :, h * LANE:(h + 1) * LANE]
            zero = jnp.zeros_like(q)
            return jnp.concatenate([jnp.where(lane < DQ_A, q, zero),
                                    jnp.where(lane >= DQ_A, q, zero)], axis=0)

        pts = [jnp.swapaxes(pages[j][...].reshape(PAGE, 2 * H_A, LANE), 0, 1) for j in range(pg)]

        def slot(s_):
            return jnp.concatenate([pts[j][s_].astype(BF16) for j in range(pg)], axis=0)

        s = jnp.concatenate([_nt(qstack(h), slot(h)) for h in range(H_A)], axis=0)
        _online_update(s, None, m_scr, l_scr, acc_scr,
                       pv_fn=lambda p: jnp.concatenate(
                           [_mm(p[h * 2 * tp:(h + 1) * 2 * tp], slot(H_A + h)) for h in range(H_A)], axis=0))

        @pl.when(gi == ng - 1)
        def _():
            lam = _diff_lambda(lam_ref[...], lam_init)
            row = _mod(lax.broadcasted_iota(jnp.int32, (2 * tp, PAGE), 0), tp)
            col = lax.broadcasted_iota(jnp.int32, (2 * tp, PAGE), 1)
            mask = (col <= row) & (col < n_valid)
            mask4 = jnp.concatenate([mask] * H_A, axis=0)
            s = jnp.concatenate([_nt(qstack(h), new_ref[0, :, h * LANE:(h + 1) * LANE])
                                 for h in range(H_A)], axis=0)
            _online_update(jnp.where(mask4, s, MASKED), None, m_scr, l_scr, acc_scr,
                           pv_fn=lambda p: jnp.concatenate(
                               [_mm(p[h * 2 * tp:(h + 1) * 2 * tp],
                                    new_ref[0, :, 512 + h * LANE:512 + (h + 1) * LANE])
                                for h in range(H_A)], axis=0))
            for h in range(H_A):
                r0 = h * 2 * tp
                acc = acc_scr[r0:r0 + 2 * tp, :]
                l = l_scr[r0:r0 + 2 * tp, :]
                y = _a_finish(acc[0:tp], l[0:tp], acc[tp:], l[tp:], lam, g_ref[...],
                              z_ref[0, :, h * LANE:(h + 1) * LANE], lam_init)
                o_ref[0, :, h * LANE:(h + 1) * LANE] = y.astype(BF16)

    def page_spec(j):
        return pl.BlockSpec((PAGE * 2 * H_A, LANE),
                            lambda b, g, pt, j=j: (page_base + pt[b, g * pg + j], 0))

    in_specs = [page_spec(j) for j in range(pg)] + [
            pl.BlockSpec((1, tp, 512), lambda b, g, pt: (b, 0, 0)),
            pl.BlockSpec((1, PAGE, 1024), lambda b, g, pt: (b, 0, 0)),
            pl.BlockSpec((1, tp, 512), lambda b, g, pt: (b, 0, _UNIT["az"] // 4)),
            pl.BlockSpec((4, DQ_A), lambda b, g, pt: (0, 0)),
            pl.BlockSpec((1, LANE), lambda b, g, pt: (0, 0))]
    grid_spec = pltpu.PrefetchScalarGridSpec(
        num_scalar_prefetch=1, grid=(db, ng), in_specs=in_specs,
        out_specs=pl.BlockSpec((1, tp, 512), lambda b, g, pt: (b, 0, 0)),
        scratch_shapes=[pltpu.VMEM((H_A * 2 * tp, LANE), F32)] * 3)
    return pl.pallas_call(
        body, grid_spec=grid_spec,
        out_shape=jax.ShapeDtypeStruct((db, tp, 512), BF16),
        compiler_params=_params("parallel", "arbitrary"),
        name="a_attn_sample")(page_table, *([cache2] * pg), qn, kvnew, proj3, lam_p,
                              out_g.reshape(1, LANE))


def _b_prep(proj, g4, wpb, *, tm, pool):
    m = proj.shape[0]
    nb = tm // CMP_BLOCK

    def body(kv_ref, g_ref, wp_ref, bkv_ref, nw_ref, kvb_ref, *pool_refs):
        kc = kv_ref[:, 0:128]
        vc = kv_ref[:, 128:256]
        ks = _rms(kv_ref[:, 256:384], g_ref[2:3, :])
        vs = kv_ref[:, 384:512]
        kw = _rms(kv_ref[:, 512:640], g_ref[3:4, :])
        vw = kv_ref[:, 640:768]
        for slot, x in enumerate((kc, vc, ks, vs)):
            bkv_ref[pl.ds(slot, tm, stride=4), :] = x
        nw_ref[pl.ds(0, tm, stride=2), :] = kw
        nw_ref[pl.ds(1, tm, stride=2), :] = vw
        kvb_ref[:, 0:128] = ks.astype(BF16)
        kvb_ref[:, 128:256] = vs.astype(BF16)
        kvb_ref[:, 256:384] = kw.astype(BF16)
        kvb_ref[:, 384:512] = vw.astype(BF16)
        if pool:
            kcb_ref, vcb_ref = pool_refs
            kp = jnp.sum(kc.reshape(nb, CMP_BLOCK, LANE) * wp_ref[0][None], axis=1)
            vp = jnp.sum(vc.reshape(nb, CMP_BLOCK, LANE) * wp_ref[1][None], axis=1)
            kcb_ref[...] = _rms(kp, g_ref[1:2, :])
            vcb_ref[...] = vp

    row = lambda w: pl.BlockSpec((tm, w), lambda i: (i, 0))
    out_specs = [pl.BlockSpec((tm * 4, LANE), lambda i: (i, 0)),
                 pl.BlockSpec((tm * 2, LANE), lambda i: (i, 0)), row(512)]
    out_shape = [jax.ShapeDtypeStruct((m * 4, LANE), F32), jax.ShapeDtypeStruct((m * 2, LANE), F32),
                 jax.ShapeDtypeStruct((m, 512), BF16)]
    if pool:
        out_specs += [pl.BlockSpec((nb, LANE), lambda i: (i, 0))] * 2
        out_shape += [jax.ShapeDtypeStruct((m // CMP_BLOCK, LANE), F32)] * 2
    return pl.pallas_call(
        body, grid=(m // tm,),
        in_specs=[pl.BlockSpec((tm, 768), lambda i: (i, _UNIT["bkv"] // 6)),
                  pl.BlockSpec((4, LANE), lambda i: (0, 0)),
                  pl.BlockSpec((2, CMP_BLOCK, LANE), lambda i: (0, 0, 0))],
        out_specs=out_specs, out_shape=out_shape,
        compiler_params=_params("parallel"),
        name="b_prep")(proj, g4, wpb)


def _cmp_scores(qf, kcb):
    qh, ql = _split2(qf)
    kh, kl = _split2(kcb)
    return _nt(qh, kh) + _nt(qh, kl) + _nt(ql, kh)


def _b_attn_prompt(proj3, kcb, vcb, kvb, g4, *, t):
    b_ = proj3.shape[0]
    tq = 2 * Q_BLOCK if t % (2 * Q_BLOCK) == 0 else Q_BLOCK
    nblk = t // CMP_BLOCK
    n_top = min(N_SEL, nblk)
    tk = min(512, t)
    span = min(WINDOW + tq, t)
    scale = HEAD ** -0.5 * SCORE_LOG2E

    def body(q_ref, kcb_ref, vcb_ref, kv_ref, bg_ref, bz_ref, g_ref, o_ref, m_scr, l_scr, acc_scr):
        qi = pl.program_id(1)
        qf = jnp.concatenate([_rms(q_ref[0, :, h * LANE:(h + 1) * LANE], g_ref[0:1, :]) * scale
                              for h in range(H_B)], axis=0)
        qb = qf.astype(BF16)
        pos = qi * tq + lax.broadcasted_iota(jnp.int32, (tq, 1), 0)

        pos_l = qi * tq + lax.broadcasted_iota(jnp.int32, (1, tq), 1)
        cur_l = _div(pos_l, CMP_BLOCK)
        blk_s = lax.broadcasted_iota(jnp.int32, (nblk, tq), 0)
        cmask = blk_s < cur_l
        kh, kl = _split2(kcb_ref[0])
        qh, ql = _split2(qf)
        cmask4 = jnp.concatenate([cmask] * H_B, axis=1)
        s = jnp.where(cmask4, _nt(kh, qh) + _nt(kl, qh) + _nt(kh, ql), NEG)
        mx = jnp.max(s, axis=0, keepdims=True)
        p = jnp.where(cmask4, jnp.exp2(s - mx), 0.0)
        pc = p / jnp.maximum(jnp.sum(p, axis=0, keepdims=True), 1e-30)
        pcs = [pc[:, h * tq:(h + 1) * tq] for h in range(H_B)]
        imp = jnp.where(cmask, pcs[0] + pcs[1] + pcs[2] + pcs[3], -1.0)
        fill = LANE - H_B * nblk
        pcq = jnp.concatenate(pcs + ([jnp.zeros((fill, tq), F32)] if fill else []), axis=0).T.astype(BF16)
        vch = vcb_ref[0].astype(BF16)
        zblk = jnp.zeros((nblk, LANE), BF16)
        vdiag = jnp.concatenate(
            [jnp.concatenate([vch if c == h else zblk for c in range(H_B)], axis=1)
             for h in range(H_B)] + ([jnp.zeros((fill, H_B * LANE), BF16)] if fill else []), axis=0)
        o_cmp = _mm(pcq, vdiag)

        rank = jnp.zeros((nblk, tq), F32)
        for mrow in range(nblk):
            cm = imp[mrow:mrow + 1, :]
            ahead = (cm > imp) | ((cm == imp) & (mrow < blk_s))
            rank = rank + ahead.astype(F32)
        sel_t = (((rank < n_top) & (imp >= 0.0)) | (blk_s == cur_l)).astype(F32)
        selq = jnp.concatenate([sel_t, jnp.zeros((LANE - nblk, tq), F32)], axis=0).T.astype(BF16)

        m_scr[...] = jnp.full(m_scr.shape, NEG, F32)
        l_scr[...] = jnp.zeros(l_scr.shape, F32)
        acc_scr[...] = jnp.zeros(acc_scr.shape, F32)
        nk = _div(qi * tq + tq + tk - 1, tk)

        def chunk(c, carry):
            ks_ = pl.ds(pl.multiple_of(c * tk, tk), tk)
            kidx = c * tk + lax.broadcasted_iota(jnp.int32, (LANE, tk), 1)
            e = (_div(kidx, CMP_BLOCK) == lax.broadcasted_iota(jnp.int32, (LANE, tk), 0)).astype(BF16)
            kpos = c * tk + lax.broadcasted_iota(jnp.int32, (tq, tk), 1)
            mk = (_mm(selq, e) > 0.5) & (kpos <= pos)
            mask = jnp.concatenate([mk] * H_B, axis=0)
            s = jnp.where(mask, _nt(qb, kv_ref[0, ks_, 0:128]), MASKED)
            _online_update(s, kv_ref[0, ks_, 128:256], m_scr, l_scr, acc_scr)
            return carry

        lax.fori_loop(0, nk, chunk, 0)

        start = jnp.clip(qi * tq + tq - span, 0, t - span)
        ws = pl.ds(pl.multiple_of(start, tq), span)
        diff = pos - (start + lax.broadcasted_iota(jnp.int32, (tq, span), 1))
        wmask = jnp.concatenate([(diff >= 0) & (diff < WINDOW)] * H_B, axis=0)
        s = jnp.where(wmask, _nt(qb, kv_ref[0, ws, 256:384]), MASKED)
        p = jnp.exp2(s - jnp.maximum(jnp.max(s, axis=-1, keepdims=True), NEG))
        o_win = _mm(p.astype(BF16), kv_ref[0, ws, 384:512]) \
            / jnp.maximum(jnp.sum(p, axis=-1, keepdims=True), 1e-30)
        o_sel = acc_scr[...] / jnp.maximum(l_scr[...], 1e-30)

        gate = _sigmoid(bg_ref[0])
        for h in range(H_B):
            rows = slice(h * tq, (h + 1) * tq)
            ob = (gate[:, 3 * h:3 * h + 1] * o_cmp[:, h * LANE:(h + 1) * LANE]
                  + gate[:, 3 * h + 1:3 * h + 2] * o_sel[rows]
                  + gate[:, 3 * h + 2:3 * h + 3] * o_win[rows])
            y = ob * _silu(bz_ref[0, :, h * LANE:(h + 1) * LANE])
            o_ref[0, :, h * LANE:(h + 1) * LANE] = y.astype(BF16)

    full = lambda rows, w: pl.BlockSpec((1, rows, w), lambda b, i: (b, 0, 0))
    return pl.pallas_call(
        body, grid=(b_, t // tq),
        in_specs=[pl.BlockSpec((1, tq, 512), lambda b, i: (b, i, _UNIT["bq"] // 4)),
                  full(nblk, LANE), full(nblk, LANE), full(t, 512),
                  pl.BlockSpec((1, tq, LANE), lambda b, i: (b, i, _UNIT["bg"])),
                  pl.BlockSpec((1, tq, 512), lambda b, i: (b, i, _UNIT["bz"] // 4)),
                  pl.BlockSpec((4, LANE), lambda b, i: (0, 0))],
        out_specs=pl.BlockSpec((1, tq, 512), lambda b, i: (b, i, 0)),
        out_shape=jax.ShapeDtypeStruct((b_, t, 512), BF16),
        scratch_shapes=[pltpu.VMEM((H_B * tq, LANE), F32)] * 3,
        compiler_params=_params("parallel", "arbitrary"),
        name="b_attn_prompt")(proj3, kcb, vcb, kvb, proj3, proj3, g4)


def _b_pool_pages(page_table, cache2, wpb2, page_base, *, pg):
    db, n_pages = page_table.shape
    ng = n_pages // pg
    per = PAGE // CMP_BLOCK

    def body(pt_ref, *refs):
        pages = refs[:pg]
        wp_ref, o_ref = refs[pg:]
        rows = []
        for j in range(pg):
            for u in range(per):
                parts = []
                for kind in range(2):
                    x = pages[j][pl.ds(u * CMP_BLOCK * 4 + kind, CMP_BLOCK, stride=4), :]
                    parts.append(jnp.sum(x * wp_ref[:, kind * LANE:(kind + 1) * LANE], axis=0, keepdims=True))
                rows.append(jnp.concatenate(parts, axis=1))
        o_ref[0] = jnp.concatenate(rows, axis=0)

    def page_spec(j):
        return pl.BlockSpec((PAGE * 4, LANE), lambda b, g, pt, j=j: (page_base + pt[b, g * pg + j], 0))

    grid_spec = pltpu.PrefetchScalarGridSpec(
        num_scalar_prefetch=1, grid=(db, ng),
        in_specs=[page_spec(j) for j in range(pg)] + [
            pl.BlockSpec((CMP_BLOCK, 256), lambda b, g, pt: (0, 0))],
        out_specs=pl.BlockSpec((1, pg * per, 256), lambda b, g, pt: (b, g, 0)))
    return pl.pallas_call(
        body, grid_spec=grid_spec,
        out_shape=jax.ShapeDtypeStruct((db, n_pages * per, 256), F32),
        compiler_params=_params("parallel", "parallel"),
        name="b_pool_pages")(page_table, *([cache2] * pg), wpb2)


def _b_select_sample(proj3, pooled, win, neww, g4, *, past_len, n_valid):
    db = neww.shape[0]
    tp = neww.shape[1] // 2
    nblk = pooled.shape[1]
    n_top = min(N_SEL, nblk + 1)
    wb = win.shape[1] // 2
    rows = H_B * tp
    scale = HEAD ** -0.5 * SCORE_LOG2E

    def body(q_ref, pool_ref, win_ref, nw_ref, bg_ref, g_ref, qs_ref, ocw_ref, g1_ref, sel_ref):
        qf = jnp.concatenate(
            [_rms(q_ref[0, :, h * LANE:(h + 1) * LANE], g_ref[0:1, :]) * scale for h in range(H_B)],
            axis=0)
        qb = qf.astype(BF16)
        qs_ref[0] = qb
        tok = _mod(lax.broadcasted_iota(jnp.int32, (rows, 1), 0), tp)
        pos = past_len + tok
        cur = _div(pos, CMP_BLOCK)
        blk = lax.broadcasted_iota(jnp.int32, (rows, nblk), 1)
        cmask = blk < cur

        kc = _rms(pool_ref[0, :, 0:128], g_ref[1:2, :])
        s = jnp.where(cmask, _cmp_scores(qf, kc), NEG)
        mx = jnp.max(s, axis=-1, keepdims=True)
        p = jnp.where(cmask, jnp.exp2(s - mx), 0.0)
        pc = p / jnp.maximum(jnp.sum(p, axis=-1, keepdims=True), 1e-30)
        o_cmp = _mm(pc.astype(BF16), pool_ref[0, :, 128:256].astype(BF16))
        imp = pc[0:tp]
        for h in range(1, H_B):
            imp = imp + pc[h * tp:(h + 1) * tp]
        imp = jnp.where(cmask[0:tp], imp, -1.0)

        pad = jnp.concatenate([imp, jnp.zeros((LANE - tp, nblk), F32)], axis=0)
        imp_t = jnp.concatenate([pad[:, u * LANE:(u + 1) * LANE].T for u in range(nblk // LANE)], axis=0)
        mi = lax.broadcasted_iota(jnp.int32, (nblk, nblk), 0)
        ni = lax.broadcasted_iota(jnp.int32, (nblk, nblk), 1)
        sels = []
        for tkn in range(tp):
            r = imp[tkn:tkn + 1, :]
            c = imp_t[:, tkn:tkn + 1]
            ahead = (c > r) | ((c == r) & (mi < ni))
            rank = jnp.sum(ahead.astype(F32), axis=0, keepdims=True)
            sels.append(((rank < n_top) & (r >= 0.0)).astype(F32))
        sel_ref[0] = jnp.concatenate(sels, axis=0)

        kw = win_ref[0, pl.ds(0, wb, stride=2), :].astype(BF16)
        vw = win_ref[0, pl.ds(1, wb, stride=2), :].astype(BF16)
        zpad = jnp.zeros((LANE - tp, LANE), F32)
        kn = jnp.concatenate([nw_ref[0, pl.ds(0, tp, stride=2), :], zpad], axis=0).astype(BF16)
        vn = jnp.concatenate([nw_ref[0, pl.ds(1, tp, stride=2), :], zpad], axis=0).astype(BF16)
        jw =lax.broadcasted_iota(jnp.int32, (rows, wb), 1)
        dw = pos - (past_len - wb + jw)
        jn = lax.broadcasted_iota(jnp.int32, (rows, LANE), 1)
        dn = tok - jn
        wmask = jnp.concatenate([(dw >= 0) & (dw < WINDOW), (dn >= 0) & (dn < WINDOW) & (jn < n_valid)],
                                axis=1)
        sw = jnp.where(wmask, jnp.concatenate([_nt(qb, kw), _nt(qb, kn)], axis=1), NEG)
        mw = jnp.max(sw, axis=-1, keepdims=True)
        pw = jnp.where(wmask, jnp.exp2(sw - mw), 0.0)
        o_win = (_mm(pw[:, 0:wb].astype(BF16), vw) + _mm(pw[:, wb:].astype(BF16), vn)) \
            / jnp.maximum(jnp.sum(pw, axis=-1, keepdims=True), 1e-30)

        gate = _sigmoid(bg_ref[0])
        g0 = jnp.concatenate([gate[:, 3 * h:3 * h + 1] for h in range(H_B)], axis=0)
        g1 = jnp.concatenate([gate[:, 3 * h + 1:3 * h + 2] for h in range(H_B)], axis=0)
        g2 = jnp.concatenate([gate[:, 3 * h + 2:3 * h + 3] for h in range(H_B)], axis=0)
        ocw_ref[0] = g0 * o_cmp + g2 * o_win
        g1_ref[0] = jnp.broadcast_to(g1, (rows, LANE))

    per_b = lambda r, w: pl.BlockSpec((1, r, w), lambda b: (b, 0, 0))
    return pl.pallas_call(
        body, grid=(db,),
        in_specs=[pl.BlockSpec((1, tp, 512), lambda b: (b, 0, _UNIT["bq"] // 4)),
                  per_b(nblk, 256), per_b(2 * wb, LANE), per_b(2 * tp, LANE),
                  pl.BlockSpec((1, tp, LANE), lambda b: (b, 0, _UNIT["bg"])),
                  pl.BlockSpec((4, LANE), lambda b: (0, 0))],
        out_specs=[per_b(rows, LANE), per_b(rows, LANE), per_b(rows, LANE), per_b(tp, nblk)],
        out_shape=[jax.ShapeDtypeStruct((db, rows, LANE), BF16),
                   jax.ShapeDtypeStruct((db, rows, LANE), F32),
                   jax.ShapeDtypeStruct((db, rows, LANE), F32),
                   jax.ShapeDtypeStruct((db, tp, nblk), F32)],
        compiler_params=_params("parallel"),
        name="b_select_sample")(proj3, pooled, win, neww, proj3, g4)


def _b_selected_sample(page_table, cache2, qs, selg, bkvs, ocw, g1b, proj3, page_base,
                       *, n_valid, pg):
    db, n_pages = page_table.shape
    ng = n_pages // pg
    rows = qs.shape[1]
    tp = rows // H_B
    per = PAGE // CMP_BLOCK
    nbg = pg * per
    width = pg * PAGE

    def body(pt_ref, *refs):
        pages = refs[:pg]
        q_ref, sel_ref, new_ref, ocw_ref, g1_ref, z_ref, o_ref, m_scr, l_scr, acc_scr = refs[pg:]
        gi = pl.program_id(1)

        @pl.when(gi == 0)
        def _():
            m_scr[...] = jnp.full(m_scr.shape, NEG, F32)
            l_scr[...] = jnp.zeros(l_scr.shape, F32)
            acc_scr[...] = jnp.zeros(acc_scr.shape, F32)

        q = q_ref[0]

        def slot(s_):
            return jnp.concatenate([pages[j][pl.ds(s_, PAGE, stride=4), :].astype(BF16)
                                    for j in range(pg)], axis=0)

        e = (_div(lax.broadcasted_iota(jnp.int32, (nbg, width), 1), CMP_BLOCK)
             == lax.broadcasted_iota(jnp.int32, (nbg, width), 0)).astype(BF16)
        mk = _mm(sel_ref[0, 0].astype(BF16), e) > 0.5
        mask = jnp.concatenate([mk] * H_B, axis=0)
        _online_update(jnp.where(mask, _nt(q, slot(2)), MASKED), slot(3), m_scr, l_scr, acc_scr)

        @pl.when(gi == ng - 1)
        def _():
            zpad = jnp.zeros((PAGE - tp, LANE), F32)
            kn = jnp.concatenate([new_ref[0, pl.ds(2, tp, stride=4), :], zpad], axis=0).astype(BF16)
            vn = jnp.concatenate([new_ref[0, pl.ds(3, tp, stride=4), :], zpad], axis=0).astype(BF16)
            tok = _mod(lax.broadcasted_iota(jnp.int32, (rows, PAGE), 0), tp)
            col = lax.broadcasted_iota(jnp.int32, (rows, PAGE), 1)
            nmask = (col <= tok) & (col < n_valid)
            _online_update(jnp.where(nmask, _nt(q, kn), MASKED), vn, m_scr, l_scr, acc_scr)
            o_sel = acc_scr[...] / jnp.maximum(l_scr[...], 1e-30)
            ob = ocw_ref[0] + g1_ref[0] * o_sel
            for h in range(H_B):
                y = ob[h * tp:(h + 1) * tp] * _silu(z_ref[0, :, h * LANE:(h + 1) * LANE])
                o_ref[0, :, h * LANE:(h + 1) * LANE] = y.astype(BF16)

    def page_spec(j):
        return pl.BlockSpec((PAGE * 4, LANE), lambda b, g, pt, j=j: (page_base + pt[b, g * pg + j], 0))

    per_b = lambda r, w: pl.BlockSpec((1, r, w), lambda b, g, pt: (b, 0, 0))
    grid_spec = pltpu.PrefetchScalarGridSpec(
        num_scalar_prefetch=1, grid=(db, ng),
        in_specs=[page_spec(j) for j in range(pg)] + [
            per_b(rows, LANE),
            pl.BlockSpec((1, 1, tp, nbg), lambda b, g, pt: (b, g, 0, 0)),
            per_b(4 * tp, LANE), per_b(rows, LANE), per_b(rows, LANE),
            pl.BlockSpec((1, tp, 512), lambda b, g, pt: (b, 0, _UNIT["bz"] // 4))],
        out_specs=per_b(tp, 512),
        scratch_shapes=[pltpu.VMEM((rows, LANE), F32)] * 3)
    return pl.pallas_call(
        body, grid_spec=grid_spec,
        out_shape=jax.ShapeDtypeStruct((db, tp, 512), BF16),
        compiler_params=_params("parallel", "arbitrary"),
        name="b_selected_sample")(page_table, *([cache2] * pg), qs, selg, bkvs, ocw, g1b, proj3)


def _mlstm(proj3, conv_w, conv_b, gate_b, out_g, conv0, c0, n0, m0, *, lc, n_valid):
    b_, t, _ = proj3.shape
    col_head = jnp.arange(H_C * LANE)[None, :] // LANE
    lane_id = jnp.arange(LANE)[:, None]
    rsel = jnp.stack([lane_id == col_head, lane_id == col_head + H_C]).astype(BF16)
    nchunk = t // lc
    dqk = H_C * DQK_C
    tail = CONV_W - 1
    base = 8
    tsq = max(lc, LANE)

    def bcast_cols(x, sel):
        hi, lo = _split2(x)
        return _mm(hi, sel) + _mm(lo, sel)

    def body(qk_ref, v_ref, if_ref, co_ref, cz_ref, cw_ref, cb_ref, gb_ref, g_ref, rsel_ref,
             conv0_ref, c0_ref, n0_ref, m0_ref,
             y_ref, conv_ref, c_ref, n_ref, m_ref, xbuf, c_scr, n_scr, m_scr):
        ci = pl.program_id(1)

        @pl.when(ci == 0)
        def _():
            xbuf[base - tail:base, :] = conv0_ref[0]
            c_scr[...] = c0_ref[0]
            n_scr[...] = n0_ref[0]
            m_scr[...] = m0_ref[0]

        xbuf[base:base + lc, :] = qk_ref[0]
        acc = cb_ref[...]
        for j in range(CONV_W):
            acc = acc + xbuf[base - tail + j:base - tail + j + lc, :] * cw_ref[j:j + 1, :]
        new_tail = xbuf[base + n_valid - tail:base + n_valid, :]
        xbuf[base - tail:base, :] = new_tail
        conv_ref[0] = new_tail
        qk = _silu(acc)

        gt = if_ref[0] + gb_ref[...]
        lf = jnp.minimum(gt, 0.0) - jnp.log(1.0 + jnp.exp(-jnp.abs(gt)))
        ti = lax.broadcasted_iota(jnp.int32, (lc, lc), 0)
        si = lax.broadcasted_iota(jnp.int32, (lc, lc), 1)
        tri = (si <= ti).astype(BF16)
        l1, l2, l3 = _split3(lf)
        bcum = _mm(tri, l1) + _mm(tri, l2) + _mm(tri, l3)
        zrow = jnp.zeros((tsq - lc, LANE), F32)
        gt_t = (jnp.concatenate([gt, zrow], axis=0) if tsq > lc else gt).T
        b_t = (jnp.concatenate([bcum, zrow], axis=0) if tsq > lc else bcum).T
        dmask = (si <= ti) & (si < n_valid)
        svalid = lax.broadcasted_iota(jnp.int32, (lc, LANE), 0) < n_valid
        icol_all = bcast_cols(gt, rsel_ref[0])
        bcol_all = bcast_cols(bcum, rsel_ref[1])
        gate_all = _sigmoid(co_ref[0]) * _silu(cz_ref[0])
        low = lax.broadcasted_iota(jnp.int32, (lc, LANE), 1) < DQK_C
        row_low = lax.broadcasted_iota(jnp.int32, (LANE, LANE), 0) < DQK_C

        c_new, n_new, m_new_all = [], [], []
        for j in range(H_C // 2):
            qp = qk[:, j * LANE:(j + 1) * LANE] * (DQK_C ** -0.5)
            kp = qk[:, dqk + j * LANE:dqk + (j + 1) * LANE]
            cp = c_scr[j]
            cpb = cp.astype(BF16)
            npair = n_scr[j:j + 1, :]
            c_upd, n_upd, carries = None, None, []
            for u in range(2):
                h = 2 * j + u
                sel = low if u == 0 else jnp.logical_not(low)
                qm = jnp.where(sel, qp, 0.0)
                km = jnp.where(sel, kp, 0.0)
                qmb, kmb = qm.astype(BF16), km.astype(BF16)
                vh = v_ref[0, :, h * HEAD:(h + 1) * HEAD]
                bcol = bcol_all[:, h * LANE:(h + 1) * LANE]
                icol = icol_all[:, h * LANE:(h + 1) * LANE]
                brow = b_t[H_C + h:H_C + h + 1, 0:lc]
                irow = gt_t[h:h + 1, 0:lc]
                m_h = m_scr[h:h + 1, :]

                d = jnp.where(dmask, bcol[:, 0:lc] - brow + irow, NEG)
                inter = bcol + m_h
                m_t = jnp.maximum(inter, jnp.max(d, axis=1, keepdims=True))
                w_intra = jnp.exp(d - m_t[:, 0:lc])
                w_inter = jnp.exp(inter - m_t)
                sqk = _nt(qmb, kmb) * w_intra
                num = w_inter * _mm(qmb, cpb) + _mm(sqk.astype(BF16), vh.astype(BF16))
                den = (w_inter * jnp.sum(qm * npair, axis=1, keepdims=True)
                       + jnp.sum(sqk, axis=1, keepdims=True))
                hh = num / jnp.maximum(jnp.abs(den), jnp.exp(-m_t))
                y_ref[0, :, h * HEAD:(h + 1) * HEAD] = (
                    _rms(hh, g_ref[...]) * gate_all[:, h * HEAD:(h + 1) * HEAD]).astype(BF16)

                b_last = bcol[n_valid - 1:n_valid, :]
                dec = jnp.where(svalid, b_last - bcol + icol, NEG)
                m_new = jnp.maximum(b_last + m_h, jnp.max(dec, axis=0, keepdims=True))
                wk = jnp.exp(dec - m_new)
                carries.append(jnp.exp(b_last + m_h - m_new))
                cu = _tn(kmb, (wk * vh).astype(BF16))
                nu = jnp.sum(wk * km, axis=0, keepdims=True)
                c_upd = cu if c_upd is None else c_upd + cu
                n_upd = nu if n_upd is None else n_upd + nu
                m_new_all.append(m_new)
            c_new.append(jnp.where(row_low, carries[0], carries[1]) * cp + c_upd)
            n_new.append(jnp.where(low[0:1], carries[0], carries[1]) * npair + n_upd)

        for j in range(H_C // 2):
            c_scr[j] = c_new[j]
            n_scr[j:j + 1, :] = n_new[j]
        for h in range(H_C):
            m_scr[h:h + 1, :] = m_new_all[h]
        c_ref[0] = c_scr[...]
        n_ref[0] = n_scr[...]
        m_ref[0] = m_scr[...]

    col = lambda u: pl.BlockSpec((1, lc, 1024), lambda b, c, u=u: (b, c, u))
    const = lambda *shape: pl.BlockSpec(shape, lambda b, c: (0,) * len(shape))
    per_b = lambda *shape: pl.BlockSpec((1,) + shape, lambda b, c: (b,) + (0,) * len(shape))
    return pl.pallas_call(
        body, grid=(b_, nchunk),
        in_specs=[col(_UNIT["cqk"] // 8), col(_UNIT["cv"] // 8),
                  pl.BlockSpec((1, lc, LANE), lambda b, c: (b, c, _UNIT["cif"])),
                  col(_UNIT["co"] // 8), col(_UNIT["cz"] // 8),
                  const(CONV_W, 1024), const(1, 1024), const(1, LANE), const(1, LANE),
                  const(2, LANE, H_C * LANE),
                  per_b(tail, 1024), per_b(H_C // 2, LANE, HEAD), per_b(H_C // 2, LANE), per_b(H_C, LANE)],
        out_specs=[pl.BlockSpec((1, lc, 1024), lambda b, c: (b, c, 0)),
                   per_b(tail, 1024), per_b(H_C // 2, LANE, HEAD), per_b(H_C // 2, LANE), per_b(H_C, LANE)],
        out_shape=[jax.ShapeDtypeStruct((b_, t, 1024), BF16),
                   jax.ShapeDtypeStruct((b_, tail, 1024), F32),
                   jax.ShapeDtypeStruct((b_, H_C // 2, LANE, HEAD), F32),
                   jax.ShapeDtypeStruct((b_, H_C // 2, LANE), F32),
                   jax.ShapeDtypeStruct((b_, H_C, LANE), F32)],
        scratch_shapes=[pltpu.VMEM((base + lc, 1024), F32), pltpu.VMEM((H_C // 2, LANE, HEAD), F32),
                        pltpu.VMEM((H_C // 2, LANE), F32), pltpu.VMEM((H_C, LANE), F32)],
        compiler_params=_params("parallel", "arbitrary"),
        name="mlstm")(proj3, proj3, proj3, proj3, proj3, conv_w, conv_b, gate_b, out_g, rsel,
                      conv0, c0, n0, m0)


def _w_prep(w_in):
    depth, d, n_in = w_in.shape
    kt_n = d // LANE
    per_col = kt_n * depth
    rows = w_in.reshape(depth, kt_n, LANE, n_in).transpose(3, 1, 0, 2).reshape(n_in * per_col, LANE)
    src_off, off = {}, 0
    for name, width in _SRC:
        src_off[name] = (off, width)
        off += width
    assert off == n_in
    starts, valids = [0] * (N_PROJ // LANE), [0] * (N_PROJ // LANE)
    for name in _DST_ORDER:
        s0, width = src_off[name]
        for u in range(-(-width // LANE)):
            starts[_UNIT[name] + u] = s0 + u * LANE
            valids[_UNIT[name] + u] = min(LANE, width - u * LANE)
    table = jnp.array([starts, valids], jnp.int32)

    def body(tbl_ref, w_ref, o_ref):
        u = pl.program_id(0)
        keep = lax.broadcasted_iota(jnp.int32, (LANE, LANE), 0) < tbl_ref[1, u]
        xt = jnp.swapaxes(w_ref[...].reshape(LANE, per_col, LANE), 0, 1)
        for l in range(depth):
            for kt in range(kt_n):
                o_ref[l, :, kt * LANE:(kt + 1) * LANE] = jnp.where(keep, xt[kt * depth + l], 0.0).astype(BF16)

    grid_spec = pltpu.PrefetchScalarGridSpec(
        num_scalar_prefetch=1, grid=(N_PROJ // LANE,),
        in_specs=[pl.BlockSpec((pl.Element(LANE * per_col), pl.Element(LANE)),
                               lambda u, tbl: (tbl[0, u] * per_col, 0))],
        out_specs=pl.BlockSpec((depth, LANE, d), lambda u, tbl: (0, u, 0)))
    return pl.pallas_call(
        body, grid_spec=grid_spec,
        out_shape=jax.ShapeDtypeStruct((depth, N_PROJ, d), BF16),
        compiler_params=_params("parallel"),
        name="w_prep")(table, rows)


def _c_from_pairs(c):
    b = c.shape[0]
    return jnp.swapaxes(c.reshape(b, H_C, DQK_C, HEAD), -1, -2)


def _pick(n, prefs):
    for p in prefs:
        if n % p == 0:
            return p
    return n


def kernel(x_prompt, x_sample, cache_a_kv, cache_b_kv, state_b_win, state_c_conv, state_c_C,
           state_c_n, state_c_m, page_table, norm_g, w_in, w_out, a_qk_g, a_lambda, a_out_g,
           b_qk_g, b_cmp_w, c_conv_w, c_conv_b, c_gate_b, c_out_g):
    bp, t, d = x_prompt.shape
    db, ts, _ = x_sample.shape
    depth = norm_g.shape[0]
    n_pool = cache_a_kv.shape[1]
    n_pages = page_table.shape[1]
    past_len = n_pages * PAGE
    wb = state_b_win.shape[2]
    tp = 8
    assert ts <= tp and t % MLSTM_CHUNK == 0 and t % Q_BLOCK == 0 and d == 2048
    mp, ms = bp * t, db * tp
    pg = _pick(n_pages, (16, 8, 4, 2, 1))

    cache_a2 = cache_a_kv.reshape(depth * n_pool * PAGE * 2 * H_A, HEAD)
    cache_b2 = cache_b_kv.reshape(depth * n_pool * PAGE * 4, HEAD)
    win_rows = state_b_win.reshape(depth, db, wb * 2, HEAD)
    hp = x_prompt.reshape(mp, d)
    hs = jnp.pad(x_sample, ((0, 0), (0, tp - ts), (0, 0))).reshape(ms, d)

    tm_p = _pick(mp, (1024, 512, 256, 128))
    tq_a = _pick(t, (512, 256, 128))
    outs_p = [[] for _ in range(7)]
    outs_s = [[] for _ in range(7)]

    w_all = _w_prep(w_in)
    for l in range(depth):
        lam_init = 0.8 - 0.6 * math.exp(-0.3 * l)
        w_o = w_out[l].astype(BF16)
        g_a = jnp.tile(a_qk_g[l], (1, 2))
        g_b = b_qk_g[l]
        wpb = jnp.broadcast_to(b_cmp_w[l][:, :, None], (2, CMP_BLOCK, LANE))
        wpb2 = jnp.concatenate([wpb[0], wpb[1]], axis=1)
        gate_b = jnp.pad(c_gate_b[l].reshape(1, 2 * H_C), ((0, 0), (0, LANE - 2 * H_C)))
        conv_b = c_conv_b[l].reshape(1, -1)
        out_gc = c_out_g[l].reshape(1, LANE)

        proj = _norm_matmul(hp, norm_g[l], w_all, l, tm=tm_p, tn=1024)
        proj3 = proj.reshape(bp, t, N_PROJ)
        qn, kn, vb, akv = _a_prep(proj, g_a, tm=_pick(mp, (512, 256, 128)))
        ya = _a_attn_prompt(qn.reshape(bp, t, 512), kn.reshape(bp, t, 512), vb.reshape(bp, t, 512),
                            proj3, a_lambda[l], a_out_g[l], lam_init, tq=tq_a)
        bkvs, neww, kvb, kcb, vcb = _b_prep(proj, g_b, wpb, tm=_pick(mp, (512, 256, 128)), pool=True)
        nblk = t // CMP_BLOCK
        yb = _b_attn_prompt(proj3, kcb.reshape(bp, nblk, LANE), vcb.reshape(bp, nblk, LANE),
                            kvb.reshape(bp, t, 512), g_b, t=t)
        yc, conv_p, c_p, n_p, m_p = _mlstm(
            proj3, c_conv_w[l], conv_b, gate_b, out_gc,
            jnp.zeros((bp, CONV_W - 1, 2 * H_C * DQK_C), F32), jnp.zeros((bp, H_C // 2, LANE, HEAD), F32),
            jnp.zeros((bp, H_C // 2, LANE), F32), jnp.zeros((bp, H_C, LANE), F32),
            lc=MLSTM_CHUNK, n_valid=MLSTM_CHUNK)
        c_p, n_p = _c_from_pairs(c_p), n_p.reshape(bp, H_C, DQK_C)
        hp = _out_proj(hp, ya.reshape(mp, 512), yb.reshape(mp, 512), yc.reshape(mp, 1024), w_o,
                       tm=tm_p, tn=1024)
        win_p = jnp.concatenate([jnp.zeros((bp, wb, 2, HEAD), F32), neww.reshape(bp, t, 2, HEAD)],
                                axis=1)[:, -wb:]
        for lst, a in zip(outs_p, (akv.reshape(bp, t, 2, H_A, HEAD), bkvs.reshape(bp, t, 4, HEAD),
                                   win_p, conv_p, c_p, n_p, m_p[:, :, 0])):
            lst.append(a)

        base = l * n_pool
        sproj = _norm_matmul(hs, norm_g[l], w_all, l, tm=ms, tn=1024)
        sproj3 = sproj.reshape(db, tp, N_PROJ)
        sqn, skn, svb, sakv = _a_prep(sproj, g_a, tm=ms)
        kvnew = jnp.concatenate([skn.reshape(db, tp, 512), svb.reshape(db, tp, 512)], axis=2)
        kvnew = jnp.pad(kvnew, ((0, 0), (0, PAGE - tp), (0, 0)))
        sya = _a_attn_sample(page_table, cache_a2, sqn.reshape(db, tp, 512), kvnew, sproj3,
                             a_lambda[l], a_out_g[l], lam_init, base, n_valid=ts, pg=pg)
        sbkvs, sneww, _ = _b_prep(sproj, g_b, wpb, tm=ms, pool=False)
        pooled = _b_pool_pages(page_table, cache_b2, wpb2, base, pg=pg)
        qs, ocw, g1b, sel = _b_select_sample(
            sproj3, pooled, win_rows[l], sneww.reshape(db, tp * 2, HEAD), g_b,
            past_len=past_len, n_valid=ts)
        nbg = pg * (PAGE // CMP_BLOCK)
        selg = sel.reshape(db, tp, n_pages // pg, nbg).transpose(0, 2, 1, 3)
        syb = _b_selected_sample(page_table, cache_b2, qs, selg, sbkvs.reshape(db, tp * 4, HEAD), ocw, g1b,
                                 sproj3, base, n_valid=ts, pg=pg)
        m0 = jnp.broadcast_to(state_c_m[l][:, :, None], (db, H_C, LANE))
        syc, conv_s, c_s, n_s, m_s = _mlstm(
            sproj3, c_conv_w[l], conv_b, gate_b, out_gc, state_c_conv[l],
            jnp.swapaxes(state_c_C[l], -1, -2).reshape(db, H_C // 2, LANE, HEAD),
            state_c_n[l].reshape(db, H_C // 2, LANE), m0, lc=tp, n_valid=ts)
        c_s, n_s = _c_from_pairs(c_s), n_s.reshape(db, H_C, DQK_C)
        hs = _out_proj(hs, sya.reshape(ms, 512), syb.reshape(ms, 512), syc.reshape(ms, 1024), w_o,
                       tm=ms, tn=1024)
        win_s = jnp.concatenate([state_b_win[l], sneww.reshape(db, tp, 2, HEAD)[:, :ts]], axis=1)[:, -wb:]
        for lst, a in zip(outs_s, (sakv.reshape(db, tp, 2, H_A, HEAD)[:, :ts],
                                   sbkvs.reshape(db, tp, 4, HEAD)[:, :ts],
                                   win_s, conv_s, c_s, n_s, m_s[:, :, 0])):
            lst.append(a)

    y_p = hp.reshape(bp, t, d)
    y_s = hs.reshape(db, tp, d)[:, :ts]
    sp = [jnp.stack(x) for x in outs_p]
    ss = [jnp.stack(x) for x in outs_s]
    return (y_p, y_s, sp[0], ss[0], sp[1], ss[1], sp[2], ss[2], sp[3], ss[3],
            sp[4], ss[4], sp[5], ss[5], sp[6], ss[6])
```

```python
import functools
import math

import jax
import jax.numpy as jnp
from jax import lax
from jax.experimental import pallas as pl
from jax.experimental.pallas import tpu as pltpu

F32 = jnp.float32
BF16 = jnp.bfloat16

EPS = 1e-6
LANE = 128
HEAD = 128
H_A = 4
DQ_A = HEAD // 2
H_B = 4
H_C = 8
DQK_C = HEAD // 2
CMP_BLOCK = 64
N_SEL = 16
WINDOW = 512
CONV_W = 4
MLSTM_CHUNK = 128
PAGE = 128
Q_BLOCK = 128
NEG = -1e30
MASKED = 2 * NEG
SCORE_LOG2E = math.log2(math.e)
VMEM_LIMIT = 56 * 1024 * 1024

_SRC = (("aq", 512), ("ak", 512), ("av", 512), ("az", 512),
        ("bq", 512), ("bkv", 768), ("bg", 12), ("bz", 512),
        ("cqk", 1024), ("cv", 1024), ("cif", 16), ("co", 1024), ("cz", 1024))
_DST_ORDER = ("aq", "ak", "av", "az", "bq", "bz", "bkv", "bg", "cif", "cqk", "cv", "co", "cz")
_UNIT = {"aq": 0, "ak": 4, "av": 8, "az": 12, "bq": 16, "bz": 20, "bkv": 24, "bg": 30,
         "cif": 31, "cqk": 32, "cv": 40, "co": 48, "cz": 56}
N_PROJ = 64 * LANE


def _nt(a, b):
    return lax.dot_general(a, b, (((1,), (1,)), ((), ())), preferred_element_type=F32)


def _tn(a, b):
    return lax.dot_general(a, b, (((0,), (0,)), ((), ())), preferred_element_type=F32)


def _mm(a, b):
    return jnp.dot(a, b, preferred_element_type=F32)


def _split2(x):
    hi = x.astype(BF16)
    lo = (x - hi.astype(F32)).astype(BF16)
    return hi, lo


def _split3(x):
    hi = x.astype(BF16)
    r = x - hi.astype(F32)
    mid = r.astype(BF16)
    lo = (r - mid.astype(F32)).astype(BF16)
    return hi, mid, lo


def _sigmoid(z):
    return 1.0 / (1.0 + jnp.exp(-z))


def _silu(z):
    return z * _sigmoid(z)


def _rms(x, g):
    return x * lax.rsqrt(jnp.mean(x * x, axis=-1, keepdims=True) + EPS) * g


def _div(x, n):
    return lax.shift_right_logical(x, int(math.log2(n)))


def _mod(x, n):
    return x & (n - 1)


def _online_update(s, v, m_ref, l_ref, acc_ref, pv_fn=None):
    w = s.shape[1]
    m_old = m_ref[...]
    m_new = jnp.maximum(m_old, jnp.max(s, axis=-1, keepdims=True))
    alpha = jnp.exp2(m_old - m_new)
    m_rep = m_new if w == LANE else jnp.concatenate([m_new] * (w // LANE), axis=1)
    p = jnp.exp2(s - m_rep)
    l_ref[...] = alpha * l_ref[...] + jnp.sum(p, axis=-1, keepdims=True)
    pb = p.astype(BF16)
    acc_ref[...] = alpha * acc_ref[...] + (_mm(pb, v) if pv_fn is None else pv_fn(pb))
    m_ref[...] = m_new


def _params(*sem):
    return pltpu.CompilerParams(dimension_semantics=sem, vmem_limit_bytes=VMEM_LIMIT)


def _norm_matmul(x, g, w, layer, *, tm, tn):
    m, d = x.shape
    n = w.shape[1]
    rc = min(tm, 256)

    def body(x_ref, g_ref, w_ref, o_ref, h_scr):
        @pl.when(pl.program_id(1) == 0)
        def _():
            def chunk(c, carry):
                r = pl.ds(pl.multiple_of(c * rc, rc), rc)
                h_scr[r, :] = _rms(x_ref[r, :], g_ref[...]).astype(BF16)
                return carry
            lax.fori_loop(0, tm // rc, chunk, 0)
        o_ref[...] = _nt(h_scr[...], w_ref[0])

    return pl.pallas_call(
        body, grid=(m // tm, n // tn),
        in_specs=[pl.BlockSpec((tm, d), lambda i, j: (i, 0)),
                  pl.BlockSpec((1, d), lambda i, j: (0, 0)),
                  pl.BlockSpec((1, tn, d), lambda i, j: (layer, j, 0))],
        out_specs=pl.BlockSpec((tm, tn), lambda i, j: (i, j)),
        out_shape=jax.ShapeDtypeStruct((m, n), F32),
        scratch_shapes=[pltpu.VMEM((tm, d), BF16)],
        compiler_params=_params("parallel", "arbitrary"),
        name="norm_matmul")(x, g.reshape(1, d), w)


def _out_proj(x, ya, yb, yc, w, *, tm, tn):
    m, d = x.shape
    da, db, dc = ya.shape[1], yb.shape[1], yc.shape[1]

    def body(x_ref, a_ref, b_ref, c_ref, w_ref, o_ref):
        mix = jnp.concatenate([a_ref[...], b_ref[...], c_ref[...]], axis=1)
        o_ref[...] = x_ref[...] + _mm(mix, w_ref[...])

    return pl.pallas_call(
        body, grid=(m // tm, d // tn),
        in_specs=[pl.BlockSpec((tm, tn), lambda i, j: (i, j)),
                  pl.BlockSpec((tm, da), lambda i, j: (i, 0)),
                  pl.BlockSpec((tm, db), lambda i, j: (i, 0)),
                  pl.BlockSpec((tm, dc), lambda i, j: (i, 0)),
                  pl.BlockSpec((da + db + dc, tn), lambda i, j: (0, j))],
        out_specs=pl.BlockSpec((tm, tn), lambda i, j: (i, j)),
        out_shape=jax.ShapeDtypeStruct((m, d), F32),
        compiler_params=_params("parallel", "parallel"),
        name="out_proj")(x, ya, yb, yc, w)


def _a_prep(proj, g2, *, tm):
    m = proj.shape[0]

    def body(q_ref, k_ref, v_ref, g_ref, qn_ref, kn_ref, vb_ref, akv_ref):
        r = _div(lax.broadcasted_iota(jnp.int32, (LANE, LANE), 0), DQ_A)
        c = _div(lax.broadcasted_iota(jnp.int32, (LANE, LANE), 1), DQ_A)
        seg = (r == c).astype(BF16)

        def segnorm(x, g):
            outs = []
            for u in range(4):
                xc = x[:, u * LANE:(u + 1) * LANE]
                hi, lo = _split2(xc * xc)
                s = _mm(hi, seg) + _mm(lo, seg)
                outs.append(xc * lax.rsqrt(s * (1.0 / DQ_A) + EPS) * g)
            return jnp.concatenate(outs, axis=1)

        qn = segnorm(q_ref[...], g_ref[0:1, :]) * (DQ_A ** -0.5 * SCORE_LOG2E)
        kn = segnorm(k_ref[...], g_ref[1:2, :])
        v = v_ref[...]
        qn_ref[...] = qn.astype(BF16)
        kn_ref[...] = kn.astype(BF16)
        vb_ref[...] = v.astype(BF16)
        for h in range(H_A):
            akv_ref[pl.ds(h, tm, stride=2 * H_A), :] = kn[:, h * LANE:(h + 1) * LANE]
            akv_ref[pl.ds(H_A + h, tm, stride=2 * H_A), :] = v[:, h * LANE:(h + 1) * LANE]

    blk = lambda u: pl.BlockSpec((tm, 512), lambda i, u=u: (i, u))
    row = lambda w: pl.BlockSpec((tm, w), lambda i: (i, 0))
    return pl.pallas_call(
        body, grid=(m // tm,),
        in_specs=[blk(_UNIT["aq"] // 4), blk(_UNIT["ak"] // 4), blk(_UNIT["av"] // 4),
                  pl.BlockSpec((2, LANE), lambda i: (0, 0))],
        out_specs=[row(512), row(512), row(512), pl.BlockSpec((tm * 2 * H_A, LANE), lambda i: (i, 0))],
        out_shape=[jax.ShapeDtypeStruct((m, 512), BF16)] * 3
        + [jax.ShapeDtypeStruct((m * 2 * H_A, LANE), F32)],
        compiler_params=_params("parallel"),
        name="a_prep")(proj, proj, proj, g2)


def _diff_lambda(lp, lam_init):
    a = jnp.sum(lp[0:1, :] * lp[1:2, :], axis=-1, keepdims=True)
    b = jnp.sum(lp[2:3, :] * lp[3:4, :], axis=-1, keepdims=True)
    return jnp.exp(a) - jnp.exp(b) + lam_init


def _a_finish(acc1, l1, acc2, l2, lam, g, z, lam_init):
    o = acc1 / l1 - lam * (acc2 / l2)
    return _rms(o, g) * (1.0 - lam_init) * _silu(z)


def _a_attn_prompt(qn, kn, vb, proj3, lam_p, out_g, lam_init, *, tq):
    b_, t, _ = qn.shape
    tk = tq

    def body(q_ref, k_ref, v_ref, z_ref, lam_ref, g_ref, o_ref, m_scr, l_scr, acc_scr):
        qi = pl.program_id(2)
        q = q_ref[0]
        lane = lax.broadcasted_iota(jnp.int32, (tq, LANE), 1)
        zero = jnp.zeros_like(q)
        qs = (jnp.where(lane < DQ_A, q, zero), jnp.where(lane >= DQ_A, q, zero))
        m_scr[...] = jnp.full(m_scr.shape, NEG, F32)
        l_scr[...] = jnp.zeros(l_scr.shape, F32)
        acc_scr[...] = jnp.zeros(acc_scr.shape, F32)

        def chunk(kc, mask):
            ks = pl.ds(pl.multiple_of(kc * tk, tk), tk)
            k = k_ref[0, ks, :]
            v = v_ref[0, ks, :]
            for c in range(2):
                s = _nt(qs[c], k)
                if mask is not None:
                    s = jnp.where(mask, s, MASKED)
                _online_update(s, v, m_scr.at[c], l_scr.at[c], acc_scr.at[c])

        def full_chunk(kc, carry):
            chunk(kc, None)
            return carry

        lax.fori_loop(0, qi, full_chunk, 0)
        chunk(qi, lax.broadcasted_iota(jnp.int32, (tq, tk), 1)
              <= lax.broadcasted_iota(jnp.int32, (tq, tk), 0))
        lam = _diff_lambda(lam_ref[...], lam_init)
        y = _a_finish(acc_scr[0], l_scr[0], acc_scr[1], l_scr[1], lam, g_ref[...], z_ref[0], lam_init)
        o_ref[0] = y.astype(BF16)

    head = lambda rows, qdep: pl.BlockSpec(
        (1, rows, LANE), (lambda b, h, i: (b, i, h)) if qdep else (lambda b, h, i: (b, 0, h)))
    return pl.pallas_call(
        body, grid=(b_, H_A, t // tq),
        in_specs=[head(tq, True), head(t, False), head(t, False),
                  pl.BlockSpec((1, tq, LANE), lambda b, h, i: (b, i, _UNIT["az"] + h)),
                  pl.BlockSpec((4, DQ_A), lambda b, h, i: (0, 0)),
                  pl.BlockSpec((1, LANE), lambda b, h, i: (0, 0))],
        out_specs=head(tq, True),
        out_shape=jax.ShapeDtypeStruct((b_, t, 512), BF16),
        scratch_shapes=[pltpu.VMEM((2, tq, LANE), F32)] * 3,
        compiler_params=_params("parallel", "parallel", "arbitrary"),
        name="a_attn_prompt")(qn, kn, vb, proj3, lam_p, out_g.reshape(1, LANE))


def _a_attn_sample(page_table, cache2, qn, kvnew, proj3, lam_p, out_g, lam_init, page_base,
                   cache_b2, wpb2, *, n_valid, pg):
    db, n_pages = page_table.shape
    tp = qn.shape[1]
    ng = n_pages // pg
    per = PAGE // CMP_BLOCK

    def body(pt_ref, *refs):
        pages = refs[:pg]
        bpages = refs[pg:2 * pg]
        (q_ref, new_ref, z_ref, lam_ref, g_ref, wp_ref, o_ref, pool_ref,
         m_scr, l_scr, acc_scr) = refs[2 * pg:]

        rows = []
        for j in range(pg):
            for u in range(per):
                parts = []
                for kind in range(2):
                    x = bpages[j][pl.ds(u * CMP_BLOCK * 4 + kind, CMP_BLOCK, stride=4), :]
                    parts.append(jnp.sum(x * wp_ref[:, kind * LANE:(kind + 1) * LANE], axis=0, keepdims=True))
                rows.append(jnp.concatenate(parts, axis=1))
        pool_ref[0] = jnp.concatenate(rows, axis=0)
        gi = pl.program_id(1)

        @pl.when(gi == 0)
        def _():
            m_scr[...] = jnp.full(m_scr.shape, NEG, F32)
            l_scr[...] = jnp.zeros(l_scr.shape, F32)
            acc_scr[...] = jnp.zeros(acc_scr.shape, F32)

        lane = lax.broadcasted_iota(jnp.int32, (tp, LANE), 1)

        def qstack(h):
            q = q_ref[0, :, h * LANE:(h + 1) * LANE]
            zero = jnp.zeros_like(q)
            return jnp.concatenate([jnp.where(lane < DQ_A, q, zero),
                                    jnp.where(lane >= DQ_A, q, zero)], axis=0)

        pts = [jnp.swapaxes(pages[j][...].reshape(PAGE, 2 * H_A, LANE), 0, 1) for j in range(pg)]

        def slot(s_):
            return jnp.concatenate([pts[j][s_].astype(BF16) for j in range(pg)], axis=0)

        s = jnp.concatenate([_nt(qstack(h), slot(h)) for h in range(H_A)], axis=0)
        _online_update(s, None, m_scr, l_scr, acc_scr,
                       pv_fn=lambda p: jnp.concatenate(
                           [_mm(p[h * 2 * tp:(h + 1) * 2 * tp], slot(H_A + h)) for h in range(H_A)], axis=0))

        @pl.when(gi == ng - 1)
        def _():
            lam = _diff_lambda(lam_ref[...], lam_init)
            row = _mod(lax.broadcasted_iota(jnp.int32, (2 * tp, PAGE), 0), tp)
            col = lax.broadcasted_iota(jnp.int32, (2 * tp, PAGE), 1)
            mask = (col <= row) & (col < n_valid)
            mask4 = jnp.concatenate([mask] * H_A, axis=0)
            s = jnp.concatenate([_nt(qstack(h), new_ref[0, :, h * LANE:(h + 1) * LANE])
                                 for h in range(H_A)], axis=0)
            _online_update(jnp.where(mask4, s, MASKED), None, m_scr, l_scr, acc_scr,
                           pv_fn=lambda p: jnp.concatenate(
                               [_mm(p[h * 2 * tp:(h + 1) * 2 * tp],
                                    new_ref[0, :, 512 + h * LANE:512 + (h + 1) * LANE])
                                for h in range(H_A)], axis=0))
            for h in range(H_A):
                r0 = h * 2 * tp
                acc = acc_scr[r0:r0 + 2 * tp, :]
                l = l_scr[r0:r0 + 2 * tp, :]
                y = _a_finish(acc[0:tp], l[0:tp], acc[tp:], l[tp:], lam, g_ref[...],
                              z_ref[0, :, h * LANE:(h + 1) * LANE], lam_init)
                o_ref[0, :, h * LANE:(h + 1) * LANE] = y.astype(BF16)

    def page_spec(j):
        return pl.BlockSpec((PAGE * 2 * H_A, LANE),
                            lambda b, g, pt, j=j: (page_base + pt[b, g * pg + j], 0))

    def bpage_spec(j):
        return pl.BlockSpec((PAGE * 4, LANE), lambda b, g, pt, j=j: (page_base + pt[b, g * pg + j], 0))

    in_specs = [page_spec(j) for j in range(pg)] + [bpage_spec(j) for j in range(pg)] + [
            pl.BlockSpec((1, tp, 512), lambda b, g, pt: (b, 0, 0)),
            pl.BlockSpec((1, PAGE, 1024), lambda b, g, pt: (b, 0, 0)),
            pl.BlockSpec((1, tp, 512), lambda b, g, pt: (b, 0, _UNIT["az"] // 4)),
            pl.BlockSpec((4, DQ_A), lambda b, g, pt: (0, 0)),
            pl.BlockSpec((1, LANE), lambda b, g, pt: (0, 0)),
            pl.BlockSpec((CMP_BLOCK, 256), lambda b, g, pt: (0, 0))]
    grid_spec = pltpu.PrefetchScalarGridSpec(
        num_scalar_prefetch=1, grid=(db, ng), in_specs=in_specs,
        out_specs=[pl.BlockSpec((1, tp, 512), lambda b, g, pt: (b, 0, 0)),
                   pl.BlockSpec((1, pg * per, 256), lambda b, g, pt: (b, g, 0))],
        scratch_shapes=[pltpu.VMEM((H_A * 2 * tp, LANE), F32)] * 3)
    return pl.pallas_call(
        body, grid_spec=grid_spec,
        out_shape=[jax.ShapeDtypeStruct((db, tp, 512), BF16),
                   jax.ShapeDtypeStruct((db, n_pages * per, 256), F32)],
        compiler_params=_params("parallel", "arbitrary"),
        name="a_attn_sample")(page_table, *([cache2] * pg), *([cache_b2] * pg), qn, kvnew, proj3,
                              lam_p, out_g.reshape(1, LANE), wpb2)


def _b_prep(proj, g4, wpb, *, tm, pool):
    m = proj.shape[0]
    nb = tm // CMP_BLOCK

    def body(kv_ref, g_ref, wp_ref, bkv_ref, nw_ref, kvb_ref, *pool_refs):
        kc = kv_ref[:, 0:128]
        vc = kv_ref[:, 128:256]
        ks = _rms(kv_ref[:, 256:384], g_ref[2:3, :])
        vs = kv_ref[:, 384:512]
        kw = _rms(kv_ref[:, 512:640], g_ref[3:4, :])
        vw = kv_ref[:, 640:768]
        for slot, x in enumerate((kc, vc, ks, vs)):
            bkv_ref[pl.ds(slot, tm, stride=4), :] = x
        nw_ref[pl.ds(0, tm, stride=2), :] = kw
        nw_ref[pl.ds(1, tm, stride=2), :] = vw
        kvb_ref[:, 0:128] = ks.astype(BF16)
        kvb_ref[:, 128:256] = vs.astype(BF16)
        kvb_ref[:, 256:384] = kw.astype(BF16)
        kvb_ref[:, 384:512] = vw.astype(BF16)
        if pool:
            kcb_ref, vcb_ref = pool_refs
            kp = jnp.sum(kc.reshape(nb, CMP_BLOCK, LANE) * wp_ref[0][None], axis=1)
            vp = jnp.sum(vc.reshape(nb, CMP_BLOCK, LANE) * wp_ref[1][None], axis=1)
            kcb_ref[...] = _rms(kp, g_ref[1:2, :])
            vcb_ref[...] = vp

    row = lambda w: pl.BlockSpec((tm, w), lambda i: (i, 0))
    out_specs = [pl.BlockSpec((tm * 4, LANE), lambda i: (i, 0)),
                 pl.BlockSpec((tm * 2, LANE), lambda i: (i, 0)), row(512)]
    out_shape = [jax.ShapeDtypeStruct((m * 4, LANE), F32), jax.ShapeDtypeStruct((m * 2, LANE), F32),
                 jax.ShapeDtypeStruct((m, 512), BF16)]
    if pool:
        out_specs += [pl.BlockSpec((nb, LANE), lambda i: (i, 0))] * 2
        out_shape += [jax.ShapeDtypeStruct((m // CMP_BLOCK, LANE), F32)] * 2
    return pl.pallas_call(
        body, grid=(m // tm,),
        in_specs=[pl.BlockSpec((tm, 768), lambda i: (i, _UNIT["bkv"] // 6)),
                  pl.BlockSpec((4, LANE), lambda i: (0, 0)),
                  pl.BlockSpec((2, CMP_BLOCK, LANE), lambda i: (0, 0, 0))],
        out_specs=out_specs, out_shape=out_shape,
        compiler_params=_params("parallel"),
        name="b_prep")(proj, g4, wpb)


def _cmp_scores(qf, kcb):
    qh, ql = _split2(qf)
    kh, kl = _split2(kcb)
    return _nt(qh, kh) + _nt(qh, kl) + _nt(ql, kh)


def _b_attn_prompt(proj3, kcb, vcb, kvb, g4, *, t):
    b_ = proj3.shape[0]
    tq = 2 * Q_BLOCK if t % (2 * Q_BLOCK) == 0 else Q_BLOCK
    nblk = t // CMP_BLOCK
    n_top = min(N_SEL, nblk)
    tk = min(512, t)
    span = min(WINDOW + tq, t)
    scale = HEAD ** -0.5 * SCORE_LOG2E

    def body(q_ref, kcb_ref, vcb_ref, kv_ref, bg_ref, bz_ref, g_ref, o_ref, m_scr, l_scr, acc_scr):
        qi = pl.program_id(1)
        qf = jnp.concatenate([_rms(q_ref[0, :, h * LANE:(h + 1) * LANE], g_ref[0:1, :]) * scale
                              for h in range(H_B)], axis=0)
        qb = qf.astype(BF16)
        pos = qi * tq + lax.broadcasted_iota(jnp.int32, (tq, 1), 0)

        pos_l = qi * tq + lax.broadcasted_iota(jnp.int32, (1, tq), 1)
        cur_l = _div(pos_l, CMP_BLOCK)
        blk_s = lax.broadcasted_iota(jnp.int32, (nblk, tq), 0)
        cmask = blk_s < cur_l
        kh, kl = _split2(kcb_ref[0])
        qh, ql = _split2(qf)
        cmask4 = jnp.concatenate([cmask] * H_B, axis=1)
        s = jnp.where(cmask4, _nt(kh, qh) + _nt(kl, qh) + _nt(kh, ql), NEG)
        mx = jnp.max(s, axis=0, keepdims=True)
        p = jnp.where(cmask4, jnp.exp2(s - mx), 0.0)
        pc = p / jnp.maximum(jnp.sum(p, axis=0, keepdims=True), 1e-30)
        pcs = [pc[:, h * tq:(h + 1) * tq] for h in range(H_B)]
        imp = jnp.where(cmask, pcs[0] + pcs[1] + pcs[2] + pcs[3], -1.0)
        fill = LANE - H_B * nblk
        pcq = jnp.concatenate(pcs + ([jnp.zeros((fill, tq), F32)] if fill else []), axis=0).T.astype(BF16)
        vch = vcb_ref[0].astype(BF16)
        zblk = jnp.zeros((nblk, LANE), BF16)
        vdiag = jnp.concatenate(
            [jnp.concatenate([vch if c == h else zblk for c in range(H_B)], axis=1)
             for h in range(H_B)] + ([jnp.zeros((fill, H_B * LANE), BF16)] if fill else []), axis=0)
        o_cmp = _mm(pcq, vdiag)

        rank = jnp.zeros((nblk, tq), F32)
        for mrow in range(nblk):
            cm = imp[mrow:mrow + 1, :]
            ahead = (cm > imp) | ((cm == imp) & (mrow < blk_s))
            rank = rank + ahead.astype(F32)
        sel_t = (((rank < n_top) & (imp >= 0.0)) | (blk_s == cur_l)).astype(F32)
        selq = jnp.concatenate([sel_t, jnp.zeros((LANE - nblk, tq), F32)], axis=0).T.astype(BF16)

        m_scr[...] = jnp.full(m_scr.shape, NEG, F32)
        l_scr[...] = jnp.zeros(l_scr.shape, F32)
        acc_scr[...] = jnp.zeros(acc_scr.shape, F32)
        nk = _div(qi * tq + tq + tk - 1, tk)

        def chunk(c, carry):
            ks_ = pl.ds(pl.multiple_of(c * tk, tk), tk)
            kidx = c * tk + lax.broadcasted_iota(jnp.int32, (LANE, tk), 1)
            e = (_div(kidx, CMP_BLOCK) == lax.broadcasted_iota(jnp.int32, (LANE, tk), 0)).astype(BF16)
            kpos = c * tk + lax.broadcasted_iota(jnp.int32, (tq, tk), 1)
            mk = (_mm(selq, e) > 0.5) & (kpos <= pos)
            mask = jnp.concatenate([mk] * H_B, axis=0)
            s = jnp.where(mask, _nt(qb, kv_ref[0, ks_, 0:128]), MASKED)
            _online_update(s, kv_ref[0, ks_, 128:256], m_scr, l_scr, acc_scr)
            return carry

        lax.fori_loop(0, nk, chunk, 0)

        start = jnp.clip(qi * tq + tq - span, 0, t - span)
        ws = pl.ds(pl.multiple_of(start, tq), span)
        diff = pos - (start + lax.broadcasted_iota(jnp.int32, (tq, span), 1))
        wmask = jnp.concatenate([(diff >= 0) & (diff < WINDOW)] * H_B, axis=0)
        s = jnp.where(wmask, _nt(qb, kv_ref[0, ws, 256:384]), MASKED)
        p = jnp.exp2(s - jnp.maximum(jnp.max(s, axis=-1, keepdims=True), NEG))
        o_win = _mm(p.astype(BF16), kv_ref[0, ws, 384:512]) \
            / jnp.maximum(jnp.sum(p, axis=-1, keepdims=True), 1e-30)
        o_sel = acc_scr[...] / jnp.maximum(l_scr[...], 1e-30)

        gate = _sigmoid(bg_ref[0])
        for h in range(H_B):
            rows = slice(h * tq, (h + 1) * tq)
            ob = (gate[:, 3 * h:3 * h + 1] * o_cmp[:, h * LANE:(h + 1) * LANE]
                  + gate[:, 3 * h + 1:3 * h + 2] * o_sel[rows]
                  + gate[:, 3 * h + 2:3 * h + 3] * o_win[rows])
            y = ob * _silu(bz_ref[0, :, h * LANE:(h + 1) * LANE])
            o_ref[0, :, h * LANE:(h + 1) * LANE] = y.astype(BF16)

    full = lambda rows, w: pl.BlockSpec((1, rows, w), lambda b, i: (b, 0, 0))
    return pl.pallas_call(
        body, grid=(b_, t // tq),
        in_specs=[pl.BlockSpec((1, tq, 512), lambda b, i: (b, i, _UNIT["bq"] // 4)),
                  full(nblk, LANE), full(nblk, LANE), full(t, 512),
                  pl.BlockSpec((1, tq, LANE), lambda b, i: (b, i, _UNIT["bg"])),
                  pl.BlockSpec((1, tq, 512), lambda b, i: (b, i, _UNIT["bz"] // 4)),
                  pl.BlockSpec((4, LANE), lambda b, i: (0, 0))],
        out_specs=pl.BlockSpec((1, tq, 512), lambda b, i: (b, i, 0)),
        out_shape=jax.ShapeDtypeStruct((b_, t, 512), BF16),
        scratch_shapes=[pltpu.VMEM((H_B * tq, LANE), F32)] * 3,
        compiler_params=_params("parallel", "arbitrary"),
        name="b_attn_prompt")(proj3, kcb, vcb, kvb, proj3, proj3, g4)


def _b_select_sample(proj3, pooled, win, neww, g4, *, past_len, n_valid):
    db = neww.shape[0]
    tp = neww.shape[1] // 2
    nblk = pooled.shape[1]
    n_top = min(N_SEL, nblk + 1)
    wb = win.shape[1] // 2
    rows = H_B * tp
    scale = HEAD ** -0.5 * SCORE_LOG2E

    def body(q_ref, pool_ref, win_ref, nw_ref, bg_ref, g_ref, qs_ref, ocw_ref, g1_ref, sel_ref):
        qf = jnp.concatenate(
            [_rms(q_ref[0, :, h * LANE:(h + 1) * LANE], g_ref[0:1, :]) * scale for h in range(H_B)],
            axis=0)
        qb = qf.astype(BF16)
        qs_ref[0] = qb
        tok = _mod(lax.broadcasted_iota(jnp.int32, (rows, 1), 0), tp)
        pos = past_len + tok
        cur = _div(pos, CMP_BLOCK)
        blk = lax.broadcasted_iota(jnp.int32, (rows, nblk), 1)
        cmask = blk < cur

        kc = _rms(pool_ref[0, :, 0:128], g_ref[1:2, :])
        s = jnp.where(cmask, _cmp_scores(qf, kc), NEG)
        mx = jnp.max(s, axis=-1, keepdims=True)
        p = jnp.where(cmask, jnp.exp2(s - mx), 0.0)
        pc = p / jnp.maximum(jnp.sum(p, axis=-1, keepdims=True), 1e-30)
        o_cmp = _mm(pc.astype(BF16), pool_ref[0, :, 128:256].astype(BF16))
        imp = pc[0:tp]
        for h in range(1, H_B):
            imp = imp + pc[h * tp:(h + 1) * tp]
        imp = jnp.where(cmask[0:tp], imp, -1.0)

        pad = jnp.concatenate([imp, jnp.zeros((LANE - tp, nblk), F32)], axis=0)
        imp_t = jnp.concatenate([pad[:, u * LANE:(u + 1) * LANE].T for u in range(nblk // LANE)], axis=0)
        mi = lax.broadcasted_iota(jnp.int32, (nblk, nblk), 0)
        ni = lax.broadcasted_iota(jnp.int32, (nblk, nblk), 1)
        sels = []
        for tkn in range(tp):
            r = imp[tkn:tkn + 1, :]
            c = imp_t[:, tkn:tkn + 1]
            ahead = (c > r) | ((c == r) & (mi < ni))
            rank = jnp.sum(ahead.astype(F32), axis=0, keepdims=True)
            sels.append(((rank < n_top) & (r >= 0.0)).astype(F32))
        sel_ref[0] = jnp.concatenate(sels, axis=0)

        kw = win_ref[0, pl.ds(0, wb, stride=2), :].astype(BF16)
        vw = win_ref[0, pl.ds(1, wb, stride=2), :].astype(BF16)
        zpad = jnp.zeros((LANE - tp, LANE), F32)
        kn = jnp.concatenate([nw_ref[0, pl.ds(0, tp, stride=2), :], zpad], axis=0).astype(BF16)
        vn = jnp.concatenate([nw_ref[0, pl.ds(1, tp, stride=2), :], zpad], axis=0).astype(BF16)
        jw =lax.broadcasted_iota(jnp.int32, (rows, wb), 1)
        dw = pos - (past_len - wb + jw)
        jn = lax.broadcasted_iota(jnp.int32, (rows, LANE), 1)
        dn = tok - jn
        wmask = jnp.concatenate([(dw >= 0) & (dw < WINDOW), (dn >= 0) & (dn < WINDOW) & (jn < n_valid)],
                                axis=1)
        sw = jnp.where(wmask, jnp.concatenate([_nt(qb, kw), _nt(qb, kn)], axis=1), NEG)
        mw = jnp.max(sw, axis=-1, keepdims=True)
        pw = jnp.where(wmask, jnp.exp2(sw - mw), 0.0)
        o_win = (_mm(pw[:, 0:wb].astype(BF16), vw) + _mm(pw[:, wb:].astype(BF16), vn)) \
            / jnp.maximum(jnp.sum(pw, axis=-1, keepdims=True), 1e-30)

        gate = _sigmoid(bg_ref[0])
        g0 = jnp.concatenate([gate[:, 3 * h:3 * h + 1] for h in range(H_B)], axis=0)
        g1 = jnp.concatenate([gate[:, 3 * h + 1:3 * h + 2] for h in range(H_B)], axis=0)
        g2 = jnp.concatenate([gate[:, 3 * h + 2:3 * h + 3] for h in range(H_B)], axis=0)
        ocw_ref[0] = g0 * o_cmp + g2 * o_win
        g1_ref[0] = jnp.broadcast_to(g1, (rows, LANE))

    per_b = lambda r, w: pl.BlockSpec((1, r, w), lambda b: (b, 0, 0))
    return pl.pallas_call(
        body, grid=(db,),
        in_specs=[pl.BlockSpec((1, tp, 512), lambda b: (b, 0, _UNIT["bq"] // 4)),
                  per_b(nblk, 256), per_b(2 * wb, LANE), per_b(2 * tp, LANE),
                  pl.BlockSpec((1, tp, LANE), lambda b: (b, 0, _UNIT["bg"])),
                  pl.BlockSpec((4, LANE), lambda b: (0, 0))],
        out_specs=[per_b(rows, LANE), per_b(rows, LANE), per_b(rows, LANE), per_b(tp, nblk)],
        out_shape=[jax.ShapeDtypeStruct((db, rows, LANE), BF16),
                   jax.ShapeDtypeStruct((db, rows, LANE), F32),
                   jax.ShapeDtypeStruct((db, rows, LANE), F32),
                   jax.ShapeDtypeStruct((db, tp, nblk), F32)],
        compiler_params=_params("parallel"),
        name="b_select_sample")(proj3, pooled, win, neww, proj3, g4)


def _b_selected_sample(page_table, cache2, qs, selg, bkvs, ocw, g1b, proj3, page_base,
                       *, n_valid, pg):
    db, n_pages = page_table.shape
    ng = n_pages // pg
    rows = qs.shape[1]
    tp = rows // H_B
    per = PAGE // CMP_BLOCK
    nbg = pg * per
    width = pg * PAGE

    def body(pt_ref, *refs):
        pages = refs[:pg]
        q_ref, sel_ref, new_ref, ocw_ref, g1_ref, z_ref, o_ref, m_scr, l_scr, acc_scr = refs[pg:]
        gi = pl.program_id(1)

        @pl.when(gi == 0)
        def _():
            m_scr[...] = jnp.full(m_scr.shape, NEG, F32)
            l_scr[...] = jnp.zeros(l_scr.shape, F32)
            acc_scr[...] = jnp.zeros(acc_scr.shape, F32)

        q = q_ref[0]

        def slot(s_):
            return jnp.concatenate([pages[j][pl.ds(s_, PAGE, stride=4), :].astype(BF16)
                                    for j in range(pg)], axis=0)

        e = (_div(lax.broadcasted_iota(jnp.int32, (nbg, width), 1), CMP_BLOCK)
             == lax.broadcasted_iota(jnp.int32, (nbg, width), 0)).astype(BF16)
        mk = _mm(sel_ref[0, 0].astype(BF16), e) > 0.5
        mask = jnp.concatenate([mk] * H_B, axis=0)
        _online_update(jnp.where(mask, _nt(q, slot(2)), MASKED), slot(3), m_scr, l_scr, acc_scr)

        @pl.when(gi == ng - 1)
        def _():
            zpad = jnp.zeros((PAGE - tp, LANE), F32)
            kn = jnp.concatenate([new_ref[0, pl.ds(2, tp, stride=4), :], zpad], axis=0).astype(BF16)
            vn = jnp.concatenate([new_ref[0, pl.ds(3, tp, stride=4), :], zpad], axis=0).astype(BF16)
            tok = _mod(lax.broadcasted_iota(jnp.int32, (rows, PAGE), 0), tp)
            col = lax.broadcasted_iota(jnp.int32, (rows, PAGE), 1)
            nmask = (col <= tok) & (col < n_valid)
            _online_update(jnp.where(nmask, _nt(q, kn), MASKED), vn, m_scr, l_scr, acc_scr)
            o_sel = acc_scr[...] / jnp.maximum(l_scr[...], 1e-30)
            ob = ocw_ref[0] + g1_ref[0] * o_sel
            for h in range(H_B):
                y = ob[h * tp:(h + 1) * tp] * _silu(z_ref[0, :, h * LANE:(h + 1) * LANE])
                o_ref[0, :, h * LANE:(h + 1) * LANE] = y.astype(BF16)

    def page_spec(j):
        return pl.BlockSpec((PAGE * 4, LANE), lambda b, g, pt, j=j: (page_base + pt[b, g * pg + j], 0))

    per_b = lambda r, w: pl.BlockSpec((1, r, w), lambda b, g, pt: (b, 0, 0))
    grid_spec = pltpu.PrefetchScalarGridSpec(
        num_scalar_prefetch=1, grid=(db, ng),
        in_specs=[page_spec(j) for j in range(pg)] + [
            per_b(rows, LANE),
            pl.BlockSpec((1, 1, tp, nbg), lambda b, g, pt: (b, g, 0, 0)),
            per_b(4 * tp, LANE), per_b(rows, LANE), per_b(rows, LANE),
            pl.BlockSpec((1, tp, 512), lambda b, g, pt: (b, 0, _UNIT["bz"] // 4))],
        out_specs=per_b(tp, 512),
        scratch_shapes=[pltpu.VMEM((rows, LANE), F32)] * 3)
    return pl.pallas_call(
        body, grid_spec=grid_spec,
        out_shape=jax.ShapeDtypeStruct((db, tp, 512), BF16),
        compiler_params=_params("parallel", "arbitrary"),
        name="b_selected_sample")(page_table, *([cache2] * pg), qs, selg, bkvs, ocw, g1b, proj3)


def _mlstm(proj3, conv_w, conv_b, gate_b, out_g, conv0, c0, n0, m0, *, lc, n_valid):
    b_, t, _ = proj3.shape
    col_head = jnp.arange(H_C * LANE)[None, :] // LANE
    lane_id = jnp.arange(LANE)[:, None]
    rsel = jnp.stack([lane_id == col_head, lane_id == col_head + H_C]).astype(BF16)
    nchunk = t // lc
    dqk = H_C * DQK_C
    tail = CONV_W - 1
    base = 8
    tsq = max(lc, LANE)

    def bcast_cols(x, sel):
        hi, lo = _split2(x)
        return _mm(hi, sel) + _mm(lo, sel)

    def body(qk_ref, v_ref, if_ref, co_ref, cz_ref, cw_ref, cb_ref, gb_ref, g_ref, rsel_ref,
             conv0_ref, c0_ref, n0_ref, m0_ref,
             y_ref, conv_ref, c_ref, n_ref, m_ref, xbuf, c_scr, n_scr, m_scr):
        ci = pl.program_id(1)

        @pl.when(ci == 0)
        def _():
            xbuf[base - tail:base, :] = conv0_ref[0]
            c_scr[...] = c0_ref[0]
            n_scr[...] = n0_ref[0]
            m_scr[...] = m0_ref[0]

        xbuf[base:base + lc, :] = qk_ref[0]
        acc = cb_ref[...]
        for j in range(CONV_W):
            acc = acc + xbuf[base - tail + j:base - tail + j + lc, :] * cw_ref[j:j + 1, :]
        new_tail = xbuf[base + n_valid - tail:base + n_valid, :]
        xbuf[base - tail:base, :] = new_tail
        conv_ref[0] = new_tail
        qk = _silu(acc)

        gt = if_ref[0] + gb_ref[...]
        lf = jnp.minimum(gt, 0.0) - jnp.log(1.0 + jnp.exp(-jnp.abs(gt)))
        ti = lax.broadcasted_iota(jnp.int32, (lc, lc), 0)
        si = lax.broadcasted_iota(jnp.int32, (lc, lc), 1)
        tri = (si <= ti).astype(BF16)
        l1, l2, l3 = _split3(lf)
        bcum = _mm(tri, l1) + _mm(tri, l2) + _mm(tri, l3)
        zrow = jnp.zeros((tsq - lc, LANE), F32)
        gt_t = (jnp.concatenate([gt, zrow], axis=0) if tsq > lc else gt).T
        b_t = (jnp.concatenate([bcum, zrow], axis=0) if tsq > lc else bcum).T
        dmask = (si <= ti) & (si < n_valid)
        svalid = lax.broadcasted_iota(jnp.int32, (lc, LANE), 0) < n_valid
        icol_all = bcast_cols(gt, rsel_ref[0])
        bcol_all = bcast_cols(bcum, rsel_ref[1])
        gate_all = _sigmoid(co_ref[0]) * _silu(cz_ref[0])
        low = lax.broadcasted_iota(jnp.int32, (lc, LANE), 1) < DQK_C
        row_low = lax.broadcasted_iota(jnp.int32, (LANE, LANE), 0) < DQK_C

        c_new, n_new, m_new_all = [], [], []
        for j in range(H_C // 2):
            qp = qk[:, j * LANE:(j + 1) * LANE] * (DQK_C ** -0.5)
            kp = qk[:, dqk + j * LANE:dqk + (j + 1) * LANE]
            cp = c_scr[j]
            cpb = cp.astype(BF16)
            npair = n_scr[j:j + 1, :]
            c_upd, n_upd, carries = None, None, []
            for u in range(2):
                h = 2 * j + u
                sel = low if u == 0 else jnp.logical_not(low)
                qm = jnp.where(sel, qp, 0.0)
                km = jnp.where(sel, kp, 0.0)
                qmb, kmb = qm.astype(BF16), km.astype(BF16)
                vh = v_ref[0, :, h * HEAD:(h + 1) * HEAD]
                bcol = bcol_all[:, h * LANE:(h + 1) * LANE]
                icol = icol_all[:, h * LANE:(h + 1) * LANE]
                brow = b_t[H_C + h:H_C + h + 1, 0:lc]
                irow = gt_t[h:h + 1, 0:lc]
                m_h = m_scr[h:h + 1, :]

                d = jnp.where(dmask, bcol[:, 0:lc] - brow + irow, NEG)
                inter = bcol + m_h
                m_t = jnp.maximum(inter, jnp.max(d, axis=1, keepdims=True))
                w_intra = jnp.exp(d - m_t[:, 0:lc])
                w_inter = jnp.exp(inter - m_t)
                sqk = _nt(qmb, kmb) * w_intra
                num = w_inter * _mm(qmb, cpb) + _mm(sqk.astype(BF16), vh.astype(BF16))
                den = (w_inter * jnp.sum(qm * npair, axis=1, keepdims=True)
                       + jnp.sum(sqk, axis=1, keepdims=True))
                hh = num / jnp.maximum(jnp.abs(den), jnp.exp(-m_t))
                y_ref[0, :, h * HEAD:(h + 1) * HEAD] = (
                    _rms(hh, g_ref[...]) * gate_all[:, h * HEAD:(h + 1) * HEAD]).astype(BF16)

                b_last = bcol[n_valid - 1:n_valid, :]
                dec = jnp.where(svalid, b_last - bcol + icol, NEG)
                m_new = jnp.maximum(b_last + m_h, jnp.max(dec, axis=0, keepdims=True))
                wk = jnp.exp(dec - m_new)
                carries.append(jnp.exp(b_last + m_h - m_new))
                cu = _tn(kmb, (wk * vh).astype(BF16))
                nu = jnp.sum(wk * km, axis=0, keepdims=True)
                c_upd = cu if c_upd is None else c_upd + cu
                n_upd = nu if n_upd is None else n_upd + nu
                m_new_all.append(m_new)
            c_new.append(jnp.where(row_low, carries[0], carries[1]) * cp + c_upd)
            n_new.append(jnp.where(low[0:1], carries[0], carries[1]) * npair + n_upd)

        for j in range(H_C // 2):
            c_scr[j] = c_new[j]
            n_scr[j:j + 1, :] = n_new[j]
        for h in range(H_C):
            m_scr[h:h + 1, :] = m_new_all[h]
        c_ref[0] = c_scr[...]
        n_ref[0] = n_scr[...]
        m_ref[0] = m_scr[...]

    col = lambda u: pl.BlockSpec((1, lc, 1024), lambda b, c, u=u: (b, c, u))
    const = lambda *shape: pl.BlockSpec(shape, lambda b, c: (0,) * len(shape))
    per_b = lambda *shape: pl.BlockSpec((1,) + shape, lambda b, c: (b,) + (0,) * len(shape))
    return pl.pallas_call(
        body, grid=(b_, nchunk),
        in_specs=[col(_UNIT["cqk"] // 8), col(_UNIT["cv"] // 8),
                  pl.BlockSpec((1, lc, LANE), lambda b, c: (b, c, _UNIT["cif"])),
                  col(_UNIT["co"] // 8), col(_UNIT["cz"] // 8),
                  const(CONV_W, 1024), const(1, 1024), const(1, LANE), const(1, LANE),
                  const(2, LANE, H_C * LANE),
                  per_b(tail, 1024), per_b(H_C // 2, LANE, HEAD), per_b(H_C // 2, LANE), per_b(H_C, LANE)],
        out_specs=[pl.BlockSpec((1, lc, 1024), lambda b, c: (b, c, 0)),
                   per_b(tail, 1024), per_b(H_C // 2, LANE, HEAD), per_b(H_C // 2, LANE), per_b(H_C, LANE)],
        out_shape=[jax.ShapeDtypeStruct((b_, t, 1024), BF16),
                   jax.ShapeDtypeStruct((b_, tail, 1024), F32),
                   jax.ShapeDtypeStruct((b_, H_C // 2, LANE, HEAD), F32),
                   jax.ShapeDtypeStruct((b_, H_C // 2, LANE), F32),
                   jax.ShapeDtypeStruct((b_, H_C, LANE), F32)],
        scratch_shapes=[pltpu.VMEM((base + lc, 1024), F32), pltpu.VMEM((H_C // 2, LANE, HEAD), F32),
                        pltpu.VMEM((H_C // 2, LANE), F32), pltpu.VMEM((H_C, LANE), F32)],
        compiler_params=_params("parallel", "arbitrary"),
        name="mlstm")(proj3, proj3, proj3, proj3, proj3, conv_w, conv_b, gate_b, out_g, rsel,
                      conv0, c0, n0, m0)


def _w_prep(w_in):
    depth, d, n_in = w_in.shape
    kt_n = d // LANE
    per_col = kt_n * depth
    rows = w_in.reshape(depth, kt_n, LANE, n_in).transpose(3, 1, 0, 2).reshape(n_in * per_col, LANE)
    src_off, off = {}, 0
    for name, width in _SRC:
        src_off[name] = (off, width)
        off += width
    assert off == n_in
    starts, valids = [0] * (N_PROJ // LANE), [0] * (N_PROJ // LANE)
    for name in _DST_ORDER:
        s0, width = src_off[name]
        for u in range(-(-width // LANE)):
            starts[_UNIT[name] + u] = s0 + u * LANE
            valids[_UNIT[name] + u] = min(LANE, width - u * LANE)
    table = jnp.array([starts, valids], jnp.int32)

    def body(tbl_ref, w_ref, o_ref):
        u = pl.program_id(0)
        keep = lax.broadcasted_iota(jnp.int32, (LANE, LANE), 0) < tbl_ref[1, u]
        xt = jnp.swapaxes(w_ref[...].reshape(LANE, per_col, LANE), 0, 1)
        for l in range(depth):
            for kt in range(kt_n):
                o_ref[l, :, kt * LANE:(kt + 1) * LANE] = jnp.where(keep, xt[kt * depth + l], 0.0).astype(BF16)

    grid_spec = pltpu.PrefetchScalarGridSpec(
        num_scalar_prefetch=1, grid=(N_PROJ // LANE,),
        in_specs=[pl.BlockSpec((pl.Element(LANE * per_col), pl.Element(LANE)),
                               lambda u, tbl: (tbl[0, u] * per_col, 0))],
        out_specs=pl.BlockSpec((depth, LANE, d), lambda u, tbl: (0, u, 0)))
    return pl.pallas_call(
        body, grid_spec=grid_spec,
        out_shape=jax.ShapeDtypeStruct((depth, N_PROJ, d), BF16),
        compiler_params=_params("parallel"),
        name="w_prep")(table, rows)


def _c_from_pairs(c):
    b = c.shape[0]
    return jnp.swapaxes(c.reshape(b, H_C, DQK_C, HEAD), -1, -2)


def _pick(n, prefs):
    for p in prefs:
        if n % p == 0:
            return p
    return n


def kernel(x_prompt, x_sample, cache_a_kv, cache_b_kv, state_b_win, state_c_conv, state_c_C,
           state_c_n, state_c_m, page_table, norm_g, w_in, w_out, a_qk_g, a_lambda, a_out_g,
           b_qk_g, b_cmp_w, c_conv_w, c_conv_b, c_gate_b, c_out_g):
    bp, t, d = x_prompt.shape
    db, ts, _ = x_sample.shape
    depth = norm_g.shape[0]
    n_pool = cache_a_kv.shape[1]
    n_pages = page_table.shape[1]
    past_len = n_pages * PAGE
    wb = state_b_win.shape[2]
    tp = 8
    assert ts <= tp and t % MLSTM_CHUNK == 0 and t % Q_BLOCK == 0 and d == 2048
    mp, ms = bp * t, db * tp
    pg = _pick(n_pages, (16, 8, 4, 2, 1))

    cache_a2 = cache_a_kv.reshape(depth * n_pool * PAGE * 2 * H_A, HEAD)
    cache_b2 = cache_b_kv.reshape(depth * n_pool * PAGE * 4, HEAD)
    win_rows = state_b_win.reshape(depth, db, wb * 2, HEAD)
    hp = x_prompt.reshape(mp, d)
    hs = jnp.pad(x_sample, ((0, 0), (0, tp - ts), (0, 0))).reshape(ms, d)

    tm_p = _pick(mp, (1024, 512, 256, 128))
    tq_a = _pick(t, (512, 256, 128))
    outs_p = [[] for _ in range(7)]
    outs_s = [[] for _ in range(7)]

    w_all = _w_prep(w_in)
    for l in range(depth):
        lam_init = 0.8 - 0.6 * math.exp(-0.3 * l)
        w_o = w_out[l].astype(BF16)
        g_a = jnp.tile(a_qk_g[l], (1, 2))
        g_b = b_qk_g[l]
        wpb = jnp.broadcast_to(b_cmp_w[l][:, :, None], (2, CMP_BLOCK, LANE))
        wpb2 = jnp.concatenate([wpb[0], wpb[1]], axis=1)
        gate_b = jnp.pad(c_gate_b[l].reshape(1, 2 * H_C), ((0, 0), (0, LANE - 2 * H_C)))
        conv_b = c_conv_b[l].reshape(1, -1)
        out_gc = c_out_g[l].reshape(1, LANE)

        proj = _norm_matmul(hp, norm_g[l], w_all, l, tm=tm_p, tn=1024)
        proj3 = proj.reshape(bp, t, N_PROJ)
        qn, kn, vb, akv = _a_prep(proj, g_a, tm=_pick(mp, (512, 256, 128)))
        ya = _a_attn_prompt(qn.reshape(bp, t, 512), kn.reshape(bp, t, 512), vb.reshape(bp, t, 512),
                            proj3, a_lambda[l], a_out_g[l], lam_init, tq=tq_a)
        bkvs, neww, kvb, kcb, vcb = _b_prep(proj, g_b, wpb, tm=_pick(mp, (512, 256, 128)), pool=True)
        nblk = t // CMP_BLOCK
        yb = _b_attn_prompt(proj3, kcb.reshape(bp, nblk, LANE), vcb.reshape(bp, nblk, LANE),
                            kvb.reshape(bp, t, 512), g_b, t=t)
        yc, conv_p, c_p, n_p, m_p = _mlstm(
            proj3, c_conv_w[l], conv_b, gate_b, out_gc,
            jnp.zeros((bp, CONV_W - 1, 2 * H_C * DQK_C), F32), jnp.zeros((bp, H_C // 2, LANE, HEAD), F32),
            jnp.zeros((bp, H_C // 2, LANE), F32), jnp.zeros((bp, H_C, LANE), F32),
            lc=MLSTM_CHUNK, n_valid=MLSTM_CHUNK)
        c_p, n_p = _c_from_pairs(c_p), n_p.reshape(bp, H_C, DQK_C)
        hp = _out_proj(hp, ya.reshape(mp, 512), yb.reshape(mp, 512), yc.reshape(mp, 1024), w_o,
                       tm=tm_p, tn=1024)
        win_p = jnp.concatenate([jnp.zeros((bp, wb, 2, HEAD), F32), neww.reshape(bp, t, 2, HEAD)],
                                axis=1)[:, -wb:]
        for lst, a in zip(outs_p, (akv.reshape(bp, t, 2, H_A, HEAD), bkvs.reshape(bp, t, 4, HEAD),
                                   win_p, conv_p, c_p, n_p, m_p[:, :, 0])):
            lst.append(a)

        base = l * n_pool
        sproj = _norm_matmul(hs, norm_g[l], w_all, l, tm=ms, tn=1024)
        sproj3 = sproj.reshape(db, tp, N_PROJ)
        sqn, skn, svb, sakv = _a_prep(sproj, g_a, tm=ms)
        kvnew = jnp.concatenate([skn.reshape(db, tp, 512), svb.reshape(db, tp, 512)], axis=2)
        kvnew = jnp.pad(kvnew, ((0, 0), (0, PAGE - tp), (0, 0)))
        sya, pooled = _a_attn_sample(page_table, cache_a2, sqn.reshape(db, tp, 512), kvnew, sproj3,
                                     a_lambda[l], a_out_g[l], lam_init, base, cache_b2, wpb2,
                                     n_valid=ts, pg=pg)
        sbkvs, sneww, _ = _b_prep(sproj, g_b, wpb, tm=ms, pool=False)
        qs, ocw, g1b, sel = _b_select_sample(
            sproj3, pooled, win_rows[l], sneww.reshape(db, tp * 2, HEAD), g_b,
            past_len=past_len, n_valid=ts)
        nbg = pg * (PAGE // CMP_BLOCK)
        selg = sel.reshape(db, tp, n_pages // pg, nbg).transpose(0, 2, 1, 3)
        syb = _b_selected_sample(page_table, cache_b2, qs, selg, sbkvs.reshape(db, tp * 4, HEAD), ocw, g1b,
                                 sproj3, base, n_valid=ts, pg=pg)
        m0 = jnp.broadcast_to(state_c_m[l][:, :, None], (db, H_C, LANE))
        syc, conv_s, c_s, n_s, m_s = _mlstm(
            sproj3, c_conv_w[l], conv_b, gate_b, out_gc, state_c_conv[l],
            jnp.swapaxes(state_c_C[l], -1, -2).reshape(db, H_C // 2, LANE, HEAD),
            state_c_n[l].reshape(db, H_C // 2, LANE), m0, lc=tp, n_valid=ts)
        c_s, n_s = _c_from_pairs(c_s), n_s.reshape(db, H_C, DQK_C)
        hs = _out_proj(hs, sya.reshape(ms, 512), syb.reshape(ms, 512), syc.reshape(ms, 1024), w_o,
                       tm=ms, tn=1024)
        win_s = jnp.concatenate([state_b_win[l], sneww.reshape(db, tp, 2, HEAD)[:, :ts]], axis=1)[:, -wb:]
        for lst, a in zip(outs_s, (sakv.reshape(db, tp, 2, H_A, HEAD)[:, :ts],
                                   sbkvs.reshape(db, tp, 4, HEAD)[:, :ts],
                                   win_s, conv_s, c_s, n_s, m_s[:, :, 0])):
            lst.append(a)

    y_p = hp.reshape(bp, t, d)
    y_s = hs.reshape(db, tp, d)[:, :ts]
    sp = [jnp.stack(x) for x in outs_p]
    ss = [jnp.stack(x) for x in outs_s]
    return (y_p, y_s, sp[0], ss[0], sp[1], ss[1], sp[2], ss[2], sp[3], ss[3],
            sp[4], ss[4], sp[5], ss[5], sp[6], ss[6])
```

```python
import math

import jax
import jax.numpy as jnp
from jax import lax
from jax.experimental import pallas as pl
from jax.experimental.pallas import tpu as pltpu

F32 = jnp.float32
BF16 = jnp.bfloat16

EPS = 1e-6
LANE = 128
HEAD = 128
H_A = 4
DQ_A = HEAD // 2
H_B = 4
H_C = 8
DQK_C = HEAD // 2
CMP_BLOCK = 64
N_SEL = 16
WINDOW = 512
CONV_W = 4
MLSTM_CHUNK = 128
PAGE = 128
Q_BLOCK = 128
NEG = -1e30
MASKED = 2 * NEG
SCORE_LOG2E = math.log2(math.e)
VMEM_LIMIT = 56 * 1024 * 1024

_SRC = (("aq", 512), ("ak", 512), ("av", 512), ("az", 512),
        ("bq", 512), ("bkv", 768), ("bg", 12), ("bz", 512),
        ("cqk", 1024), ("cv", 1024), ("cif", 16), ("co", 1024), ("cz", 1024))
_DST_ORDER = ("aq", "ak", "av", "az", "bq", "bz", "bkv", "bg", "cif", "cqk", "cv", "co", "cz")
_UNIT = {"aq": 0, "ak": 4, "av": 8, "az": 12, "bq": 16, "bz": 20, "bkv": 24, "bg": 30,
         "cif": 31, "cqk": 32, "cv": 40, "co": 48, "cz": 56}
N_PROJ = 64 * LANE


def _nt(a, b):
    return lax.dot_general(a, b, (((1,), (1,)), ((), ())), preferred_element_type=F32)


def _tn(a, b):
    return lax.dot_general(a, b, (((0,), (0,)), ((), ())), preferred_element_type=F32)


def _mm(a, b):
    return jnp.dot(a, b, preferred_element_type=F32)


def _split2(x):
    hi = x.astype(BF16)
    lo = (x - hi.astype(F32)).astype(BF16)
    return hi, lo


def _split3(x):
    hi = x.astype(BF16)
    r = x - hi.astype(F32)
    mid = r.astype(BF16)
    lo = (r - mid.astype(F32)).astype(BF16)
    return hi, mid, lo


def _sigmoid(z):
    return 1.0 / (1.0 + jnp.exp(-z))


def _silu(z):
    return z * _sigmoid(z)


def _rms(x, g):
    return x * lax.rsqrt(jnp.mean(x * x, axis=-1, keepdims=True) + EPS) * g


def _div(x, n):
    return lax.shift_right_logical(x, int(math.log2(n)))


def _mod(x, n):
    return x & (n - 1)


def _online_update(s, v, m_ref, l_ref, acc_ref, pv_fn=None):
    w = s.shape[1]
    m_old = m_ref[...]
    m_new = jnp.maximum(m_old, jnp.max(s, axis=-1, keepdims=True))
    alpha = jnp.exp2(m_old - m_new)
    m_rep = m_new if w == LANE else jnp.concatenate([m_new] * (w // LANE), axis=1)
    p = jnp.exp2(s - m_rep)
    l_ref[...] = alpha * l_ref[...] + jnp.sum(p, axis=-1, keepdims=True)
    pb = p.astype(BF16)
    acc_ref[...] = alpha * acc_ref[...] + (_mm(pb, v) if pv_fn is None else pv_fn(pb))
    m_ref[...] = m_new


def _params(*sem):
    return pltpu.CompilerParams(dimension_semantics=sem, vmem_limit_bytes=VMEM_LIMIT)


def _norm_matmul(x, g, w, layer, *, tm, tn):
    m, d = x.shape
    n = w.shape[1]
    rc = min(tm, 256)

    def body(x_ref, g_ref, w_ref, o_ref, h_scr):
        @pl.when(pl.program_id(1) == 0)
        def _():
            def chunk(c, carry):
                r = pl.ds(pl.multiple_of(c * rc, rc), rc)
                h_scr[r, :] = _rms(x_ref[r, :], g_ref[...]).astype(BF16)
                return carry
            lax.fori_loop(0, tm // rc, chunk, 0)
        o_ref[...] = _nt(h_scr[...], w_ref[0])

    return pl.pallas_call(
        body, grid=(m // tm, n // tn),
        in_specs=[pl.BlockSpec((tm, d), lambda i, j: (i, 0)),
                  pl.BlockSpec((1, d), lambda i, j: (0, 0)),
                  pl.BlockSpec((1, tn, d), lambda i, j: (layer, j, 0))],
        out_specs=pl.BlockSpec((tm, tn), lambda i, j: (i, j)),
        out_shape=jax.ShapeDtypeStruct((m, n), F32),
        scratch_shapes=[pltpu.VMEM((tm, d), BF16)],
        compiler_params=_params("parallel", "arbitrary"),
        name="norm_matmul")(x, g.reshape(1, d), w)


def _out_proj(x, ya, yb, yc, w, *, tm, tn):
    m, d = x.shape
    da, db, dc = ya.shape[1], yb.shape[1], yc.shape[1]

    def body(x_ref, a_ref, b_ref, c_ref, w_ref, o_ref):
        mix = jnp.concatenate([a_ref[...], b_ref[...], c_ref[...]], axis=1)
        o_ref[...] = x_ref[...] + _mm(mix, w_ref[...])

    return pl.pallas_call(
        body, grid=(m // tm, d // tn),
        in_specs=[pl.BlockSpec((tm, tn), lambda i, j: (i, j)),
                  pl.BlockSpec((tm, da), lambda i, j: (i, 0)),
                  pl.BlockSpec((tm, db), lambda i, j: (i, 0)),
                  pl.BlockSpec((tm, dc), lambda i, j: (i, 0)),
                  pl.BlockSpec((da + db + dc, tn), lambda i, j: (0, j))],
        out_specs=pl.BlockSpec((tm, tn), lambda i, j: (i, j)),
        out_shape=jax.ShapeDtypeStruct((m, d), F32),
        compiler_params=_params("parallel", "parallel"),
        name="out_proj")(x, ya, yb, yc, w)


def _a_prep(proj, g2, *, tm):
    m = proj.shape[0]

    def body(q_ref, k_ref, v_ref, g_ref, qn_ref, kn_ref, vb_ref, akv_ref):
        r = _div(lax.broadcasted_iota(jnp.int32, (LANE, LANE), 0), DQ_A)
        c = _div(lax.broadcasted_iota(jnp.int32, (LANE, LANE), 1), DQ_A)
        seg = (r == c).astype(BF16)

        def segnorm(x, g):
            outs = []
            for u in range(4):
                xc = x[:, u * LANE:(u + 1) * LANE]
                hi, lo = _split2(xc * xc)
                s = _mm(hi, seg) + _mm(lo, seg)
                outs.append(xc * lax.rsqrt(s * (1.0 / DQ_A) + EPS) * g)
            return jnp.concatenate(outs, axis=1)

        qn = segnorm(q_ref[...], g_ref[0:1, :]) * (DQ_A ** -0.5 * SCORE_LOG2E)
        kn = segnorm(k_ref[...], g_ref[1:2, :])
        v = v_ref[...]
        qn_ref[...] = qn.astype(BF16)
        kn_ref[...] = kn.astype(BF16)
        vb_ref[...] = v.astype(BF16)
        for h in range(H_A):
            akv_ref[pl.ds(h, tm, stride=2 * H_A), :] = kn[:, h * LANE:(h + 1) * LANE]
            akv_ref[pl.ds(H_A + h, tm, stride=2 * H_A), :] = v[:, h * LANE:(h + 1) * LANE]

    blk = lambda u: pl.BlockSpec((tm, 512), lambda i, u=u: (i, u))
    row = lambda w: pl.BlockSpec((tm, w), lambda i: (i, 0))
    return pl.pallas_call(
        body, grid=(m // tm,),
        in_specs=[blk(_UNIT["aq"] // 4), blk(_UNIT["ak"] // 4), blk(_UNIT["av"] // 4),
                  pl.BlockSpec((2, LANE), lambda i: (0, 0))],
        out_specs=[row(512), row(512), row(512), pl.BlockSpec((tm * 2 * H_A, LANE), lambda i: (i, 0))],
        out_shape=[jax.ShapeDtypeStruct((m, 512), BF16)] * 3
        + [jax.ShapeDtypeStruct((m * 2 * H_A, LANE), F32)],
        compiler_params=_params("parallel"),
        name="a_prep")(proj, proj, proj, g2)


def _diff_lambda(lp, lam_init):
    a = jnp.sum(lp[0:1, :] * lp[1:2, :], axis=-1, keepdims=True)
    b = jnp.sum(lp[2:3, :] * lp[3:4, :], axis=-1, keepdims=True)
    return jnp.exp(a) - jnp.exp(b) + lam_init


def _a_finish(acc1, l1, acc2, l2, lam, g, z, lam_init):
    o = acc1 / l1 - lam * (acc2 / l2)
    return _rms(o, g) * (1.0 - lam_init) * _silu(z)


def _a_attn_prompt(qn, kn, vb, proj3, lam_p, out_g, lam_init, *, tq):
    b_, t, _ = qn.shape
    tk = tq

    def body(q_ref, k_ref, v_ref, z_ref, lam_ref, g_ref, o_ref, m_scr, l_scr, acc_scr):
        qi = pl.program_id(2)
        q = q_ref[0]
        lane = lax.broadcasted_iota(jnp.int32, (tq, LANE), 1)
        zero = jnp.zeros_like(q)
        qs = (jnp.where(lane < DQ_A, q, zero), jnp.where(lane >= DQ_A, q, zero))
        m_scr[...] = jnp.full(m_scr.shape, NEG, F32)
        l_scr[...] = jnp.zeros(l_scr.shape, F32)
        acc_scr[...] = jnp.zeros(acc_scr.shape, F32)

        def chunk(kc, mask):
            ks = pl.ds(pl.multiple_of(kc * tk, tk), tk)
            k = k_ref[0, ks, :]
            v = v_ref[0, ks, :]
            for c in range(2):
                s = _nt(qs[c], k)
                if mask is not None:
                    s = jnp.where(mask, s, MASKED)
                _online_update(s, v, m_scr.at[c], l_scr.at[c], acc_scr.at[c])

        def full_chunk(kc, carry):
            chunk(kc, None)
            return carry

        lax.fori_loop(0, qi, full_chunk, 0)
        chunk(qi, lax.broadcasted_iota(jnp.int32, (tq, tk), 1)
              <= lax.broadcasted_iota(jnp.int32, (tq, tk), 0))
        lam = _diff_lambda(lam_ref[...], lam_init)
        y = _a_finish(acc_scr[0], l_scr[0], acc_scr[1], l_scr[1], lam, g_ref[...], z_ref[0], lam_init)
        o_ref[0] = y.astype(BF16)

    head = lambda rows, qdep: pl.BlockSpec(
        (1, rows, LANE), (lambda b, h, i: (b, i, h)) if qdep else (lambda b, h, i: (b, 0, h)))
    return pl.pallas_call(
        body, grid=(b_, H_A, t // tq),
        in_specs=[head(tq, True), head(t, False), head(t, False),
                  pl.BlockSpec((1, tq, LANE), lambda b, h, i: (b, i, _UNIT["az"] + h)),
                  pl.BlockSpec((4, DQ_A), lambda b, h, i: (0, 0)),
                  pl.BlockSpec((1, LANE), lambda b, h, i: (0, 0))],
        out_specs=head(tq, True),
        out_shape=jax.ShapeDtypeStruct((b_, t, 512), BF16),
        scratch_shapes=[pltpu.VMEM((2, tq, LANE), F32)] * 3,
        compiler_params=_params("parallel", "parallel", "arbitrary"),
        name="a_attn_prompt")(qn, kn, vb, proj3, lam_p, out_g.reshape(1, LANE))


def _a_attn_sample(page_table, cache2, qn, kvnew, proj3, lam_p, out_g, lam_init, page_base,
                   cache_b2, wpb2, *, n_valid, pg):
    db, n_pages = page_table.shape
    tp = qn.shape[1]
    ng = n_pages // pg
    per = PAGE // CMP_BLOCK

    def body(pt_ref, *refs):
        pages = refs[:pg]
        bpages = refs[pg:2 * pg]
        (q_ref, new_ref, z_ref, lam_ref, g_ref, wp_ref, o_ref, pool_ref,
         m_scr, l_scr, acc_scr) = refs[2 * pg:]

        rows = []
        for j in range(pg):
            for u in range(per):
                parts = []
                for kind in range(2):
                    x = bpages[j][pl.ds(u * CMP_BLOCK * 4 + kind, CMP_BLOCK, stride=4), :]
                    parts.append(jnp.sum(x * wp_ref[:, kind * LANE:(kind + 1) * LANE], axis=0, keepdims=True))
                rows.append(jnp.concatenate(parts, axis=1))
        pool_ref[0] = jnp.concatenate(rows, axis=0)
        gi = pl.program_id(1)

        @pl.when(gi == 0)
        def _():
            m_scr[...] = jnp.full(m_scr.shape, NEG, F32)
            l_scr[...] = jnp.zeros(l_scr.shape, F32)
            acc_scr[...] = jnp.zeros(acc_scr.shape, F32)

        lane = lax.broadcasted_iota(jnp.int32, (tp, LANE), 1)

        def qstack(h):
            q = q_ref[0, :, h * LANE:(h + 1) * LANE]
            zero = jnp.zeros_like(q)
            return jnp.concatenate([jnp.where(lane < DQ_A, q, zero),
                                    jnp.where(lane >= DQ_A, q, zero)], axis=0)

        pts = [jnp.swapaxes(pages[j][...].reshape(PAGE, 2 * H_A, LANE), 0, 1) for j in range(pg)]

        def slot(s_):
            return jnp.concatenate([pts[j][s_].astype(BF16) for j in range(pg)], axis=0)

        s = jnp.concatenate([_nt(qstack(h), slot(h)) for h in range(H_A)], axis=0)
        _online_update(s, None, m_scr, l_scr, acc_scr,
                       pv_fn=lambda p: jnp.concatenate(
                           [_mm(p[h * 2 * tp:(h + 1) * 2 * tp], slot(H_A + h)) for h in range(H_A)], axis=0))

        @pl.when(gi == ng - 1)
        def _():
            lam = _diff_lambda(lam_ref[...], lam_init)
            row = _mod(lax.broadcasted_iota(jnp.int32, (2 * tp, PAGE), 0), tp)
            col = lax.broadcasted_iota(jnp.int32, (2 * tp, PAGE), 1)
            mask = (col <= row) & (col < n_valid)
            mask4 = jnp.concatenate([mask] * H_A, axis=0)
            s = jnp.concatenate([_nt(qstack(h), new_ref[0, :, h * LANE:(h + 1) * LANE])
                                 for h in range(H_A)], axis=0)
            _online_update(jnp.where(mask4, s, MASKED), None, m_scr, l_scr, acc_scr,
                           pv_fn=lambda p: jnp.concatenate(
                               [_mm(p[h * 2 * tp:(h + 1) * 2 * tp],
                                    new_ref[0, :, 512 + h * LANE:512 + (h + 1) * LANE])
                                for h in range(H_A)], axis=0))
            for h in range(H_A):
                r0 = h * 2 * tp
                acc = acc_scr[r0:r0 + 2 * tp, :]
                l = l_scr[r0:r0 + 2 * tp, :]
                y = _a_finish(acc[0:tp], l[0:tp], acc[tp:], l[tp:], lam, g_ref[...],
                              z_ref[0, :, h * LANE:(h + 1) * LANE], lam_init)
                o_ref[0, :, h * LANE:(h + 1) * LANE] = y.astype(BF16)

    def page_spec(j):
        return pl.BlockSpec((PAGE * 2 * H_A, LANE),
                            lambda b, g, pt, j=j: (page_base + pt[b, g * pg + j], 0))

    def bpage_spec(j):
        return pl.BlockSpec((PAGE * 4, LANE), lambda b, g, pt, j=j: (page_base + pt[b, g * pg + j], 0))

    in_specs = [page_spec(j) for j in range(pg)] + [bpage_spec(j) for j in range(pg)] + [
            pl.BlockSpec((1, tp, 512), lambda b, g, pt: (b, 0, 0)),
            pl.BlockSpec((1, PAGE, 1024), lambda b, g, pt: (b, 0, 0)),
            pl.BlockSpec((1, tp, 512), lambda b, g, pt: (b, 0, _UNIT["az"] // 4)),
            pl.BlockSpec((4, DQ_A), lambda b, g, pt: (0, 0)),
            pl.BlockSpec((1, LANE), lambda b, g, pt: (0, 0)),
            pl.BlockSpec((CMP_BLOCK, 256), lambda b, g, pt: (0, 0))]
    grid_spec = pltpu.PrefetchScalarGridSpec(
        num_scalar_prefetch=1, grid=(db, ng), in_specs=in_specs,
        out_specs=[pl.BlockSpec((1, tp, 512), lambda b, g, pt: (b, 0, 0)),
                   pl.BlockSpec((1, pg * per, 256), lambda b, g, pt: (b, g, 0))],
        scratch_shapes=[pltpu.VMEM((H_A * 2 * tp, LANE), F32)] * 3)
    return pl.pallas_call(
        body, grid_spec=grid_spec,
        out_shape=[jax.ShapeDtypeStruct((db, tp, 512), BF16),
                   jax.ShapeDtypeStruct((db, n_pages * per, 256), F32)],
        compiler_params=_params("parallel", "arbitrary"),
        name="a_attn_sample")(page_table, *([cache2] * pg), *([cache_b2] * pg), qn, kvnew, proj3,
                              lam_p, out_g.reshape(1, LANE), wpb2)


def _b_prep(proj, g4, wpb, *, tm, pool):
    m = proj.shape[0]
    nb = tm // CMP_BLOCK

    def body(kv_ref, g_ref, wp_ref, bkv_ref, nw_ref, kvb_ref, *pool_refs):
        kc = kv_ref[:, 0:128]
        vc = kv_ref[:, 128:256]
        ks = _rms(kv_ref[:, 256:384], g_ref[2:3, :])
        vs = kv_ref[:, 384:512]
        kw = _rms(kv_ref[:, 512:640], g_ref[3:4, :])
        vw = kv_ref[:, 640:768]
        for slot, x in enumerate((kc, vc, ks, vs)):
            bkv_ref[pl.ds(slot, tm, stride=4), :] = x
        nw_ref[pl.ds(0, tm, stride=2), :] = kw
        nw_ref[pl.ds(1, tm, stride=2), :] = vw
        kvb_ref[:, 0:128] = ks.astype(BF16)
        kvb_ref[:, 128:256] = vs.astype(BF16)
        kvb_ref[:, 256:384] = kw.astype(BF16)
        kvb_ref[:, 384:512] = vw.astype(BF16)
        if pool:
            kcb_ref, vcb_ref = pool_refs
            kp = jnp.sum(kc.reshape(nb, CMP_BLOCK, LANE) * wp_ref[0][None], axis=1)
            vp = jnp.sum(vc.reshape(nb, CMP_BLOCK, LANE) * wp_ref[1][None], axis=1)
            kcb_ref[...] = _rms(kp, g_ref[1:2, :])
            vcb_ref[...] = vp

    row = lambda w: pl.BlockSpec((tm, w), lambda i: (i, 0))
    out_specs = [pl.BlockSpec((tm * 4, LANE), lambda i: (i, 0)),
                 pl.BlockSpec((tm * 2, LANE), lambda i: (i, 0)), row(512)]
    out_shape = [jax.ShapeDtypeStruct((m * 4, LANE), F32), jax.ShapeDtypeStruct((m * 2, LANE), F32),
                 jax.ShapeDtypeStruct((m, 512), BF16)]
    if pool:
        out_specs += [pl.BlockSpec((nb, LANE), lambda i: (i, 0))] * 2
        out_shape += [jax.ShapeDtypeStruct((m // CMP_BLOCK, LANE), F32)] * 2
    return pl.pallas_call(
        body, grid=(m // tm,),
        in_specs=[pl.BlockSpec((tm, 768), lambda i: (i, _UNIT["bkv"] // 6)),
                  pl.BlockSpec((4, LANE), lambda i: (0, 0)),
                  pl.BlockSpec((2, CMP_BLOCK, LANE), lambda i: (0, 0, 0))],
        out_specs=out_specs, out_shape=out_shape,
        compiler_params=_params("parallel"),
        name="b_prep")(proj, g4, wpb)


def _cmp_scores(qf, kcb):
    qh, ql = _split2(qf)
    kh, kl = _split2(kcb)
    return _nt(qh, kh) + _nt(qh, kl) + _nt(ql, kh)


def _b_attn_prompt(proj3, kcb, vcb, kvb, g4, *, t):
    b_ = proj3.shape[0]
    tq = 2 * Q_BLOCK if t % (2 * Q_BLOCK) == 0 else Q_BLOCK
    nblk = t // CMP_BLOCK
    n_top = min(N_SEL, nblk)
    tk = min(512, t)
    span = min(WINDOW + tq, t)
    scale = HEAD ** -0.5 * SCORE_LOG2E

    def body(q_ref, kcb_ref, vcb_ref, kv_ref, bg_ref, bz_ref, g_ref, o_ref, m_scr, l_scr, acc_scr):
        qi = pl.program_id(1)
        qf = jnp.concatenate([_rms(q_ref[0, :, h * LANE:(h + 1) * LANE], g_ref[0:1, :]) * scale
                              for h in range(H_B)], axis=0)
        qb = qf.astype(BF16)
        pos = qi * tq + lax.broadcasted_iota(jnp.int32, (tq, 1), 0)

        pos_l = qi * tq + lax.broadcasted_iota(jnp.int32, (1, tq), 1)
        cur_l = _div(pos_l, CMP_BLOCK)
        blk_s = lax.broadcasted_iota(jnp.int32, (nblk, tq), 0)
        cmask = blk_s < cur_l
        kh, kl = _split2(kcb_ref[0])
        qh, ql = _split2(qf)
        cmask4 = jnp.concatenate([cmask] * H_B, axis=1)
        s = jnp.where(cmask4, _nt(kh, qh) + _nt(kl, qh) + _nt(kh, ql), NEG)
        mx = jnp.max(s, axis=0, keepdims=True)
        p = jnp.where(cmask4, jnp.exp2(s - mx), 0.0)
        pc = p / jnp.maximum(jnp.sum(p, axis=0, keepdims=True), 1e-30)
        pcs = [pc[:, h * tq:(h + 1) * tq] for h in range(H_B)]
        imp = jnp.where(cmask, pcs[0] + pcs[1] + pcs[2] + pcs[3], -1.0)
        fill = LANE - H_B * nblk
        pcq = jnp.concatenate(pcs + ([jnp.zeros((fill, tq), F32)] if fill else []), axis=0).T.astype(BF16)
        vch = vcb_ref[0].astype(BF16)
        zblk = jnp.zeros((nblk, LANE), BF16)
        vdiag = jnp.concatenate(
            [jnp.concatenate([vch if c == h else zblk for c in range(H_B)], axis=1)
             for h in range(H_B)] + ([jnp.zeros((fill, H_B * LANE), BF16)] if fill else []), axis=0)
        o_cmp = _mm(pcq, vdiag)

        rank = jnp.zeros((nblk, tq), F32)
        for mrow in range(nblk):
            cm = imp[mrow:mrow + 1, :]
            ahead = (cm > imp) | ((cm == imp) & (mrow < blk_s))
            rank = rank + ahead.astype(F32)
        sel_t = (((rank < n_top) & (imp >= 0.0)) | (blk_s == cur_l)).astype(F32)
        selq = jnp.concatenate([sel_t, jnp.zeros((LANE - nblk, tq), F32)], axis=0).T.astype(BF16)

        m_scr[...] = jnp.full(m_scr.shape, NEG, F32)
        l_scr[...] = jnp.zeros(l_scr.shape, F32)
        acc_scr[...] = jnp.zeros(acc_scr.shape, F32)
        nk = _div(qi * tq + tq + tk - 1, tk)

        def chunk(c, carry):
            ks_ = pl.ds(pl.multiple_of(c * tk, tk), tk)
            kidx = c * tk + lax.broadcasted_iota(jnp.int32, (LANE, tk), 1)
            e = (_div(kidx, CMP_BLOCK) == lax.broadcasted_iota(jnp.int32, (LANE, tk), 0)).astype(BF16)
            kpos = c * tk + lax.broadcasted_iota(jnp.int32, (tq, tk), 1)
            mk = (_mm(selq, e) > 0.5) & (kpos <= pos)
            mask = jnp.concatenate([mk] * H_B, axis=0)
            s = jnp.where(mask, _nt(qb, kv_ref[0, ks_, 0:128]), MASKED)
            _online_update(s, kv_ref[0, ks_, 128:256], m_scr, l_scr, acc_scr)
            return carry

        lax.fori_loop(0, nk, chunk, 0)

        start = jnp.clip(qi * tq + tq - span, 0, t - span)
        ws = pl.ds(pl.multiple_of(start, tq), span)
        diff = pos - (start + lax.broadcasted_iota(jnp.int32, (tq, span), 1))
        wmask = jnp.concatenate([(diff >= 0) & (diff < WINDOW)] * H_B, axis=0)
        s = jnp.where(wmask, _nt(qb, kv_ref[0, ws, 256:384]), MASKED)
        p = jnp.exp2(s - jnp.maximum(jnp.max(s, axis=-1, keepdims=True), NEG))
        o_win = _mm(p.astype(BF16), kv_ref[0, ws, 384:512]) \
            / jnp.maximum(jnp.sum(p, axis=-1, keepdims=True), 1e-30)
        o_sel = acc_scr[...] / jnp.maximum(l_scr[...], 1e-30)

        gate = _sigmoid(bg_ref[0])
        for h in range(H_B):
            rows = slice(h * tq, (h + 1) * tq)
            ob = (gate[:, 3 * h:3 * h + 1] * o_cmp[:, h * LANE:(h + 1) * LANE]
                  + gate[:, 3 * h + 1:3 * h + 2] * o_sel[rows]
                  + gate[:, 3 * h + 2:3 * h + 3] * o_win[rows])
            y = ob * _silu(bz_ref[0, :, h * LANE:(h + 1) * LANE])
            o_ref[0, :, h * LANE:(h + 1) * LANE] = y.astype(BF16)

    full = lambda rows, w: pl.BlockSpec((1, rows, w), lambda b, i: (b, 0, 0))
    return pl.pallas_call(
        body, grid=(b_, t // tq),
        in_specs=[pl.BlockSpec((1, tq, 512), lambda b, i: (b, i, _UNIT["bq"] // 4)),
                  full(nblk, LANE), full(nblk, LANE), full(t, 512),
                  pl.BlockSpec((1, tq, LANE), lambda b, i: (b, i, _UNIT["bg"])),
                  pl.BlockSpec((1, tq, 512), lambda b, i: (b, i, _UNIT["bz"] // 4)),
                  pl.BlockSpec((4, LANE), lambda b, i: (0, 0))],
        out_specs=pl.BlockSpec((1, tq, 512), lambda b, i: (b, i, 0)),
        out_shape=jax.ShapeDtypeStruct((b_, t, 512), BF16),
        scratch_shapes=[pltpu.VMEM((H_B * tq, LANE), F32)] * 3,
        compiler_params=_params("parallel", "arbitrary"),
        name="b_attn_prompt")(proj3, kcb, vcb, kvb, proj3, proj3, g4)


def _b_select_sample(proj3, pooled, win, neww, g4, *, past_len, n_valid):
    db = neww.shape[0]
    tp = neww.shape[1] // 2
    nblk = pooled.shape[1]
    n_top = min(N_SEL, nblk + 1)
    wb = win.shape[1] // 2
    rows = H_B * tp
    scale = HEAD ** -0.5 * SCORE_LOG2E

    def body(q_ref, pool_ref, win_ref, nw_ref, bg_ref, g_ref, qs_ref, ocw_ref, g1_ref, sel_ref):
        qf = jnp.concatenate(
            [_rms(q_ref[0, :, h * LANE:(h + 1) * LANE], g_ref[0:1, :]) * scale for h in range(H_B)],
            axis=0)
        qb = qf.astype(BF16)
        qs_ref[0] = qb
        tok = _mod(lax.broadcasted_iota(jnp.int32, (rows, 1), 0), tp)
        pos = past_len + tok
        cur = _div(pos, CMP_BLOCK)
        blk = lax.broadcasted_iota(jnp.int32, (rows, nblk), 1)
        cmask = blk < cur

        kc = _rms(pool_ref[0, :, 0:128], g_ref[1:2, :])
        s = jnp.where(cmask, _cmp_scores(qf, kc), NEG)
        mx = jnp.max(s, axis=-1, keepdims=True)
        p = jnp.where(cmask, jnp.exp2(s - mx), 0.0)
        pc = p / jnp.maximum(jnp.sum(p, axis=-1, keepdims=True), 1e-30)
        o_cmp = _mm(pc.astype(BF16), pool_ref[0, :, 128:256].astype(BF16))
        imp = pc[0:tp]
        for h in range(1, H_B):
            imp = imp + pc[h * tp:(h + 1) * tp]
        imp = jnp.where(cmask[0:tp], imp, -1.0)

        pad = jnp.concatenate([imp, jnp.zeros((LANE - tp, nblk), F32)], axis=0)
        imp_t = jnp.concatenate([pad[:, u * LANE:(u + 1) * LANE].T for u in range(nblk // LANE)], axis=0)
        mi = lax.broadcasted_iota(jnp.int32, (nblk, nblk), 0)
        ni = lax.broadcasted_iota(jnp.int32, (nblk, nblk), 1)
        sels = []
        for tkn in range(tp):
            r = imp[tkn:tkn + 1, :]
            c = imp_t[:, tkn:tkn + 1]
            ahead = (c > r) | ((c == r) & (mi < ni))
            rank = jnp.sum(ahead.astype(F32), axis=0, keepdims=True)
            sels.append(((rank < n_top) & (r >= 0.0)).astype(F32))
        sel_ref[0] = jnp.concatenate(sels, axis=0)

        kw = win_ref[0, pl.ds(0, wb, stride=2), :].astype(BF16)
        vw = win_ref[0, pl.ds(1, wb, stride=2), :].astype(BF16)
        zpad = jnp.zeros((LANE - tp, LANE), F32)
        kn = jnp.concatenate([nw_ref[0, pl.ds(0, tp, stride=2), :], zpad], axis=0).astype(BF16)
        vn = jnp.concatenate([nw_ref[0, pl.ds(1, tp, stride=2), :], zpad], axis=0).astype(BF16)
        jw =lax.broadcasted_iota(jnp.int32, (rows, wb), 1)
        dw = pos - (past_len - wb + jw)
        jn = lax.broadcasted_iota(jnp.int32, (rows, LANE), 1)
        dn = tok - jn
        wmask = jnp.concatenate([(dw >= 0) & (dw < WINDOW), (dn >= 0) & (dn < WINDOW) & (jn < n_valid)],
                                axis=1)
        sw = jnp.where(wmask, jnp.concatenate([_nt(qb, kw), _nt(qb, kn)], axis=1), NEG)
        mw = jnp.max(sw, axis=-1, keepdims=True)
        pw = jnp.where(wmask, jnp.exp2(sw - mw), 0.0)
        o_win = (_mm(pw[:, 0:wb].astype(BF16), vw) + _mm(pw[:, wb:].astype(BF16), vn)) \
            / jnp.maximum(jnp.sum(pw, axis=-1, keepdims=True), 1e-30)

        gate = _sigmoid(bg_ref[0])
        g0 = jnp.concatenate([gate[:, 3 * h:3 * h + 1] for h in range(H_B)], axis=0)
        g1 = jnp.concatenate([gate[:, 3 * h + 1:3 * h + 2] for h in range(H_B)], axis=0)
        g2 = jnp.concatenate([gate[:, 3 * h + 2:3 * h + 3] for h in range(H_B)], axis=0)
        ocw_ref[0] = g0 * o_cmp + g2 * o_win
        g1_ref[0] = jnp.broadcast_to(g1, (rows, LANE))

    per_b = lambda r, w: pl.BlockSpec((1, r, w), lambda b: (b, 0, 0))
    return pl.pallas_call(
        body, grid=(db,),
        in_specs=[pl.BlockSpec((1, tp, 512), lambda b: (b, 0, _UNIT["bq"] // 4)),
                  per_b(nblk, 256), per_b(2 * wb, LANE), per_b(2 * tp, LANE),
                  pl.BlockSpec((1, tp, LANE), lambda b: (b, 0, _UNIT["bg"])),
                  pl.BlockSpec((4, LANE), lambda b: (0, 0))],
        out_specs=[per_b(rows, LANE), per_b(rows, LANE), per_b(rows, LANE), per_b(tp, nblk)],
        out_shape=[jax.ShapeDtypeStruct((db, rows, LANE), BF16),
                   jax.ShapeDtypeStruct((db, rows, LANE), F32),
                   jax.ShapeDtypeStruct((db, rows, LANE), F32),
                   jax.ShapeDtypeStruct((db, tp, nblk), F32)],
        compiler_params=_params("parallel"),
        name="b_select_sample")(proj3, pooled, win, neww, proj3, g4)


def _b_selected_sample(page_table, cache2, qs, selg, bkvs, ocw, g1b, proj3, page_base,
                       *, n_valid, pg):
    db, n_pages = page_table.shape
    ng = n_pages // pg
    rows = qs.shape[1]
    tp = rows // H_B
    per = PAGE // CMP_BLOCK
    nbg = pg * per
    width = pg * PAGE

    def body(pt_ref, *refs):
        pages = refs[:pg]
        q_ref, sel_ref, new_ref, ocw_ref, g1_ref, z_ref, o_ref, m_scr, l_scr, acc_scr = refs[pg:]
        gi = pl.program_id(1)

        @pl.when(gi == 0)
        def _():
            m_scr[...] = jnp.full(m_scr.shape, NEG, F32)
            l_scr[...] = jnp.zeros(l_scr.shape, F32)
            acc_scr[...] = jnp.zeros(acc_scr.shape, F32)

        q = q_ref[0]

        def slot(s_):
            return jnp.concatenate([pages[j][pl.ds(s_, PAGE, stride=4), :].astype(BF16)
                                    for j in range(pg)], axis=0)

        e = (_div(lax.broadcasted_iota(jnp.int32, (nbg, width), 1), CMP_BLOCK)
             == lax.broadcasted_iota(jnp.int32, (nbg, width), 0)).astype(BF16)
        mk = _mm(sel_ref[0, 0].astype(BF16), e) > 0.5
        mask = jnp.concatenate([mk] * H_B, axis=0)
        _online_update(jnp.where(mask, _nt(q, slot(2)), MASKED), slot(3), m_scr, l_scr, acc_scr)

        @pl.when(gi == ng - 1)
        def _():
            zpad = jnp.zeros((PAGE - tp, LANE), F32)
            kn = jnp.concatenate([new_ref[0, pl.ds(2, tp, stride=4), :], zpad], axis=0).astype(BF16)
            vn = jnp.concatenate([new_ref[0, pl.ds(3, tp, stride=4), :], zpad], axis=0).astype(BF16)
            tok = _mod(lax.broadcasted_iota(jnp.int32, (rows, PAGE), 0), tp)
            col = lax.broadcasted_iota(jnp.int32, (rows, PAGE), 1)
            nmask = (col <= tok) & (col < n_valid)
            _online_update(jnp.where(nmask, _nt(q, kn), MASKED), vn, m_scr, l_scr, acc_scr)
            o_sel = acc_scr[...] / jnp.maximum(l_scr[...], 1e-30)
            ob = ocw_ref[0] + g1_ref[0] * o_sel
            for h in range(H_B):
                y = ob[h * tp:(h + 1) * tp] * _silu(z_ref[0, :, h * LANE:(h + 1) * LANE])
                o_ref[0, :, h * LANE:(h + 1) * LANE] = y.astype(BF16)

    def page_spec(j):
        return pl.BlockSpec((PAGE * 4, LANE), lambda b, g, pt, j=j: (page_base + pt[b, g * pg + j], 0))

    per_b = lambda r, w: pl.BlockSpec((1, r, w), lambda b, g, pt: (b, 0, 0))
    grid_spec = pltpu.PrefetchScalarGridSpec(
        num_scalar_prefetch=1, grid=(db, ng),
        in_specs=[page_spec(j) for j in range(pg)] + [
            per_b(rows, LANE),
            pl.BlockSpec((1, 1, tp, nbg), lambda b, g, pt: (b, g, 0, 0)),
            per_b(4 * tp, LANE), per_b(rows, LANE), per_b(rows, LANE),
            pl.BlockSpec((1, tp, 512), lambda b, g, pt: (b, 0, _UNIT["bz"] // 4))],
        out_specs=per_b(tp, 512),
        scratch_shapes=[pltpu.VMEM((rows, LANE), F32)] * 3)
    return pl.pallas_call(
        body, grid_spec=grid_spec,
        out_shape=jax.ShapeDtypeStruct((db, tp, 512), BF16),
        compiler_params=_params("parallel", "arbitrary"),
        name="b_selected_sample")(page_table, *([cache2] * pg), qs, selg, bkvs, ocw, g1b, proj3)


def _mlstm(proj3, conv_w, conv_b, gate_b, out_g, conv0, c0, n0, m0, *, lc, n_valid):
    b_, t, _ = proj3.shape
    col_head = jnp.arange(H_C * LANE)[None, :] // LANE
    lane_id = jnp.arange(LANE)[:, None]
    rsel = jnp.stack([lane_id == col_head, lane_id == col_head + H_C]).astype(BF16)
    nchunk = t // lc
    dqk = H_C * DQK_C
    tail = CONV_W - 1
    base = 8
    tsq = max(lc, LANE)

    def bcast_cols(x, sel):
        hi, lo = _split2(x)
        return _mm(hi, sel) + _mm(lo, sel)

    def body(qk_ref, v_ref, if_ref, co_ref, cz_ref, cw_ref, cb_ref, gb_ref, g_ref, rsel_ref,
             conv0_ref, c0_ref, n0_ref, m0_ref,
             y_ref, conv_ref, c_ref, n_ref, m_ref, xbuf, c_scr, n_scr, m_scr):
        ci = pl.program_id(1)

        @pl.when(ci == 0)
        def _():
            xbuf[base - tail:base, :] = conv0_ref[0]
            c_scr[...] = c0_ref[0]
            n_scr[...] = n0_ref[0]
            m_scr[...] = m0_ref[0]

        xbuf[base:base + lc, :] = qk_ref[0]
        acc = cb_ref[...]
        for j in range(CONV_W):
            acc = acc + xbuf[base - tail + j:base - tail + j + lc, :] * cw_ref[j:j + 1, :]
        new_tail = xbuf[base + n_valid - tail:base + n_valid, :]
        xbuf[base - tail:base, :] = new_tail
        conv_ref[0] = new_tail
        qk = _silu(acc)

        gt = if_ref[0] + gb_ref[...]
        lf = jnp.minimum(gt, 0.0) - jnp.log(1.0 + jnp.exp(-jnp.abs(gt)))
        ti = lax.broadcasted_iota(jnp.int32, (lc, lc), 0)
        si = lax.broadcasted_iota(jnp.int32, (lc, lc), 1)
        tri = (si <= ti).astype(BF16)
        l1, l2, l3 = _split3(lf)
        bcum = _mm(tri, l1) + _mm(tri, l2) + _mm(tri, l3)
        zrow = jnp.zeros((tsq - lc, LANE), F32)
        gt_t = (jnp.concatenate([gt, zrow], axis=0) if tsq > lc else gt).T
        b_t = (jnp.concatenate([bcum, zrow], axis=0) if tsq > lc else bcum).T
        dmask = (si <= ti) & (si < n_valid)
        svalid = lax.broadcasted_iota(jnp.int32, (lc, LANE), 0) < n_valid
        icol_all = bcast_cols(gt, rsel_ref[0])
        bcol_all = bcast_cols(bcum, rsel_ref[1])
        gate_all = _sigmoid(co_ref[0]) * _silu(cz_ref[0])
        low = lax.broadcasted_iota(jnp.int32, (lc, LANE), 1) < DQK_C
        row_low = lax.broadcasted_iota(jnp.int32, (LANE, LANE), 0) < DQK_C

        c_new, n_new, m_new_all = [], [], []
        for j in range(H_C // 2):
            qp = qk[:, j * LANE:(j + 1) * LANE] * (DQK_C ** -0.5)
            kp = qk[:, dqk + j * LANE:dqk + (j + 1) * LANE]
            cp = c_scr[j]
            cpb = cp.astype(BF16)
            npair = n_scr[j:j + 1, :]
            c_upd, n_upd, carries = None, None, []
            for u in range(2):
                h = 2 * j + u
                sel = low if u == 0 else jnp.logical_not(low)
                qm = jnp.where(sel, qp, 0.0)
                km = jnp.where(sel, kp, 0.0)
                qmb, kmb = qm.astype(BF16), km.astype(BF16)
                vh = v_ref[0, :, h * HEAD:(h + 1) * HEAD]
                bcol = bcol_all[:, h * LANE:(h + 1) * LANE]
                icol = icol_all[:, h * LANE:(h + 1) * LANE]
                brow = b_t[H_C + h:H_C + h + 1, 0:lc]
                irow = gt_t[h:h + 1, 0:lc]
                m_h = m_scr[h:h + 1, :]

                d = jnp.where(dmask, bcol[:, 0:lc] - brow + irow, NEG)
                inter = bcol + m_h
                m_t = jnp.maximum(inter, jnp.max(d, axis=1, keepdims=True))
                w_intra = jnp.exp(d - m_t[:, 0:lc])
                w_inter = jnp.exp(inter - m_t)
                sqk = _nt(qmb, kmb) * w_intra
                num = w_inter * _mm(qmb, cpb) + _mm(sqk.astype(BF16), vh.astype(BF16))
                den = (w_inter * jnp.sum(qm * npair, axis=1, keepdims=True)
                       + jnp.sum(sqk, axis=1, keepdims=True))
                hh = num / jnp.maximum(jnp.abs(den), jnp.exp(-m_t))
                y_ref[0, :, h * HEAD:(h + 1) * HEAD] = (
                    _rms(hh, g_ref[...]) * gate_all[:, h * HEAD:(h + 1) * HEAD]).astype(BF16)

                b_last = bcol[n_valid - 1:n_valid, :]
                dec = jnp.where(svalid, b_last - bcol + icol, NEG)
                m_new = jnp.maximum(b_last + m_h, jnp.max(dec, axis=0, keepdims=True))
                wk = jnp.exp(dec - m_new)
                carries.append(jnp.exp(b_last + m_h - m_new))
                cu = _tn(kmb, (wk * vh).astype(BF16))
                nu = jnp.sum(wk * km, axis=0, keepdims=True)
                c_upd = cu if c_upd is None else c_upd + cu
                n_upd = nu if n_upd is None else n_upd + nu
                m_new_all.append(m_new)
            c_new.append(jnp.where(row_low, carries[0], carries[1]) * cp + c_upd)
            n_new.append(jnp.where(low[0:1], carries[0], carries[1]) * npair + n_upd)

        for j in range(H_C // 2):
            c_scr[j] = c_new[j]
            n_scr[j:j + 1, :] = n_new[j]
        for h in range(H_C):
            m_scr[h:h + 1, :] = m_new_all[h]
        c_ref[0] = c_scr[...]
        n_ref[0] = n_scr[...]
        m_ref[0] = m_scr[...]

    col = lambda u: pl.BlockSpec((1, lc, 1024), lambda b, c, u=u: (b, c, u))
    const = lambda *shape: pl.BlockSpec(shape, lambda b, c: (0,) * len(shape))
    per_b = lambda *shape: pl.BlockSpec((1,) + shape, lambda b, c: (b,) + (0,) * len(shape))
    return pl.pallas_call(
        body, grid=(b_, nchunk),
        in_specs=[col(_UNIT["cqk"] // 8), col(_UNIT["cv"] // 8),
                  pl.BlockSpec((1, lc, LANE), lambda b, c: (b, c, _UNIT["cif"])),
                  col(_UNIT["co"] // 8), col(_UNIT["cz"] // 8),
                  const(CONV_W, 1024), const(1, 1024), const(1, LANE), const(1, LANE),
                  const(2, LANE, H_C * LANE),
                  per_b(tail, 1024), per_b(H_C // 2, LANE, HEAD), per_b(H_C // 2, LANE), per_b(H_C, LANE)],
        out_specs=[pl.BlockSpec((1, lc, 1024), lambda b, c: (b, c, 0)),
                   per_b(tail, 1024), per_b(H_C // 2, LANE, HEAD), per_b(H_C // 2, LANE), per_b(H_C, LANE)],
        out_shape=[jax.ShapeDtypeStruct((b_, t, 1024), BF16),
                   jax.ShapeDtypeStruct((b_, tail, 1024), F32),
                   jax.ShapeDtypeStruct((b_, H_C // 2, LANE, HEAD), F32),
                   jax.ShapeDtypeStruct((b_, H_C // 2, LANE), F32),
                   jax.ShapeDtypeStruct((b_, H_C, LANE), F32)],
        scratch_shapes=[pltpu.VMEM((base + lc, 1024), F32), pltpu.VMEM((H_C // 2, LANE, HEAD), F32),
                        pltpu.VMEM((H_C // 2, LANE), F32), pltpu.VMEM((H_C, LANE), F32)],
        compiler_params=_params("parallel", "arbitrary"),
        name="mlstm")(proj3, proj3, proj3, proj3, proj3, conv_w, conv_b, gate_b, out_g, rsel,
                      conv0, c0, n0, m0)


def _w_prep(w_in):
    depth, d, n_in = w_in.shape
    kt_n = d // LANE
    per_col = kt_n * depth
    rows = w_in.reshape(depth, kt_n, LANE, n_in).transpose(3, 1, 0, 2).reshape(n_in * per_col, LANE)
    src_off, off = {}, 0
    for name, width in _SRC:
        src_off[name] = (off, width)
        off += width
    assert off == n_in
    starts, valids = [0] * (N_PROJ // LANE), [0] * (N_PROJ // LANE)
    for name in _DST_ORDER:
        s0, width = src_off[name]
        for u in range(-(-width // LANE)):
            starts[_UNIT[name] + u] = s0 + u * LANE
            valids[_UNIT[name] + u] = min(LANE, width - u * LANE)
    table = jnp.array([starts, valids], jnp.int32)

    upb = 4

    def body(tbl_ref, *refs):
        w_refs, o_ref = refs[:upb], refs[upb]
        for k in range(upb):
            keep = lax.broadcasted_iota(jnp.int32, (LANE, LANE), 0) < tbl_ref[1, pl.program_id(0) * upb + k]
            xt = jnp.swapaxes(w_refs[k][...].reshape(LANE, per_col, LANE), 0, 1)
            for l in range(depth):
                for kt in range(kt_n):
                    o_ref[l, k * LANE:(k + 1) * LANE, kt * LANE:(kt + 1) * LANE] = (
                        jnp.where(keep, xt[kt * depth + l], 0.0).astype(BF16))

    def src_spec(k):
        return pl.BlockSpec((pl.Element(LANE * per_col), pl.Element(LANE)),
                            lambda u, tbl, k=k: (tbl[0, u * upb + k] * per_col, 0))

    grid_spec = pltpu.PrefetchScalarGridSpec(
        num_scalar_prefetch=1, grid=(N_PROJ // LANE // upb,),
        in_specs=[src_spec(k) for k in range(upb)],
        out_specs=pl.BlockSpec((depth, upb * LANE, d), lambda u, tbl: (0, u, 0)))
    return pl.pallas_call(
        body, grid_spec=grid_spec,
        out_shape=jax.ShapeDtypeStruct((depth, N_PROJ, d), BF16),
        compiler_params=_params("parallel"),
        name="w_prep")(table, *([rows] * upb))


def _c_from_pairs(c):
    b = c.shape[0]
    return jnp.swapaxes(c.reshape(b, H_C, DQK_C, HEAD), -1, -2)


def _pick(n, prefs):
    for p in prefs:
        if n % p == 0:
            return p
    return n


def kernel(x_prompt, x_sample, cache_a_kv, cache_b_kv, state_b_win, state_c_conv, state_c_C,
           state_c_n, state_c_m, page_table, norm_g, w_in, w_out, a_qk_g, a_lambda, a_out_g,
           b_qk_g, b_cmp_w, c_conv_w, c_conv_b, c_gate_b, c_out_g):
    bp, t, d = x_prompt.shape
    db, ts, _ = x_sample.shape
    depth = norm_g.shape[0]
    n_pool = cache_a_kv.shape[1]
    n_pages = page_table.shape[1]
    past_len = n_pages * PAGE
    wb = state_b_win.shape[2]
    tp = 8
    assert ts <= tp and t % MLSTM_CHUNK == 0 and t % Q_BLOCK == 0 and d == 2048
    mp, ms = bp * t, db * tp
    pg = _pick(n_pages, (16, 8, 4, 2, 1))

    cache_a2 = cache_a_kv.reshape(depth * n_pool * PAGE * 2 * H_A, HEAD)
    cache_b2 = cache_b_kv.reshape(depth * n_pool * PAGE * 4, HEAD)
    win_rows = state_b_win.reshape(depth, db, wb * 2, HEAD)
    hp = x_prompt.reshape(mp, d)
    hs = jnp.pad(x_sample, ((0, 0), (0, tp - ts), (0, 0))).reshape(ms, d)

    tm_p = _pick(mp, (1024, 512, 256, 128))
    tq_a = _pick(t, (512, 256, 128))
    outs_p = [[] for _ in range(7)]
    outs_s = [[] for _ in range(7)]

    w_all = _w_prep(w_in)
    for l in range(depth):
        lam_init = 0.8 - 0.6 * math.exp(-0.3 * l)
        w_o = w_out[l].astype(BF16)
        g_a = jnp.tile(a_qk_g[l], (1, 2))
        g_b = b_qk_g[l]
        wpb = jnp.broadcast_to(b_cmp_w[l][:, :, None], (2, CMP_BLOCK, LANE))
        wpb2 = jnp.concatenate([wpb[0], wpb[1]], axis=1)
        gate_b = jnp.pad(c_gate_b[l].reshape(1, 2 * H_C), ((0, 0), (0, LANE - 2 * H_C)))
        conv_b = c_conv_b[l].reshape(1, -1)
        out_gc = c_out_g[l].reshape(1, LANE)

        proj = _norm_matmul(hp, norm_g[l], w_all, l, tm=tm_p, tn=1024)
        proj3 = proj.reshape(bp, t, N_PROJ)
        qn, kn, vb, akv = _a_prep(proj, g_a, tm=_pick(mp, (512, 256, 128)))
        ya = _a_attn_prompt(qn.reshape(bp, t, 512), kn.reshape(bp, t, 512), vb.reshape(bp, t, 512),
                            proj3, a_lambda[l], a_out_g[l], lam_init, tq=tq_a)
        bkvs, neww, kvb, kcb, vcb = _b_prep(proj, g_b, wpb, tm=_pick(mp, (512, 256, 128)), pool=True)
        nblk = t // CMP_BLOCK
        yb = _b_attn_prompt(proj3, kcb.reshape(bp, nblk, LANE), vcb.reshape(bp, nblk, LANE),
                            kvb.reshape(bp, t, 512), g_b, t=t)
        yc, conv_p, c_p, n_p, m_p = _mlstm(
            proj3, c_conv_w[l], conv_b, gate_b, out_gc,
            jnp.zeros((bp, CONV_W - 1, 2 * H_C * DQK_C), F32), jnp.zeros((bp, H_C // 2, LANE, HEAD), F32),
            jnp.zeros((bp, H_C // 2, LANE), F32), jnp.zeros((bp, H_C, LANE), F32),
            lc=MLSTM_CHUNK, n_valid=MLSTM_CHUNK)
        c_p, n_p = _c_from_pairs(c_p), n_p.reshape(bp, H_C, DQK_C)
        hp = _out_proj(hp, ya.reshape(mp, 512), yb.reshape(mp, 512), yc.reshape(mp, 1024), w_o,
                       tm=tm_p, tn=1024)
        win_p = jnp.concatenate([jnp.zeros((bp, wb, 2, HEAD), F32), neww.reshape(bp, t, 2, HEAD)],
                                axis=1)[:, -wb:]
        for lst, a in zip(outs_p, (akv.reshape(bp, t, 2, H_A, HEAD), bkvs.reshape(bp, t, 4, HEAD),
                                   win_p, conv_p, c_p, n_p, m_p[:, :, 0])):
            lst.append(a)

        base = l * n_pool
        sproj = _norm_matmul(hs, norm_g[l], w_all, l, tm=ms, tn=1024)
        sproj3 = sproj.reshape(db, tp, N_PROJ)
        sqn, skn, svb, sakv = _a_prep(sproj, g_a, tm=ms)
        kvnew = jnp.concatenate([skn.reshape(db, tp, 512), svb.reshape(db, tp, 512)], axis=2)
        kvnew = jnp.pad(kvnew, ((0, 0), (0, PAGE - tp), (0, 0)))
        sya, pooled = _a_attn_sample(page_table, cache_a2, sqn.reshape(db, tp, 512), kvnew, sproj3,
                                     a_lambda[l], a_out_g[l], lam_init, base, cache_b2, wpb2,
                                     n_valid=ts, pg=pg)
        sbkvs, sneww, _ = _b_prep(sproj, g_b, wpb, tm=ms, pool=False)
        qs, ocw, g1b, sel = _b_select_sample(
            sproj3, pooled, win_rows[l], sneww.reshape(db, tp * 2, HEAD), g_b,
            past_len=past_len, n_valid=ts)
        nbg = pg * (PAGE // CMP_BLOCK)
        selg = sel.reshape(db, tp, n_pages // pg, nbg).transpose(0, 2, 1, 3)
        syb = _b_selected_sample(page_table, cache_b2, qs, selg, sbkvs.reshape(db, tp * 4, HEAD), ocw, g1b,
                                 sproj3, base, n_valid=ts, pg=pg)
        m0 = jnp.broadcast_to(state_c_m[l][:, :, None], (db, H_C, LANE))
        syc, conv_s, c_s, n_s, m_s = _mlstm(
            sproj3, c_conv_w[l], conv_b, gate_b, out_gc, state_c_conv[l],
            jnp.swapaxes(state_c_C[l], -1, -2).reshape(db, H_C // 2, LANE, HEAD),
            state_c_n[l].reshape(db, H_C // 2, LANE), m0, lc=tp, n_valid=ts)
        c_s, n_s = _c_from_pairs(c_s), n_s.reshape(db, H_C, DQK_C)
        hs = _out_proj(hs, sya.reshape(ms, 512), syb.reshape(ms, 512), syc.reshape(ms, 1024), w_o,
                       tm=ms, tn=1024)
        win_s = jnp.concatenate([state_b_win[l], sneww.reshape(db, tp, 2, HEAD)[:, :ts]], axis=1)[:, -wb:]
        for lst, a in zip(outs_s, (sakv.reshape(db, tp, 2, H_A, HEAD)[:, :ts],
                                   sbkvs.reshape(db, tp, 4, HEAD)[:, :ts],
                                   win_s, conv_s, c_s, n_s, m_s[:, :, 0])):
            lst.append(a)

    y_p = hp.reshape(bp, t, d)
    y_s = hs.reshape(db, tp, d)[:, :ts]
    sp = [jnp.stack(x) for x in outs_p]
    ss = [jnp.stack(x) for x in outs_s]
    return (y_p, y_s, sp[0], ss[0], sp[1], ss[1], sp[2], ss[2], sp[3], ss[3],
            sp[4], ss[4], sp[5], ss[5], sp[6], ss[6])
```

```python
import math

import jax
import jax.numpy as jnp
from jax import lax
from jax.experimental import pallas as pl
from jax.experimental.pallas import tpu as pltpu

F32 = jnp.float32
BF16 = jnp.bfloat16

EPS = 1e-6
LANE = 128
HEAD = 128
H_A = 4
DQ_A = HEAD // 2
H_B = 4
H_C = 8
DQK_C = HEAD // 2
CMP_BLOCK = 64
N_SEL = 16
WINDOW = 512
CONV_W = 4
MLSTM_CHUNK = 128
PAGE = 128
Q_BLOCK = 128
NEG = -1e30
MASKED = 2 * NEG
SCORE_LOG2E = math.log2(math.e)
VMEM_LIMIT = 56 * 1024 * 1024

_SRC = (("aq", 512), ("ak", 512), ("av", 512), ("az", 512),
        ("bq", 512), ("bkv", 768), ("bg", 12), ("bz", 512),
        ("cqk", 1024), ("cv", 1024), ("cif", 16), ("co", 1024), ("cz", 1024))
_DST_ORDER = ("aq", "ak", "av", "az", "bq", "bz", "bkv", "bg", "cif", "cqk", "cv", "co", "cz")
_UNIT = {"aq": 0, "ak": 4, "av": 8, "az": 12, "bq": 16, "bz": 20, "bkv": 24, "bg": 30,
         "cif": 31, "cqk": 32, "cv": 40, "co": 48, "cz": 56}
N_PROJ = 64 * LANE


def _nt(a, b):
    return lax.dot_general(a, b, (((1,), (1,)), ((), ())), preferred_element_type=F32)


def _tn(a, b):
    return lax.dot_general(a, b, (((0,), (0,)), ((), ())), preferred_element_type=F32)


def _mm(a, b):
    return jnp.dot(a, b, preferred_element_type=F32)


def _split2(x):
    hi = x.astype(BF16)
    lo = (x - hi.astype(F32)).astype(BF16)
    return hi, lo


def _split3(x):
    hi = x.astype(BF16)
    r = x - hi.astype(F32)
    mid = r.astype(BF16)
    lo = (r - mid.astype(F32)).astype(BF16)
    return hi, mid, lo


def _sigmoid(z):
    return 1.0 / (1.0 + jnp.exp(-z))


def _silu(z):
    return z * _sigmoid(z)


def _rms(x, g):
    return x * lax.rsqrt(jnp.mean(x * x, axis=-1, keepdims=True) + EPS) * g


def _div(x, n):
    return lax.shift_right_logical(x, int(math.log2(n)))


def _mod(x, n):
    return x & (n - 1)


def _online_update(s, v, m_ref, l_ref, acc_ref, pv_fn=None):
    w = s.shape[1]
    m_old = m_ref[...]
    m_new = jnp.maximum(m_old, jnp.max(s, axis=-1, keepdims=True))
    alpha = jnp.exp2(m_old - m_new)
    m_rep = m_new if w == LANE else jnp.concatenate([m_new] * (w // LANE), axis=1)
    p = jnp.exp2(s - m_rep)
    l_ref[...] = alpha * l_ref[...] + jnp.sum(p, axis=-1, keepdims=True)
    pb = p.astype(BF16)
    acc_ref[...] = alpha * acc_ref[...] + (_mm(pb, v) if pv_fn is None else pv_fn(pb))
    m_ref[...] = m_new


def _params(*sem):
    return pltpu.CompilerParams(dimension_semantics=sem, vmem_limit_bytes=VMEM_LIMIT)


def _norm_matmul(x, g, w, layer, *, tm, tn):
    m, d = x.shape
    n = w.shape[1]
    rc = min(tm, 256)

    def body(x_ref, g_ref, w_ref, o_ref, h_scr):
        @pl.when(pl.program_id(1) == 0)
        def _():
            def chunk(c, carry):
                r = pl.ds(pl.multiple_of(c * rc, rc), rc)
                h_scr[r, :] = _rms(x_ref[r, :], g_ref[...]).astype(BF16)
                return carry
            lax.fori_loop(0, tm // rc, chunk, 0)
        o_ref[...] = _nt(h_scr[...], w_ref[0])

    return pl.pallas_call(
        body, grid=(m // tm, n // tn),
        in_specs=[pl.BlockSpec((tm, d), lambda i, j: (i, 0)),
                  pl.BlockSpec((1, d), lambda i, j: (0, 0)),
                  pl.BlockSpec((1, tn, d), lambda i, j: (layer, j, 0))],
        out_specs=pl.BlockSpec((tm, tn), lambda i, j: (i, j)),
        out_shape=jax.ShapeDtypeStruct((m, n), F32),
        scratch_shapes=[pltpu.VMEM((tm, d), BF16)],
        compiler_params=_params("parallel", "arbitrary"),
        name="norm_matmul")(x, g.reshape(1, d), w)


def _out_proj(x, ya, yb, yc, w, *, tm, tn):
    m, d = x.shape
    da, db, dc = ya.shape[1], yb.shape[1], yc.shape[1]

    def body(x_ref, a_ref, b_ref, c_ref, w_ref, o_ref):
        mix = jnp.concatenate([a_ref[...], b_ref[...], c_ref[...]], axis=1)
        o_ref[...] = x_ref[...] + _mm(mix, w_ref[...])

    return pl.pallas_call(
        body, grid=(m // tm, d // tn),
        in_specs=[pl.BlockSpec((tm, tn), lambda i, j: (i, j)),
                  pl.BlockSpec((tm, da), lambda i, j: (i, 0)),
                  pl.BlockSpec((tm, db), lambda i, j: (i, 0)),
                  pl.BlockSpec((tm, dc), lambda i, j: (i, 0)),
                  pl.BlockSpec((da + db + dc, tn), lambda i, j: (0, j))],
        out_specs=pl.BlockSpec((tm, tn), lambda i, j: (i, j)),
        out_shape=jax.ShapeDtypeStruct((m, d), F32),
        compiler_params=_params("parallel", "parallel"),
        name="out_proj")(x, ya, yb, yc, w)


def _a_prep(proj, g2, *, tm):
    m = proj.shape[0]

    def body(q_ref, k_ref, v_ref, g_ref, qn_ref, kn_ref, vb_ref, akv_ref):
        r = _div(lax.broadcasted_iota(jnp.int32, (LANE, LANE), 0), DQ_A)
        c = _div(lax.broadcasted_iota(jnp.int32, (LANE, LANE), 1), DQ_A)
        seg = (r == c).astype(BF16)

        def segnorm(x, g):
            outs = []
            for u in range(4):
                xc = x[:, u * LANE:(u + 1) * LANE]
                hi, lo = _split2(xc * xc)
                s = _mm(hi, seg) + _mm(lo, seg)
                outs.append(xc * lax.rsqrt(s * (1.0 / DQ_A) + EPS) * g)
            return jnp.concatenate(outs, axis=1)

        qn = segnorm(q_ref[...], g_ref[0:1, :]) * (DQ_A ** -0.5 * SCORE_LOG2E)
        kn = segnorm(k_ref[...], g_ref[1:2, :])
        v = v_ref[...]
        qn_ref[...] = qn.astype(BF16)
        kn_ref[...] = kn.astype(BF16)
        vb_ref[...] = v.astype(BF16)
        for h in range(H_A):
            akv_ref[pl.ds(h, tm, stride=2 * H_A), :] = kn[:, h * LANE:(h + 1) * LANE]
            akv_ref[pl.ds(H_A + h, tm, stride=2 * H_A), :] = v[:, h * LANE:(h + 1) * LANE]

    blk = lambda u: pl.BlockSpec((tm, 512), lambda i, u=u: (i, u))
    row = lambda w: pl.BlockSpec((tm, w), lambda i: (i, 0))
    return pl.pallas_call(
        body, grid=(m // tm,),
        in_specs=[blk(_UNIT["aq"] // 4), blk(_UNIT["ak"] // 4), blk(_UNIT["av"] // 4),
                  pl.BlockSpec((2, LANE), lambda i: (0, 0))],
        out_specs=[row(512), row(512), row(512), pl.BlockSpec((tm * 2 * H_A, LANE), lambda i: (i, 0))],
        out_shape=[jax.ShapeDtypeStruct((m, 512), BF16)] * 3
        + [jax.ShapeDtypeStruct((m * 2 * H_A, LANE), F32)],
        compiler_params=_params("parallel"),
        name="a_prep")(proj, proj, proj, g2)


def _diff_lambda(lp, lam_init):
    a = jnp.sum(lp[0:1, :] * lp[1:2, :], axis=-1, keepdims=True)
    b = jnp.sum(lp[2:3, :] * lp[3:4, :], axis=-1, keepdims=True)
    return jnp.exp(a) - jnp.exp(b) + lam_init


def _a_finish(acc1, l1, acc2, l2, lam, g, z, lam_init):
    o = acc1 / l1 - lam * (acc2 / l2)
    return _rms(o, g) * (1.0 - lam_init) * _silu(z)


def _a_attn_prompt(qn, kn, vb, proj3, lam_p, out_g, lam_init, *, tq):
    b_, t, _ = qn.shape
    tk = tq

    def body(q_ref, k_ref, v_ref, z_ref, lam_ref, g_ref, o_ref, m_scr, l_scr, acc_scr):
        qi = pl.program_id(2)
        q = q_ref[0]
        lane = lax.broadcasted_iota(jnp.int32, (tq, LANE), 1)
        zero = jnp.zeros_like(q)
        qs = (jnp.where(lane < DQ_A, q, zero), jnp.where(lane >= DQ_A, q, zero))
        m_scr[...] = jnp.full(m_scr.shape, NEG, F32)
        l_scr[...] = jnp.zeros(l_scr.shape, F32)
        acc_scr[...] = jnp.zeros(acc_scr.shape, F32)

        def chunk(kc, mask):
            ks = pl.ds(pl.multiple_of(kc * tk, tk), tk)
            k = k_ref[0, ks, :]
            v = v_ref[0, ks, :]
            for c in range(2):
                s = _nt(qs[c], k)
                if mask is not None:
                    s = jnp.where(mask, s, MASKED)
                _online_update(s, v, m_scr.at[c], l_scr.at[c], acc_scr.at[c])

        def full_chunk(kc, carry):
            chunk(kc, None)
            return carry

        lax.fori_loop(0, qi, full_chunk, 0)
        chunk(qi, lax.broadcasted_iota(jnp.int32, (tq, tk), 1)
              <= lax.broadcasted_iota(jnp.int32, (tq, tk), 0))
        lam = _diff_lambda(lam_ref[...], lam_init)
        y = _a_finish(acc_scr[0], l_scr[0], acc_scr[1], l_scr[1], lam, g_ref[...], z_ref[0], lam_init)
        o_ref[0] = y.astype(BF16)

    head = lambda rows, qdep: pl.BlockSpec(
        (1, rows, LANE), (lambda b, h, i: (b, i, h)) if qdep else (lambda b, h, i: (b, 0, h)))
    return pl.pallas_call(
        body, grid=(b_, H_A, t // tq),
        in_specs=[head(tq, True), head(t, False), head(t, False),
                  pl.BlockSpec((1, tq, LANE), lambda b, h, i: (b, i, _UNIT["az"] + h)),
                  pl.BlockSpec((4, DQ_A), lambda b, h, i: (0, 0)),
                  pl.BlockSpec((1, LANE), lambda b, h, i: (0, 0))],
        out_specs=head(tq, True),
        out_shape=jax.ShapeDtypeStruct((b_, t, 512), BF16),
        scratch_shapes=[pltpu.VMEM((2, tq, LANE), F32)] * 3,
        compiler_params=_params("parallel", "parallel", "arbitrary"),
        name="a_attn_prompt")(qn, kn, vb, proj3, lam_p, out_g.reshape(1, LANE))


def _a_attn_sample(page_table, cache2, qn, kvnew, proj3, lam_p, out_g, lam_init, page_base,
                   cache_b2, wpb2, *, n_valid, pg):
    db, n_pages = page_table.shape
    tp = qn.shape[1]
    ng = n_pages // pg
    per = PAGE // CMP_BLOCK

    def body(pt_ref, *refs):
        pages = refs[:pg]
        bpages = refs[pg:2 * pg]
        (q_ref, new_ref, z_ref, lam_ref, g_ref, wp_ref, o_ref, pool_ref,
         m_scr, l_scr, acc_scr) = refs[2 * pg:]

        rows = []
        for j in range(pg):
            for u in range(per):
                parts = []
                for kind in range(2):
                    x = bpages[j][pl.ds(u * CMP_BLOCK * 4 + kind, CMP_BLOCK, stride=4), :]
                    parts.append(jnp.sum(x * wp_ref[:, kind * LANE:(kind + 1) * LANE], axis=0, keepdims=True))
                rows.append(jnp.concatenate(parts, axis=1))
        pool_ref[0] = jnp.concatenate(rows, axis=0)
        gi = pl.program_id(1)

        @pl.when(gi == 0)
        def _():
            m_scr[...] = jnp.full(m_scr.shape, NEG, F32)
            l_scr[...] = jnp.zeros(l_scr.shape, F32)
            acc_scr[...] = jnp.zeros(acc_scr.shape, F32)

        lane = lax.broadcasted_iota(jnp.int32, (tp, LANE), 1)

        def qstack(h):
            q = q_ref[0, :, h * LANE:(h + 1) * LANE]
            zero = jnp.zeros_like(q)
            return jnp.concatenate([jnp.where(lane < DQ_A, q, zero),
                                    jnp.where(lane >= DQ_A, q, zero)], axis=0)

        pts = [jnp.swapaxes(pages[j][...].reshape(PAGE, 2 * H_A, LANE), 0, 1) for j in range(pg)]

        def slot(s_):
            return jnp.concatenate([pts[j][s_].astype(BF16) for j in range(pg)], axis=0)

        s = jnp.concatenate([_nt(qstack(h), slot(h)) for h in range(H_A)], axis=0)
        _online_update(s, None, m_scr, l_scr, acc_scr,
                       pv_fn=lambda p: jnp.concatenate(
                           [_mm(p[h * 2 * tp:(h + 1) * 2 * tp], slot(H_A + h)) for h in range(H_A)], axis=0))

        @pl.when(gi == ng - 1)
        def _():
            lam = _diff_lambda(lam_ref[...], lam_init)
            row = _mod(lax.broadcasted_iota(jnp.int32, (2 * tp, PAGE), 0), tp)
            col = lax.broadcasted_iota(jnp.int32, (2 * tp, PAGE), 1)
            mask = (col <= row) & (col < n_valid)
            mask4 = jnp.concatenate([mask] * H_A, axis=0)
            s = jnp.concatenate([_nt(qstack(h), new_ref[0, :, h * LANE:(h + 1) * LANE])
                                 for h in range(H_A)], axis=0)
            _online_update(jnp.where(mask4, s, MASKED), None, m_scr, l_scr, acc_scr,
                           pv_fn=lambda p: jnp.concatenate(
                               [_mm(p[h * 2 * tp:(h + 1) * 2 * tp],
                                    new_ref[0, :, 512 + h * LANE:512 + (h + 1) * LANE])
                                for h in range(H_A)], axis=0))
            for h in range(H_A):
                r0 = h * 2 * tp
                acc = acc_scr[r0:r0 + 2 * tp, :]
                l = l_scr[r0:r0 + 2 * tp, :]
                y = _a_finish(acc[0:tp], l[0:tp], acc[tp:], l[tp:], lam, g_ref[...],
                              z_ref[0, :, h * LANE:(h + 1) * LANE], lam_init)
                o_ref[0, :, h * LANE:(h + 1) * LANE] = y.astype(BF16)

    def page_spec(j):
        return pl.BlockSpec((PAGE * 2 * H_A, LANE),
                            lambda b, g, pt, j=j: (page_base + pt[b, g * pg + j], 0))

    def bpage_spec(j):
        return pl.BlockSpec((PAGE * 4, LANE), lambda b, g, pt, j=j: (page_base + pt[b, g * pg + j], 0))

    in_specs = [page_spec(j) for j in range(pg)] + [bpage_spec(j) for j in range(pg)] + [
            pl.BlockSpec((1, tp, 512), lambda b, g, pt: (b, 0, 0)),
            pl.BlockSpec((1, PAGE, 1024), lambda b, g, pt: (b, 0, 0)),
            pl.BlockSpec((1, tp, 512), lambda b, g, pt: (b, 0, _UNIT["az"] // 4)),
            pl.BlockSpec((4, DQ_A), lambda b, g, pt: (0, 0)),
            pl.BlockSpec((1, LANE), lambda b, g, pt: (0, 0)),
            pl.BlockSpec((CMP_BLOCK, 256), lambda b, g, pt: (0, 0))]
    grid_spec = pltpu.PrefetchScalarGridSpec(
        num_scalar_prefetch=1, grid=(db, ng), in_specs=in_specs,
        out_specs=[pl.BlockSpec((1, tp, 512), lambda b, g, pt: (b, 0, 0)),
                   pl.BlockSpec((1, pg * per, 256), lambda b, g, pt: (b, g, 0))],
        scratch_shapes=[pltpu.VMEM((H_A * 2 * tp, LANE), F32)] * 3)
    return pl.pallas_call(
        body, grid_spec=grid_spec,
        out_shape=[jax.ShapeDtypeStruct((db, tp, 512), BF16),
                   jax.ShapeDtypeStruct((db, n_pages * per, 256), F32)],
        compiler_params=_params("parallel", "arbitrary"),
        name="a_attn_sample")(page_table, *([cache2] * pg), *([cache_b2] * pg), qn, kvnew, proj3,
                              lam_p, out_g.reshape(1, LANE), wpb2)


def _b_prep(proj, g4, wpb, *, tm, pool):
    m = proj.shape[0]
    nb = tm // CMP_BLOCK

    def body(kv_ref, g_ref, wp_ref, bkv_ref, nw_ref, kvb_ref, *pool_refs):
        kc = kv_ref[:, 0:128]
        vc = kv_ref[:, 128:256]
        ks = _rms(kv_ref[:, 256:384], g_ref[2:3, :])
        vs = kv_ref[:, 384:512]
        kw = _rms(kv_ref[:, 512:640], g_ref[3:4, :])
        vw = kv_ref[:, 640:768]
        for slot, x in enumerate((kc, vc, ks, vs)):
            bkv_ref[pl.ds(slot, tm, stride=4), :] = x
        nw_ref[pl.ds(0, tm, stride=2), :] = kw
        nw_ref[pl.ds(1, tm, stride=2), :] = vw
        kvb_ref[:, 0:128] = ks.astype(BF16)
        kvb_ref[:, 128:256] = vs.astype(BF16)
        kvb_ref[:, 256:384] = kw.astype(BF16)
        kvb_ref[:, 384:512] = vw.astype(BF16)
        if pool:
            kcb_ref, vcb_ref = pool_refs
            kp = jnp.sum(kc.reshape(nb, CMP_BLOCK, LANE) * wp_ref[0][None], axis=1)
            vp = jnp.sum(vc.reshape(nb, CMP_BLOCK, LANE) * wp_ref[1][None], axis=1)
            kcb_ref[...] = _rms(kp, g_ref[1:2, :])
            vcb_ref[...] = vp

    row = lambda w: pl.BlockSpec((tm, w), lambda i: (i, 0))
    out_specs = [pl.BlockSpec((tm * 4, LANE), lambda i: (i, 0)),
                 pl.BlockSpec((tm * 2, LANE), lambda i: (i, 0)), row(512)]
    out_shape = [jax.ShapeDtypeStruct((m * 4, LANE), F32), jax.ShapeDtypeStruct((m * 2, LANE), F32),
                 jax.ShapeDtypeStruct((m, 512), BF16)]
    if pool:
        out_specs += [pl.BlockSpec((nb, LANE), lambda i: (i, 0))] * 2
        out_shape += [jax.ShapeDtypeStruct((m // CMP_BLOCK, LANE), F32)] * 2
    return pl.pallas_call(
        body, grid=(m // tm,),
        in_specs=[pl.BlockSpec((tm, 768), lambda i: (i, _UNIT["bkv"] // 6)),
                  pl.BlockSpec((4, LANE), lambda i: (0, 0)),
                  pl.BlockSpec((2, CMP_BLOCK, LANE), lambda i: (0, 0, 0))],
        out_specs=out_specs, out_shape=out_shape,
        compiler_params=_params("parallel"),
        name="b_prep")(proj, g4, wpb)


def _cmp_scores(qf, kcb):
    qh, ql = _split2(qf)
    kh, kl = _split2(kcb)
    return _nt(qh, kh) + _nt(qh, kl) + _nt(ql, kh)


def _b_attn_prompt(proj3, kcb, vcb, kvb, g4, *, t):
    b_ = proj3.shape[0]
    tq = 2 * Q_BLOCK if t % (2 * Q_BLOCK) == 0 else Q_BLOCK
    nblk = t // CMP_BLOCK
    n_top = min(N_SEL, nblk)
    tk = min(512, t)
    span = min(WINDOW + tq, t)
    scale = HEAD ** -0.5 * SCORE_LOG2E

    def body(q_ref, kcb_ref, vcb_ref, kv_ref, bg_ref, bz_ref, g_ref, o_ref, m_scr, l_scr, acc_scr):
        qi = pl.program_id(1)
        qf = jnp.concatenate([_rms(q_ref[0, :, h * LANE:(h + 1) * LANE], g_ref[0:1, :]) * scale
                              for h in range(H_B)], axis=0)
        qb = qf.astype(BF16)
        pos = qi * tq + lax.broadcasted_iota(jnp.int32, (tq, 1), 0)

        pos_l = qi * tq + lax.broadcasted_iota(jnp.int32, (1, tq), 1)
        cur_l = _div(pos_l, CMP_BLOCK)
        blk_s = lax.broadcasted_iota(jnp.int32, (nblk, tq), 0)
        cmask = blk_s < cur_l
        kh, kl = _split2(kcb_ref[0])
        qh, ql = _split2(qf)
        cmask4 = jnp.concatenate([cmask] * H_B, axis=1)
        s = jnp.where(cmask4, _nt(kh, qh) + _nt(kl, qh) + _nt(kh, ql), NEG)
        mx = jnp.max(s, axis=0, keepdims=True)
        p = jnp.where(cmask4, jnp.exp2(s - mx), 0.0)
        pc = p / jnp.maximum(jnp.sum(p, axis=0, keepdims=True), 1e-30)
        pcs = [pc[:, h * tq:(h + 1) * tq] for h in range(H_B)]
        imp = jnp.where(cmask, pcs[0] + pcs[1] + pcs[2] + pcs[3], -1.0)
        fill = LANE - H_B * nblk
        pcq = jnp.concatenate(pcs + ([jnp.zeros((fill, tq), F32)] if fill else []), axis=0).T.astype(BF16)
        vch = vcb_ref[0].astype(BF16)
        zblk = jnp.zeros((nblk, LANE), BF16)
        vdiag = jnp.concatenate(
            [jnp.concatenate([vch if c == h else zblk for c in range(H_B)], axis=1)
             for h in range(H_B)] + ([jnp.zeros((fill, H_B * LANE), BF16)] if fill else []), axis=0)
        o_cmp = _mm(pcq, vdiag)

        rank = jnp.zeros((nblk, tq), F32)
        for mrow in range(nblk):
            cm = imp[mrow:mrow + 1, :]
            ahead = (cm > imp) | ((cm == imp) & (mrow < blk_s))
            rank = rank + ahead.astype(F32)
        sel_t = (((rank < n_top) & (imp >= 0.0)) | (blk_s == cur_l)).astype(F32)
        selq = jnp.concatenate([sel_t, jnp.zeros((LANE - nblk, tq), F32)], axis=0).T.astype(BF16)

        m_scr[...] = jnp.full(m_scr.shape, NEG, F32)
        l_scr[...] = jnp.zeros(l_scr.shape, F32)
        acc_scr[...] = jnp.zeros(acc_scr.shape, F32)
        nk = _div(qi * tq + tq + tk - 1, tk)

        def chunk(c, carry):
            ks_ = pl.ds(pl.multiple_of(c * tk, tk), tk)
            kidx = c * tk + lax.broadcasted_iota(jnp.int32, (LANE, tk), 1)
            e = (_div(kidx, CMP_BLOCK) == lax.broadcasted_iota(jnp.int32, (LANE, tk), 0)).astype(BF16)
            kpos = c * tk + lax.broadcasted_iota(jnp.int32, (tq, tk), 1)
            mk = (_mm(selq, e) > 0.5) & (kpos <= pos)
            mask = jnp.concatenate([mk] * H_B, axis=0)
            s = jnp.where(mask, _nt(qb, kv_ref[0, ks_, 0:128]), MASKED)
            _online_update(s, kv_ref[0, ks_, 128:256], m_scr, l_scr, acc_scr)
            return carry

        lax.fori_loop(0, nk, chunk, 0)

        start = jnp.clip(qi * tq + tq - span, 0, t - span)
        ws = pl.ds(pl.multiple_of(start, tq), span)
        diff = pos - (start + lax.broadcasted_iota(jnp.int32, (tq, span), 1))
        wmask = jnp.concatenate([(diff >= 0) & (diff < WINDOW)] * H_B, axis=0)
        s = jnp.where(wmask, _nt(qb, kv_ref[0, ws, 256:384]), MASKED)
        p = jnp.exp2(s - jnp.maximum(jnp.max(s, axis=-1, keepdims=True), NEG))
        o_win = _mm(p.astype(BF16), kv_ref[0, ws, 384:512]) \
            / jnp.maximum(jnp.sum(p, axis=-1, keepdims=True), 1e-30)
        o_sel = acc_scr[...] / jnp.maximum(l_scr[...], 1e-30)

        gate = _sigmoid(bg_ref[0])
        for h in range(H_B):
            rows = slice(h * tq, (h + 1) * tq)
            ob = (gate[:, 3 * h:3 * h + 1] * o_cmp[:, h * LANE:(h + 1) * LANE]
                  + gate[:, 3 * h + 1:3 * h + 2] * o_sel[rows]
                  + gate[:, 3 * h + 2:3 * h + 3] * o_win[rows])
            y = ob * _silu(bz_ref[0, :, h * LANE:(h + 1) * LANE])
            o_ref[0, :, h * LANE:(h + 1) * LANE] = y.astype(BF16)

    full = lambda rows, w: pl.BlockSpec((1, rows, w), lambda b, i: (b, 0, 0))
    return pl.pallas_call(
        body, grid=(b_, t // tq),
        in_specs=[pl.BlockSpec((1, tq, 512), lambda b, i: (b, i, _UNIT["bq"] // 4)),
                  full(nblk, LANE), full(nblk, LANE), full(t, 512),
                  pl.BlockSpec((1, tq, LANE), lambda b, i: (b, i, _UNIT["bg"])),
                  pl.BlockSpec((1, tq, 512), lambda b, i: (b, i, _UNIT["bz"] // 4)),
                  pl.BlockSpec((4, LANE), lambda b, i: (0, 0))],
        out_specs=pl.BlockSpec((1, tq, 512), lambda b, i: (b, i, 0)),
        out_shape=jax.ShapeDtypeStruct((b_, t, 512), BF16),
        scratch_shapes=[pltpu.VMEM((H_B * tq, LANE), F32)] * 3,
        compiler_params=_params("parallel", "arbitrary"),
        name="b_attn_prompt")(proj3, kcb, vcb, kvb, proj3, proj3, g4)


def _b_select_sample(proj3, pooled, win, neww, g4, *, past_len, n_valid):
    db = neww.shape[0]
    tp = neww.shape[1] // 2
    nblk = pooled.shape[1]
    n_top = min(N_SEL, nblk + 1)
    wb = win.shape[1] // 2
    rows = H_B * tp
    scale = HEAD ** -0.5 * SCORE_LOG2E

    def body(q_ref, pool_ref, win_ref, nw_ref, bg_ref, g_ref, qs_ref, ocw_ref, g1_ref, sel_ref):
        qf = jnp.concatenate(
            [_rms(q_ref[0, :, h * LANE:(h + 1) * LANE], g_ref[0:1, :]) * scale for h in range(H_B)],
            axis=0)
        qb = qf.astype(BF16)
        qs_ref[0] = qb
        tok = _mod(lax.broadcasted_iota(jnp.int32, (rows, 1), 0), tp)
        pos = past_len + tok
        cur = _div(pos, CMP_BLOCK)
        blk = lax.broadcasted_iota(jnp.int32, (rows, nblk), 1)
        cmask = blk < cur

        kc = _rms(pool_ref[0, :, 0:128], g_ref[1:2, :])
        s = jnp.where(cmask, _cmp_scores(qf, kc), NEG)
        mx = jnp.max(s, axis=-1, keepdims=True)
        p = jnp.where(cmask, jnp.exp2(s - mx), 0.0)
        pc = p / jnp.maximum(jnp.sum(p, axis=-1, keepdims=True), 1e-30)
        o_cmp = _mm(pc.astype(BF16), pool_ref[0, :, 128:256].astype(BF16))
        imp = pc[0:tp]
        for h in range(1, H_B):
            imp = imp + pc[h * tp:(h + 1) * tp]
        imp = jnp.where(cmask[0:tp], imp, -1.0)

        pad = jnp.concatenate([imp, jnp.zeros((LANE - tp, nblk), F32)], axis=0)
        imp_t = jnp.concatenate([pad[:, u * LANE:(u + 1) * LANE].T for u in range(nblk // LANE)], axis=0)
        mi = lax.broadcasted_iota(jnp.int32, (nblk, nblk), 0)
        ni = lax.broadcasted_iota(jnp.int32, (nblk, nblk), 1)
        sels = []
        for tkn in range(tp):
            r = imp[tkn:tkn + 1, :]
            c = imp_t[:, tkn:tkn + 1]
            ahead = (c > r) | ((c == r) & (mi < ni))
            rank = jnp.sum(ahead.astype(F32), axis=0, keepdims=True)
            sels.append(((rank < n_top) & (r >= 0.0)).astype(F32))
        sel_ref[0] = jnp.concatenate(sels, axis=0)

        kw = win_ref[0, pl.ds(0, wb, stride=2), :].astype(BF16)
        vw = win_ref[0, pl.ds(1, wb, stride=2), :].astype(BF16)
        zpad = jnp.zeros((LANE - tp, LANE), F32)
        kn = jnp.concatenate([nw_ref[0, pl.ds(0, tp, stride=2), :], zpad], axis=0).astype(BF16)
        vn = jnp.concatenate([nw_ref[0, pl.ds(1, tp, stride=2), :], zpad], axis=0).astype(BF16)
        jw =lax.broadcasted_iota(jnp.int32, (rows, wb), 1)
        dw = pos - (past_len - wb + jw)
        jn = lax.broadcasted_iota(jnp.int32, (rows, LANE), 1)
        dn = tok - jn
        wmask = jnp.concatenate([(dw >= 0) & (dw < WINDOW), (dn >= 0) & (dn < WINDOW) & (jn < n_valid)],
                                axis=1)
        sw = jnp.where(wmask, jnp.concatenate([_nt(qb, kw), _nt(qb, kn)], axis=1), NEG)
        mw = jnp.max(sw, axis=-1, keepdims=True)
        pw = jnp.where(wmask, jnp.exp2(sw - mw), 0.0)
        o_win = (_mm(pw[:, 0:wb].astype(BF16), vw) + _mm(pw[:, wb:].astype(BF16), vn)) \
            / jnp.maximum(jnp.sum(pw, axis=-1, keepdims=True), 1e-30)

        gate = _sigmoid(bg_ref[0])
        g0 = jnp.concatenate([gate[:, 3 * h:3 * h + 1] for h in range(H_B)], axis=0)
        g1 = jnp.concatenate([gate[:, 3 * h + 1:3 * h + 2] for h in range(H_B)], axis=0)
        g2 = jnp.concatenate([gate[:, 3 * h + 2:3 * h + 3] for h in range(H_B)], axis=0)
        ocw_ref[0] = g0 * o_cmp + g2 * o_win
        g1_ref[0] = jnp.broadcast_to(g1, (rows, LANE))

    per_b = lambda r, w: pl.BlockSpec((1, r, w), lambda b: (b, 0, 0))
    return pl.pallas_call(
        body, grid=(db,),
        in_specs=[pl.BlockSpec((1, tp, 512), lambda b: (b, 0, _UNIT["bq"] // 4)),
                  per_b(nblk, 256), per_b(2 * wb, LANE), per_b(2 * tp, LANE),
                  pl.BlockSpec((1, tp, LANE), lambda b: (b, 0, _UNIT["bg"])),
                  pl.BlockSpec((4, LANE), lambda b: (0, 0))],
        out_specs=[per_b(rows, LANE), per_b(rows, LANE), per_b(rows, LANE), per_b(tp, nblk)],
        out_shape=[jax.ShapeDtypeStruct((db, rows, LANE), BF16),
                   jax.ShapeDtypeStruct((db, rows, LANE), F32),
                   jax.ShapeDtypeStruct((db, rows, LANE), F32),
                   jax.ShapeDtypeStruct((db, tp, nblk), F32)],
        compiler_params=_params("parallel"),
        name="b_select_sample")(proj3, pooled, win, neww, proj3, g4)


def _selected_step(gi, ng, pages, q_ref, sel_ref, new_ref, ocw_ref, g1_ref, z_ref, o_ref,
                   m_scr, l_scr, acc_scr, *, n_valid):
    pg = len(pages)
    rows = q_ref.shape[1]
    tp = rows // H_B
    nbg = pg * (PAGE // CMP_BLOCK)
    width = pg * PAGE

    @pl.when(gi == 0)
    def _():
        m_scr[...] = jnp.full(m_scr.shape, NEG, F32)
        l_scr[...] = jnp.zeros(l_scr.shape, F32)
        acc_scr[...] = jnp.zeros(acc_scr.shape, F32)

    q = q_ref[0]

    def slot(s_):
        return jnp.concatenate([pages[j][pl.ds(s_, PAGE, stride=4), :].astype(BF16)
                                for j in range(pg)], axis=0)

    e = (_div(lax.broadcasted_iota(jnp.int32, (nbg, width), 1), CMP_BLOCK)
         == lax.broadcasted_iota(jnp.int32, (nbg, width), 0)).astype(BF16)
    mk = _mm(sel_ref[0, 0].astype(BF16), e) > 0.5
    mask = jnp.concatenate([mk] * H_B, axis=0)
    _online_update(jnp.where(mask, _nt(q, slot(2)), MASKED), slot(3), m_scr, l_scr, acc_scr)

    @pl.when(gi == ng - 1)
    def _():
        zpad = jnp.zeros((PAGE - tp, LANE), F32)
        kn = jnp.concatenate([new_ref[0, pl.ds(2, tp, stride=4), :], zpad], axis=0).astype(BF16)
        vn = jnp.concatenate([new_ref[0, pl.ds(3, tp, stride=4), :], zpad], axis=0).astype(BF16)
        tok = _mod(lax.broadcasted_iota(jnp.int32, (rows, PAGE), 0), tp)
        col = lax.broadcasted_iota(jnp.int32, (rows, PAGE), 1)
        nmask = (col <= tok) & (col < n_valid)
        _online_update(jnp.where(nmask, _nt(q, kn), MASKED), vn, m_scr, l_scr, acc_scr)
        o_sel = acc_scr[...] / jnp.maximum(l_scr[...], 1e-30)
        ob = ocw_ref[0] + g1_ref[0] * o_sel
        for h in range(H_B):
            y = ob[h * tp:(h + 1) * tp] * _silu(z_ref[0, :, h * LANE:(h + 1) * LANE])
            o_ref[0, :, h * LANE:(h + 1) * LANE] = y.astype(BF16)


def _b_selected_sample(page_table, cache2, qs, selg, bkvs, ocw, g1b, proj3, page_base,
                       *, n_valid, pg):
    db, n_pages = page_table.shape
    ng = n_pages // pg
    rows = qs.shape[1]
    tp = rows // H_B
    nbg = pg * (PAGE // CMP_BLOCK)

    def body(pt_ref, *refs):
        _selected_step(pl.program_id(1), ng, refs[:pg], *refs[pg:], n_valid=n_valid)

    def page_spec(j):
        return pl.BlockSpec((PAGE * 4, LANE), lambda b, g, pt, j=j: (page_base + pt[b, g * pg + j], 0))

    per_b = lambda r, w: pl.BlockSpec((1, r, w), lambda b, g, pt: (b, 0, 0))
    grid_spec = pltpu.PrefetchScalarGridSpec(
        num_scalar_prefetch=1, grid=(db, ng),
        in_specs=[page_spec(j) for j in range(pg)] + [
            per_b(rows, LANE),
            pl.BlockSpec((1, 1, tp, nbg), lambda b, g, pt: (b, g, 0, 0)),
            per_b(4 * tp, LANE), per_b(rows, LANE), per_b(rows, LANE),
            pl.BlockSpec((1, tp, 512), lambda b, g, pt: (b, 0, _UNIT["bz"] // 4))],
        out_specs=per_b(tp, 512),
        scratch_shapes=[pltpu.VMEM((rows, LANE), F32)] * 3)
    return pl.pallas_call(
        body, grid_spec=grid_spec,
        out_shape=jax.ShapeDtypeStruct((db, tp, 512), BF16),
        compiler_params=_params("parallel", "arbitrary"),
        name="b_selected_sample")(page_table, *([cache2] * pg), qs, selg, bkvs, ocw, g1b, proj3)


def _mlstm(proj3, conv_w, conv_b, gate_b, out_g, conv0, c0, n0, m0, *, lc, n_valid, ride=None):
    b_, t, _ = proj3.shape
    col_head = jnp.arange(H_C * LANE)[None, :] // LANE
    lane_id = jnp.arange(LANE)[:, None]
    rsel = jnp.stack([lane_id == col_head, lane_id == col_head + H_C]).astype(BF16)
    nchunk = t // lc
    dqk = H_C * DQK_C
    tail = CONV_W - 1
    base = 8
    tsq = max(lc, LANE)

    def bcast_cols(x, sel):
        hi, lo = _split2(x)
        return _mm(hi, sel) + _mm(lo, sel)

    def body(qk_ref, v_ref, if_ref, co_ref, cz_ref, cw_ref, cb_ref, gb_ref, g_ref, rsel_ref,
             conv0_ref, c0_ref, n0_ref, m0_ref,
             y_ref, conv_ref, c_ref, n_ref, m_ref, xbuf, c_scr, n_scr, m_scr):
        ci = pl.program_id(1)

        @pl.when(ci == 0)
        def _():
            xbuf[base - tail:base, :] = conv0_ref[0]
            c_scr[...] = c0_ref[0]
            n_scr[...] = n0_ref[0]
            m_scr[...] = m0_ref[0]

        xbuf[base:base + lc, :] = qk_ref[0]
        acc = cb_ref[...]
        for j in range(CONV_W):
            acc = acc + xbuf[base - tail + j:base - tail + j + lc, :] * cw_ref[j:j + 1, :]
        new_tail = xbuf[base + n_valid - tail:base + n_valid, :]
        xbuf[base - tail:base, :] = new_tail
        conv_ref[0] = new_tail
        qk = _silu(acc)

        gt = if_ref[0] + gb_ref[...]
        lf = jnp.minimum(gt, 0.0) - jnp.log(1.0 + jnp.exp(-jnp.abs(gt)))
        ti = lax.broadcasted_iota(jnp.int32, (lc, lc), 0)
        si = lax.broadcasted_iota(jnp.int32, (lc, lc), 1)
        tri = (si <= ti).astype(BF16)
        l1, l2, l3 = _split3(lf)
        bcum = _mm(tri, l1) + _mm(tri, l2) + _mm(tri, l3)
        zrow = jnp.zeros((tsq - lc, LANE), F32)
        gt_t = (jnp.concatenate([gt, zrow], axis=0) if tsq > lc else gt).T
        b_t = (jnp.concatenate([bcum, zrow], axis=0) if tsq > lc else bcum).T
        dmask = (si <= ti) & (si < n_valid)
        svalid = lax.broadcasted_iota(jnp.int32, (lc, LANE), 0) < n_valid
        icol_all = bcast_cols(gt, rsel_ref[0])
        bcol_all = bcast_cols(bcum, rsel_ref[1])
        gate_all = _sigmoid(co_ref[0]) * _silu(cz_ref[0])
        low = lax.broadcasted_iota(jnp.int32, (lc, LANE), 1) < DQK_C
        row_low = lax.broadcasted_iota(jnp.int32, (LANE, LANE), 0) < DQK_C

        c_new, n_new, m_new_all = [], [], []
        for j in range(H_C // 2):
            qp = qk[:, j * LANE:(j + 1) * LANE] * (DQK_C ** -0.5)
            kp = qk[:, dqk + j * LANE:dqk + (j + 1) * LANE]
            cp = c_scr[j]
            cpb = cp.astype(BF16)
            npair = n_scr[j:j + 1, :]
            c_upd, n_upd, carries = None, None, []
            for u in range(2):
                h = 2 * j + u
                sel = low if u == 0 else jnp.logical_not(low)
                qm = jnp.where(sel, qp, 0.0)
                km = jnp.where(sel, kp, 0.0)
                qmb, kmb = qm.astype(BF16), km.astype(BF16)
                vh = v_ref[0, :, h * HEAD:(h + 1) * HEAD]
                bcol = bcol_all[:, h * LANE:(h + 1) * LANE]
                icol = icol_all[:, h * LANE:(h + 1) * LANE]
                brow = b_t[H_C + h:H_C + h + 1, 0:lc]
                irow = gt_t[h:h + 1, 0:lc]
                m_h = m_scr[h:h + 1, :]

                d = jnp.where(dmask, bcol[:, 0:lc] - brow + irow, NEG)
                inter = bcol + m_h
                m_t = jnp.maximum(inter, jnp.max(d, axis=1, keepdims=True))
                w_intra = jnp.exp(d - m_t[:, 0:lc])
                w_inter = jnp.exp(inter - m_t)
                sqk = _nt(qmb, kmb) * w_intra
                num = w_inter * _mm(qmb, cpb) + _mm(sqk.astype(BF16), vh.astype(BF16))
                den = (w_inter * jnp.sum(qm * npair, axis=1, keepdims=True)
                       + jnp.sum(sqk, axis=1, keepdims=True))
                hh = num / jnp.maximum(jnp.abs(den), jnp.exp(-m_t))
                y_ref[0, :, h * HEAD:(h + 1) * HEAD] = (
                    _rms(hh, g_ref[...]) * gate_all[:, h * HEAD:(h + 1) * HEAD]).astype(BF16)

                b_last = bcol[n_valid - 1:n_valid, :]
                dec = jnp.where(svalid, b_last - bcol + icol, NEG)
                m_new = jnp.maximum(b_last + m_h, jnp.max(dec, axis=0, keepdims=True))
                wk = jnp.exp(dec - m_new)
                carries.append(jnp.exp(b_last + m_h - m_new))
                cu = _tn(kmb, (wk * vh).astype(BF16))
                nu = jnp.sum(wk * km, axis=0, keepdims=True)
                c_upd = cu if c_upd is None else c_upd + cu
                n_upd = nu if n_upd is None else n_upd + nu
                m_new_all.append(m_new)
            c_new.append(jnp.where(row_low, carries[0], carries[1]) * cp + c_upd)
            n_new.append(jnp.where(low[0:1], carries[0], carries[1]) * npair + n_upd)

        for j in range(H_C // 2):
            c_scr[j] = c_new[j]
            n_scr[j:j + 1, :] = n_new[j]
        for h in range(H_C):
            m_scr[h:h + 1, :] = m_new_all[h]
        c_ref[0] = c_scr[...]
        n_ref[0] = n_scr[...]
        m_ref[0] = m_scr[...]

    spec = lambda shape, f: pl.BlockSpec(shape, (lambda b, c, pt: f(b, c)) if ride is not None else f)
    col = lambda u: spec((1, lc, 1024), lambda b, c: (b, c, u))
    const = lambda *shape: spec(shape, lambda b, c: (0,) * len(shape))
    per_b = lambda *shape: spec((1,) + shape, lambda b, c: (b,) + (0,) * len(shape))
    in_specs = [col(_UNIT["cqk"] // 8), col(_UNIT["cv"] // 8),
                spec((1, lc, LANE), lambda b, c: (b, c, _UNIT["cif"])),
                col(_UNIT["co"] // 8), col(_UNIT["cz"] // 8),
                const(CONV_W, 1024), const(1, 1024), const(1, LANE), const(1, LANE),
                const(2, LANE, H_C * LANE),
                per_b(tail, 1024), per_b(H_C // 2, LANE, HEAD), per_b(H_C // 2, LANE), per_b(H_C, LANE)]
    out_specs = [spec((1, lc, 1024), lambda b, c: (b, c, 0)),
                 per_b(tail, 1024), per_b(H_C // 2, LANE, HEAD), per_b(H_C // 2, LANE), per_b(H_C, LANE)]
    out_shape = [jax.ShapeDtypeStruct((b_, t, 1024), BF16),
                 jax.ShapeDtypeStruct((b_, tail, 1024), F32),
                 jax.ShapeDtypeStruct((b_, H_C // 2, LANE, HEAD), F32),
                 jax.ShapeDtypeStruct((b_, H_C // 2, LANE), F32),
                 jax.ShapeDtypeStruct((b_, H_C, LANE), F32)]
    scratch = [pltpu.VMEM((base + lc, 1024), F32), pltpu.VMEM((H_C // 2, LANE, HEAD), F32),
               pltpu.VMEM((H_C // 2, LANE), F32), pltpu.VMEM((H_C, LANE), F32)]
    args = [proj3, proj3, proj3, proj3, proj3, conv_w, conv_b, gate_b, out_g, rsel, conv0, c0, n0, m0]
    if ride is None:
        return pl.pallas_call(
            body, grid=(b_, nchunk), in_specs=in_specs, out_specs=out_specs, out_shape=out_shape,
            scratch_shapes=scratch, compiler_params=_params("parallel", "arbitrary"),
            name="mlstm")(*args)

    page_table, cache2, qs, selg, bkvs, ocw, g1b, sproj3, page_base, r_valid, pg = ride
    db, n_pages = page_table.shape
    ng = n_pages // pg
    assert b_ * nchunk == db * ng
    rrows = qs.shape[1]
    rtp = rrows // H_B
    nbg = pg * (PAGE // CMP_BLOCK)
    n_in, n_out = len(in_specs), len(out_specs)
    step = lambda b, c: b * nchunk + c
    sb = lambda b, c: step(b, c) // ng
    sg = lambda b, c: step(b, c) % ng

    def fused_body(pt_ref, *refs):
        ins, r_ins = refs[:n_in], refs[n_in:n_in + pg + 6]
        outs = refs[n_in + pg + 6:n_in + pg + 6 + n_out]
        r_out = refs[n_in + pg + 6 + n_out]
        scr = refs[n_in + pg + 7 + n_out:]
        _selected_step(sg(pl.program_id(0), pl.program_id(1)), ng, r_ins[:pg], *r_ins[pg:], r_out,
                       *scr[len(scratch):], n_valid=r_valid)
        body(*ins, *outs, *scr[:len(scratch)])

    rb = lambda r, w: pl.BlockSpec((1, r, w), lambda b, c, pt: (sb(b, c), 0, 0))
    in_specs += [pl.BlockSpec((PAGE * 4, LANE),
                              lambda b, c, pt, j=j: (page_base + pt[sb(b, c), sg(b, c) * pg + j], 0))
                 for j in range(pg)]
    in_specs += [rb(rrows, LANE),
                 pl.BlockSpec((1, 1, rtp, nbg), lambda b, c, pt: (sb(b, c), sg(b, c), 0, 0)),
                 rb(4 * rtp, LANE), rb(rrows, LANE), rb(rrows, LANE),
                 pl.BlockSpec((1, rtp, 512), lambda b, c, pt: (sb(b, c), 0, _UNIT["bz"] // 4))]
    grid_spec = pltpu.PrefetchScalarGridSpec(
        num_scalar_prefetch=1, grid=(b_, nchunk), in_specs=in_specs,
        out_specs=out_specs + [rb(rtp, 512)],
        scratch_shapes=scratch + [pltpu.VMEM((rrows, LANE), F32)] * 3)
    return pl.pallas_call(
        fused_body, grid_spec=grid_spec,
        out_shape=out_shape + [jax.ShapeDtypeStruct((db, rtp, 512), BF16)],
        compiler_params=_params("arbitrary", "arbitrary"),
        name="mlstm_with_selected")(page_table, *args, *([cache2] * pg), qs, selg, bkvs, ocw, g1b, sproj3)


def _w_prep(w_in):
    depth, d, n_in = w_in.shape
    kt_n = d // LANE
    per_col = kt_n * depth
    rows = w_in.reshape(depth, kt_n, LANE, n_in).transpose(3, 1, 0, 2).reshape(n_in * per_col, LANE)
    src_off, off = {}, 0
    for name, width in _SRC:
        src_off[name] = (off, width)
        off += width
    assert off == n_in
    starts, valids = [0] * (N_PROJ // LANE), [0] * (N_PROJ // LANE)
    for name in _DST_ORDER:
        s0, width = src_off[name]
        for u in range(-(-width // LANE)):
            starts[_UNIT[name] + u] = s0 + u * LANE
            valids[_UNIT[name] + u] = min(LANE, width - u * LANE)
    table = jnp.array([starts, valids], jnp.int32)

    upb = 4

    def body(tbl_ref, *refs):
        w_refs, o_ref = refs[:upb], refs[upb]
        for k in range(upb):
            keep = lax.broadcasted_iota(jnp.int32, (LANE, LANE), 0) < tbl_ref[1, pl.program_id(0) * upb + k]
            xt = jnp.swapaxes(w_refs[k][...].reshape(LANE, per_col, LANE), 0, 1)
            for l in range(depth):
                for kt in range(kt_n):
                    o_ref[l, k * LANE:(k + 1) * LANE, kt * LANE:(kt + 1) * LANE] = (
                        jnp.where(keep, xt[kt * depth + l], 0.0).astype(BF16))

    def src_spec(k):
        return pl.BlockSpec((pl.Element(LANE * per_col), pl.Element(LANE)),
                            lambda u, tbl, k=k: (tbl[0, u * upb + k] * per_col, 0))

    grid_spec = pltpu.PrefetchScalarGridSpec(
        num_scalar_prefetch=1, grid=(N_PROJ // LANE // upb,),
        in_specs=[src_spec(k) for k in range(upb)],
        out_specs=pl.BlockSpec((depth, upb * LANE, d), lambda u, tbl: (0, u, 0)))
    return pl.pallas_call(
        body, grid_spec=grid_spec,
        out_shape=jax.ShapeDtypeStruct((depth, N_PROJ, d), BF16),
        compiler_params=_params("parallel"),
        name="w_prep")(table, *([rows] * upb))


def _c_from_pairs(c):
    b = c.shape[0]
    return jnp.swapaxes(c.reshape(b, H_C, DQK_C, HEAD), -1, -2)


def _pick(n, prefs):
    for p in prefs:
        if n % p == 0:
            return p
    return n


def kernel(x_prompt, x_sample, cache_a_kv, cache_b_kv, state_b_win, state_c_conv, state_c_C,
           state_c_n, state_c_m, page_table, norm_g, w_in, w_out, a_qk_g, a_lambda, a_out_g,
           b_qk_g, b_cmp_w, c_conv_w, c_conv_b, c_gate_b, c_out_g):
    bp, t, d = x_prompt.shape
    db, ts, _ = x_sample.shape
    depth = norm_g.shape[0]
    n_pool = cache_a_kv.shape[1]
    n_pages = page_table.shape[1]
    past_len = n_pages * PAGE
    wb = state_b_win.shape[2]
    tp = 8
    assert ts <= tp and t % MLSTM_CHUNK == 0 and t % Q_BLOCK == 0 and d == 2048
    mp, ms = bp * t, db * tp
    pg = _pick(n_pages, (16, 8, 4, 2, 1))

    cache_a2 = cache_a_kv.reshape(depth * n_pool * PAGE * 2 * H_A, HEAD)
    cache_b2 = cache_b_kv.reshape(depth * n_pool * PAGE * 4, HEAD)
    win_rows = state_b_win.reshape(depth, db, wb * 2, HEAD)
    hp = x_prompt.reshape(mp, d)
    hs = jnp.pad(x_sample, ((0, 0), (0, tp - ts), (0, 0))).reshape(ms, d)

    tm_p = _pick(mp, (1024, 512, 256, 128))
    tq_a = _pick(t, (512, 256, 128))
    outs_p = [[] for _ in range(7)]
    outs_s = [[] for _ in range(7)]

    w_all = _w_prep(w_in)
    for l in range(depth):
        lam_init = 0.8 - 0.6 * math.exp(-0.3 * l)
        w_o = w_out[l].astype(BF16)
        g_a = jnp.tile(a_qk_g[l], (1, 2))
        g_b = b_qk_g[l]
        wpb = jnp.broadcast_to(b_cmp_w[l][:, :, None], (2, CMP_BLOCK, LANE))
        wpb2 = jnp.concatenate([wpb[0], wpb[1]], axis=1)
        gate_b = jnp.pad(c_gate_b[l].reshape(1, 2 * H_C), ((0, 0), (0, LANE - 2 * H_C)))
        conv_b = c_conv_b[l].reshape(1, -1)
        out_gc = c_out_g[l].reshape(1, LANE)

        proj = _norm_matmul(hp, norm_g[l], w_all, l, tm=tm_p, tn=1024)
        proj3 = proj.reshape(bp, t, N_PROJ)
        qn, kn, vb, akv = _a_prep(proj, g_a, tm=_pick(mp, (512, 256, 128)))
        ya = _a_attn_prompt(qn.reshape(bp, t, 512), kn.reshape(bp, t, 512), vb.reshape(bp, t, 512),
                            proj3, a_lambda[l], a_out_g[l], lam_init, tq=tq_a)
        bkvs, neww, kvb, kcb, vcb = _b_prep(proj, g_b, wpb, tm=_pick(mp, (512, 256, 128)), pool=True)
        nblk = t // CMP_BLOCK
        yb = _b_attn_prompt(proj3, kcb.reshape(bp, nblk, LANE), vcb.reshape(bp, nblk, LANE),
                            kvb.reshape(bp, t, 512), g_b, t=t)

        base = l * n_pool
        sproj = _norm_matmul(hs, norm_g[l], w_all, l, tm=ms, tn=1024)
        sproj3 = sproj.reshape(db, tp, N_PROJ)
        sqn, skn, svb, sakv = _a_prep(sproj, g_a, tm=ms)
        kvnew = jnp.concatenate([skn.reshape(db, tp, 512), svb.reshape(db, tp, 512)], axis=2)
        kvnew = jnp.pad(kvnew, ((0, 0), (0, PAGE - tp), (0, 0)))
        sya, pooled = _a_attn_sample(page_table, cache_a2, sqn.reshape(db, tp, 512), kvnew, sproj3,
                                     a_lambda[l], a_out_g[l], lam_init, base, cache_b2, wpb2,
                                     n_valid=ts, pg=pg)
        sbkvs, sneww, _ = _b_prep(sproj, g_b, wpb, tm=ms, pool=False)
        qs, ocw, g1b, sel = _b_select_sample(
            sproj3, pooled, win_rows[l], sneww.reshape(db, tp * 2, HEAD), g_b,
            past_len=past_len, n_valid=ts)

        p_args = (proj3, c_conv_w[l], conv_b, gate_b, out_gc,
                  jnp.zeros((bp, CONV_W - 1, 2 * H_C * DQK_C), F32), jnp.zeros((bp, H_C // 2, LANE, HEAD), F32),
                  jnp.zeros((bp, H_C // 2, LANE), F32), jnp.zeros((bp, H_C, LANE), F32))
        steps = bp * (t // MLSTM_CHUNK)
        pg_r = db * n_pages // steps if (db * n_pages) % steps == 0 else 0
        rides = 1 <= pg_r <= 16 and n_pages % pg_r == 0
        pg_s = pg_r if rides else pg
        nbg = pg_s * (PAGE // CMP_BLOCK)
        selg = sel.reshape(db, tp, n_pages // pg_s, nbg).transpose(0, 2, 1, 3)
        s_args = (page_table, cache_b2, qs, selg, sbkvs.reshape(db, tp * 4, HEAD), ocw, g1b, sproj3, base)
        if rides:
            yc, conv_p, c_p, n_p, m_p, syb = _mlstm(*p_args, lc=MLSTM_CHUNK, n_valid=MLSTM_CHUNK,
                                                    ride=s_args + (ts, pg_s))
        else:
            yc, conv_p, c_p, n_p, m_p = _mlstm(*p_args, lc=MLSTM_CHUNK, n_valid=MLSTM_CHUNK)
            syb = _b_selected_sample(*s_args, n_valid=ts, pg=pg_s)
        c_p, n_p = _c_from_pairs(c_p), n_p.reshape(bp, H_C, DQK_C)
        hp = _out_proj(hp, ya.reshape(mp, 512), yb.reshape(mp, 512), yc.reshape(mp, 1024), w_o,
                       tm=tm_p, tn=1024)
        win_p = jnp.concatenate([jnp.zeros((bp, wb, 2, HEAD), F32), neww.reshape(bp, t, 2, HEAD)],
                                axis=1)[:, -wb:]
        for lst, a in zip(outs_p, (akv.reshape(bp, t, 2, H_A, HEAD), bkvs.reshape(bp, t, 4, HEAD),
                                   win_p, conv_p, c_p, n_p, m_p[:, :, 0])):
            lst.append(a)

        m0 = jnp.broadcast_to(state_c_m[l][:, :, None], (db, H_C, LANE))
        syc, conv_s, c_s, n_s, m_s = _mlstm(
            sproj3, c_conv_w[l], conv_b, gate_b, out_gc, state_c_conv[l],
            jnp.swapaxes(state_c_C[l], -1, -2).reshape(db, H_C // 2, LANE, HEAD),
            state_c_n[l].reshape(db, H_C // 2, LANE), m0, lc=tp, n_valid=ts)
        c_s, n_s = _c_from_pairs(c_s), n_s.reshape(db, H_C, DQK_C)
        hs = _out_proj(hs, sya.reshape(ms, 512), syb.reshape(ms, 512), syc.reshape(ms, 1024), w_o,
                       tm=ms, tn=1024)
        win_s = jnp.concatenate([state_b_win[l], sneww.reshape(db, tp, 2, HEAD)[:, :ts]], axis=1)[:, -wb:]
        for lst, a in zip(outs_s, (sakv.reshape(db, tp, 2, H_A, HEAD)[:, :ts],
                                   sbkvs.reshape(db, tp, 4, HEAD)[:, :ts],
                                   win_s, conv_s, c_s, n_s, m_s[:, :, 0])):
            lst.append(a)

    y_p = hp.reshape(bp, t, d)
    y_s = hs.reshape(db, tp, d)[:, :ts]
    sp = [jnp.stack(x) for x in outs_p]
    ss = [jnp.stack(x) for x in outs_s]
    return (y_p, y_s, sp[0], ss[0], sp[1], ss[1], sp[2], ss[2], sp[3], ss[3],
            sp[4], ss[4], sp[5], ss[5], sp[6], ss[6])
```

```python
import math

import jax
import jax.numpy as jnp
from jax import lax
from jax.experimental import pallas as pl
from jax.experimental.pallas import tpu as pltpu

F32 = jnp.float32
BF16 = jnp.bfloat16

EPS = 1e-6
LANE = 128
HEAD = 128
H_A = 4
DQ_A = HEAD // 2
H_B = 4
H_C = 8
DQK_C = HEAD // 2
CMP_BLOCK = 64
N_SEL = 16
WINDOW = 512
CONV_W = 4
MLSTM_CHUNK = 128
PAGE = 128
Q_BLOCK = 128
NEG = -1e30
MASKED = 2 * NEG
SCORE_LOG2E = math.log2(math.e)
VMEM_LIMIT = 56 * 1024 * 1024

_SRC = (("aq", 512), ("ak", 512), ("av", 512), ("az", 512),
        ("bq", 512), ("bkv", 768), ("bg", 12), ("bz", 512),
        ("cqk", 1024), ("cv", 1024), ("cif", 16), ("co", 1024), ("cz", 1024))
_DST_ORDER = ("aq", "ak", "av", "az", "bq", "bz", "bkv", "bg", "cif", "cqk", "cv", "co", "cz")
_UNIT = {"aq": 0, "ak": 4, "av": 8, "az": 12, "bq": 16, "bz": 20, "bkv": 24, "bg": 30,
         "cif": 31, "cqk": 32, "cv": 40, "co": 48, "cz": 56}
N_PROJ = 64 * LANE


def _nt(a, b):
    return lax.dot_general(a, b, (((1,), (1,)), ((), ())), preferred_element_type=F32)


def _tn(a, b):
    return lax.dot_general(a, b, (((0,), (0,)), ((), ())), preferred_element_type=F32)


def _mm(a, b):
    return jnp.dot(a, b, preferred_element_type=F32)


def _split2(x):
    hi = x.astype(BF16)
    lo = (x - hi.astype(F32)).astype(BF16)
    return hi, lo


def _split3(x):
    hi = x.astype(BF16)
    r = x - hi.astype(F32)
    mid = r.astype(BF16)
    lo = (r - mid.astype(F32)).astype(BF16)
    return hi, mid, lo


def _sigmoid(z):
    return 1.0 / (1.0 + jnp.exp(-z))


def _silu(z):
    return z * _sigmoid(z)


def _rms(x, g):
    return x * lax.rsqrt(jnp.mean(x * x, axis=-1, keepdims=True) + EPS) * g


def _div(x, n):
    return lax.shift_right_logical(x, int(math.log2(n)))


def _mod(x, n):
    return x & (n - 1)


def _online_update(s, v, m_ref, l_ref, acc_ref, pv_fn=None):
    w = s.shape[1]
    m_old = m_ref[...]
    m_new = jnp.maximum(m_old, jnp.max(s, axis=-1, keepdims=True))
    alpha = jnp.exp2(m_old - m_new)
    m_rep = m_new if w == LANE else jnp.concatenate([m_new] * (w // LANE), axis=1)
    p = jnp.exp2(s - m_rep)
    l_ref[...] = alpha * l_ref[...] + jnp.sum(p, axis=-1, keepdims=True)
    pb = p.astype(BF16)
    acc_ref[...] = alpha * acc_ref[...] + (_mm(pb, v) if pv_fn is None else pv_fn(pb))
    m_ref[...] = m_new


def _params(*sem):
    return pltpu.CompilerParams(dimension_semantics=sem, vmem_limit_bytes=VMEM_LIMIT)


def _norm_matmul(x, g, w, layer, *, tm, tn):
    m, d = x.shape
    n = w.shape[1]
    rc = min(tm, 256)

    def body(x_ref, g_ref, w_ref, o_ref, h_scr):
        @pl.when(pl.program_id(1) == 0)
        def _():
            def chunk(c, carry):
                r = pl.ds(pl.multiple_of(c * rc, rc), rc)
                h_scr[r, :] = _rms(x_ref[r, :], g_ref[...]).astype(BF16)
                return carry
            lax.fori_loop(0, tm // rc, chunk, 0)
        o_ref[...] = _nt(h_scr[...], w_ref[0])

    return pl.pallas_call(
        body, grid=(m // tm, n // tn),
        in_specs=[pl.BlockSpec((tm, d), lambda i, j: (i, 0)),
                  pl.BlockSpec((1, d), lambda i, j: (0, 0)),
                  pl.BlockSpec((1, tn, d), lambda i, j: (layer, j, 0))],
        out_specs=pl.BlockSpec((tm, tn), lambda i, j: (i, j)),
        out_shape=jax.ShapeDtypeStruct((m, n), F32),
        scratch_shapes=[pltpu.VMEM((tm, d), BF16)],
        compiler_params=_params("parallel", "arbitrary"),
        name="norm_matmul")(x, g.reshape(1, d), w)


def _out_proj(x, ya, yb, yc, w, *, tm, tn):
    m, d = x.shape
    da, db, dc = ya.shape[1], yb.shape[1], yc.shape[1]

    def body(x_ref, a_ref, b_ref, c_ref, w_ref, o_ref):
        mix = jnp.concatenate([a_ref[...], b_ref[...], c_ref[...]], axis=1)
        o_ref[...] = x_ref[...] + _mm(mix, w_ref[...])

    return pl.pallas_call(
        body, grid=(m // tm, d // tn),
        in_specs=[pl.BlockSpec((tm, tn), lambda i, j: (i, j)),
                  pl.BlockSpec((tm, da), lambda i, j: (i, 0)),
                  pl.BlockSpec((tm, db), lambda i, j: (i, 0)),
                  pl.BlockSpec((tm, dc), lambda i, j: (i, 0)),
                  pl.BlockSpec((da + db + dc, tn), lambda i, j: (0, j))],
        out_specs=pl.BlockSpec((tm, tn), lambda i, j: (i, j)),
        out_shape=jax.ShapeDtypeStruct((m, d), F32),
        compiler_params=_params("parallel", "parallel"),
        name="out_proj")(x, ya, yb, yc, w)


def _a_prep(proj, g2, *, tm):
    m = proj.shape[0]

    def body(q_ref, k_ref, v_ref, g_ref, qn_ref, kn_ref, vb_ref, akv_ref):
        r = _div(lax.broadcasted_iota(jnp.int32, (LANE, LANE), 0), DQ_A)
        c = _div(lax.broadcasted_iota(jnp.int32, (LANE, LANE), 1), DQ_A)
        seg = (r == c).astype(BF16)

        def segnorm(x, g):
            outs = []
            for u in range(4):
                xc = x[:, u * LANE:(u + 1) * LANE]
                hi, lo = _split2(xc * xc)
                s = _mm(hi, seg) + _mm(lo, seg)
                outs.append(xc * lax.rsqrt(s * (1.0 / DQ_A) + EPS) * g)
            return jnp.concatenate(outs, axis=1)

        qn = segnorm(q_ref[...], g_ref[0:1, :]) * (DQ_A ** -0.5 * SCORE_LOG2E)
        kn = segnorm(k_ref[...], g_ref[1:2, :])
        v = v_ref[...]
        qn_ref[...] = qn.astype(BF16)
        kn_ref[...] = kn.astype(BF16)
        vb_ref[...] = v.astype(BF16)
        for h in range(H_A):
            akv_ref[pl.ds(h, tm, stride=2 * H_A), :] = kn[:, h * LANE:(h + 1) * LANE]
            akv_ref[pl.ds(H_A + h, tm, stride=2 * H_A), :] = v[:, h * LANE:(h + 1) * LANE]

    blk = lambda u: pl.BlockSpec((tm, 512), lambda i, u=u: (i, u))
    row = lambda w: pl.BlockSpec((tm, w), lambda i: (i, 0))
    return pl.pallas_call(
        body, grid=(m // tm,),
        in_specs=[blk(_UNIT["aq"] // 4), blk(_UNIT["ak"] // 4), blk(_UNIT["av"] // 4),
                  pl.BlockSpec((2, LANE), lambda i: (0, 0))],
        out_specs=[row(512), row(512), row(512), pl.BlockSpec((tm * 2 * H_A, LANE), lambda i: (i, 0))],
        out_shape=[jax.ShapeDtypeStruct((m, 512), BF16)] * 3
        + [jax.ShapeDtypeStruct((m * 2 * H_A, LANE), F32)],
        compiler_params=_params("parallel"),
        name="a_prep")(proj, proj, proj, g2)


def _diff_lambda(lp, lam_init):
    a = jnp.sum(lp[0:1, :] * lp[1:2, :], axis=-1, keepdims=True)
    b = jnp.sum(lp[2:3, :] * lp[3:4, :], axis=-1, keepdims=True)
    return jnp.exp(a) - jnp.exp(b) + lam_init


def _a_finish(acc1, l1, acc2, l2, lam, g, z, lam_init):
    o = acc1 / l1 - lam * (acc2 / l2)
    return _rms(o, g) * (1.0 - lam_init) * _silu(z)


def _a_attn_prompt(proj3, g2, lam_p, out_g, lam_init, *, tq):
    b_, t, _ = proj3.shape
    tk = tq
    slots = 2 * H_A

    def body(q_ref, k_ref, v_ref, z_ref, g2_ref, lam_ref, g_ref, o_ref, akv_ref,
             kn_scr, vb_scr, m_scr, l_scr, acc_scr):
        h = pl.program_id(1)
        qi = pl.program_id(2)
        r_ = _div(lax.broadcasted_iota(jnp.int32, (LANE, LANE), 0), DQ_A)
        c_ = _div(lax.broadcasted_iota(jnp.int32, (LANE, LANE), 1), DQ_A)
        seg = (r_ == c_).astype(BF16)

        def segnorm(x, g):
            hi, lo = _split2(x * x)
            return x * lax.rsqrt((_mm(hi, seg) + _mm(lo, seg)) * (1.0 / DQ_A) + EPS) * g

        @pl.when(qi == 0)
        def _():
            def rows(c, carry):
                r = pl.ds(pl.multiple_of(c * tk, tk), tk)
                kn = segnorm(k_ref[0, r, :], g2_ref[1:2, :])
                v = v_ref[0, r, :]
                kn_scr[r, :] = kn.astype(BF16)
                vb_scr[r, :] = v.astype(BF16)
                akv_ref[pl.ds(c * tk * slots + h, tk, stride=slots), :] = kn
                akv_ref[pl.ds(c * tk * slots + H_A + h, tk, stride=slots), :] = v
                return carry
            lax.fori_loop(0, t // tk, rows, 0)

        q = (segnorm(q_ref[0], g2_ref[0:1, :]) * (DQ_A ** -0.5 * SCORE_LOG2E)).astype(BF16)
        lane = lax.broadcasted_iota(jnp.int32, (tq, LANE), 1)
        zero = jnp.zeros_like(q)
        qs = (jnp.where(lane < DQ_A, q, zero), jnp.where(lane >= DQ_A, q, zero))
        m_scr[...] = jnp.full(m_scr.shape, NEG, F32)
        l_scr[...] = jnp.zeros(l_scr.shape, F32)
        acc_scr[...] = jnp.zeros(acc_scr.shape, F32)

        def chunk(kc, mask):
            ks = pl.ds(pl.multiple_of(kc * tk, tk), tk)
            k = kn_scr[ks, :]
            v = vb_scr[ks, :]
            for c in range(2):
                s = _nt(qs[c], k)
                if mask is not None:
                    s = jnp.where(mask, s, MASKED)
                _online_update(s, v, m_scr.at[c], l_scr.at[c], acc_scr.at[c])

        def full_chunk(kc, carry):
            chunk(kc, None)
            return carry

        lax.fori_loop(0, qi, full_chunk, 0)
        chunk(qi, lax.broadcasted_iota(jnp.int32, (tq, tk), 1)
              <= lax.broadcasted_iota(jnp.int32, (tq, tk), 0))
        lam = _diff_lambda(lam_ref[...], lam_init)
        y = _a_finish(acc_scr[0], l_scr[0], acc_scr[1], l_scr[1], lam, g_ref[...], z_ref[0], lam_init)
        o_ref[0] = y.astype(BF16)

    unit = lambda rows, u0, qdep: pl.BlockSpec(
        (1, rows, LANE), (lambda b, h, i: (b, i, u0 + h)) if qdep else (lambda b, h, i: (b, 0, u0 + h)))
    return pl.pallas_call(
        body, grid=(b_, H_A, t // tq),
        in_specs=[unit(tq, _UNIT["aq"], True), unit(t, _UNIT["ak"], False), unit(t, _UNIT["av"], False),
                  unit(tq, _UNIT["az"], True),
                  pl.BlockSpec((2, LANE), lambda b, h, i: (0, 0)),
                  pl.BlockSpec((4, DQ_A), lambda b, h, i: (0, 0)),
                  pl.BlockSpec((1, LANE), lambda b, h, i: (0, 0))],
        out_specs=[pl.BlockSpec((1, tq, LANE), lambda b, h, i: (b, i, h)),
                   pl.BlockSpec((t * slots, LANE), lambda b, h, i: (b, 0))],
        out_shape=[jax.ShapeDtypeStruct((b_, t, 512), BF16),
                   jax.ShapeDtypeStruct((b_ * t * slots, LANE), F32)],
        scratch_shapes=[pltpu.VMEM((t, LANE), BF16)] * 2 + [pltpu.VMEM((2, tq, LANE), F32)] * 3,
        compiler_params=_params("parallel", "arbitrary", "arbitrary"),
        name="a_attn_prompt")(proj3, proj3, proj3, proj3, g2, lam_p, out_g.reshape(1, LANE))


def _a_attn_sample(page_table, cache2, qn, kvnew, proj3, lam_p, out_g, lam_init, page_base,
                   cache_b2, wpb2, *, n_valid, pg):
    db, n_pages = page_table.shape
    tp = qn.shape[1]
    ng = n_pages // pg
    per = PAGE // CMP_BLOCK

    def body(pt_ref, *refs):
        pages = refs[:pg]
        bpages = refs[pg:2 * pg]
        (q_ref, new_ref, z_ref, lam_ref, g_ref, wp_ref, o_ref, pool_ref,
         m_scr, l_scr, acc_scr) = refs[2 * pg:]

        rows = []
        for j in range(pg):
            for u in range(per):
                parts = []
                for kind in range(2):
                    x = bpages[j][pl.ds(u * CMP_BLOCK * 4 + kind, CMP_BLOCK, stride=4), :]
                    parts.append(jnp.sum(x * wp_ref[:, kind * LANE:(kind + 1) * LANE], axis=0, keepdims=True))
                rows.append(jnp.concatenate(parts, axis=1))
        pool_ref[0] = jnp.concatenate(rows, axis=0)
        gi = pl.program_id(1)

        @pl.when(gi == 0)
        def _():
            m_scr[...] = jnp.full(m_scr.shape, NEG, F32)
            l_scr[...] = jnp.zeros(l_scr.shape, F32)
            acc_scr[...] = jnp.zeros(acc_scr.shape, F32)

        lane = lax.broadcasted_iota(jnp.int32, (tp, LANE), 1)

        def qstack(h):
            q = q_ref[0, :, h * LANE:(h + 1) * LANE]
            zero = jnp.zeros_like(q)
            return jnp.concatenate([jnp.where(lane < DQ_A, q, zero),
                                    jnp.where(lane >= DQ_A, q, zero)], axis=0)

        pts = [jnp.swapaxes(pages[j][...].reshape(PAGE, 2 * H_A, LANE), 0, 1) for j in range(pg)]

        def slot(s_):
            return jnp.concatenate([pts[j][s_].astype(BF16) for j in range(pg)], axis=0)

        s = jnp.concatenate([_nt(qstack(h), slot(h)) for h in range(H_A)], axis=0)
        _online_update(s, None, m_scr, l_scr, acc_scr,
                       pv_fn=lambda p: jnp.concatenate(
                           [_mm(p[h * 2 * tp:(h + 1) * 2 * tp], slot(H_A + h)) for h in range(H_A)], axis=0))

        @pl.when(gi == ng - 1)
        def _():
            lam = _diff_lambda(lam_ref[...], lam_init)
            row = _mod(lax.broadcasted_iota(jnp.int32, (2 * tp, PAGE), 0), tp)
            col = lax.broadcasted_iota(jnp.int32, (2 * tp, PAGE), 1)
            mask = (col <= row) & (col < n_valid)
            mask4 = jnp.concatenate([mask] * H_A, axis=0)
            s = jnp.concatenate([_nt(qstack(h), new_ref[0, :, h * LANE:(h + 1) * LANE])
                                 for h in range(H_A)], axis=0)
            _online_update(jnp.where(mask4, s, MASKED), None, m_scr, l_scr, acc_scr,
                           pv_fn=lambda p: jnp.concatenate(
                               [_mm(p[h * 2 * tp:(h + 1) * 2 * tp],
                                    new_ref[0, :, 512 + h * LANE:512 + (h + 1) * LANE])
                                for h in range(H_A)], axis=0))
            for h in range(H_A):
                r0 = h * 2 * tp
                acc = acc_scr[r0:r0 + 2 * tp, :]
                l = l_scr[r0:r0 + 2 * tp, :]
                y = _a_finish(acc[0:tp], l[0:tp], acc[tp:], l[tp:], lam, g_ref[...],
                              z_ref[0, :, h * LANE:(h + 1) * LANE], lam_init)
                o_ref[0, :, h * LANE:(h + 1) * LANE] = y.astype(BF16)

    def page_spec(j):
        return pl.BlockSpec((PAGE * 2 * H_A, LANE),
                            lambda b, g, pt, j=j: (page_base + pt[b, g * pg + j], 0))

    def bpage_spec(j):
        return pl.BlockSpec((PAGE * 4, LANE), lambda b, g, pt, j=j: (page_base + pt[b, g * pg + j], 0))

    in_specs = [page_spec(j) for j in range(pg)] + [bpage_spec(j) for j in range(pg)] + [
            pl.BlockSpec((1, tp, 512), lambda b, g, pt: (b, 0, 0)),
            pl.BlockSpec((1, PAGE, 1024), lambda b, g, pt: (b, 0, 0)),
            pl.BlockSpec((1, tp, 512), lambda b, g, pt: (b, 0, _UNIT["az"] // 4)),
            pl.BlockSpec((4, DQ_A), lambda b, g, pt: (0, 0)),
            pl.BlockSpec((1, LANE), lambda b, g, pt: (0, 0)),
            pl.BlockSpec((CMP_BLOCK, 256), lambda b, g, pt: (0, 0))]
    grid_spec = pltpu.PrefetchScalarGridSpec(
        num_scalar_prefetch=1, grid=(db, ng), in_specs=in_specs,
        out_specs=[pl.BlockSpec((1, tp, 512), lambda b, g, pt: (b, 0, 0)),
                   pl.BlockSpec((1, pg * per, 256), lambda b, g, pt: (b, g, 0))],
        scratch_shapes=[pltpu.VMEM((H_A * 2 * tp, LANE), F32)] * 3)
    return pl.pallas_call(
        body, grid_spec=grid_spec,
        out_shape=[jax.ShapeDtypeStruct((db, tp, 512), BF16),
                   jax.ShapeDtypeStruct((db, n_pages * per, 256), F32)],
        compiler_params=_params("parallel", "arbitrary"),
        name="a_attn_sample")(page_table, *([cache2] * pg), *([cache_b2] * pg), qn, kvnew, proj3,
                              lam_p, out_g.reshape(1, LANE), wpb2)


def _b_prep(proj, g4, wpb, *, tm, pool):
    m = proj.shape[0]
    nb = tm // CMP_BLOCK

    def body(kv_ref, g_ref, wp_ref, bkv_ref, nw_ref, kvb_ref, *pool_refs):
        kc = kv_ref[:, 0:128]
        vc = kv_ref[:, 128:256]
        ks = _rms(kv_ref[:, 256:384], g_ref[2:3, :])
        vs = kv_ref[:, 384:512]
        kw = _rms(kv_ref[:, 512:640], g_ref[3:4, :])
        vw = kv_ref[:, 640:768]
        for slot, x in enumerate((kc, vc, ks, vs)):
            bkv_ref[pl.ds(slot, tm, stride=4), :] = x
        nw_ref[pl.ds(0, tm, stride=2), :] = kw
        nw_ref[pl.ds(1, tm, stride=2), :] = vw
        kvb_ref[:, 0:128] = ks.astype(BF16)
        kvb_ref[:, 128:256] = vs.astype(BF16)
        kvb_ref[:, 256:384] = kw.astype(BF16)
        kvb_ref[:, 384:512] = vw.astype(BF16)
        if pool:
            kcb_ref, vcb_ref = pool_refs
            kp = jnp.sum(kc.reshape(nb, CMP_BLOCK, LANE) * wp_ref[0][None], axis=1)
            vp = jnp.sum(vc.reshape(nb, CMP_BLOCK, LANE) * wp_ref[1][None], axis=1)
            kcb_ref[...] = _rms(kp, g_ref[1:2, :])
            vcb_ref[...] = vp

    row = lambda w: pl.BlockSpec((tm, w), lambda i: (i, 0))
    out_specs = [pl.BlockSpec((tm * 4, LANE), lambda i: (i, 0)),
                 pl.BlockSpec((tm * 2, LANE), lambda i: (i, 0)), row(512)]
    out_shape = [jax.ShapeDtypeStruct((m * 4, LANE), F32), jax.ShapeDtypeStruct((m * 2, LANE), F32),
                 jax.ShapeDtypeStruct((m, 512), BF16)]
    if pool:
        out_specs += [pl.BlockSpec((nb, LANE), lambda i: (i, 0))] * 2
        out_shape += [jax.ShapeDtypeStruct((m // CMP_BLOCK, LANE), F32)] * 2
    return pl.pallas_call(
        body, grid=(m // tm,),
        in_specs=[pl.BlockSpec((tm, 768), lambda i: (i, _UNIT["bkv"] // 6)),
                  pl.BlockSpec((4, LANE), lambda i: (0, 0)),
                  pl.BlockSpec((2, CMP_BLOCK, LANE), lambda i: (0, 0, 0))],
        out_specs=out_specs, out_shape=out_shape,
        compiler_params=_params("parallel"),
        name="b_prep")(proj, g4, wpb)


def _cmp_scores(qf, kcb):
    qh, ql = _split2(qf)
    kh, kl = _split2(kcb)
    return _nt(qh, kh) + _nt(qh, kl) + _nt(ql, kh)


def _b_attn_prompt(proj3, kcb, vcb, kvb, g4, *, t):
    b_ = proj3.shape[0]
    tq = 2 * Q_BLOCK if t % (2 * Q_BLOCK) == 0 else Q_BLOCK
    nblk = t // CMP_BLOCK
    n_top = min(N_SEL, nblk)
    tk = min(512, t)
    span = min(WINDOW + tq, t)
    scale = HEAD ** -0.5 * SCORE_LOG2E

    def body(q_ref, kcb_ref, vcb_ref, kv_ref, bg_ref, bz_ref, g_ref, o_ref, m_scr, l_scr, acc_scr):
        qi = pl.program_id(1)
        qf = jnp.concatenate([_rms(q_ref[0, :, h * LANE:(h + 1) * LANE], g_ref[0:1, :]) * scale
                              for h in range(H_B)], axis=0)
        qb = qf.astype(BF16)
        pos = qi * tq + lax.broadcasted_iota(jnp.int32, (tq, 1), 0)

        pos_l = qi * tq + lax.broadcasted_iota(jnp.int32, (1, tq), 1)
        cur_l = _div(pos_l, CMP_BLOCK)
        blk_s = lax.broadcasted_iota(jnp.int32, (nblk, tq), 0)
        cmask = blk_s < cur_l
        kh, kl = _split2(kcb_ref[0])
        qh, ql = _split2(qf)
        cmask4 = jnp.concatenate([cmask] * H_B, axis=1)
        s = jnp.where(cmask4, _nt(kh, qh) + _nt(kl, qh) + _nt(kh, ql), NEG)
        mx = jnp.max(s, axis=0, keepdims=True)
        p = jnp.where(cmask4, jnp.exp2(s - mx), 0.0)
        pc = p / jnp.maximum(jnp.sum(p, axis=0, keepdims=True), 1e-30)
        pcs = [pc[:, h * tq:(h + 1) * tq] for h in range(H_B)]
        imp = jnp.where(cmask, pcs[0] + pcs[1] + pcs[2] + pcs[3], -1.0)
        fill = LANE - H_B * nblk
        pcq = jnp.concatenate(pcs + ([jnp.zeros((fill, tq), F32)] if fill else []), axis=0).T.astype(BF16)
        vch = vcb_ref[0].astype(BF16)
        zblk = jnp.zeros((nblk, LANE), BF16)
        vdiag = jnp.concatenate(
            [jnp.concatenate([vch if c == h else zblk for c in range(H_B)], axis=1)
             for h in range(H_B)] + ([jnp.zeros((fill, H_B * LANE), BF16)] if fill else []), axis=0)
        o_cmp = _mm(pcq, vdiag)

        rank = jnp.zeros((nblk, tq), F32)
        for mrow in range(nblk):
            cm = imp[mrow:mrow + 1, :]
            ahead = (cm > imp) | ((cm == imp) & (mrow < blk_s))
            rank = rank + ahead.astype(F32)
        sel_t = (((rank < n_top) & (imp >= 0.0)) | (blk_s == cur_l)).astype(F32)
        selq = jnp.concatenate([sel_t, jnp.zeros((LANE - nblk, tq), F32)], axis=0).T.astype(BF16)

        m_scr[...] = jnp.full(m_scr.shape, NEG, F32)
        l_scr[...] = jnp.zeros(l_scr.shape, F32)
        acc_scr[...] = jnp.zeros(acc_scr.shape, F32)
        nk = _div(qi * tq + tq + tk - 1, tk)

        def chunk(c, carry):
            ks_ = pl.ds(pl.multiple_of(c * tk, tk), tk)
            kidx = c * tk + lax.broadcasted_iota(jnp.int32, (LANE, tk), 1)
            e = (_div(kidx, CMP_BLOCK) == lax.broadcasted_iota(jnp.int32, (LANE, tk), 0)).astype(BF16)
            kpos = c * tk + lax.broadcasted_iota(jnp.int32, (tq, tk), 1)
            mk = (_mm(selq, e) > 0.5) & (kpos <= pos)
            mask = jnp.concatenate([mk] * H_B, axis=0)
            s = jnp.where(mask, _nt(qb, kv_ref[0, ks_, 0:128]), MASKED)
            _online_update(s, kv_ref[0, ks_, 128:256], m_scr, l_scr, acc_scr)
            return carry

        lax.fori_loop(0, nk, chunk, 0)

        start = jnp.clip(qi * tq + tq - span, 0, t - span)
        ws = pl.ds(pl.multiple_of(start, tq), span)
        diff = pos - (start + lax.broadcasted_iota(jnp.int32, (tq, span), 1))
        wmask = jnp.concatenate([(diff >= 0) & (diff < WINDOW)] * H_B, axis=0)
        s = jnp.where(wmask, _nt(qb, kv_ref[0, ws, 256:384]), MASKED)
        p = jnp.exp2(s - jnp.maximum(jnp.max(s, axis=-1, keepdims=True), NEG))
        o_win = _mm(p.astype(BF16), kv_ref[0, ws, 384:512]) \
            / jnp.maximum(jnp.sum(p, axis=-1, keepdims=True), 1e-30)
        o_sel = acc_scr[...] / jnp.maximum(l_scr[...], 1e-30)

        gate = _sigmoid(bg_ref[0])
        for h in range(H_B):
            rows = slice(h * tq, (h + 1) * tq)
            ob = (gate[:, 3 * h:3 * h + 1] * o_cmp[:, h * LANE:(h + 1) * LANE]
                  + gate[:, 3 * h + 1:3 * h + 2] * o_sel[rows]
                  + gate[:, 3 * h + 2:3 * h + 3] * o_win[rows])
            y = ob * _silu(bz_ref[0, :, h * LANE:(h + 1) * LANE])
            o_ref[0, :, h * LANE:(h + 1) * LANE] = y.astype(BF16)

    full = lambda rows, w: pl.BlockSpec((1, rows, w), lambda b, i: (b, 0, 0))
    return pl.pallas_call(
        body, grid=(b_, t // tq),
        in_specs=[pl.BlockSpec((1, tq, 512), lambda b, i: (b, i, _UNIT["bq"] // 4)),
                  full(nblk, LANE), full(nblk, LANE), full(t, 512),
                  pl.BlockSpec((1, tq, LANE), lambda b, i: (b, i, _UNIT["bg"])),
                  pl.BlockSpec((1, tq, 512), lambda b, i: (b, i, _UNIT["bz"] // 4)),
                  pl.BlockSpec((4, LANE), lambda b, i: (0, 0))],
        out_specs=pl.BlockSpec((1, tq, 512), lambda b, i: (b, i, 0)),
        out_shape=jax.ShapeDtypeStruct((b_, t, 512), BF16),
        scratch_shapes=[pltpu.VMEM((H_B * tq, LANE), F32)] * 3,
        compiler_params=_params("parallel", "arbitrary"),
        name="b_attn_prompt")(proj3, kcb, vcb, kvb, proj3, proj3, g4)


def _b_select_sample(proj3, pooled, win, neww, g4, *, past_len, n_valid):
    db = neww.shape[0]
    tp = neww.shape[1] // 2
    nblk = pooled.shape[1]
    n_top = min(N_SEL, nblk + 1)
    wb = win.shape[1] // 2
    rows = H_B * tp
    scale = HEAD ** -0.5 * SCORE_LOG2E

    def body(q_ref, pool_ref, win_ref, nw_ref, bg_ref, g_ref, qs_ref, ocw_ref, g1_ref, sel_ref):
        qf = jnp.concatenate(
            [_rms(q_ref[0, :, h * LANE:(h + 1) * LANE], g_ref[0:1, :]) * scale for h in range(H_B)],
            axis=0)
        qb = qf.astype(BF16)
        qs_ref[0] = qb
        tok = _mod(lax.broadcasted_iota(jnp.int32, (rows, 1), 0), tp)
        pos = past_len + tok
        cur = _div(pos, CMP_BLOCK)
        blk = lax.broadcasted_iota(jnp.int32, (rows, nblk), 1)
        cmask = blk < cur

        kc = _rms(pool_ref[0, :, 0:128], g_ref[1:2, :])
        s = jnp.where(cmask, _cmp_scores(qf, kc), NEG)
        mx = jnp.max(s, axis=-1, keepdims=True)
        p = jnp.where(cmask, jnp.exp2(s - mx), 0.0)
        pc = p / jnp.maximum(jnp.sum(p, axis=-1, keepdims=True), 1e-30)
        o_cmp = _mm(pc.astype(BF16), pool_ref[0, :, 128:256].astype(BF16))
        imp = pc[0:tp]
        for h in range(1, H_B):
            imp = imp + pc[h * tp:(h + 1) * tp]
        imp = jnp.where(cmask[0:tp], imp, -1.0)

        pad = jnp.concatenate([imp, jnp.zeros((LANE - tp, nblk), F32)], axis=0)
        imp_t = jnp.concatenate([pad[:, u * LANE:(u + 1) * LANE].T for u in range(nblk // LANE)], axis=0)
        mi = lax.broadcasted_iota(jnp.int32, (nblk, nblk), 0)
        ni = lax.broadcasted_iota(jnp.int32, (nblk, nblk), 1)
        sels = []
        for tkn in range(tp):
            r = imp[tkn:tkn + 1, :]
            c = imp_t[:, tkn:tkn + 1]
            ahead = (c > r) | ((c == r) & (mi < ni))
            rank = jnp.sum(ahead.astype(F32), axis=0, keepdims=True)
            sels.append(((rank < n_top) & (r >= 0.0)).astype(F32))
        sel_ref[0] = jnp.concatenate(sels, axis=0)

        kw = win_ref[0, pl.ds(0, wb, stride=2), :].astype(BF16)
        vw = win_ref[0, pl.ds(1, wb, stride=2), :].astype(BF16)
        zpad = jnp.zeros((LANE - tp, LANE), F32)
        kn = jnp.concatenate([nw_ref[0, pl.ds(0, tp, stride=2), :], zpad], axis=0).astype(BF16)
        vn = jnp.concatenate([nw_ref[0, pl.ds(1, tp, stride=2), :], zpad], axis=0).astype(BF16)
        jw =lax.broadcasted_iota(jnp.int32, (rows, wb), 1)
        dw = pos - (past_len - wb + jw)
        jn = lax.broadcasted_iota(jnp.int32, (rows, LANE), 1)
        dn = tok - jn
        wmask = jnp.concatenate([(dw >= 0) & (dw < WINDOW), (dn >= 0) & (dn < WINDOW) & (jn < n_valid)],
                                axis=1)
        sw = jnp.where(wmask, jnp.concatenate([_nt(qb, kw), _nt(qb, kn)], axis=1), NEG)
        mw = jnp.max(sw, axis=-1, keepdims=True)
        pw = jnp.where(wmask, jnp.exp2(sw - mw), 0.0)
        o_win = (_mm(pw[:, 0:wb].astype(BF16), vw) + _mm(pw[:, wb:].astype(BF16), vn)) \
            / jnp.maximum(jnp.sum(pw, axis=-1, keepdims=True), 1e-30)

        gate = _sigmoid(bg_ref[0])
        g0 = jnp.concatenate([gate[:, 3 * h:3 * h + 1] for h in range(H_B)], axis=0)
        g1 = jnp.concatenate([gate[:, 3 * h + 1:3 * h + 2] for h in range(H_B)], axis=0)
        g2 = jnp.concatenate([gate[:, 3 * h + 2:3 * h + 3] for h in range(H_B)], axis=0)
        ocw_ref[0] = g0 * o_cmp + g2 * o_win
        g1_ref[0] = jnp.broadcast_to(g1, (rows, LANE))

    per_b = lambda r, w: pl.BlockSpec((1, r, w), lambda b: (b, 0, 0))
    return pl.pallas_call(
        body, grid=(db,),
        in_specs=[pl.BlockSpec((1, tp, 512), lambda b: (b, 0, _UNIT["bq"] // 4)),
                  per_b(nblk, 256), per_b(2 * wb, LANE), per_b(2 * tp, LANE),
                  pl.BlockSpec((1, tp, LANE), lambda b: (b, 0, _UNIT["bg"])),
                  pl.BlockSpec((4, LANE), lambda b: (0, 0))],
        out_specs=[per_b(rows, LANE), per_b(rows, LANE), per_b(rows, LANE), per_b(tp, nblk)],
        out_shape=[jax.ShapeDtypeStruct((db, rows, LANE), BF16),
                   jax.ShapeDtypeStruct((db, rows, LANE), F32),
                   jax.ShapeDtypeStruct((db, rows, LANE), F32),
                   jax.ShapeDtypeStruct((db, tp, nblk), F32)],
        compiler_params=_params("parallel"),
        name="b_select_sample")(proj3, pooled, win, neww, proj3, g4)


def _selected_step(gi, ng, pages, q_ref, sel_ref, new_ref, ocw_ref, g1_ref, z_ref, o_ref,
                   m_scr, l_scr, acc_scr, *, n_valid):
    pg = len(pages)
    rows = q_ref.shape[1]
    tp = rows // H_B
    nbg = pg * (PAGE // CMP_BLOCK)
    width = pg * PAGE

    @pl.when(gi == 0)
    def _():
        m_scr[...] = jnp.full(m_scr.shape, NEG, F32)
        l_scr[...] = jnp.zeros(l_scr.shape, F32)
        acc_scr[...] = jnp.zeros(acc_scr.shape, F32)

    q = q_ref[0]

    def slot(s_):
        return jnp.concatenate([pages[j][pl.ds(s_, PAGE, stride=4), :].astype(BF16)
                                for j in range(pg)], axis=0)

    e = (_div(lax.broadcasted_iota(jnp.int32, (nbg, width), 1), CMP_BLOCK)
         == lax.broadcasted_iota(jnp.int32, (nbg, width), 0)).astype(BF16)
    mk = _mm(sel_ref[0, 0].astype(BF16), e) > 0.5
    mask = jnp.concatenate([mk] * H_B, axis=0)
    _online_update(jnp.where(mask, _nt(q, slot(2)), MASKED), slot(3), m_scr, l_scr, acc_scr)

    @pl.when(gi == ng - 1)
    def _():
        zpad = jnp.zeros((PAGE - tp, LANE), F32)
        kn = jnp.concatenate([new_ref[0, pl.ds(2, tp, stride=4), :], zpad], axis=0).astype(BF16)
        vn = jnp.concatenate([new_ref[0, pl.ds(3, tp, stride=4), :], zpad], axis=0).astype(BF16)
        tok = _mod(lax.broadcasted_iota(jnp.int32, (rows, PAGE), 0), tp)
        col = lax.broadcasted_iota(jnp.int32, (rows, PAGE), 1)
        nmask = (col <= tok) & (col < n_valid)
        _online_update(jnp.where(nmask, _nt(q, kn), MASKED), vn, m_scr, l_scr, acc_scr)
        o_sel = acc_scr[...] / jnp.maximum(l_scr[...], 1e-30)
        ob = ocw_ref[0] + g1_ref[0] * o_sel
        for h in range(H_B):
            y = ob[h * tp:(h + 1) * tp] * _silu(z_ref[0, :, h * LANE:(h + 1) * LANE])
            o_ref[0, :, h * LANE:(h + 1) * LANE] = y.astype(BF16)


def _b_selected_sample(page_table, cache2, qs, selg, bkvs, ocw, g1b, proj3, page_base,
                       *, n_valid, pg):
    db, n_pages = page_table.shape
    ng = n_pages // pg
    rows = qs.shape[1]
    tp = rows // H_B
    nbg = pg * (PAGE // CMP_BLOCK)

    def body(pt_ref, *refs):
        _selected_step(pl.program_id(1), ng, refs[:pg], *refs[pg:], n_valid=n_valid)

    def page_spec(j):
        return pl.BlockSpec((PAGE * 4, LANE), lambda b, g, pt, j=j: (page_base + pt[b, g * pg + j], 0))

    per_b = lambda r, w: pl.BlockSpec((1, r, w), lambda b, g, pt: (b, 0, 0))
    grid_spec = pltpu.PrefetchScalarGridSpec(
        num_scalar_prefetch=1, grid=(db, ng),
        in_specs=[page_spec(j) for j in range(pg)] + [
            per_b(rows, LANE),
            pl.BlockSpec((1, 1, tp, nbg), lambda b, g, pt: (b, g, 0, 0)),
            per_b(4 * tp, LANE), per_b(rows, LANE), per_b(rows, LANE),
            pl.BlockSpec((1, tp, 512), lambda b, g, pt: (b, 0, _UNIT["bz"] // 4))],
        out_specs=per_b(tp, 512),
        scratch_shapes=[pltpu.VMEM((rows, LANE), F32)] * 3)
    return pl.pallas_call(
        body, grid_spec=grid_spec,
        out_shape=jax.ShapeDtypeStruct((db, tp, 512), BF16),
        compiler_params=_params("parallel", "arbitrary"),
        name="b_selected_sample")(page_table, *([cache2] * pg), qs, selg, bkvs, ocw, g1b, proj3)


def _mlstm(proj3, conv_w, conv_b, gate_b, out_g, conv0, c0, n0, m0, *, lc, n_valid, ride=None):
    b_, t, _ = proj3.shape
    col_head = jnp.arange(H_C * LANE)[None, :] // LANE
    lane_id = jnp.arange(LANE)[:, None]
    rsel = jnp.stack([lane_id == col_head, lane_id == col_head + H_C]).astype(BF16)
    nchunk = t // lc
    dqk = H_C * DQK_C
    tail = CONV_W - 1
    base = 8
    tsq = max(lc, LANE)

    def bcast_cols(x, sel):
        hi, lo = _split2(x)
        return _mm(hi, sel) + _mm(lo, sel)

    def body(qk_ref, v_ref, if_ref, co_ref, cz_ref, cw_ref, cb_ref, gb_ref, g_ref, rsel_ref,
             conv0_ref, c0_ref, n0_ref, m0_ref,
             y_ref, conv_ref, c_ref, n_ref, m_ref, xbuf, c_scr, n_scr, m_scr):
        ci = pl.program_id(1)

        @pl.when(ci == 0)
        def _():
            xbuf[base - tail:base, :] = conv0_ref[0]
            c_scr[...] = c0_ref[0]
            n_scr[...] = n0_ref[0]
            m_scr[...] = m0_ref[0]

        xbuf[base:base + lc, :] = qk_ref[0]
        acc = cb_ref[...]
        for j in range(CONV_W):
            acc = acc + xbuf[base - tail + j:base - tail + j + lc, :] * cw_ref[j:j + 1, :]
        new_tail = xbuf[base + n_valid - tail:base + n_valid, :]
        xbuf[base - tail:base, :] = new_tail
        conv_ref[0] = new_tail
        qk = _silu(acc)

        gt = if_ref[0] + gb_ref[...]
        lf = jnp.minimum(gt, 0.0) - jnp.log(1.0 + jnp.exp(-jnp.abs(gt)))
        ti = lax.broadcasted_iota(jnp.int32, (lc, lc), 0)
        si = lax.broadcasted_iota(jnp.int32, (lc, lc), 1)
        tri = (si <= ti).astype(BF16)
        l1, l2, l3 = _split3(lf)
        bcum = _mm(tri, l1) + _mm(tri, l2) + _mm(tri, l3)
        zrow = jnp.zeros((tsq - lc, LANE), F32)
        gt_t = (jnp.concatenate([gt, zrow], axis=0) if tsq > lc else gt).T
        b_t = (jnp.concatenate([bcum, zrow], axis=0) if tsq > lc else bcum).T
        dmask = (si <= ti) & (si < n_valid)
        svalid = lax.broadcasted_iota(jnp.int32, (lc, LANE), 0) < n_valid
        icol_all = bcast_cols(gt, rsel_ref[0])
        bcol_all = bcast_cols(bcum, rsel_ref[1])
        gate_all = _sigmoid(co_ref[0]) * _silu(cz_ref[0])
        low = lax.broadcasted_iota(jnp.int32, (lc, LANE), 1) < DQK_C
        row_low = lax.broadcasted_iota(jnp.int32, (LANE, LANE), 0) < DQK_C

        c_new, n_new, m_new_all = [], [], []
        for j in range(H_C // 2):
            qp = qk[:, j * LANE:(j + 1) * LANE] * (DQK_C ** -0.5)
            kp = qk[:, dqk + j * LANE:dqk + (j + 1) * LANE]
            cp = c_scr[j]
            cpb = cp.astype(BF16)
            npair = n_scr[j:j + 1, :]
            c_upd, n_upd, carries = None, None, []
            for u in range(2):
                h = 2 * j + u
                sel = low if u == 0 else jnp.logical_not(low)
                qm = jnp.where(sel, qp, 0.0)
                km = jnp.where(sel, kp, 0.0)
                qmb, kmb = qm.astype(BF16), km.astype(BF16)
                vh = v_ref[0, :, h * HEAD:(h + 1) * HEAD]
                bcol = bcol_all[:, h * LANE:(h + 1) * LANE]
                icol = icol_all[:, h * LANE:(h + 1) * LANE]
                brow = b_t[H_C + h:H_C + h + 1, 0:lc]
                irow = gt_t[h:h + 1, 0:lc]
                m_h = m_scr[h:h + 1, :]

                d = jnp.where(dmask, bcol[:, 0:lc] - brow + irow, NEG)
                inter = bcol + m_h
                m_t = jnp.maximum(inter, jnp.max(d, axis=1, keepdims=True))
                w_intra = jnp.exp(d - m_t[:, 0:lc])
                w_inter = jnp.exp(inter - m_t)
                sqk = _nt(qmb, kmb) * w_intra
                num = w_inter * _mm(qmb, cpb) + _mm(sqk.astype(BF16), vh.astype(BF16))
                den = (w_inter * jnp.sum(qm * npair, axis=1, keepdims=True)
                       + jnp.sum(sqk, axis=1, keepdims=True))
                hh = num / jnp.maximum(jnp.abs(den), jnp.exp(-m_t))
                y_ref[0, :, h * HEAD:(h + 1) * HEAD] = (
                    _rms(hh, g_ref[...]) * gate_all[:, h * HEAD:(h + 1) * HEAD]).astype(BF16)

                b_last = bcol[n_valid - 1:n_valid, :]
                dec = jnp.where(svalid, b_last - bcol + icol, NEG)
                m_new = jnp.maximum(b_last + m_h, jnp.max(dec, axis=0, keepdims=True))
                wk = jnp.exp(dec - m_new)
                carries.append(jnp.exp(b_last + m_h - m_new))
                cu = _tn(kmb, (wk * vh).astype(BF16))
                nu = jnp.sum(wk * km, axis=0, keepdims=True)
                c_upd = cu if c_upd is None else c_upd + cu
                n_upd = nu if n_upd is None else n_upd + nu
                m_new_all.append(m_new)
            c_new.append(jnp.where(row_low, carries[0], carries[1]) * cp + c_upd)
            n_new.append(jnp.where(low[0:1], carries[0], carries[1]) * npair + n_upd)

        for j in range(H_C // 2):
            c_scr[j] = c_new[j]
            n_scr[j:j + 1, :] = n_new[j]
        for h in range(H_C):
            m_scr[h:h + 1, :] = m_new_all[h]
        c_ref[0] = c_scr[...]
        n_ref[0] = n_scr[...]
        m_ref[0] = m_scr[...]

    spec = lambda shape, f: pl.BlockSpec(shape, (lambda b, c, pt: f(b, c)) if ride is not None else f)
    col = lambda u: spec((1, lc, 1024), lambda b, c: (b, c, u))
    const = lambda *shape: spec(shape, lambda b, c: (0,) * len(shape))
    per_b = lambda *shape: spec((1,) + shape, lambda b, c: (b,) + (0,) * len(shape))
    in_specs = [col(_UNIT["cqk"] // 8), col(_UNIT["cv"] // 8),
                spec((1, lc, LANE), lambda b, c: (b, c, _UNIT["cif"])),
                col(_UNIT["co"] // 8), col(_UNIT["cz"] // 8),
                const(CONV_W, 1024), const(1, 1024), const(1, LANE), const(1, LANE),
                const(2, LANE, H_C * LANE),
                per_b(tail, 1024), per_b(H_C // 2, LANE, HEAD), per_b(H_C // 2, LANE), per_b(H_C, LANE)]
    out_specs = [spec((1, lc, 1024), lambda b, c: (b, c, 0)),
                 per_b(tail, 1024), per_b(H_C // 2, LANE, HEAD), per_b(H_C // 2, LANE), per_b(H_C, LANE)]
    out_shape = [jax.ShapeDtypeStruct((b_, t, 1024), BF16),
                 jax.ShapeDtypeStruct((b_, tail, 1024), F32),
                 jax.ShapeDtypeStruct((b_, H_C // 2, LANE, HEAD), F32),
                 jax.ShapeDtypeStruct((b_, H_C // 2, LANE), F32),
                 jax.ShapeDtypeStruct((b_, H_C, LANE), F32)]
    scratch = [pltpu.VMEM((base + lc, 1024), F32), pltpu.VMEM((H_C // 2, LANE, HEAD), F32),
               pltpu.VMEM((H_C // 2, LANE), F32), pltpu.VMEM((H_C, LANE), F32)]
    args = [proj3, proj3, proj3, proj3, proj3, conv_w, conv_b, gate_b, out_g, rsel, conv0, c0, n0, m0]
    if ride is None:
        return pl.pallas_call(
            body, grid=(b_, nchunk), in_specs=in_specs, out_specs=out_specs, out_shape=out_shape,
            scratch_shapes=scratch, compiler_params=_params("parallel", "arbitrary"),
            name="mlstm")(*args)

    page_table, cache2, qs, selg, bkvs, ocw, g1b, sproj3, page_base, r_valid, pg = ride
    db, n_pages = page_table.shape
    ng = n_pages // pg
    assert b_ * nchunk == db * ng
    rrows = qs.shape[1]
    rtp = rrows // H_B
    nbg = pg * (PAGE // CMP_BLOCK)
    n_in, n_out = len(in_specs), len(out_specs)
    step = lambda b, c: b * nchunk + c
    sb = lambda b, c: step(b, c) // ng
    sg = lambda b, c: step(b, c) % ng

    def fused_body(pt_ref, *refs):
        ins, r_ins = refs[:n_in], refs[n_in:n_in + pg + 6]
        outs = refs[n_in + pg + 6:n_in + pg + 6 + n_out]
        r_out = refs[n_in + pg + 6 + n_out]
        scr = refs[n_in + pg + 7 + n_out:]
        _selected_step(sg(pl.program_id(0), pl.program_id(1)), ng, r_ins[:pg], *r_ins[pg:], r_out,
                       *scr[len(scratch):], n_valid=r_valid)
        body(*ins, *outs, *scr[:len(scratch)])

    rb = lambda r, w: pl.BlockSpec((1, r, w), lambda b, c, pt: (sb(b, c), 0, 0))
    in_specs += [pl.BlockSpec((PAGE * 4, LANE),
                              lambda b, c, pt, j=j: (page_base + pt[sb(b, c), sg(b, c) * pg + j], 0))
                 for j in range(pg)]
    in_specs += [rb(rrows, LANE),
                 pl.BlockSpec((1, 1, rtp, nbg), lambda b, c, pt: (sb(b, c), sg(b, c), 0, 0)),
                 rb(4 * rtp, LANE), rb(rrows, LANE), rb(rrows, LANE),
                 pl.BlockSpec((1, rtp, 512), lambda b, c, pt: (sb(b, c), 0, _UNIT["bz"] // 4))]
    grid_spec = pltpu.PrefetchScalarGridSpec(
        num_scalar_prefetch=1, grid=(b_, nchunk), in_specs=in_specs,
        out_specs=out_specs + [rb(rtp, 512)],
        scratch_shapes=scratch + [pltpu.VMEM((rrows, LANE), F32)] * 3)
    return pl.pallas_call(
        fused_body, grid_spec=grid_spec,
        out_shape=out_shape + [jax.ShapeDtypeStruct((db, rtp, 512), BF16)],
        compiler_params=_params("arbitrary", "arbitrary"),
        name="mlstm_with_selected")(page_table, *args, *([cache2] * pg), qs, selg, bkvs, ocw, g1b, sproj3)


def _w_prep(w_in):
    depth, d, n_in = w_in.shape
    kt_n = d // LANE
    per_col = kt_n * depth
    rows = w_in.reshape(depth, kt_n, LANE, n_in).transpose(3, 1, 0, 2).reshape(n_in * per_col, LANE)
    src_off, off = {}, 0
    for name, width in _SRC:
        src_off[name] = (off, width)
        off += width
    assert off == n_in
    starts, valids = [0] * (N_PROJ // LANE), [0] * (N_PROJ // LANE)
    for name in _DST_ORDER:
        s0, width = src_off[name]
        for u in range(-(-width // LANE)):
            starts[_UNIT[name] + u] = s0 + u * LANE
            valids[_UNIT[name] + u] = min(LANE, width - u * LANE)
    table = jnp.array([starts, valids], jnp.int32)

    upb = 4

    def body(tbl_ref, *refs):
        w_refs, o_ref = refs[:upb], refs[upb]
        for k in range(upb):
            keep = lax.broadcasted_iota(jnp.int32, (LANE, LANE), 0) < tbl_ref[1, pl.program_id(0) * upb + k]
            xt = jnp.swapaxes(w_refs[k][...].reshape(LANE, per_col, LANE), 0, 1)
            for l in range(depth):
                for kt in range(kt_n):
                    o_ref[l, k * LANE:(k + 1) * LANE, kt * LANE:(kt + 1) * LANE] = (
                        jnp.where(keep, xt[kt * depth + l], 0.0).astype(BF16))

    def src_spec(k):
        return pl.BlockSpec((pl.Element(LANE * per_col), pl.Element(LANE)),
                            lambda u, tbl, k=k: (tbl[0, u * upb + k] * per_col, 0))

    grid_spec = pltpu.PrefetchScalarGridSpec(
        num_scalar_prefetch=1, grid=(N_PROJ // LANE // upb,),
        in_specs=[src_spec(k) for k in range(upb)],
        out_specs=pl.BlockSpec((depth, upb * LANE, d), lambda u, tbl: (0, u, 0)))
    return pl.pallas_call(
        body, grid_spec=grid_spec,
        out_shape=jax.ShapeDtypeStruct((depth, N_PROJ, d), BF16),
        compiler_params=_params("parallel"),
        name="w_prep")(table, *([rows] * upb))


def _c_from_pairs(c):
    b = c.shape[0]
    return jnp.swapaxes(c.reshape(b, H_C, DQK_C, HEAD), -1, -2)


def _pick(n, prefs):
    for p in prefs:
        if n % p == 0:
            return p
    return n


def kernel(x_prompt, x_sample, cache_a_kv, cache_b_kv, state_b_win, state_c_conv, state_c_C,
           state_c_n, state_c_m, page_table, norm_g, w_in, w_out, a_qk_g, a_lambda, a_out_g,
           b_qk_g, b_cmp_w, c_conv_w, c_conv_b, c_gate_b, c_out_g):
    bp, t, d = x_prompt.shape
    db, ts, _ = x_sample.shape
    depth = norm_g.shape[0]
    n_pool = cache_a_kv.shape[1]
    n_pages = page_table.shape[1]
    past_len = n_pages * PAGE
    wb = state_b_win.shape[2]
    tp = 8
    assert ts <= tp and t % MLSTM_CHUNK == 0 and t % Q_BLOCK == 0 and d == 2048
    mp, ms = bp * t, db * tp
    pg = _pick(n_pages, (16, 8, 4, 2, 1))

    cache_a2 = cache_a_kv.reshape(depth * n_pool * PAGE * 2 * H_A, HEAD)
    cache_b2 = cache_b_kv.reshape(depth * n_pool * PAGE * 4, HEAD)
    win_rows = state_b_win.reshape(depth, db, wb * 2, HEAD)
    hp = x_prompt.reshape(mp, d)
    hs = jnp.pad(x_sample, ((0, 0), (0, tp - ts), (0, 0))).reshape(ms, d)

    tm_p = _pick(mp, (1024, 512, 256, 128))
    tq_a = _pick(t, (512, 256, 128))
    outs_p = [[] for _ in range(7)]
    outs_s = [[] for _ in range(7)]

    w_all = _w_prep(w_in)
    for l in range(depth):
        lam_init = 0.8 - 0.6 * math.exp(-0.3 * l)
        w_o = w_out[l].astype(BF16)
        g_a = jnp.tile(a_qk_g[l], (1, 2))
        g_b = b_qk_g[l]
        wpb = jnp.broadcast_to(b_cmp_w[l][:, :, None], (2, CMP_BLOCK, LANE))
        wpb2 = jnp.concatenate([wpb[0], wpb[1]], axis=1)
        gate_b = jnp.pad(c_gate_b[l].reshape(1, 2 * H_C), ((0, 0), (0, LANE - 2 * H_C)))
        conv_b = c_conv_b[l].reshape(1, -1)
        out_gc = c_out_g[l].reshape(1, LANE)

        proj = _norm_matmul(hp, norm_g[l], w_all, l, tm=tm_p, tn=1024)
        proj3 = proj.reshape(bp, t, N_PROJ)
        ya, akv = _a_attn_prompt(proj3, g_a, a_lambda[l], a_out_g[l], lam_init, tq=tq_a)
        bkvs, neww, kvb, kcb, vcb = _b_prep(proj, g_b, wpb, tm=_pick(mp, (512, 256, 128)), pool=True)
        nblk = t // CMP_BLOCK
        yb = _b_attn_prompt(proj3, kcb.reshape(bp, nblk, LANE), vcb.reshape(bp, nblk, LANE),
                            kvb.reshape(bp, t, 512), g_b, t=t)

        base = l * n_pool
        sproj = _norm_matmul(hs, norm_g[l], w_all, l, tm=ms, tn=1024)
        sproj3 = sproj.reshape(db, tp, N_PROJ)
        sqn, skn, svb, sakv = _a_prep(sproj, g_a, tm=ms)
        kvnew = jnp.concatenate([skn.reshape(db, tp, 512), svb.reshape(db, tp, 512)], axis=2)
        kvnew = jnp.pad(kvnew, ((0, 0), (0, PAGE - tp), (0, 0)))
        sya, pooled = _a_attn_sample(page_table, cache_a2, sqn.reshape(db, tp, 512), kvnew, sproj3,
                                     a_lambda[l], a_out_g[l], lam_init, base, cache_b2, wpb2,
                                     n_valid=ts, pg=pg)
        sbkvs, sneww, _ = _b_prep(sproj, g_b, wpb, tm=ms, pool=False)
        qs, ocw, g1b, sel = _b_select_sample(
            sproj3, pooled, win_rows[l], sneww.reshape(db, tp * 2, HEAD), g_b,
            past_len=past_len, n_valid=ts)

        p_args = (proj3, c_conv_w[l], conv_b, gate_b, out_gc,
                  jnp.zeros((bp, CONV_W - 1, 2 * H_C * DQK_C), F32), jnp.zeros((bp, H_C // 2, LANE, HEAD), F32),
                  jnp.zeros((bp, H_C // 2, LANE), F32), jnp.zeros((bp, H_C, LANE), F32))
        steps = bp * (t // MLSTM_CHUNK)
        pg_r = db * n_pages // steps if (db * n_pages) % steps == 0 else 0
        rides = 1 <= pg_r <= 16 and n_pages % pg_r == 0
        pg_s = pg_r if rides else pg
        nbg = pg_s * (PAGE // CMP_BLOCK)
        selg = sel.reshape(db, tp, n_pages // pg_s, nbg).transpose(0, 2, 1, 3)
        s_args = (page_table, cache_b2, qs, selg, sbkvs.reshape(db, tp * 4, HEAD), ocw, g1b, sproj3, base)
        if rides:
            yc, conv_p, c_p, n_p, m_p, syb = _mlstm(*p_args, lc=MLSTM_CHUNK, n_valid=MLSTM_CHUNK,
                                                    ride=s_args + (ts, pg_s))
        else:
            yc, conv_p, c_p, n_p, m_p = _mlstm(*p_args, lc=MLSTM_CHUNK, n_valid=MLSTM_CHUNK)
            syb = _b_selected_sample(*s_args, n_valid=ts, pg=pg_s)
        c_p, n_p = _c_from_pairs(c_p), n_p.reshape(bp, H_C, DQK_C)
        hp = _out_proj(hp, ya.reshape(mp, 512), yb.reshape(mp, 512), yc.reshape(mp, 1024), w_o,
                       tm=tm_p, tn=1024)
        win_p = jnp.concatenate([jnp.zeros((bp, wb, 2, HEAD), F32), neww.reshape(bp, t, 2, HEAD)],
                                axis=1)[:, -wb:]
        for lst, a in zip(outs_p, (akv.reshape(bp, t, 2, H_A, HEAD), bkvs.reshape(bp, t, 4, HEAD),
                                   win_p, conv_p, c_p, n_p, m_p[:, :, 0])):
            lst.append(a)

        m0 = jnp.broadcast_to(state_c_m[l][:, :, None], (db, H_C, LANE))
        syc, conv_s, c_s, n_s, m_s = _mlstm(
            sproj3, c_conv_w[l], conv_b, gate_b, out_gc, state_c_conv[l],
            jnp.swapaxes(state_c_C[l], -1, -2).reshape(db, H_C // 2, LANE, HEAD),
            state_c_n[l].reshape(db, H_C // 2, LANE), m0, lc=tp, n_valid=ts)
        c_s, n_s = _c_from_pairs(c_s), n_s.reshape(db, H_C, DQK_C)
        hs = _out_proj(hs, sya.reshape(ms, 512), syb.reshape(ms, 512), syc.reshape(ms, 1024), w_o,
                       tm=ms, tn=1024)
        win_s = jnp.concatenate([state_b_win[l], sneww.reshape(db, tp, 2, HEAD)[:, :ts]], axis=1)[:, -wb:]
        for lst, a in zip(outs_s, (sakv.reshape(db, tp, 2, H_A, HEAD)[:, :ts],
                                   sbkvs.reshape(db, tp, 4, HEAD)[:, :ts],
                                   win_s, conv_s, c_s, n_s, m_s[:, :, 0])):
            lst.append(a)

    y_p = hp.reshape(bp, t, d)
    y_s = hs.reshape(db, tp, d)[:, :ts]
    sp = [jnp.stack(x) for x in outs_p]
    ss = [jnp.stack(x) for x in outs_s]
    return (y_p, y_s, sp[0], ss[0], sp[1], ss[1], sp[2], ss[2], sp[3], ss[3],
            sp[4], ss[4], sp[5], ss[5], sp[6], ss[6])
```

```python
import math

import jax
import jax.numpy as jnp
from jax import lax
from jax.experimental import pallas as pl
from jax.experimental.pallas import tpu as pltpu

F32 = jnp.float32
BF16 = jnp.bfloat16

EPS = 1e-6
LANE = 128
HEAD = 128
H_A = 4
DQ_A = HEAD // 2
H_B = 4
H_C = 8
DQK_C = HEAD // 2
CMP_BLOCK = 64
N_SEL = 16
WINDOW = 512
CONV_W = 4
MLSTM_CHUNK = 128
PAGE = 128
Q_BLOCK = 128
NEG = -1e30
MASKED = 2 * NEG
SCORE_LOG2E = math.log2(math.e)
VMEM_LIMIT = 56 * 1024 * 1024

_SRC = (("aq", 512), ("ak", 512), ("av", 512), ("az", 512),
        ("bq", 512), ("bkv", 768), ("bg", 12), ("bz", 512),
        ("cqk", 1024), ("cv", 1024), ("cif", 16), ("co", 1024), ("cz", 1024))
_DST_ORDER = ("aq", "ak", "av", "az", "bq", "bz", "bkv", "bg", "cif", "cqk", "cv", "co", "cz")
_UNIT = {"aq": 0, "ak": 4, "av": 8, "az": 12, "bq": 16, "bz": 20, "bkv": 24, "bg": 30,
         "cif": 31, "cqk": 32, "cv": 40, "co": 48, "cz": 56}
N_PROJ = 64 * LANE


def _nt(a, b):
    return lax.dot_general(a, b, (((1,), (1,)), ((), ())), preferred_element_type=F32)


def _tn(a, b):
    return lax.dot_general(a, b, (((0,), (0,)), ((), ())), preferred_element_type=F32)


def _mm(a, b):
    return jnp.dot(a, b, preferred_element_type=F32)


def _split2(x):
    hi = x.astype(BF16)
    lo = (x - hi.astype(F32)).astype(BF16)
    return hi, lo


def _split3(x):
    hi = x.astype(BF16)
    r = x - hi.astype(F32)
    mid = r.astype(BF16)
    lo = (r - mid.astype(F32)).astype(BF16)
    return hi, mid, lo


def _sigmoid(z):
    return 1.0 / (1.0 + jnp.exp(-z))


def _silu(z):
    return z * _sigmoid(z)


def _rms(x, g):
    return x * lax.rsqrt(jnp.mean(x * x, axis=-1, keepdims=True) + EPS) * g


def _div(x, n):
    return lax.shift_right_logical(x, int(math.log2(n)))


def _mod(x, n):
    return x & (n - 1)


def _online_update(s, v, m_ref, l_ref, acc_ref, pv_fn=None):
    w = s.shape[1]
    m_old = m_ref[...]
    m_new = jnp.maximum(m_old, jnp.max(s, axis=-1, keepdims=True))
    alpha = jnp.exp2(m_old - m_new)
    m_rep = m_new if w == LANE else jnp.concatenate([m_new] * (w // LANE), axis=1)
    p = jnp.exp2(s - m_rep)
    l_ref[...] = alpha * l_ref[...] + jnp.sum(p, axis=-1, keepdims=True)
    pb = p.astype(BF16)
    acc_ref[...] = alpha * acc_ref[...] + (_mm(pb, v) if pv_fn is None else pv_fn(pb))
    m_ref[...] = m_new


def _params(*sem):
    return pltpu.CompilerParams(dimension_semantics=sem, vmem_limit_bytes=VMEM_LIMIT)


def _norm_matmul(x, g, w, layer, *, tm, tn):
    m, d = x.shape
    n = w.shape[1]
    rc = min(tm, 256)

    def body(x_ref, g_ref, w_ref, o_ref, h_scr):
        @pl.when(pl.program_id(1) == 0)
        def _():
            def chunk(c, carry):
                r = pl.ds(pl.multiple_of(c * rc, rc), rc)
                h_scr[r, :] = _rms(x_ref[r, :], g_ref[...]).astype(BF16)
                return carry
            lax.fori_loop(0, tm // rc, chunk, 0)
        o_ref[...] = _nt(h_scr[...], w_ref[0])

    return pl.pallas_call(
        body, grid=(m // tm, n // tn),
        in_specs=[pl.BlockSpec((tm, d), lambda i, j: (i, 0)),
                  pl.BlockSpec((1, d), lambda i, j: (0, 0)),
                  pl.BlockSpec((1, tn, d), lambda i, j: (layer, j, 0))],
        out_specs=pl.BlockSpec((tm, tn), lambda i, j: (i, j)),
        out_shape=jax.ShapeDtypeStruct((m, n), F32),
        scratch_shapes=[pltpu.VMEM((tm, d), BF16)],
        compiler_params=_params("parallel", "arbitrary"),
        name="norm_matmul")(x, g.reshape(1, d), w)


def _out_proj(x, ya, yb, yc, w, *, tm, tn):
    m, d = x.shape
    da, db, dc = ya.shape[1], yb.shape[1], yc.shape[1]

    def body(x_ref, a_ref, b_ref, c_ref, w_ref, o_ref):
        mix = jnp.concatenate([a_ref[...], b_ref[...], c_ref[...]], axis=1)
        o_ref[...] = x_ref[...] + _mm(mix, w_ref[...])

    return pl.pallas_call(
        body, grid=(m // tm, d // tn),
        in_specs=[pl.BlockSpec((tm, tn), lambda i, j: (i, j)),
                  pl.BlockSpec((tm, da), lambda i, j: (i, 0)),
                  pl.BlockSpec((tm, db), lambda i, j: (i, 0)),
                  pl.BlockSpec((tm, dc), lambda i, j: (i, 0)),
                  pl.BlockSpec((da + db + dc, tn), lambda i, j: (0, j))],
        out_specs=pl.BlockSpec((tm, tn), lambda i, j: (i, j)),
        out_shape=jax.ShapeDtypeStruct((m, d), F32),
        compiler_params=_params("parallel", "parallel"),
        name="out_proj")(x, ya, yb, yc, w)


def _a_prep(proj, g2, *, tm):
    m = proj.shape[0]

    def body(q_ref, k_ref, v_ref, g_ref, qn_ref, kn_ref, vb_ref, akv_ref):
        r = _div(lax.broadcasted_iota(jnp.int32, (LANE, LANE), 0), DQ_A)
        c = _div(lax.broadcasted_iota(jnp.int32, (LANE, LANE), 1), DQ_A)
        seg = (r == c).astype(BF16)

        def segnorm(x, g):
            outs = []
            for u in range(4):
                xc = x[:, u * LANE:(u + 1) * LANE]
                hi, lo = _split2(xc * xc)
                s = _mm(hi, seg) + _mm(lo, seg)
                outs.append(xc * lax.rsqrt(s * (1.0 / DQ_A) + EPS) * g)
            return jnp.concatenate(outs, axis=1)

        qn = segnorm(q_ref[...], g_ref[0:1, :]) * (DQ_A ** -0.5 * SCORE_LOG2E)
        kn = segnorm(k_ref[...], g_ref[1:2, :])
        v = v_ref[...]
        qn_ref[...] = qn.astype(BF16)
        kn_ref[...] = kn.astype(BF16)
        vb_ref[...] = v.astype(BF16)
        for h in range(H_A):
            akv_ref[pl.ds(h, tm, stride=2 * H_A), :] = kn[:, h * LANE:(h + 1) * LANE]
            akv_ref[pl.ds(H_A + h, tm, stride=2 * H_A), :] = v[:, h * LANE:(h + 1) * LANE]

    blk = lambda u: pl.BlockSpec((tm, 512), lambda i, u=u: (i, u))
    row = lambda w: pl.BlockSpec((tm, w), lambda i: (i, 0))
    return pl.pallas_call(
        body, grid=(m // tm,),
        in_specs=[blk(_UNIT["aq"] // 4), blk(_UNIT["ak"] // 4), blk(_UNIT["av"] // 4),
                  pl.BlockSpec((2, LANE), lambda i: (0, 0))],
        out_specs=[row(512), row(512), row(512), pl.BlockSpec((tm * 2 * H_A, LANE), lambda i: (i, 0))],
        out_shape=[jax.ShapeDtypeStruct((m, 512), BF16)] * 3
        + [jax.ShapeDtypeStruct((m * 2 * H_A, LANE), F32)],
        compiler_params=_params("parallel"),
        name="a_prep")(proj, proj, proj, g2)


def _diff_lambda(lp, lam_init):
    a = jnp.sum(lp[0:1, :] * lp[1:2, :], axis=-1, keepdims=True)
    b = jnp.sum(lp[2:3, :] * lp[3:4, :], axis=-1, keepdims=True)
    return jnp.exp(a) - jnp.exp(b) + lam_init


def _a_finish(acc1, l1, acc2, l2, lam, g, z, lam_init):
    o = acc1 / l1 - lam * (acc2 / l2)
    return _rms(o, g) * (1.0 - lam_init) * _silu(z)


def _a_attn_prompt(proj3, g2, lam_p, out_g, lam_init, *, tq):
    b_, t, _ = proj3.shape
    tk = tq
    slots = 2 * H_A

    def body(q_ref, k_ref, v_ref, z_ref, g2_ref, lam_ref, g_ref, o_ref, akv_ref,
             kn_scr, vb_scr, m_scr, l_scr, acc_scr):
        h = pl.program_id(1)
        qi = pl.program_id(2)
        r_ = _div(lax.broadcasted_iota(jnp.int32, (LANE, LANE), 0), DQ_A)
        c_ = _div(lax.broadcasted_iota(jnp.int32, (LANE, LANE), 1), DQ_A)
        seg = (r_ == c_).astype(BF16)

        def segnorm(x, g):
            hi, lo = _split2(x * x)
            return x * lax.rsqrt((_mm(hi, seg) + _mm(lo, seg)) * (1.0 / DQ_A) + EPS) * g

        @pl.when(qi == 0)
        def _():
            def rows(c, carry):
                r = pl.ds(pl.multiple_of(c * tk, tk), tk)
                kn = segnorm(k_ref[0, r, :], g2_ref[1:2, :])
                v = v_ref[0, r, :]
                kn_scr[r, :] = kn.astype(BF16)
                vb_scr[r, :] = v.astype(BF16)
                akv_ref[pl.ds(c * tk * slots + h, tk, stride=slots), :] = kn
                akv_ref[pl.ds(c * tk * slots + H_A + h, tk, stride=slots), :] = v
                return carry
            lax.fori_loop(0, t // tk, rows, 0)

        q = (segnorm(q_ref[0], g2_ref[0:1, :]) * (DQ_A ** -0.5 * SCORE_LOG2E)).astype(BF16)
        lane = lax.broadcasted_iota(jnp.int32, (tq, LANE), 1)
        zero = jnp.zeros_like(q)
        qs = (jnp.where(lane < DQ_A, q, zero), jnp.where(lane >= DQ_A, q, zero))
        m_scr[...] = jnp.full(m_scr.shape, NEG, F32)
        l_scr[...] = jnp.zeros(l_scr.shape, F32)
        acc_scr[...] = jnp.zeros(acc_scr.shape, F32)

        def chunk(kc, mask):
            ks = pl.ds(pl.multiple_of(kc * tk, tk), tk)
            k = kn_scr[ks, :]
            v = vb_scr[ks, :]
            for c in range(2):
                s = _nt(qs[c], k)
                if mask is not None:
                    s = jnp.where(mask, s, MASKED)
                _online_update(s, v, m_scr.at[c], l_scr.at[c], acc_scr.at[c])

        def full_chunk(kc, carry):
            chunk(kc, None)
            return carry

        lax.fori_loop(0, qi, full_chunk, 0)
        chunk(qi, lax.broadcasted_iota(jnp.int32, (tq, tk), 1)
              <= lax.broadcasted_iota(jnp.int32, (tq, tk), 0))
        lam = _diff_lambda(lam_ref[...], lam_init)
        y = _a_finish(acc_scr[0], l_scr[0], acc_scr[1], l_scr[1], lam, g_ref[...], z_ref[0], lam_init)
        o_ref[0] = y.astype(BF16)

    unit = lambda rows, u0, qdep: pl.BlockSpec(
        (1, rows, LANE), (lambda b, h, i: (b, i, u0 + h)) if qdep else (lambda b, h, i: (b, 0, u0 + h)))
    return pl.pallas_call(
        body, grid=(b_, H_A, t // tq),
        in_specs=[unit(tq, _UNIT["aq"], True), unit(t, _UNIT["ak"], False), unit(t, _UNIT["av"], False),
                  unit(tq, _UNIT["az"], True),
                  pl.BlockSpec((2, LANE), lambda b, h, i: (0, 0)),
                  pl.BlockSpec((4, DQ_A), lambda b, h, i: (0, 0)),
                  pl.BlockSpec((1, LANE), lambda b, h, i: (0, 0))],
        out_specs=[pl.BlockSpec((1, tq, LANE), lambda b, h, i: (b, i, h)),
                   pl.BlockSpec((t * slots, LANE), lambda b, h, i: (b, 0))],
        out_shape=[jax.ShapeDtypeStruct((b_, t, 512), BF16),
                   jax.ShapeDtypeStruct((b_ * t * slots, LANE), F32)],
        scratch_shapes=[pltpu.VMEM((t, LANE), BF16)] * 2 + [pltpu.VMEM((2, tq, LANE), F32)] * 3,
        compiler_params=_params("parallel", "arbitrary", "arbitrary"),
        name="a_attn_prompt")(proj3, proj3, proj3, proj3, g2, lam_p, out_g.reshape(1, LANE))


def _a_attn_sample(page_table, cache2, qn, kvnew, proj3, lam_p, out_g, lam_init, page_base,
                   cache_b2, wpb2, *, n_valid, pg):
    db, n_pages = page_table.shape
    tp = qn.shape[1]
    ng = n_pages // pg
    per = PAGE // CMP_BLOCK

    def body(pt_ref, *refs):
        pages = refs[:pg]
        bpages = refs[pg:2 * pg]
        (q_ref, new_ref, z_ref, lam_ref, g_ref, wp_ref, o_ref, pool_ref,
         m_scr, l_scr, acc_scr) = refs[2 * pg:]

        rows = []
        for j in range(pg):
            for u in range(per):
                parts = []
                for kind in range(2):
                    x = bpages[j][pl.ds(u * CMP_BLOCK * 4 + kind, CMP_BLOCK, stride=4), :]
                    parts.append(jnp.sum(x * wp_ref[:, kind * LANE:(kind + 1) * LANE], axis=0, keepdims=True))
                rows.append(jnp.concatenate(parts, axis=1))
        pool_ref[0] = jnp.concatenate(rows, axis=0)
        gi = pl.program_id(1)

        @pl.when(gi == 0)
        def _():
            m_scr[...] = jnp.full(m_scr.shape, NEG, F32)
            l_scr[...] = jnp.zeros(l_scr.shape, F32)
            acc_scr[...] = jnp.zeros(acc_scr.shape, F32)

        lane = lax.broadcasted_iota(jnp.int32, (tp, LANE), 1)

        def qstack(h):
            q = q_ref[0, :, h * LANE:(h + 1) * LANE]
            zero = jnp.zeros_like(q)
            return jnp.concatenate([jnp.where(lane < DQ_A, q, zero),
                                    jnp.where(lane >= DQ_A, q, zero)], axis=0)

        pts = [jnp.swapaxes(pages[j][...].reshape(PAGE, 2 * H_A, LANE), 0, 1) for j in range(pg)]

        def slot(s_):
            return jnp.concatenate([pts[j][s_].astype(BF16) for j in range(pg)], axis=0)

        s = jnp.concatenate([_nt(qstack(h), slot(h)) for h in range(H_A)], axis=0)
        _online_update(s, None, m_scr, l_scr, acc_scr,
                       pv_fn=lambda p: jnp.concatenate(
                           [_mm(p[h * 2 * tp:(h + 1) * 2 * tp], slot(H_A + h)) for h in range(H_A)], axis=0))

        @pl.when(gi == ng - 1)
        def _():
            lam = _diff_lambda(lam_ref[...], lam_init)
            row = _mod(lax.broadcasted_iota(jnp.int32, (2 * tp, PAGE), 0), tp)
            col = lax.broadcasted_iota(jnp.int32, (2 * tp, PAGE), 1)
            mask = (col <= row) & (col < n_valid)
            mask4 = jnp.concatenate([mask] * H_A, axis=0)
            s = jnp.concatenate([_nt(qstack(h), new_ref[0, :, h * LANE:(h + 1) * LANE])
                                 for h in range(H_A)], axis=0)
            _online_update(jnp.where(mask4, s, MASKED), None, m_scr, l_scr, acc_scr,
                           pv_fn=lambda p: jnp.concatenate(
                               [_mm(p[h * 2 * tp:(h + 1) * 2 * tp],
                                    new_ref[0, :, 512 + h * LANE:512 + (h + 1) * LANE])
                                for h in range(H_A)], axis=0))
            for h in range(H_A):
                r0 = h * 2 * tp
                acc = acc_scr[r0:r0 + 2 * tp, :]
                l = l_scr[r0:r0 + 2 * tp, :]
                y = _a_finish(acc[0:tp], l[0:tp], acc[tp:], l[tp:], lam, g_ref[...],
                              z_ref[0, :, h * LANE:(h + 1) * LANE], lam_init)
                o_ref[0, :, h * LANE:(h + 1) * LANE] = y.astype(BF16)

    def page_spec(j):
        return pl.BlockSpec((PAGE * 2 * H_A, LANE),
                            lambda b, g, pt, j=j: (page_base + pt[b, g * pg + j], 0))

    def bpage_spec(j):
        return pl.BlockSpec((PAGE * 4, LANE), lambda b, g, pt, j=j: (page_base + pt[b, g * pg + j], 0))

    in_specs = [page_spec(j) for j in range(pg)] + [bpage_spec(j) for j in range(pg)] + [
            pl.BlockSpec((1, tp, 512), lambda b, g, pt: (b, 0, 0)),
            pl.BlockSpec((1, PAGE, 1024), lambda b, g, pt: (b, 0, 0)),
            pl.BlockSpec((1, tp, 512), lambda b, g, pt: (b, 0, _UNIT["az"] // 4)),
            pl.BlockSpec((4, DQ_A), lambda b, g, pt: (0, 0)),
            pl.BlockSpec((1, LANE), lambda b, g, pt: (0, 0)),
            pl.BlockSpec((CMP_BLOCK, 256), lambda b, g, pt: (0, 0))]
    grid_spec = pltpu.PrefetchScalarGridSpec(
        num_scalar_prefetch=1, grid=(db, ng), in_specs=in_specs,
        out_specs=[pl.BlockSpec((1, tp, 512), lambda b, g, pt: (b, 0, 0)),
                   pl.BlockSpec((1, pg * per, 256), lambda b, g, pt: (b, g, 0))],
        scratch_shapes=[pltpu.VMEM((H_A * 2 * tp, LANE), F32)] * 3)
    return pl.pallas_call(
        body, grid_spec=grid_spec,
        out_shape=[jax.ShapeDtypeStruct((db, tp, 512), BF16),
                   jax.ShapeDtypeStruct((db, n_pages * per, 256), F32)],
        compiler_params=_params("parallel", "arbitrary"),
        name="a_attn_sample")(page_table, *([cache2] * pg), *([cache_b2] * pg), qn, kvnew, proj3,
                              lam_p, out_g.reshape(1, LANE), wpb2)


def _b_prep(proj, g4, wpb, *, tm, pool):
    m = proj.shape[0]
    nb = tm // CMP_BLOCK

    def body(kv_ref, g_ref, wp_ref, bkv_ref, nw_ref, kvb_ref, *pool_refs):
        kc = kv_ref[:, 0:128]
        vc = kv_ref[:, 128:256]
        ks = _rms(kv_ref[:, 256:384], g_ref[2:3, :])
        vs = kv_ref[:, 384:512]
        kw = _rms(kv_ref[:, 512:640], g_ref[3:4, :])
        vw = kv_ref[:, 640:768]
        for slot, x in enumerate((kc, vc, ks, vs)):
            bkv_ref[pl.ds(slot, tm, stride=4), :] = x
        nw_ref[pl.ds(0, tm, stride=2), :] = kw
        nw_ref[pl.ds(1, tm, stride=2), :] = vw
        kvb_ref[:, 0:128] = ks.astype(BF16)
        kvb_ref[:, 128:256] = vs.astype(BF16)
        kvb_ref[:, 256:384] = kw.astype(BF16)
        kvb_ref[:, 384:512] = vw.astype(BF16)
        if pool:
            kcb_ref, vcb_ref = pool_refs
            kp = jnp.sum(kc.reshape(nb, CMP_BLOCK, LANE) * wp_ref[0][None], axis=1)
            vp = jnp.sum(vc.reshape(nb, CMP_BLOCK, LANE) * wp_ref[1][None], axis=1)
            kcb_ref[...] = _rms(kp, g_ref[1:2, :])
            vcb_ref[...] = vp

    row = lambda w: pl.BlockSpec((tm, w), lambda i: (i, 0))
    out_specs = [pl.BlockSpec((tm * 4, LANE), lambda i: (i, 0)),
                 pl.BlockSpec((tm * 2, LANE), lambda i: (i, 0)), row(512)]
    out_shape = [jax.ShapeDtypeStruct((m * 4, LANE), F32), jax.ShapeDtypeStruct((m * 2, LANE), F32),
                 jax.ShapeDtypeStruct((m, 512), BF16)]
    if pool:
        out_specs += [pl.BlockSpec((nb, LANE), lambda i: (i, 0))] * 2
        out_shape += [jax.ShapeDtypeStruct((m // CMP_BLOCK, LANE), F32)] * 2
    return pl.pallas_call(
        body, grid=(m // tm,),
        in_specs=[pl.BlockSpec((tm, 768), lambda i: (i, _UNIT["bkv"] // 6)),
                  pl.BlockSpec((4, LANE), lambda i: (0, 0)),
                  pl.BlockSpec((2, CMP_BLOCK, LANE), lambda i: (0, 0, 0))],
        out_specs=out_specs, out_shape=out_shape,
        compiler_params=_params("parallel"),
        name="b_prep")(proj, g4, wpb)


def _cmp_scores(qf, kcb):
    qh, ql = _split2(qf)
    kh, kl = _split2(kcb)
    return _nt(qh, kh) + _nt(qh, kl) + _nt(ql, kh)


def _b_attn_prompt(proj3, g4, wpb, *, t):
    b_ = proj3.shape[0]
    tq = 2 * Q_BLOCK if t % (2 * Q_BLOCK) == 0 else Q_BLOCK
    nblk = t // CMP_BLOCK
    n_top = min(N_SEL, nblk)
    tk = min(512, t)
    span = min(WINDOW + tq, t)
    scale = HEAD ** -0.5 * SCORE_LOG2E

    def body(q_ref, src_ref, bg_ref, bz_ref, g_ref, wp_ref, o_ref, bkv_ref, nw_ref,
             kv_scr, kcb_scr, vcb_scr, m_scr, l_scr, acc_scr):
        qi = pl.program_id(1)

        @pl.when(qi == 0)
        def _():
            nb = tk // CMP_BLOCK

            def rows(c, carry):
                r = pl.ds(pl.multiple_of(c * tk, tk), tk)
                kc = src_ref[0, r, 0:128]
                vc = src_ref[0, r, 128:256]
                ks = _rms(src_ref[0, r, 256:384], g_ref[2:3, :])
                vs = src_ref[0, r, 384:512]
                kw = _rms(src_ref[0, r, 512:640], g_ref[3:4, :])
                vw = src_ref[0, r, 640:768]
                for slot, x in enumerate((kc, vc, ks, vs)):
                    bkv_ref[pl.ds(c * tk * 4 + slot, tk, stride=4), :] = x
                nw_ref[pl.ds(c * tk * 2, tk, stride=2), :] = kw
                nw_ref[pl.ds(c * tk * 2 + 1, tk, stride=2), :] = vw
                for u, x in enumerate((ks, vs, kw, vw)):
                    kv_scr[r, u * LANE:(u + 1) * LANE] = x.astype(BF16)
                pr = pl.ds(pl.multiple_of(c * nb, nb), nb)
                kp = jnp.sum(kc.reshape(nb, CMP_BLOCK, LANE) * wp_ref[0][None], axis=1)
                kcb_scr[pr, :] = _rms(kp, g_ref[1:2, :])
                vcb_scr[pr, :] = jnp.sum(vc.reshape(nb, CMP_BLOCK, LANE) * wp_ref[1][None], axis=1)
                return carry
            lax.fori_loop(0, t // tk, rows, 0)

        qf = jnp.concatenate([_rms(q_ref[0, :, h * LANE:(h + 1) * LANE], g_ref[0:1, :]) * scale
                              for h in range(H_B)], axis=0)
        qb = qf.astype(BF16)
        pos = qi * tq + lax.broadcasted_iota(jnp.int32, (tq, 1), 0)

        pos_l = qi * tq + lax.broadcasted_iota(jnp.int32, (1, tq), 1)
        cur_l = _div(pos_l, CMP_BLOCK)
        blk_s = lax.broadcasted_iota(jnp.int32, (nblk, tq), 0)
        cmask = blk_s < cur_l
        kh, kl = _split2(kcb_scr[...])
        qh, ql = _split2(qf)
        cmask4 = jnp.concatenate([cmask] * H_B, axis=1)
        s = jnp.where(cmask4, _nt(kh, qh) + _nt(kl, qh) + _nt(kh, ql), NEG)
        mx = jnp.max(s, axis=0, keepdims=True)
        p = jnp.where(cmask4, jnp.exp2(s - mx), 0.0)
        pc = p / jnp.maximum(jnp.sum(p, axis=0, keepdims=True), 1e-30)
        pcs = [pc[:, h * tq:(h + 1) * tq] for h in range(H_B)]
        imp = jnp.where(cmask, pcs[0] + pcs[1] + pcs[2] + pcs[3], -1.0)
        fill = LANE - H_B * nblk
        pcq = jnp.concatenate(pcs + ([jnp.zeros((fill, tq), F32)] if fill else []), axis=0).T.astype(BF16)
        vch = vcb_scr[...].astype(BF16)
        zblk = jnp.zeros((nblk, LANE), BF16)
        vdiag = jnp.concatenate(
            [jnp.concatenate([vch if c == h else zblk for c in range(H_B)], axis=1)
             for h in range(H_B)] + ([jnp.zeros((fill, H_B * LANE), BF16)] if fill else []), axis=0)
        o_cmp = _mm(pcq, vdiag)

        rank = jnp.zeros((nblk, tq), F32)
        for mrow in range(nblk):
            cm = imp[mrow:mrow + 1, :]
            ahead = (cm > imp) | ((cm == imp) & (mrow < blk_s))
            rank = rank + ahead.astype(F32)
        sel_t = (((rank < n_top) & (imp >= 0.0)) | (blk_s == cur_l)).astype(F32)
        selq = jnp.concatenate([sel_t, jnp.zeros((LANE - nblk, tq), F32)], axis=0).T.astype(BF16)

        m_scr[...] = jnp.full(m_scr.shape, NEG, F32)
        l_scr[...] = jnp.zeros(l_scr.shape, F32)
        acc_scr[...] = jnp.zeros(acc_scr.shape, F32)
        nk = _div(qi * tq + tq + tk - 1, tk)

        def chunk(c, carry):
            ks_ = pl.ds(pl.multiple_of(c * tk, tk), tk)
            kidx = c * tk + lax.broadcasted_iota(jnp.int32, (LANE, tk), 1)
            e = (_div(kidx, CMP_BLOCK) == lax.broadcasted_iota(jnp.int32, (LANE, tk), 0)).astype(BF16)
            kpos = c * tk + lax.broadcasted_iota(jnp.int32, (tq, tk), 1)
            mk = (_mm(selq, e) > 0.5) & (kpos <= pos)
            mask = jnp.concatenate([mk] * H_B, axis=0)
            s = jnp.where(mask, _nt(qb, kv_scr[ks_, 0:128]), MASKED)
            _online_update(s, kv_scr[ks_, 128:256], m_scr, l_scr, acc_scr)
            return carry

        lax.fori_loop(0, nk, chunk, 0)

        start = jnp.clip(qi * tq + tq - span, 0, t - span)
        ws = pl.ds(pl.multiple_of(start, tq), span)
        diff = pos - (start + lax.broadcasted_iota(jnp.int32, (tq, span), 1))
        wmask = jnp.concatenate([(diff >= 0) & (diff < WINDOW)] * H_B, axis=0)
        s = jnp.where(wmask, _nt(qb, kv_scr[ws, 256:384]), MASKED)
        p = jnp.exp2(s - jnp.maximum(jnp.max(s, axis=-1, keepdims=True), NEG))
        o_win = _mm(p.astype(BF16), kv_scr[ws, 384:512]) \
            / jnp.maximum(jnp.sum(p, axis=-1, keepdims=True), 1e-30)
        o_sel = acc_scr[...] / jnp.maximum(l_scr[...], 1e-30)

        gate = _sigmoid(bg_ref[0])
        for h in range(H_B):
            rows = slice(h * tq, (h + 1) * tq)
            ob = (gate[:, 3 * h:3 * h + 1] * o_cmp[:, h * LANE:(h + 1) * LANE]
                  + gate[:, 3 * h + 1:3 * h + 2] * o_sel[rows]
                  + gate[:, 3 * h + 2:3 * h + 3] * o_win[rows])
            y = ob * _silu(bz_ref[0, :, h * LANE:(h + 1) * LANE])
            o_ref[0, :, h * LANE:(h + 1) * LANE] = y.astype(BF16)

    return pl.pallas_call(
        body, grid=(b_, t // tq),
        in_specs=[pl.BlockSpec((1, tq, 512), lambda b, i: (b, i, _UNIT["bq"] // 4)),
                  pl.BlockSpec((1, t, 768), lambda b, i: (b, 0, _UNIT["bkv"] // 6)),
                  pl.BlockSpec((1, tq, LANE), lambda b, i: (b, i, _UNIT["bg"])),
                  pl.BlockSpec((1, tq, 512), lambda b, i: (b, i, _UNIT["bz"] // 4)),
                  pl.BlockSpec((4, LANE), lambda b, i: (0, 0)),
                  pl.BlockSpec((2, CMP_BLOCK, LANE), lambda b, i: (0, 0, 0))],
        out_specs=[pl.BlockSpec((1, tq, 512), lambda b, i: (b, i, 0)),
                   pl.BlockSpec((t * 4, LANE), lambda b, i: (b, 0)),
                   pl.BlockSpec((t * 2, LANE), lambda b, i: (b, 0))],
        out_shape=[jax.ShapeDtypeStruct((b_, t, 512), BF16),
                   jax.ShapeDtypeStruct((b_ * t * 4, LANE), F32),
                   jax.ShapeDtypeStruct((b_ * t * 2, LANE), F32)],
        scratch_shapes=[pltpu.VMEM((t, 512), BF16), pltpu.VMEM((nblk, LANE), F32),
                        pltpu.VMEM((nblk, LANE), F32)] + [pltpu.VMEM((H_B * tq, LANE), F32)] * 3,
        compiler_params=_params("parallel", "arbitrary"),
        name="b_attn_prompt")(proj3, proj3, proj3, proj3, g4, wpb)


def _b_select_sample(proj3, pooled, win, neww, g4, *, past_len, n_valid):
    db = neww.shape[0]
    tp = neww.shape[1] // 2
    nblk = pooled.shape[1]
    n_top = min(N_SEL, nblk + 1)
    wb = win.shape[1] // 2
    rows = H_B * tp
    scale = HEAD ** -0.5 * SCORE_LOG2E

    def body(q_ref, pool_ref, win_ref, nw_ref, bg_ref, g_ref, qs_ref, ocw_ref, g1_ref, sel_ref):
        qf = jnp.concatenate(
            [_rms(q_ref[0, :, h * LANE:(h + 1) * LANE], g_ref[0:1, :]) * scale for h in range(H_B)],
            axis=0)
        qb = qf.astype(BF16)
        qs_ref[0] = qb
        tok = _mod(lax.broadcasted_iota(jnp.int32, (rows, 1), 0), tp)
        pos = past_len + tok
        cur = _div(pos, CMP_BLOCK)
        blk = lax.broadcasted_iota(jnp.int32, (rows, nblk), 1)
        cmask = blk < cur

        kc = _rms(pool_ref[0, :, 0:128], g_ref[1:2, :])
        s = jnp.where(cmask, _cmp_scores(qf, kc), NEG)
        mx = jnp.max(s, axis=-1, keepdims=True)
        p = jnp.where(cmask, jnp.exp2(s - mx), 0.0)
        pc = p / jnp.maximum(jnp.sum(p, axis=-1, keepdims=True), 1e-30)
        o_cmp = _mm(pc.astype(BF16), pool_ref[0, :, 128:256].astype(BF16))
        imp = pc[0:tp]
        for h in range(1, H_B):
            imp = imp + pc[h * tp:(h + 1) * tp]
        imp = jnp.where(cmask[0:tp], imp, -1.0)

        pad = jnp.concatenate([imp, jnp.zeros((LANE - tp, nblk), F32)], axis=0)
        imp_t = jnp.concatenate([pad[:, u * LANE:(u + 1) * LANE].T for u in range(nblk // LANE)], axis=0)
        mi = lax.broadcasted_iota(jnp.int32, (nblk, nblk), 0)
        ni = lax.broadcasted_iota(jnp.int32, (nblk, nblk), 1)
        sels = []
        for tkn in range(tp):
            r = imp[tkn:tkn + 1, :]
            c = imp_t[:, tkn:tkn + 1]
            ahead = (c > r) | ((c == r) & (mi < ni))
            rank = jnp.sum(ahead.astype(F32), axis=0, keepdims=True)
            sels.append(((rank < n_top) & (r >= 0.0)).astype(F32))
        sel_ref[0] = jnp.concatenate(sels, axis=0)

        kw = win_ref[0, pl.ds(0, wb, stride=2), :].astype(BF16)
        vw = win_ref[0, pl.ds(1, wb, stride=2), :].astype(BF16)
        zpad = jnp.zeros((LANE - tp, LANE), F32)
        kn = jnp.concatenate([nw_ref[0, pl.ds(0, tp, stride=2), :], zpad], axis=0).astype(BF16)
        vn = jnp.concatenate([nw_ref[0, pl.ds(1, tp, stride=2), :], zpad], axis=0).astype(BF16)
        jw =lax.broadcasted_iota(jnp.int32, (rows, wb), 1)
        dw = pos - (past_len - wb + jw)
        jn = lax.broadcasted_iota(jnp.int32, (rows, LANE), 1)
        dn = tok - jn
        wmask = jnp.concatenate([(dw >= 0) & (dw < WINDOW), (dn >= 0) & (dn < WINDOW) & (jn < n_valid)],
                                axis=1)
        sw = jnp.where(wmask, jnp.concatenate([_nt(qb, kw), _nt(qb, kn)], axis=1), NEG)
        mw = jnp.max(sw, axis=-1, keepdims=True)
        pw = jnp.where(wmask, jnp.exp2(sw - mw), 0.0)
        o_win = (_mm(pw[:, 0:wb].astype(BF16), vw) + _mm(pw[:, wb:].astype(BF16), vn)) \
            / jnp.maximum(jnp.sum(pw, axis=-1, keepdims=True), 1e-30)

        gate = _sigmoid(bg_ref[0])
        g0 = jnp.concatenate([gate[:, 3 * h:3 * h + 1] for h in range(H_B)], axis=0)
        g1 = jnp.concatenate([gate[:, 3 * h + 1:3 * h + 2] for h in range(H_B)], axis=0)
        g2 = jnp.concatenate([gate[:, 3 * h + 2:3 * h + 3] for h in range(H_B)], axis=0)
        ocw_ref[0] = g0 * o_cmp + g2 * o_win
        g1_ref[0] = jnp.broadcast_to(g1, (rows, LANE))

    per_b = lambda r, w: pl.BlockSpec((1, r, w), lambda b: (b, 0, 0))
    return pl.pallas_call(
        body, grid=(db,),
        in_specs=[pl.BlockSpec((1, tp, 512), lambda b: (b, 0, _UNIT["bq"] // 4)),
                  per_b(nblk, 256), per_b(2 * wb, LANE), per_b(2 * tp, LANE),
                  pl.BlockSpec((1, tp, LANE), lambda b: (b, 0, _UNIT["bg"])),
                  pl.BlockSpec((4, LANE), lambda b: (0, 0))],
        out_specs=[per_b(rows, LANE), per_b(rows, LANE), per_b(rows, LANE), per_b(tp, nblk)],
        out_shape=[jax.ShapeDtypeStruct((db, rows, LANE), BF16),
                   jax.ShapeDtypeStruct((db, rows, LANE), F32),
                   jax.ShapeDtypeStruct((db, rows, LANE), F32),
                   jax.ShapeDtypeStruct((db, tp, nblk), F32)],
        compiler_params=_params("parallel"),
        name="b_select_sample")(proj3, pooled, win, neww, proj3, g4)


def _selected_step(gi, ng, pages, q_ref, sel_ref, new_ref, ocw_ref, g1_ref, z_ref, o_ref,
                   m_scr, l_scr, acc_scr, *, n_valid):
    pg = len(pages)
    rows = q_ref.shape[1]
    tp = rows // H_B
    nbg = pg * (PAGE // CMP_BLOCK)
    width = pg * PAGE

    @pl.when(gi == 0)
    def _():
        m_scr[...] = jnp.full(m_scr.shape, NEG, F32)
        l_scr[...] = jnp.zeros(l_scr.shape, F32)
        acc_scr[...] = jnp.zeros(acc_scr.shape, F32)

    q = q_ref[0]

    def slot(s_):
        return jnp.concatenate([pages[j][pl.ds(s_, PAGE, stride=4), :].astype(BF16)
                                for j in range(pg)], axis=0)

    e = (_div(lax.broadcasted_iota(jnp.int32, (nbg, width), 1), CMP_BLOCK)
         == lax.broadcasted_iota(jnp.int32, (nbg, width), 0)).astype(BF16)
    mk = _mm(sel_ref[0, 0].astype(BF16), e) > 0.5
    mask = jnp.concatenate([mk] * H_B, axis=0)
    _online_update(jnp.where(mask, _nt(q, slot(2)), MASKED), slot(3), m_scr, l_scr, acc_scr)

    @pl.when(gi == ng - 1)
    def _():
        zpad = jnp.zeros((PAGE - tp, LANE), F32)
        kn = jnp.concatenate([new_ref[0, pl.ds(2, tp, stride=4), :], zpad], axis=0).astype(BF16)
        vn = jnp.concatenate([new_ref[0, pl.ds(3, tp, stride=4), :], zpad], axis=0).astype(BF16)
        tok = _mod(lax.broadcasted_iota(jnp.int32, (rows, PAGE), 0), tp)
        col = lax.broadcasted_iota(jnp.int32, (rows, PAGE), 1)
        nmask = (col <= tok) & (col < n_valid)
        _online_update(jnp.where(nmask, _nt(q, kn), MASKED), vn, m_scr, l_scr, acc_scr)
        o_sel = acc_scr[...] / jnp.maximum(l_scr[...], 1e-30)
        ob = ocw_ref[0] + g1_ref[0] * o_sel
        for h in range(H_B):
            y = ob[h * tp:(h + 1) * tp] * _silu(z_ref[0, :, h * LANE:(h + 1) * LANE])
            o_ref[0, :, h * LANE:(h + 1) * LANE] = y.astype(BF16)


def _b_selected_sample(page_table, cache2, qs, selg, bkvs, ocw, g1b, proj3, page_base,
                       *, n_valid, pg):
    db, n_pages = page_table.shape
    ng = n_pages // pg
    rows = qs.shape[1]
    tp = rows // H_B
    nbg = pg * (PAGE // CMP_BLOCK)

    def body(pt_ref, *refs):
        _selected_step(pl.program_id(1), ng, refs[:pg], *refs[pg:], n_valid=n_valid)

    def page_spec(j):
        return pl.BlockSpec((PAGE * 4, LANE), lambda b, g, pt, j=j: (page_base + pt[b, g * pg + j], 0))

    per_b = lambda r, w: pl.BlockSpec((1, r, w), lambda b, g, pt: (b, 0, 0))
    grid_spec = pltpu.PrefetchScalarGridSpec(
        num_scalar_prefetch=1, grid=(db, ng),
        in_specs=[page_spec(j) for j in range(pg)] + [
            per_b(rows, LANE),
            pl.BlockSpec((1, 1, tp, nbg), lambda b, g, pt: (b, g, 0, 0)),
            per_b(4 * tp, LANE), per_b(rows, LANE), per_b(rows, LANE),
            pl.BlockSpec((1, tp, 512), lambda b, g, pt: (b, 0, _UNIT["bz"] // 4))],
        out_specs=per_b(tp, 512),
        scratch_shapes=[pltpu.VMEM((rows, LANE), F32)] * 3)
    return pl.pallas_call(
        body, grid_spec=grid_spec,
        out_shape=jax.ShapeDtypeStruct((db, tp, 512), BF16),
        compiler_params=_params("parallel", "arbitrary"),
        name="b_selected_sample")(page_table, *([cache2] * pg), qs, selg, bkvs, ocw, g1b, proj3)


def _mlstm(proj3, conv_w, conv_b, gate_b, out_g, conv0, c0, n0, m0, *, lc, n_valid, ride=None):
    b_, t, _ = proj3.shape
    col_head = jnp.arange(H_C * LANE)[None, :] // LANE
    lane_id = jnp.arange(LANE)[:, None]
    rsel = jnp.stack([lane_id == col_head, lane_id == col_head + H_C]).astype(BF16)
    nchunk = t // lc
    dqk = H_C * DQK_C
    tail = CONV_W - 1
    base = 8
    tsq = max(lc, LANE)

    def bcast_cols(x, sel):
        hi, lo = _split2(x)
        return _mm(hi, sel) + _mm(lo, sel)

    def body(qk_ref, v_ref, if_ref, co_ref, cz_ref, cw_ref, cb_ref, gb_ref, g_ref, rsel_ref,
             conv0_ref, c0_ref, n0_ref, m0_ref,
             y_ref, conv_ref, c_ref, n_ref, m_ref, xbuf, c_scr, n_scr, m_scr):
        ci = pl.program_id(1)

        @pl.when(ci == 0)
        def _():
            xbuf[base - tail:base, :] = conv0_ref[0]
            c_scr[...] = c0_ref[0]
            n_scr[...] = n0_ref[0]
            m_scr[...] = m0_ref[0]

        xbuf[base:base + lc, :] = qk_ref[0]
        acc = cb_ref[...]
        for j in range(CONV_W):
            acc = acc + xbuf[base - tail + j:base - tail + j + lc, :] * cw_ref[j:j + 1, :]
        new_tail = xbuf[base + n_valid - tail:base + n_valid, :]
        xbuf[base - tail:base, :] = new_tail
        conv_ref[0] = new_tail
        qk = _silu(acc)

        gt = if_ref[0] + gb_ref[...]
        lf = jnp.minimum(gt, 0.0) - jnp.log(1.0 + jnp.exp(-jnp.abs(gt)))
        ti = lax.broadcasted_iota(jnp.int32, (lc, lc), 0)
        si = lax.broadcasted_iota(jnp.int32, (lc, lc), 1)
        tri = (si <= ti).astype(BF16)
        l1, l2, l3 = _split3(lf)
        bcum = _mm(tri, l1) + _mm(tri, l2) + _mm(tri, l3)
        zrow = jnp.zeros((tsq - lc, LANE), F32)
        gt_t = (jnp.concatenate([gt, zrow], axis=0) if tsq > lc else gt).T
        b_t = (jnp.concatenate([bcum, zrow], axis=0) if tsq > lc else bcum).T
        dmask = (si <= ti) & (si < n_valid)
        svalid = lax.broadcasted_iota(jnp.int32, (lc, LANE), 0) < n_valid
        icol_all = bcast_cols(gt, rsel_ref[0])
        bcol_all = bcast_cols(bcum, rsel_ref[1])
        gate_all = _sigmoid(co_ref[0]) * _silu(cz_ref[0])
        low = lax.broadcasted_iota(jnp.int32, (lc, LANE), 1) < DQK_C
        row_low = lax.broadcasted_iota(jnp.int32, (LANE, LANE), 0) < DQK_C

        c_new, n_new, m_new_all = [], [], []
        for j in range(H_C // 2):
            qp = qk[:, j * LANE:(j + 1) * LANE] * (DQK_C ** -0.5)
            kp = qk[:, dqk + j * LANE:dqk + (j + 1) * LANE]
            cp = c_scr[j]
            cpb = cp.astype(BF16)
            npair = n_scr[j:j + 1, :]
            c_upd, n_upd, carries = None, None, []
            for u in range(2):
                h = 2 * j + u
                sel = low if u == 0 else jnp.logical_not(low)
                qm = jnp.where(sel, qp, 0.0)
                km = jnp.where(sel, kp, 0.0)
                qmb, kmb = qm.astype(BF16), km.astype(BF16)
                vh = v_ref[0, :, h * HEAD:(h + 1) * HEAD]
                bcol = bcol_all[:, h * LANE:(h + 1) * LANE]
                icol = icol_all[:, h * LANE:(h + 1) * LANE]
                brow = b_t[H_C + h:H_C + h + 1, 0:lc]
                irow = gt_t[h:h + 1, 0:lc]
                m_h = m_scr[h:h + 1, :]

                d = jnp.where(dmask, bcol[:, 0:lc] - brow + irow, NEG)
                inter = bcol + m_h
                m_t = jnp.maximum(inter, jnp.max(d, axis=1, keepdims=True))
                w_intra = jnp.exp(d - m_t[:, 0:lc])
                w_inter = jnp.exp(inter - m_t)
                sqk = _nt(qmb, kmb) * w_intra
                num = w_inter * _mm(qmb, cpb) + _mm(sqk.astype(BF16), vh.astype(BF16))
                den = (w_inter * jnp.sum(qm * npair, axis=1, keepdims=True)
                       + jnp.sum(sqk, axis=1, keepdims=True))
                hh = num / jnp.maximum(jnp.abs(den), jnp.exp(-m_t))
                y_ref[0, :, h * HEAD:(h + 1) * HEAD] = (
                    _rms(hh, g_ref[...]) * gate_all[:, h * HEAD:(h + 1) * HEAD]).astype(BF16)

                b_last = bcol[n_valid - 1:n_valid, :]
                dec = jnp.where(svalid, b_last - bcol + icol, NEG)
                m_new = jnp.maximum(b_last + m_h, jnp.max(dec, axis=0, keepdims=True))
                wk = jnp.exp(dec - m_new)
                carries.append(jnp.exp(b_last + m_h - m_new))
                cu = _tn(kmb, (wk * vh).astype(BF16))
                nu = jnp.sum(wk * km, axis=0, keepdims=True)
                c_upd = cu if c_upd is None else c_upd + cu
                n_upd = nu if n_upd is None else n_upd + nu
                m_new_all.append(m_new)
            c_new.append(jnp.where(row_low, carries[0], carries[1]) * cp + c_upd)
            n_new.append(jnp.where(low[0:1], carries[0], carries[1]) * npair + n_upd)

        for j in range(H_C // 2):
            c_scr[j] = c_new[j]
            n_scr[j:j + 1, :] = n_new[j]
        for h in range(H_C):
            m_scr[h:h + 1, :] = m_new_all[h]
        c_ref[0] = c_scr[...]
        n_ref[0] = n_scr[...]
        m_ref[0] = m_scr[...]

    spec = lambda shape, f: pl.BlockSpec(shape, (lambda b, c, pt: f(b, c)) if ride is not None else f)
    col = lambda u: spec((1, lc, 1024), lambda b, c: (b, c, u))
    const = lambda *shape: spec(shape, lambda b, c: (0,) * len(shape))
    per_b = lambda *shape: spec((1,) + shape, lambda b, c: (b,) + (0,) * len(shape))
    in_specs = [col(_UNIT["cqk"] // 8), col(_UNIT["cv"] // 8),
                spec((1, lc, LANE), lambda b, c: (b, c, _UNIT["cif"])),
                col(_UNIT["co"] // 8), col(_UNIT["cz"] // 8),
                const(CONV_W, 1024), const(1, 1024), const(1, LANE), const(1, LANE),
                const(2, LANE, H_C * LANE),
                per_b(tail, 1024), per_b(H_C // 2, LANE, HEAD), per_b(H_C // 2, LANE), per_b(H_C, LANE)]
    out_specs = [spec((1, lc, 1024), lambda b, c: (b, c, 0)),
                 per_b(tail, 1024), per_b(H_C // 2, LANE, HEAD), per_b(H_C // 2, LANE), per_b(H_C, LANE)]
    out_shape = [jax.ShapeDtypeStruct((b_, t, 1024), BF16),
                 jax.ShapeDtypeStruct((b_, tail, 1024), F32),
                 jax.ShapeDtypeStruct((b_, H_C // 2, LANE, HEAD), F32),
                 jax.ShapeDtypeStruct((b_, H_C // 2, LANE), F32),
                 jax.ShapeDtypeStruct((b_, H_C, LANE), F32)]
    scratch = [pltpu.VMEM((base + lc, 1024), F32), pltpu.VMEM((H_C // 2, LANE, HEAD), F32),
               pltpu.VMEM((H_C // 2, LANE), F32), pltpu.VMEM((H_C, LANE), F32)]
    args = [proj3, proj3, proj3, proj3, proj3, conv_w, conv_b, gate_b, out_g, rsel, conv0, c0, n0, m0]
    if ride is None:
        return pl.pallas_call(
            body, grid=(b_, nchunk), in_specs=in_specs, out_specs=out_specs, out_shape=out_shape,
            scratch_shapes=scratch, compiler_params=_params("parallel", "arbitrary"),
            name="mlstm")(*args)

    page_table, cache2, qs, selg, bkvs, ocw, g1b, sproj3, page_base, r_valid, pg = ride
    db, n_pages = page_table.shape
    ng = n_pages // pg
    assert b_ * nchunk == db * ng
    rrows = qs.shape[1]
    rtp = rrows // H_B
    nbg = pg * (PAGE // CMP_BLOCK)
    n_in, n_out = len(in_specs), len(out_specs)
    step = lambda b, c: b * nchunk + c
    sb = lambda b, c: step(b, c) // ng
    sg = lambda b, c: step(b, c) % ng

    def fused_body(pt_ref, *refs):
        ins, r_ins = refs[:n_in], refs[n_in:n_in + pg + 6]
        outs = refs[n_in + pg + 6:n_in + pg + 6 + n_out]
        r_out = refs[n_in + pg + 6 + n_out]
        scr = refs[n_in + pg + 7 + n_out:]
        _selected_step(sg(pl.program_id(0), pl.program_id(1)), ng, r_ins[:pg], *r_ins[pg:], r_out,
                       *scr[len(scratch):], n_valid=r_valid)
        body(*ins, *outs, *scr[:len(scratch)])

    rb = lambda r, w: pl.BlockSpec((1, r, w), lambda b, c, pt: (sb(b, c), 0, 0))
    in_specs += [pl.BlockSpec((PAGE * 4, LANE),
                              lambda b, c, pt, j=j: (page_base + pt[sb(b, c), sg(b, c) * pg + j], 0))
                 for j in range(pg)]
    in_specs += [rb(rrows, LANE),
                 pl.BlockSpec((1, 1, rtp, nbg), lambda b, c, pt: (sb(b, c), sg(b, c), 0, 0)),
                 rb(4 * rtp, LANE), rb(rrows, LANE), rb(rrows, LANE),
                 pl.BlockSpec((1, rtp, 512), lambda b, c, pt: (sb(b, c), 0, _UNIT["bz"] // 4))]
    grid_spec = pltpu.PrefetchScalarGridSpec(
        num_scalar_prefetch=1, grid=(b_, nchunk), in_specs=in_specs,
        out_specs=out_specs + [rb(rtp, 512)],
        scratch_shapes=scratch + [pltpu.VMEM((rrows, LANE), F32)] * 3)
    return pl.pallas_call(
        fused_body, grid_spec=grid_spec,
        out_shape=out_shape + [jax.ShapeDtypeStruct((db, rtp, 512), BF16)],
        compiler_params=_params("arbitrary", "arbitrary"),
        name="mlstm_with_selected")(page_table, *args, *([cache2] * pg), qs, selg, bkvs, ocw, g1b, sproj3)


def _w_prep(w_in):
    depth, d, n_in = w_in.shape
    kt_n = d // LANE
    per_col = kt_n * depth
    rows = w_in.reshape(depth, kt_n, LANE, n_in).transpose(3, 1, 0, 2).reshape(n_in * per_col, LANE)
    src_off, off = {}, 0
    for name, width in _SRC:
        src_off[name] = (off, width)
        off += width
    assert off == n_in
    starts, valids = [0] * (N_PROJ // LANE), [0] * (N_PROJ // LANE)
    for name in _DST_ORDER:
        s0, width = src_off[name]
        for u in range(-(-width // LANE)):
            starts[_UNIT[name] + u] = s0 + u * LANE
            valids[_UNIT[name] + u] = min(LANE, width - u * LANE)
    table = jnp.array([starts, valids], jnp.int32)

    upb = 4

    def body(tbl_ref, *refs):
        w_refs, o_ref = refs[:upb], refs[upb]
        for k in range(upb):
            keep = lax.broadcasted_iota(jnp.int32, (LANE, LANE), 0) < tbl_ref[1, pl.program_id(0) * upb + k]
            xt = jnp.swapaxes(w_refs[k][...].reshape(LANE, per_col, LANE), 0, 1)
            for l in range(depth):
                for kt in range(kt_n):
                    o_ref[l, k * LANE:(k + 1) * LANE, kt * LANE:(kt + 1) * LANE] = (
                        jnp.where(keep, xt[kt * depth + l], 0.0).astype(BF16))

    def src_spec(k):
        return pl.BlockSpec((pl.Element(LANE * per_col), pl.Element(LANE)),
                            lambda u, tbl, k=k: (tbl[0, u * upb + k] * per_col, 0))

    grid_spec = pltpu.PrefetchScalarGridSpec(
        num_scalar_prefetch=1, grid=(N_PROJ // LANE // upb,),
        in_specs=[src_spec(k) for k in range(upb)],
        out_specs=pl.BlockSpec((depth, upb * LANE, d), lambda u, tbl: (0, u, 0)))
    return pl.pallas_call(
        body, grid_spec=grid_spec,
        out_shape=jax.ShapeDtypeStruct((depth, N_PROJ, d), BF16),
        compiler_params=_params("parallel"),
        name="w_prep")(table, *([rows] * upb))


def _c_from_pairs(c):
    b = c.shape[0]
    return jnp.swapaxes(c.reshape(b, H_C, DQK_C, HEAD), -1, -2)


def _pick(n, prefs):
    for p in prefs:
        if n % p == 0:
            return p
    return n


def kernel(x_prompt, x_sample, cache_a_kv, cache_b_kv, state_b_win, state_c_conv, state_c_C,
           state_c_n, state_c_m, page_table, norm_g, w_in, w_out, a_qk_g, a_lambda, a_out_g,
           b_qk_g, b_cmp_w, c_conv_w, c_conv_b, c_gate_b, c_out_g):
    bp, t, d = x_prompt.shape
    db, ts, _ = x_sample.shape
    depth = norm_g.shape[0]
    n_pool = cache_a_kv.shape[1]
    n_pages = page_table.shape[1]
    past_len = n_pages * PAGE
    wb = state_b_win.shape[2]
    tp = 8
    assert ts <= tp and t % MLSTM_CHUNK == 0 and t % Q_BLOCK == 0 and d == 2048
    mp, ms = bp * t, db * tp
    pg = _pick(n_pages, (16, 8, 4, 2, 1))

    cache_a2 = cache_a_kv.reshape(depth * n_pool * PAGE * 2 * H_A, HEAD)
    cache_b2 = cache_b_kv.reshape(depth * n_pool * PAGE * 4, HEAD)
    win_rows = state_b_win.reshape(depth, db, wb * 2, HEAD)
    hp = x_prompt.reshape(mp, d)
    hs = jnp.pad(x_sample, ((0, 0), (0, tp - ts), (0, 0))).reshape(ms, d)

    tm_p = _pick(mp, (1024, 512, 256, 128))
    tq_a = _pick(t, (512, 256, 128))
    outs_p = [[] for _ in range(7)]
    outs_s = [[] for _ in range(7)]

    w_all = _w_prep(w_in)
    for l in range(depth):
        lam_init = 0.8 - 0.6 * math.exp(-0.3 * l)
        w_o = w_out[l].astype(BF16)
        g_a = jnp.tile(a_qk_g[l], (1, 2))
        g_b = b_qk_g[l]
        wpb = jnp.broadcast_to(b_cmp_w[l][:, :, None], (2, CMP_BLOCK, LANE))
        wpb2 = jnp.concatenate([wpb[0], wpb[1]], axis=1)
        gate_b = jnp.pad(c_gate_b[l].reshape(1, 2 * H_C), ((0, 0), (0, LANE - 2 * H_C)))
        conv_b = c_conv_b[l].reshape(1, -1)
        out_gc = c_out_g[l].reshape(1, LANE)

        proj = _norm_matmul(hp, norm_g[l], w_all, l, tm=tm_p, tn=1024)
        proj3 = proj.reshape(bp, t, N_PROJ)
        ya, akv = _a_attn_prompt(proj3, g_a, a_lambda[l], a_out_g[l], lam_init, tq=tq_a)
        yb, bkvs, neww = _b_attn_prompt(proj3, g_b, wpb, t=t)

        base = l * n_pool
        sproj = _norm_matmul(hs, norm_g[l], w_all, l, tm=ms, tn=1024)
        sproj3 = sproj.reshape(db, tp, N_PROJ)
        sqn, skn, svb, sakv = _a_prep(sproj, g_a, tm=ms)
        kvnew = jnp.concatenate([skn.reshape(db, tp, 512), svb.reshape(db, tp, 512)], axis=2)
        kvnew = jnp.pad(kvnew, ((0, 0), (0, PAGE - tp), (0, 0)))
        sya, pooled = _a_attn_sample(page_table, cache_a2, sqn.reshape(db, tp, 512), kvnew, sproj3,
                                     a_lambda[l], a_out_g[l], lam_init, base, cache_b2, wpb2,
                                     n_valid=ts, pg=pg)
        sbkvs, sneww, _ = _b_prep(sproj, g_b, wpb, tm=ms, pool=False)
        qs, ocw, g1b, sel = _b_select_sample(
            sproj3, pooled, win_rows[l], sneww.reshape(db, tp * 2, HEAD), g_b,
            past_len=past_len, n_valid=ts)

        p_args = (proj3, c_conv_w[l], conv_b, gate_b, out_gc,
                  jnp.zeros((bp, CONV_W - 1, 2 * H_C * DQK_C), F32), jnp.zeros((bp, H_C // 2, LANE, HEAD), F32),
                  jnp.zeros((bp, H_C // 2, LANE), F32), jnp.zeros((bp, H_C, LANE), F32))
        steps = bp * (t // MLSTM_CHUNK)
        pg_r = db * n_pages // steps if (db * n_pages) % steps == 0 else 0
        rides = 1 <= pg_r <= 16 and n_pages % pg_r == 0
        pg_s = pg_r if rides else pg
        nbg = pg_s * (PAGE // CMP_BLOCK)
        selg = sel.reshape(db, tp, n_pages // pg_s, nbg).transpose(0, 2, 1, 3)
        s_args = (page_table, cache_b2, qs, selg, sbkvs.reshape(db, tp * 4, HEAD), ocw, g1b, sproj3, base)
        if rides:
            yc, conv_p, c_p, n_p, m_p, syb = _mlstm(*p_args, lc=MLSTM_CHUNK, n_valid=MLSTM_CHUNK,
                                                    ride=s_args + (ts, pg_s))
        else:
            yc, conv_p, c_p, n_p, m_p = _mlstm(*p_args, lc=MLSTM_CHUNK, n_valid=MLSTM_CHUNK)
            syb = _b_selected_sample(*s_args, n_valid=ts, pg=pg_s)
        c_p, n_p = _c_from_pairs(c_p), n_p.reshape(bp, H_C, DQK_C)
        hp = _out_proj(hp, ya.reshape(mp, 512), yb.reshape(mp, 512), yc.reshape(mp, 1024), w_o,
                       tm=tm_p, tn=1024)
        win_p = jnp.concatenate([jnp.zeros((bp, wb, 2, HEAD), F32), neww.reshape(bp, t, 2, HEAD)],
                                axis=1)[:, -wb:]
        for lst, a in zip(outs_p, (akv.reshape(bp, t, 2, H_A, HEAD), bkvs.reshape(bp, t, 4, HEAD),
                                   win_p, conv_p, c_p, n_p, m_p[:, :, 0])):
            lst.append(a)

        m0 = jnp.broadcast_to(state_c_m[l][:, :, None], (db, H_C, LANE))
        syc, conv_s, c_s, n_s, m_s = _mlstm(
            sproj3, c_conv_w[l], conv_b, gate_b, out_gc, state_c_conv[l],
            jnp.swapaxes(state_c_C[l], -1, -2).reshape(db, H_C // 2, LANE, HEAD),
            state_c_n[l].reshape(db, H_C // 2, LANE), m0, lc=tp, n_valid=ts)
        c_s, n_s = _c_from_pairs(c_s), n_s.reshape(db, H_C, DQK_C)
        hs = _out_proj(hs, sya.reshape(ms, 512), syb.reshape(ms, 512), syc.reshape(ms, 1024), w_o,
                       tm=ms, tn=1024)
        win_s = jnp.concatenate([state_b_win[l], sneww.reshape(db, tp, 2, HEAD)[:, :ts]], axis=1)[:, -wb:]
        for lst, a in zip(outs_s, (sakv.reshape(db, tp, 2, H_A, HEAD)[:, :ts],
                                   sbkvs.reshape(db, tp, 4, HEAD)[:, :ts],
                                   win_s, conv_s, c_s, n_s, m_s[:, :, 0])):
            lst.append(a)

    y_p = hp.reshape(bp, t, d)
    y_s = hs.reshape(db, tp, d)[:, :ts]
    sp = [jnp.stack(x) for x in outs_p]
    ss = [jnp.stack(x) for x in outs_s]
    return (y_p, y_s, sp[0], ss[0], sp[1], ss[1], sp[2], ss[2], sp[3], ss[3],
            sp[4], ss[4], sp[5], ss[5], sp[6], ss[6])
```
